```python
import jax, jax.numpy as jnp
from jax import lax
import numpy as np

D_MODEL = 1024
BATCH = 16
SEQ = 2048
DEPTH = 4

SSM_EXPAND = 2
SSM_WIDTH = SSM_EXPAND * D_MODEL
SSM_HEAD_DIM = 64
SSM_HEADS = SSM_WIDTH // SSM_HEAD_DIM
SSM_GROUPS = 2
SSM_STATE = 128
CONV_WIDTH = 4
CHUNK = 128
CONV_CH = SSM_WIDTH + 2 * SSM_GROUPS * SSM_STATE
POOL_WIDTH = D_MODEL
POOL_WINDOWS = (2, 4, 8, 16)
POOL_GROUPS = 4
POOL_GROUP_DIM = POOL_WIDTH // POOL_GROUPS
SB_WIDTH = D_MODEL
SB_HEAD_DIM = 64
SB_HEADS = SB_WIDTH // SB_HEAD_DIM
SB_BLOCK = 128
N_BRANCHES = 3
EPS = 1e-6
IN_SIZES = (SSM_WIDTH, CONV_CH, SSM_HEADS, POOL_WIDTH, POOL_WIDTH, 3 * SB_WIDTH, SB_WIDTH, N_BRANCHES * D_MODEL)
IN_COLS = SSM_WIDTH + CONV_CH + SSM_HEADS + 2 * POOL_WIDTH + 4 * SB_WIDTH + N_BRANCHES * D_MODEL

kernel_name = "hybrid_ssd_pool_stickbreak_gated_block"


def _split_points():
    pts, run = [], 0
    for s in IN_SIZES[:-1]:
        run += s
        pts.append(run)
    return pts


def rms_norm(x, w):
    xf = x.astype(jnp.float32)
    var = jnp.mean(xf * xf, axis=-1, keepdims=True)
    return (xf * lax.rsqrt(var + EPS)).astype(x.dtype) * w


def causal_dwconv(u, w, b):
    S = u.shape[1]
    up = jnp.pad(u, ((0, 0), (CONV_WIDTH - 1, 0), (0, 0)))
    out = b
    for k in range(CONV_WIDTH):
        out = out + up[:, k:k + S] * w[k]
    return out


def segsum(a):
    T = a.shape[-1]
    cs = jnp.cumsum(a, axis=-1)
    seg = cs[..., :, None] - cs[..., None, :]
    mask = jnp.tril(jnp.ones((T, T), dtype=bool))
    return jnp.where(mask, seg, -jnp.inf)


def ssd_chunked(xh, dt, a, Bg, Cg):
    Bsz, S, H, P = xh.shape
    G, N = Bg.shape[2], Bg.shape[3]
    hpg = H // G
    nc = S // CHUNK
    xdt = (xh * dt[..., None]).reshape(Bsz, nc, CHUNK, G, hpg, P)
    adt = (dt.astype(jnp.float32) * a.astype(jnp.float32)).reshape(Bsz, nc, CHUNK, G, hpg)
    adt = jnp.moveaxis(adt, 2, -1)
    Bc = Bg.reshape(Bsz, nc, CHUNK, G, N)
    Cc = Cg.reshape(Bsz, nc, CHUNK, G, N)
    a_cum = jnp.cumsum(adt, axis=-1)
    decay_in = jnp.exp(segsum(adt))
    cb = jnp.einsum('bclgn,bcsgn->bcgls', Cc, Bc)
    y_diag = jnp.einsum('bcghls,bcsghp->bclghp', cb[:, :, :, None] * decay_in, xdt)
    decay_states = jnp.exp(a_cum[..., -1:] - a_cum)
    states = jnp.einsum('bclgn,bcghl,bclghp->bcghpn', Bc, decay_states, xdt)
    chunk_decay = jnp.exp(a_cum[..., -1])

    def step(carry, inp):
        st, dec = inp
        return carry * dec[..., None, None] + st, carry

    init = jnp.zeros_like(states[:, 0])
    _, prev = lax.scan(step, init, (jnp.moveaxis(states, 1, 0), jnp.moveaxis(chunk_decay, 1, 0)))
    prev = jnp.moveaxis(prev, 0, 1)
    y_off = jnp.einsum('bclgn,bcghpn,bcghl->bclghp', Cc, prev, jnp.exp(a_cum))
    return (y_diag + y_off).reshape(Bsz, S, H, P)


def mamba2_branch(z, xbc, dt_raw, conv_w, conv_b, dt_bias, a_log, d_skip, ssm_norm_w):
    Bsz, S, _ = z.shape
    xbc = jax.nn.silu(causal_dwconv(xbc, conv_w, conv_b))
    xs, Bg, Cg = jnp.split(xbc, [SSM_WIDTH, SSM_WIDTH + SSM_GROUPS * SSM_STATE], axis=-1)
    xh = xs.reshape(Bsz, S, SSM_HEADS, SSM_HEAD_DIM)
    Bg = Bg.reshape(Bsz, S, SSM_GROUPS, SSM_STATE)
    Cg = Cg.reshape(Bsz, S, SSM_GROUPS, SSM_STATE)
    dt = jax.nn.softplus((dt_raw + dt_bias).astype(jnp.float32))
    a = -jnp.exp(a_log.astype(jnp.float32))
    y = ssd_chunked(xh, dt, a, Bg, Cg) + xh * d_skip[:, None]
    y = y.reshape(Bsz, S, SSM_WIDTH).astype(z.dtype)
    return rms_norm(y * jax.nn.silu(z), ssm_norm_w)


def pool_branch(u, gate, pool_w, pool_scale):
    Bsz, S, _ = u.shape
    uf = u.astype(jnp.float32).reshape(Bsz, S, POOL_GROUPS, POOL_GROUP_DIM)
    cs = jnp.cumsum(uf, axis=1)
    pos = jnp.arange(S)
    pooled = []
    for g, w in enumerate(POOL_WINDOWS):
        csw = jnp.pad(cs[:, :, g], ((0, 0), (w, 0), (0, 0)))
        win_sum = csw[:, w:] - csw[:, :S]
        cnt = jnp.minimum(pos + 1, w).astype(jnp.float32)
        pooled.append(win_sum / cnt[None, :, None])
    mixed = jnp.stack(pooled, axis=2) - uf
    mixed = jnp.einsum('bsgi,gio->bsgo', mixed.astype(u.dtype), pool_w).reshape(Bsz, S, POOL_WIDTH)
    return (mixed * pool_scale * jax.nn.silu(gate)).astype(u.dtype)


def stick_breaking_branch(qkv, gate):
    Bsz, S, _ = qkv.shape
    q, k, v = jnp.split(qkv, 3, axis=-1)

    def heads(t):
        return t.reshape(Bsz, S, SB_HEADS, SB_HEAD_DIM).transpose(0, 2, 1, 3)

    q, k, v = heads(q), heads(k), heads(v)
    scale = SB_HEAD_DIM ** -0.5
    outs = []
    for i in range(S // SB_BLOCK):
        q0 = i * SB_BLOCK
        kend = q0 + SB_BLOCK
        qb = q[:, :, q0:kend]
        kb = k[:, :, :kend]
        vb = v[:, :, :kend]
        z = jnp.einsum('bhtd,bhsd->bhts', qb, kb).astype(jnp.float32) * scale
        causal = (q0 + jnp.arange(SB_BLOCK))[:, None] > jnp.arange(kend)[None, :]
        log_beta = jax.nn.log_sigmoid(z)
        log_one_minus = jnp.where(causal, jax.nn.log_sigmoid(-z), 0.0)
        later = lax.cumsum(log_one_minus, axis=3, reverse=True) - log_one_minus
        att = jnp.where(causal, jnp.exp(log_beta + later), 0.0)
        outs.append(jnp.einsum('bhts,bhsd->bhtd', att.astype(vb.dtype), vb))
    o = jnp.concatenate(outs, axis=2).transpose(0, 2, 1, 3).reshape(Bsz, S, SB_WIDTH)
    return o * jax.nn.silu(gate)


def hybrid_layer(x, norm_w, w_in, conv_w, conv_b, dt_bias, a_log, d_skip, ssm_norm_w,
                 pool_w, pool_scale, w_proj_ssm, w_proj_pool, w_proj_sb, w_out):
    Bsz, S, D = x.shape
    h = rms_norm(x, norm_w)
    proj = h @ w_in
    z, xbc, dt_raw, pool_u, pool_gate, qkv, sb_gate, merge = jnp.split(proj, _split_points(), axis=-1)
    y_ssm = mamba2_branch(z, xbc, dt_raw, conv_w, conv_b, dt_bias, a_log, d_skip, ssm_norm_w) @ w_proj_ssm
    y_pool = pool_branch(pool_u, pool_gate, pool_w, pool_scale) @ w_proj_pool
    y_sb = stick_breaking_branch(qkv, sb_gate) @ w_proj_sb
    g = jax.nn.sigmoid(merge.astype(jnp.float32)).reshape(Bsz, S, N_BRANCHES, D).astype(x.dtype)
    merged = g[:, :, 0] * y_ssm + g[:, :, 1] * y_pool + g[:, :, 2] * y_sb
    return x + (merged @ w_out).astype(x.dtype)


def _fwd_setup_inputs(seed: int = 0) -> dict:
    key = jax.random.key(seed)
    ks = jax.random.split(key, 17)
    f32 = jnp.float32
    nrm = lambda k, shape, s: jax.random.normal(k, shape, f32) * s
    dt = jnp.exp(jax.random.uniform(ks[5], (DEPTH, SSM_HEADS), f32, float(np.log(1e-3)), float(np.log(1e-1))))
    return {
        "x": nrm(ks[0], (BATCH, SEQ, D_MODEL), 1.0),
        "norm_w": 1.0 + nrm(ks[1], (DEPTH, D_MODEL), 0.02),
        "w_in": nrm(ks[2], (DEPTH, D_MODEL, IN_COLS), D_MODEL ** -0.5),
        "conv_w": nrm(ks[3], (DEPTH, CONV_WIDTH, CONV_CH), CONV_WIDTH ** -0.5),
        "conv_b": nrm(ks[4], (DEPTH, CONV_CH), 0.02),
        "dt_bias": dt + jnp.log(-jnp.expm1(-dt)),
        "a_log": jnp.log(jax.random.uniform(ks[6], (DEPTH, SSM_HEADS), f32, 1.0, 16.0)),
        "d_skip": 1.0 + nrm(ks[7], (DEPTH, SSM_HEADS), 0.02),
        "ssm_norm_w": 1.0 + nrm(ks[8], (DEPTH, SSM_WIDTH), 0.02),
        "pool_w": nrm(ks[9], (DEPTH, POOL_GROUPS, POOL_GROUP_DIM, POOL_GROUP_DIM), POOL_GROUP_DIM ** -0.5),
        "pool_scale": 1.0 + nrm(ks[10], (DEPTH, POOL_WIDTH), 0.02),
        "w_proj_ssm": nrm(ks[11], (DEPTH, SSM_WIDTH, D_MODEL), SSM_WIDTH ** -0.5),
        "w_proj_pool": nrm(ks[12], (DEPTH, POOL_WIDTH, D_MODEL), POOL_WIDTH ** -0.5),
        "w_proj_sb": nrm(ks[13], (DEPTH, SB_WIDTH, D_MODEL), SB_WIDTH ** -0.5),
        "w_out": nrm(ks[14], (DEPTH, D_MODEL, D_MODEL), (N_BRANCHES * D_MODEL) ** -0.5),
        "final_norm_w": 1.0 + nrm(ks[15], (D_MODEL,), 0.02),
    }


def _fwd_reference(x, norm_w, w_in, conv_w, conv_b, dt_bias, a_log, d_skip, ssm_norm_w,
              pool_w, pool_scale, w_proj_ssm, w_proj_pool, w_proj_sb, w_out, final_norm_w):
    for l in range(DEPTH):
        x = hybrid_layer(x, norm_w[l], w_in[l], conv_w[l], conv_b[l], dt_bias[l], a_log[l],
                         d_skip[l], ssm_norm_w[l], pool_w[l], pool_scale[l], w_proj_ssm[l],
                         w_proj_pool[l], w_proj_sb[l], w_out[l])
    return rms_norm(x, final_norm_w)


import jax as _jax
import jax.numpy as _jnp

TWIN_FORMAT = 'train_step'
FWD_PARAMS = ['x', 'norm_w', 'w_in', 'conv_w', 'conv_b', 'dt_bias', 'a_log', 'd_skip', 'ssm_norm_w', 'pool_w', 'pool_scale', 'w_proj_ssm', 'w_proj_pool', 'w_proj_sb', 'w_out', 'final_norm_w']
TWIN_WEIGHTS = ['norm_w', 'w_in', 'conv_w', 'conv_b', 'dt_bias', 'a_log', 'd_skip', 'ssm_norm_w', 'pool_w', 'pool_scale', 'w_proj_ssm', 'w_proj_pool', 'w_proj_sb', 'w_out', 'final_norm_w']
TWIN_DIFF_INPUT = 'x'
TWIN_INPUTS = ['x', 'norm_w', 'w_in', 'conv_w', 'conv_b', 'dt_bias', 'a_log', 'd_skip', 'ssm_norm_w', 'pool_w', 'pool_scale', 'w_proj_ssm', 'w_proj_pool', 'w_proj_sb', 'w_out', 'final_norm_w', 'loss_target', 'm_norm_w', 'm_w_in', 'm_conv_w', 'm_conv_b', 'm_dt_bias', 'm_a_log', 'm_d_skip', 'm_ssm_norm_w', 'm_pool_w', 'm_pool_scale', 'm_w_proj_ssm', 'm_w_proj_pool', 'm_w_proj_sb', 'm_w_out', 'm_final_norm_w', 'v_norm_w', 'v_w_in', 'v_conv_w', 'v_conv_b', 'v_dt_bias', 'v_a_log', 'v_d_skip', 'v_ssm_norm_w', 'v_pool_w', 'v_pool_scale', 'v_w_proj_ssm', 'v_w_proj_pool', 'v_w_proj_sb', 'v_w_out', 'v_final_norm_w']
TWIN_OUTPUTS = ['loss', 'grad_x', 'grad_norm_w', 'grad_w_in', 'grad_conv_w', 'grad_conv_b', 'grad_dt_bias', 'grad_a_log', 'grad_d_skip', 'grad_ssm_norm_w', 'grad_pool_w', 'grad_pool_scale', 'grad_w_proj_ssm', 'grad_w_proj_pool', 'grad_w_proj_sb', 'grad_w_out', 'grad_final_norm_w', 'delta_norm_w', 'delta_w_in', 'delta_conv_w', 'delta_conv_b', 'delta_dt_bias', 'delta_a_log', 'delta_d_skip', 'delta_ssm_norm_w', 'delta_pool_w', 'delta_pool_scale', 'delta_w_proj_ssm', 'delta_w_proj_pool', 'delta_w_proj_sb', 'delta_w_out', 'delta_final_norm_w', 'new_m_norm_w', 'new_m_w_in', 'new_m_conv_w', 'new_m_conv_b', 'new_m_dt_bias', 'new_m_a_log', 'new_m_d_skip', 'new_m_ssm_norm_w', 'new_m_pool_w', 'new_m_pool_scale', 'new_m_w_proj_ssm', 'new_m_w_proj_pool', 'new_m_w_proj_sb', 'new_m_w_out', 'new_m_final_norm_w', 'new_v_norm_w', 'new_v_w_in', 'new_v_conv_w', 'new_v_conv_b', 'new_v_dt_bias', 'new_v_a_log', 'new_v_d_skip', 'new_v_ssm_norm_w', 'new_v_pool_w', 'new_v_pool_scale', 'new_v_w_proj_ssm', 'new_v_w_proj_pool', 'new_v_w_proj_sb', 'new_v_w_out', 'new_v_final_norm_w']
TWIN_LEAF_KINDS = {'loss': 'loss', 'grad_x': 'grad_x', 'grad_norm_w': 'grad_w', 'grad_w_in': 'grad_w', 'grad_conv_w': 'grad_w', 'grad_conv_b': 'grad_w', 'grad_dt_bias': 'grad_w', 'grad_a_log': 'grad_w', 'grad_d_skip': 'grad_w', 'grad_ssm_norm_w': 'grad_w', 'grad_pool_w': 'grad_w', 'grad_pool_scale': 'grad_w', 'grad_w_proj_ssm': 'grad_w', 'grad_w_proj_pool': 'grad_w', 'grad_w_proj_sb': 'grad_w', 'grad_w_out': 'grad_w', 'grad_final_norm_w': 'grad_w', 'delta_norm_w': 'delta_w', 'delta_w_in': 'delta_w', 'delta_conv_w': 'delta_w', 'delta_conv_b': 'delta_w', 'delta_dt_bias': 'delta_w', 'delta_a_log': 'delta_w', 'delta_d_skip': 'delta_w', 'delta_ssm_norm_w': 'delta_w', 'delta_pool_w': 'delta_w', 'delta_pool_scale': 'delta_w', 'delta_w_proj_ssm': 'delta_w', 'delta_w_proj_pool': 'delta_w', 'delta_w_proj_sb': 'delta_w', 'delta_w_out': 'delta_w', 'delta_final_norm_w': 'delta_w', 'new_m_norm_w': 'new_m', 'new_m_w_in': 'new_m', 'new_m_conv_w': 'new_m', 'new_m_conv_b': 'new_m', 'new_m_dt_bias': 'new_m', 'new_m_a_log': 'new_m', 'new_m_d_skip': 'new_m', 'new_m_ssm_norm_w': 'new_m', 'new_m_pool_w': 'new_m', 'new_m_pool_scale': 'new_m', 'new_m_w_proj_ssm': 'new_m', 'new_m_w_proj_pool': 'new_m', 'new_m_w_proj_sb': 'new_m', 'new_m_w_out': 'new_m', 'new_m_final_norm_w': 'new_m', 'new_v_norm_w': 'new_v', 'new_v_w_in': 'new_v', 'new_v_conv_w': 'new_v', 'new_v_conv_b': 'new_v', 'new_v_dt_bias': 'new_v', 'new_v_a_log': 'new_v', 'new_v_d_skip': 'new_v', 'new_v_ssm_norm_w': 'new_v', 'new_v_pool_w': 'new_v', 'new_v_pool_scale': 'new_v', 'new_v_w_proj_ssm': 'new_v', 'new_v_w_proj_pool': 'new_v', 'new_v_w_proj_sb': 'new_v', 'new_v_w_out': 'new_v', 'new_v_final_norm_w': 'new_v'}


def _forward(args):
    return _fwd_reference(*[args[k] for k in FWD_PARAMS])


def _output_shape():
    out = _jax.eval_shape(lambda: _forward(_fwd_setup_inputs(0)))
    return out.shape, out.dtype

N_MICROBATCH = 1
ADAM_LR = 0.001
ADAM_B1 = 0.9
ADAM_B2 = 0.999
ADAM_EPS = 1e-08
ADAM_WD = 0.01
ADAM_STEP = 10
PER_EXAMPLE_BATCH_AXIS = {'x': 0, 'loss_target': 0}
SHARED_INPUTS = []
_WEIGHT_DTYPES = {'norm_w': _jnp.float32, 'w_in': _jnp.float32, 'conv_w': _jnp.float32, 'conv_b': _jnp.float32, 'dt_bias': _jnp.float32, 'a_log': _jnp.float32, 'd_skip': _jnp.float32, 'ssm_norm_w': _jnp.float32, 'pool_w': _jnp.float32, 'pool_scale': _jnp.float32, 'w_proj_ssm': _jnp.float32, 'w_proj_pool': _jnp.float32, 'w_proj_sb': _jnp.float32, 'w_out': _jnp.float32, 'final_norm_w': _jnp.float32}
MOMENT_SCALE = {'norm_w': 1.010947e-01, 'w_in': 2.703479e-02, 'conv_w': 3.649873e-02, 'conv_b': 4.921737e-02, 'dt_bias': 1.055185e-01, 'a_log': 1.280666e-01, 'd_skip': 2.111296e-01, 'ssm_norm_w': 3.860049e-02, 'pool_w': 2.788549e-02, 'pool_scale': 2.814841e-02, 'w_proj_ssm': 5.368403e-02, 'w_proj_pool': 2.790154e-02, 'w_proj_sb': 2.067121e-02, 'w_out': 1.104367e-01, 'final_norm_w': 3.199599e+01}


def _to_microbatches(a, axis):
    t = _jnp.moveaxis(a, axis, 0)
    t = t.reshape((N_MICROBATCH, t.shape[0] // N_MICROBATCH) + t.shape[1:])
    return _jnp.moveaxis(t, 1, axis + 1)


def setup_inputs(seed: int = 0) -> dict:
    inp = _fwd_setup_inputs(seed)
    key = _jax.random.fold_in(_jax.random.key(seed), 7919)
    shape, _ = _output_shape()
    out = dict(inp)
    out["loss_target"] = _jax.random.normal(_jax.random.fold_in(key, 0), shape, _jnp.float32)
    for i, name in enumerate(TWIN_WEIGHTS):
        w = inp[name].astype(_jnp.float32)
        if MOMENT_SCALE is None:
            s = _jnp.sqrt(_jnp.mean(_jnp.square(w)) + 1e-30)
        else:
            s = MOMENT_SCALE[name]
        km, kv = _jax.random.split(_jax.random.fold_in(key, i + 1))
        out[name] = w
        out["m_" + name] = s * _jax.random.normal(km, w.shape, _jnp.float32)
        out["v_" + name] = (s * s) * _jax.random.uniform(kv, w.shape, _jnp.float32, 0.5, 1.5)
    if N_MICROBATCH > 1:
        for name, axis in PER_EXAMPLE_BATCH_AXIS.items():
            out[name] = _to_microbatches(out[name], axis)
    return {'x': out['x'], 'norm_w': out['norm_w'], 'w_in': out['w_in'], 'conv_w': out['conv_w'], 'conv_b': out['conv_b'], 'dt_bias': out['dt_bias'], 'a_log': out['a_log'], 'd_skip': out['d_skip'], 'ssm_norm_w': out['ssm_norm_w'], 'pool_w': out['pool_w'], 'pool_scale': out['pool_scale'], 'w_proj_ssm': out['w_proj_ssm'], 'w_proj_pool': out['w_proj_pool'], 'w_proj_sb': out['w_proj_sb'], 'w_out': out['w_out'], 'final_norm_w': out['final_norm_w'], 'loss_target': out['loss_target'], 'm_norm_w': out['m_norm_w'], 'm_w_in': out['m_w_in'], 'm_conv_w': out['m_conv_w'], 'm_conv_b': out['m_conv_b'], 'm_dt_bias': out['m_dt_bias'], 'm_a_log': out['m_a_log'], 'm_d_skip': out['m_d_skip'], 'm_ssm_norm_w': out['m_ssm_norm_w'], 'm_pool_w': out['m_pool_w'], 'm_pool_scale': out['m_pool_scale'], 'm_w_proj_ssm': out['m_w_proj_ssm'], 'm_w_proj_pool': out['m_w_proj_pool'], 'm_w_proj_sb': out['m_w_proj_sb'], 'm_w_out': out['m_w_out'], 'm_final_norm_w': out['m_final_norm_w'], 'v_norm_w': out['v_norm_w'], 'v_w_in': out['v_w_in'], 'v_conv_w': out['v_conv_w'], 'v_conv_b': out['v_conv_b'], 'v_dt_bias': out['v_dt_bias'], 'v_a_log': out['v_a_log'], 'v_d_skip': out['v_d_skip'], 'v_ssm_norm_w': out['v_ssm_norm_w'], 'v_pool_w': out['v_pool_w'], 'v_pool_scale': out['v_pool_scale'], 'v_w_proj_ssm': out['v_w_proj_ssm'], 'v_w_proj_pool': out['v_w_proj_pool'], 'v_w_proj_sb': out['v_w_proj_sb'], 'v_w_out': out['v_w_out'], 'v_final_norm_w': out['v_final_norm_w']}


def _loss(weights, diff, rest, loss_target):
    with _jax.named_scope("forward"):
        args = {**rest, TWIN_DIFF_INPUT: diff, **{k: w.astype(_WEIGHT_DTYPES[k]) for k, w in weights.items()}}
        y = _forward(args)
    with _jax.named_scope("loss_head"):
        err = _jnp.square(y.astype(_jnp.float32) - loss_target)
        return 0.5 * _jnp.sum(_jnp.mean(err, axis=-1)) if err.ndim else 0.5 * err


def _adamw(w, g, m, v):
    m = ADAM_B1 * m + (1.0 - ADAM_B1) * g
    v = ADAM_B2 * v + (1.0 - ADAM_B2) * _jnp.square(g)
    m_hat = m / (1.0 - ADAM_B1 ** ADAM_STEP)
    v_hat = v / (1.0 - ADAM_B2 ** ADAM_STEP)
    delta = -ADAM_LR * (m_hat / (_jnp.sqrt(v_hat) + ADAM_EPS) + ADAM_WD * w)
    return delta, m, v


def reference(x, norm_w, w_in, conv_w, conv_b, dt_bias, a_log, d_skip, ssm_norm_w, pool_w, pool_scale, w_proj_ssm, w_proj_pool, w_proj_sb, w_out, final_norm_w, loss_target, m_norm_w, m_w_in, m_conv_w, m_conv_b, m_dt_bias, m_a_log, m_d_skip, m_ssm_norm_w, m_pool_w, m_pool_scale, m_w_proj_ssm, m_w_proj_pool, m_w_proj_sb, m_w_out, m_final_norm_w, v_norm_w, v_w_in, v_conv_w, v_conv_b, v_dt_bias, v_a_log, v_d_skip, v_ssm_norm_w, v_pool_w, v_pool_scale, v_w_proj_ssm, v_w_proj_pool, v_w_proj_sb, v_w_out, v_final_norm_w):
    given = dict(x=x, norm_w=norm_w, w_in=w_in, conv_w=conv_w, conv_b=conv_b, dt_bias=dt_bias, a_log=a_log, d_skip=d_skip, ssm_norm_w=ssm_norm_w, pool_w=pool_w, pool_scale=pool_scale, w_proj_ssm=w_proj_ssm, w_proj_pool=w_proj_pool, w_proj_sb=w_proj_sb, w_out=w_out, final_norm_w=final_norm_w, loss_target=loss_target, m_norm_w=m_norm_w, m_w_in=m_w_in, m_conv_w=m_conv_w, m_conv_b=m_conv_b, m_dt_bias=m_dt_bias, m_a_log=m_a_log, m_d_skip=m_d_skip, m_ssm_norm_w=m_ssm_norm_w, m_pool_w=m_pool_w, m_pool_scale=m_pool_scale, m_w_proj_ssm=m_w_proj_ssm, m_w_proj_pool=m_w_proj_pool, m_w_proj_sb=m_w_proj_sb, m_w_out=m_w_out, m_final_norm_w=m_final_norm_w, v_norm_w=v_norm_w, v_w_in=v_w_in, v_conv_w=v_conv_w, v_conv_b=v_conv_b, v_dt_bias=v_dt_bias, v_a_log=v_a_log, v_d_skip=v_d_skip, v_ssm_norm_w=v_ssm_norm_w, v_pool_w=v_pool_w, v_pool_scale=v_pool_scale, v_w_proj_ssm=v_w_proj_ssm, v_w_proj_pool=v_w_proj_pool, v_w_proj_sb=v_w_proj_sb, v_w_out=v_w_out, v_final_norm_w=v_final_norm_w)
    weights = {n: given[n] for n in TWIN_WEIGHTS}
    shared = {n: given[n] for n in SHARED_INPUTS}
    per_example = {n: given[n] for n in ['x']}
    grad_fn = _jax.value_and_grad(_loss, argnums=(0, 1))

    def one_microbatch(ex, loss_target):
        ex = dict(ex)
        diff = ex.pop(TWIN_DIFF_INPUT)
        return grad_fn(weights, diff, {**shared, **ex}, loss_target)

    if N_MICROBATCH == 1:
        loss, (grad_w, grad_x) = one_microbatch(per_example, given["loss_target"])
    else:
        def body(carry, xs):
            loss_sum, grad_sum = carry
            l_k, (gw_k, gx_k) = one_microbatch(xs[0], xs[1])
            with _jax.named_scope("update"):
                return (loss_sum + l_k, _jax.tree.map(_jnp.add, grad_sum, gw_k)), gx_k

        init = (_jnp.zeros((), _jnp.float32), _jax.tree.map(_jnp.zeros_like, weights))
        (loss, grad_w), grad_x = _jax.lax.scan(body, init, (per_example, given["loss_target"]))
    with _jax.named_scope("update"):
        delta_w, new_m, new_v = {}, {}, {}
        for n in TWIN_WEIGHTS:
            delta_w[n], new_m[n], new_v[n] = _adamw(weights[n], grad_w[n], given["m_" + n], given["v_" + n])
    return (loss, grad_x, *[grad_w[n] for n in TWIN_WEIGHTS], *[delta_w[n] for n in TWIN_WEIGHTS],
            *[new_m[n] for n in TWIN_WEIGHTS], *[new_v[n] for n in TWIN_WEIGHTS])
```

```python
import functools
import math

import jax
import jax.numpy as jnp
from jax import lax
from jax.experimental import pallas as pl
from jax.experimental.pallas import tpu as pltpu

f32 = jnp.float32
bf16 = jnp.bfloat16
MXU_DTYPE = jnp.bfloat16
ACT_DTYPE = jnp.bfloat16
WIRE_DTYPE = jnp.bfloat16

EPS = 1e-6
LANES = 128
HEAD = 64
SSM_GROUPS = 2
CONV_WIDTH = 4
POOL_GROUPS = 4
DT_PAD = 512
N_BRANCHES = 3
VMEM_LIMIT = 56 * 1024 * 1024

ADAM_LR, ADAM_B1, ADAM_B2, ADAM_EPS, ADAM_WD, ADAM_STEP = 0.001, 0.9, 0.999, 1e-08, 0.01, 10
MESH = pl.DeviceIdType.MESH


def _cp(**kw):
    return pltpu.CompilerParams(vmem_limit_bytes=VMEM_LIMIT, **kw)


def _pick(n, prefs):
    for p in prefs:
        if n % p == 0:
            return p
    return n


def _dg(a, b, ca, cb):
    return lax.dot_general(a, b, (((ca,), (cb,)), ((), ())), preferred_element_type=f32)


def _nn(a, b):
    return _dg(a, b, 1, 0)


def _nt(a, b):
    return _dg(a, b, 1, 1)


def _tn(a, b):
    return _dg(a, b, 0, 0)


def _mx(a):
    return a.astype(MXU_DTYPE)


def _split3(a):
    hi = a.astype(bf16)
    r = a - hi.astype(f32)
    mid = r.astype(bf16)
    lo = (r - mid.astype(f32)).astype(bf16)
    return hi, mid, lo


def _exact_nn(a, u):
    hi, mid, lo = _split3(a)
    return _nn(hi, u) + _nn(mid, u) + _nn(lo, u)


def _iota2(shape, dim):
    return lax.broadcasted_iota(jnp.int32, shape, dim)


def _sigmoid(x):
    return 1.0 / (1.0 + jnp.exp(-x))


def _shift_down(v, sh, row):
    return jnp.where(row >= sh, pltpu.roll(v, sh, 0), 0.0)


def _shift_up(v, sh, row):
    n = v.shape[0]
    return jnp.where(row < n - sh, pltpu.roll(v, n - sh, 0), 0.0)


class Dims:
    def __init__(self, bl, s, d):
        self.Bl, self.S, self.D = bl, s, d
        self.T = bl * s
        self.W2 = 2 * d
        self.H = self.W2 // HEAD
        self.hpg = self.H // SSM_GROUPS
        self.CC = self.W2 + 2 * SSM_GROUPS * LANES
        self.Dg = d // POOL_GROUPS
        self.nc = s // LANES
        self.o_dt = self.W2 + self.CC
        self.IN_COLS = 13 * d + 2 * SSM_GROUPS * LANES + self.H
        self.c_z = 0
        self.c_xbc = self.W2
        self.c_pu = self.W2 + self.CC
        self.c_pg = self.c_pu + d
        self.c_qkv = self.c_pg + d
        self.c_sbg = self.c_qkv + 3 * d
        self.c_mrg = self.c_sbg + d
        self.c_dt = self.c_mrg + 3 * d
        self.NP = self.c_dt + DT_PAD
        assert self.c_dt == self.IN_COLS - self.H
        assert s % LANES == 0 and d % 512 == 0 and self.H <= LANES


def _permute_cols(w, dm):
    pad = jnp.zeros(w.shape[:-1] + (DT_PAD - dm.H,), w.dtype)
    return jnp.concatenate([w[..., :dm.o_dt], w[..., dm.o_dt + dm.H:], w[..., dm.o_dt:dm.o_dt + dm.H], pad], axis=-1)


def _unpermute_cols(w, dm):
    return jnp.concatenate([w[..., :dm.o_dt], w[..., dm.c_dt:dm.c_dt + dm.H], w[..., dm.o_dt:dm.c_dt]], axis=-1)


def _mm(a, b, *, ta=False, tb=False, out_dtype=f32, res=None, name):
    M, K = (a.shape[1], a.shape[0]) if ta else a.shape
    N = b.shape[0] if tb else b.shape[1]
    tm = _pick(M, (512, 256, 128))
    tn = _pick(N, (1024, 512, 256, 128))
    tk = _pick(K, (1024, 512, 256, 128))
    nk = K // tk

    def body(*refs):
        if res is None:
            a_ref, b_ref, o_ref, acc = refs
        else:
            a_ref, b_ref, r_ref, o_ref, acc = refs
        k = pl.program_id(2)

        @pl.when(k == 0)
        def _():
            acc[...] = jnp.zeros_like(acc)

        acc[...] += _dg(_mx(a_ref[...]), _mx(b_ref[...]), 0 if ta else 1, 1 if tb else 0)

        @pl.when(k == nk - 1)
        def _():
            v = acc[...]
            if res is not None:
                v = v + r_ref[...]
            o_ref[...] = v.astype(out_dtype)

    a_spec = pl.BlockSpec((tk, tm), lambda i, j, k: (k, i)) if ta else pl.BlockSpec((tm, tk), lambda i, j, k: (i, k))
    b_spec = pl.BlockSpec((tn, tk), lambda i, j, k: (j, k)) if tb else pl.BlockSpec((tk, tn), lambda i, j, k: (k, j))
    o_spec = pl.BlockSpec((tm, tn), lambda i, j, k: (i, j))
    in_specs = [a_spec, b_spec] + ([o_spec] if res is not None else [])
    args = (a, b) + ((res,) if res is not None else ())
    return pl.pallas_call(
        body, name=name, grid=(M // tm, N // tn, nk), in_specs=in_specs, out_specs=o_spec,
        out_shape=jax.ShapeDtypeStruct((M, N), out_dtype), scratch_shapes=[pltpu.VMEM((tm, tn), f32)],
        compiler_params=_cp(dimension_semantics=("parallel", "parallel", "arbitrary")),
    )(*args)


def _rms_fwd(x, w, dm):
    tr = _pick(dm.T, (256, 128))

    def body(x_ref, w_ref, o_ref):
        xf = x_ref[...]
        r = lax.rsqrt(jnp.mean(xf * xf, axis=-1, keepdims=True) + EPS)
        o_ref[...] = (xf * r * w_ref[...]).astype(o_ref.dtype)

    return pl.pallas_call(
        body, name="rms_fwd", grid=(dm.T // tr,),
        in_specs=[pl.BlockSpec((tr, dm.D), lambda i: (i, 0)), pl.BlockSpec((1, dm.D), lambda i: (0, 0))],
        out_specs=pl.BlockSpec((tr, dm.D), lambda i: (i, 0)),
        out_shape=jax.ShapeDtypeStruct((dm.T, dm.D), ACT_DTYPE), compiler_params=_cp(),
    )(x, w.reshape(1, dm.D))


def _rms_bwd(x, dh, dres, w, dm):
    tr = _pick(dm.T, (256, 128))

    def body(x_ref, dh_ref, dr_ref, w_ref, dx_ref, dw_ref):
        @pl.when(pl.program_id(0) == 0)
        def _():
            dw_ref[...] = jnp.zeros_like(dw_ref)

        xf = x_ref[...]
        r = lax.rsqrt(jnp.mean(xf * xf, axis=-1, keepdims=True) + EPS)
        xh = xf * r
        dh_ = dh_ref[...]
        dxh = dh_ * w_ref[...]
        dx_ref[...] = dr_ref[...] + r * (dxh - xh * jnp.mean(dxh * xh, axis=-1, keepdims=True))
        dw_ref[...] += jnp.sum(dh_ * xh, axis=0, keepdims=True)

    row = pl.BlockSpec((tr, dm.D), lambda i: (i, 0))
    vec = pl.BlockSpec((1, dm.D), lambda i: (0, 0))
    return pl.pallas_call(
        body, name="rms_bwd", grid=(dm.T // tr,), in_specs=[row, row, row, vec], out_specs=[row, vec],
        out_shape=[jax.ShapeDtypeStruct((dm.T, dm.D), f32), jax.ShapeDtypeStruct((1, dm.D), f32)],
        compiler_params=_cp(dimension_semantics=("arbitrary",)),
    )(x, dh, dres, w.reshape(1, dm.D))


def _loss_head(x, tgt, w, dm):
    tr = _pick(dm.T, (256, 128))

    def body(x_ref, t_ref, w_ref, dx_ref, dw_ref, ls_ref):
        @pl.when(pl.program_id(0) == 0)
        def _():
            dw_ref[...] = jnp.zeros_like(dw_ref)
            ls_ref[...] = jnp.zeros_like(ls_ref)

        xf = x_ref[...]
        r = lax.rsqrt(jnp.mean(xf * xf, axis=-1, keepdims=True) + EPS)
        xh = xf * r
        err = xh * w_ref[...] - t_ref[...]
        per_tok = jnp.mean(err * err, axis=-1, keepdims=True)
        ls_ref[...] += 0.5 * jnp.sum(per_tok, axis=0, keepdims=True)
        dy = err * (1.0 / dm.D)
        dxh = dy * w_ref[...]
        dx_ref[...] = r * (dxh - xh * jnp.mean(dxh * xh, axis=-1, keepdims=True))
        dw_ref[...] += jnp.sum(dy * xh, axis=0, keepdims=True)

    row = pl.BlockSpec((tr, dm.D), lambda i: (i, 0))
    vec = pl.BlockSpec((1, dm.D), lambda i: (0, 0))
    return pl.pallas_call(
        body, name="loss_head", grid=(dm.T // tr,), in_specs=[row, row, vec],
        out_specs=[row, vec, pl.BlockSpec((1, LANES), lambda i: (0, 0))],
        out_shape=[jax.ShapeDtypeStruct((dm.T, dm.D), f32), jax.ShapeDtypeStruct((1, dm.D), f32),
                   jax.ShapeDtypeStruct((1, LANES), f32)],
        compiler_params=_cp(dimension_semantics=("arbitrary",)),
    )(x, tgt, w.reshape(1, dm.D))


def _conv_pre(u, w_ref, b_ref, row):
    acc = b_ref[...] + w_ref[CONV_WIDTH - 1:CONV_WIDTH, :] * u
    for k in range(CONV_WIDTH - 1):
        acc = acc + w_ref[k:k + 1, :] * _shift_down(u, CONV_WIDTH - 1 - k, row)
    return acc


def _conv_fwd(proj, cw, cb, dm):
    cwid = LANES
    off = dm.c_xbc // cwid

    def body(u_ref, w_ref, b_ref, o_ref):
        u = u_ref[...]
        row = _iota2(u.shape, 0)
        pre = _conv_pre(u, w_ref, b_ref, row)
        o_ref[...] = pre * _sigmoid(pre)

    return pl.pallas_call(
        body, name="conv_fwd", grid=(dm.Bl, dm.CC // cwid),
        in_specs=[pl.BlockSpec((dm.S, cwid), lambda b, j: (b, off + j)),
                  pl.BlockSpec((CONV_WIDTH, cwid), lambda b, j: (0, j)), pl.BlockSpec((1, cwid), lambda b, j: (0, j))],
        out_specs=pl.BlockSpec((dm.S, cwid), lambda b, j: (b, j)),
        out_shape=jax.ShapeDtypeStruct((dm.T, dm.CC), f32), compiler_params=_cp(),
    )(proj, cw, cb.reshape(1, dm.CC))


def _conv_bwd(proj, d_out, cw, cb, dm):
    cwid = LANES
    off = dm.c_xbc // cwid

    def body(u_ref, d_ref, w_ref, b_ref, du_ref, dw_ref, db_ref):
        @pl.when(pl.program_id(1) == 0)
        def _():
            dw_ref[...] = jnp.zeros_like(dw_ref)
            db_ref[...] = jnp.zeros_like(db_ref)

        u = u_ref[...]
        row = _iota2(u.shape, 0)
        pre = _conv_pre(u, w_ref, b_ref, row)
        sg = _sigmoid(pre)
        dpre = d_ref[...] * (sg * (1.0 + pre * (1.0 - sg)))
        du = w_ref[CONV_WIDTH - 1:CONV_WIDTH, :] * dpre
        dw_ref[CONV_WIDTH - 1:CONV_WIDTH, :] += jnp.sum(dpre * u, axis=0, keepdims=True)
        for k in range(CONV_WIDTH - 1):
            sh = CONV_WIDTH - 1 - k
            du = du + w_ref[k:k + 1, :] * _shift_up(dpre, sh, row)
            dw_ref[k:k + 1, :] += jnp.sum(dpre * _shift_down(u, sh, row), axis=0, keepdims=True)
        du_ref[...] = du.astype(du_ref.dtype)
        db_ref[...] += jnp.sum(dpre, axis=0, keepdims=True)

    return pl.pallas_call(
        body, name="conv_bwd", grid=(dm.CC // cwid, dm.Bl),
        in_specs=[pl.BlockSpec((dm.S, cwid), lambda j, b: (b, off + j)), pl.BlockSpec((dm.S, cwid), lambda j, b: (b, j)),
                  pl.BlockSpec((CONV_WIDTH, cwid), lambda j, b: (0, j)), pl.BlockSpec((1, cwid), lambda j, b: (0, j))],
        out_specs=[pl.BlockSpec((dm.S, cwid), lambda j, b: (b, j)), pl.BlockSpec((CONV_WIDTH, cwid), lambda j, b: (0, j)),
                   pl.BlockSpec((1, cwid), lambda j, b: (0, j))],
        out_shape=[jax.ShapeDtypeStruct((dm.T, dm.CC), ACT_DTYPE), jax.ShapeDtypeStruct((CONV_WIDTH, dm.CC), f32),
                   jax.ShapeDtypeStruct((1, dm.CC), f32)],
        compiler_params=_cp(dimension_semantics=("arbitrary", "arbitrary")),
    )(proj, d_out, cw, cb.reshape(1, dm.CC))


def _pad_lanes(v):
    return jnp.pad(v, (0, LANES - v.shape[0])).reshape(1, LANES)


def _softplus(x):
    return jnp.maximum(x, 0.0) + jnp.log(1.0 + jnp.exp(-jnp.abs(x)))


def _dt_prep(proj, dt_bias, a_log, dm):
    off = dm.c_dt // LANES

    def body(r_ref, b_ref, al_ref, dt_ref, cum_ref, cumt_ref):
        dt = _softplus(r_ref[...] + b_ref[...])
        adt = dt * (-jnp.exp(al_ref[...]))
        tril = (_iota2((LANES, LANES), 1) <= _iota2((LANES, LANES), 0)).astype(bf16)
        cum = _exact_nn_left(tril, adt)
        dt_ref[...] = dt
        cum_ref[...] = cum
        cumt_ref[...] = cum.T

    blk = pl.BlockSpec((LANES, LANES), lambda i: (i, 0))
    vec = pl.BlockSpec((1, LANES), lambda i: (0, 0))
    return pl.pallas_call(
        body, name="dt_prep", grid=(dm.T // LANES,),
        in_specs=[pl.BlockSpec((LANES, LANES), lambda i: (i, off)), vec, vec],
        out_specs=[blk, blk, pl.BlockSpec((LANES, LANES), lambda i: (0, i))],
        out_shape=[jax.ShapeDtypeStruct((dm.T, LANES), f32), jax.ShapeDtypeStruct((dm.T, LANES), f32),
                   jax.ShapeDtypeStruct((LANES, dm.T), f32)],
        compiler_params=_cp(),
    )(proj, _pad_lanes(dt_bias), _pad_lanes(a_log))


def _exact_nn_left(u, a):
    hi, mid, lo = _split3(a)
    return _nn(u, hi) + _nn(u, mid) + _nn(u, lo)


def _dt_bwd(proj, dt, dcum, dcum_t, ddt, dt_bias, a_log, dm):
    off = dm.c_dt // LANES

    def body(r_ref, dt_ref, dc_ref, dct_ref, dd_ref, b_ref, al_ref, o_ref, db_ref, da_ref):
        @pl.when(pl.program_id(0) == 0)
        def _():
            db_ref[...] = jnp.zeros_like(db_ref)
            da_ref[...] = jnp.zeros_like(da_ref)

        a = -jnp.exp(al_ref[...])
        triu = (_iota2((LANES, LANES), 1) >= _iota2((LANES, LANES), 0)).astype(bf16)
        dadt = _exact_nn_left(triu, dc_ref[...] + dct_ref[...].T)
        d_dt = dd_ref[...] + dadt * a
        d_raw = d_dt * _sigmoid(r_ref[...] + b_ref[...])
        o_ref[...] = d_raw.astype(o_ref.dtype)
        db_ref[...] += jnp.sum(d_raw, axis=0, keepdims=True)
        da_ref[...] += jnp.sum(dadt * dt_ref[...], axis=0, keepdims=True) * a

    blk = pl.BlockSpec((LANES, LANES), lambda i: (i, 0))
    vec = pl.BlockSpec((1, LANES), lambda i: (0, 0))
    return pl.pallas_call(
        body, name="dt_bwd", grid=(dm.T // LANES,),
        in_specs=[pl.BlockSpec((LANES, LANES), lambda i: (i, off)), blk, blk, pl.BlockSpec((LANES, LANES), lambda i: (0, i)),
                  blk, vec, vec],
        out_specs=[blk, vec, vec],
        out_shape=[jax.ShapeDtypeStruct((dm.T, LANES), ACT_DTYPE), jax.ShapeDtypeStruct((1, LANES), f32),
                   jax.ShapeDtypeStruct((1, LANES), f32)],
        compiler_params=_cp(dimension_semantics=("arbitrary",)),
    )(proj, dt, dcum, dcum_t, ddt, _pad_lanes(dt_bias), _pad_lanes(a_log))


def _ssd_common(dm):
    L = LANES
    tri = _iota2((L, L), 0) >= _iota2((L, L), 1)
    lo = _iota2((L, L), 1) < HEAD
    return tri, lo


def _ssd_fwd(xbc, dt, cum, cumt, d_skip, dm):
    L, W2, hpg = LANES, dm.W2, dm.hpg
    nb = W2 // L

    def body(x_ref, b_ref, c_ref, dt_ref, cum_ref, cumt_ref, dsk_ref, y_ref, hs_ref, h_scr):
        @pl.when(pl.program_id(1) == 0)
        def _():
            h_scr[...] = jnp.zeros_like(h_scr)

        hs_ref[0] = h_scr[...]
        tri, lo = _ssd_common(dm)
        for g in range(SSM_GROUPS):
            bb = _mx(b_ref[:, g * L:(g + 1) * L])
            cb_ = _mx(c_ref[:, g * L:(g + 1) * L])
            cbm = _nt(cb_, bb)
            for i in range(hpg // 2):
                h0 = g * hpg + 2 * i
                h1 = h0 + 1
                sl = slice(h0 * HEAD, h0 * HEAD + L)
                x_p = x_ref[:, sl]
                cum0, cum1 = cum_ref[:, h0:h0 + 1], cum_ref[:, h1:h1 + 1]
                cums = jnp.where(lo, cum0, cum1)
                xdt = x_p * jnp.where(lo, dt_ref[:, h0:h0 + 1], dt_ref[:, h1:h1 + 1])
                tot = jnp.where(lo[0:1], cum_ref[L - 1:L, h0:h0 + 1], cum_ref[L - 1:L, h1:h1 + 1])
                y_p = jnp.zeros((L, L), f32)
                for hh, m in ((h0, lo), (h1, jnp.logical_not(lo))):
                    diff = cum_ref[:, hh:hh + 1] - cumt_ref[hh:hh + 1, :]
                    lm = jnp.where(tri, jnp.exp(jnp.minimum(diff, 0.0)), 0.0)
                    y_p = y_p + _nn(_mx(cbm * lm), _mx(jnp.where(m, xdt, 0.0)))
                hp = h_scr[:, sl]
                y_p = y_p + _nn(cb_, _mx(hp)) * jnp.exp(cums)
                y_p = y_p + x_p * jnp.where(lo[0:1], dsk_ref[0:1, h0:h0 + 1], dsk_ref[0:1, h1:h1 + 1])
                y_ref[:, sl] = y_p
                h_scr[:, sl] = hp * jnp.exp(tot) + _tn(bb, _mx(xdt * jnp.exp(tot - cums)))

    nc = dm.nc
    ob = W2 // (SSM_GROUPS * L)
    blk = pl.BlockSpec((L, L), lambda b, c: (b * nc + c, 0))
    return pl.pallas_call(
        body, name="ssd_fwd", grid=(dm.Bl, nc),
        in_specs=[pl.BlockSpec((L, W2), lambda b, c: (b * nc + c, 0)),
                  pl.BlockSpec((L, SSM_GROUPS * L), lambda b, c: (b * nc + c, ob)),
                  pl.BlockSpec((L, SSM_GROUPS * L), lambda b, c: (b * nc + c, ob + 1)),
                  blk, blk, pl.BlockSpec((L, L), lambda b, c: (0, b * nc + c)), pl.BlockSpec((1, L), lambda b, c: (0, 0))],
        out_specs=[pl.BlockSpec((L, W2), lambda b, c: (b * nc + c, 0)), pl.BlockSpec((1, L, W2), lambda b, c: (b * nc + c, 0, 0))],
        out_shape=[jax.ShapeDtypeStruct((dm.T, W2), f32), jax.ShapeDtypeStruct((dm.Bl * nc, L, W2), f32)],
        scratch_shapes=[pltpu.VMEM((L, W2), f32)],
        compiler_params=_cp(dimension_semantics=("arbitrary", "arbitrary")),
    )(xbc, xbc, xbc, dt, cum, cumt, _pad_lanes(d_skip))


def _ssd_bwd(xbc, dt, cum, cumt, d_skip, hs, dy, dm):
    L, W2, hpg = LANES, dm.W2, dm.hpg
    nc = dm.nc

    def body(x_ref, b_ref, c_ref, dt_ref, cum_ref, cumt_ref, dsk_ref, hs_ref, dy_ref,
             dx_ref, db_ref, dc_ref, ddt_ref, dcum_ref, dcr_ref, dd_ref, dh_scr, lane_cum, lane_dt, lane_d):
        @pl.when(pl.program_id(1) == 0)
        def _():
            dh_scr[...] = jnp.zeros_like(dh_scr)

        @pl.when((pl.program_id(0) == 0) & (pl.program_id(1) == 0))
        def _():
            dd_ref[...] = jnp.zeros_like(dd_ref)

        dcr_ref[...] = jnp.zeros_like(dcr_ref)
        tri, lo = _ssd_common(dm)
        last = _iota2((L, L), 0) == L - 1
        for g in range(SSM_GROUPS):
            gs = slice(g * L, (g + 1) * L)
            bb = _mx(b_ref[:, gs])
            cb_ = _mx(c_ref[:, gs])
            cbm = _nt(cb_, bb)
            dcb = jnp.zeros((L, L), f32)
            dc_g = jnp.zeros((L, L), f32)
            db_g = jnp.zeros((L, L), f32)
            for i in range(hpg // 2):
                h0 = g * hpg + 2 * i
                h1 = h0 + 1
                sl = slice(h0 * HEAD, h0 * HEAD + L)
                x_p = x_ref[:, sl]
                dy_p = dy_ref[:, sl]
                dt_p = jnp.where(lo, dt_ref[:, h0:h0 + 1], dt_ref[:, h1:h1 + 1])
                cums = jnp.where(lo, cum_ref[:, h0:h0 + 1], cum_ref[:, h1:h1 + 1])
                tot = jnp.where(lo[0:1], cum_ref[L - 1:L, h0:h0 + 1], cum_ref[L - 1:L, h1:h1 + 1])
                dsk_p = jnp.where(lo[0:1], dsk_ref[0:1, h0:h0 + 1], dsk_ref[0:1, h1:h1 + 1])
                xdt = x_p * dt_p
                ecum = jnp.exp(cums)
                dec = jnp.exp(tot - cums)
                etot = jnp.exp(tot)
                hp = hs_ref[0, :, sl]
                hp_b = _mx(hp)
                dhn = dh_scr[:, sl]
                dhn_b = _mx(dhn)
                y_off = _nn(cb_, hp_b) * ecum
                dch = _mx(dy_p * ecum)
                dc_g = dc_g + _nt(dch, hp_b)
                dh_off = _tn(cb_, dch)
                bds = _nn(bb, dhn_b)
                xdec = xdt * dec
                db_g = db_g + _nt(_mx(xdec), dhn_b)
                dxdt = bds * dec
                sdec = bds * xdec
                tot_lane = jnp.sum(sdec, axis=0, keepdims=True) + jnp.sum(dhn * hp, axis=0, keepdims=True) * etot
                dh_scr[:, sl] = etot * dhn + dh_off
                rsum = []
                for hh, m in ((h0, lo), (h1, jnp.logical_not(lo))):
                    diff = cum_ref[:, hh:hh + 1] - cumt_ref[hh:hh + 1, :]
                    lm = jnp.where(tri, jnp.exp(jnp.minimum(diff, 0.0)), 0.0)
                    w32 = cbm * lm
                    dyh = _mx(jnp.where(m, dy_p, 0.0))
                    dw = _nt(dyh, _mx(jnp.where(m, xdt, 0.0)))
                    dcb = dcb + dw * lm
                    e = dw * w32
                    rsum.append(jnp.sum(e, axis=1, keepdims=True))
                    dcr_ref[hh:hh + 1, :] = -jnp.sum(e, axis=0, keepdims=True)
                    dxdt = dxdt + _tn(_mx(w32), dyh)
                lane_cum[:, sl] = (dy_p * y_off + jnp.where(lo, rsum[0], rsum[1]) * (1.0 / HEAD) - sdec
                                   + jnp.where(last, tot_lane, 0.0))
                lane_dt[:, sl] = dxdt * x_p
                lane_d[:, sl] = dy_p * x_p
                dx_ref[:, sl] = dxdt * dt_p + dsk_p * dy_p
            dcb_b = _mx(dcb)
            dc_ref[:, gs] = dc_g + _nn(dcb_b, bb)
            db_ref[:, gs] = db_g + _tn(dcb_b, cb_)
        sel = (_iota2((W2, L), 0) // HEAD == _iota2((W2, L), 1)).astype(bf16)
        dcum_ref[...] = _exact_nn(lane_cum[...], sel)
        ddt_ref[...] = _exact_nn(lane_dt[...], sel)
        dd_ref[...] += jnp.sum(_exact_nn(lane_d[...], sel), axis=0, keepdims=True)

    ob = W2 // (SSM_GROUPS * L)

    def rc(b, c):
        return b * nc + (nc - 1 - c)

    blk = pl.BlockSpec((L, L), lambda b, c: (rc(b, c), 0))
    blk_t = pl.BlockSpec((L, L), lambda b, c: (0, rc(b, c)))
    wide = pl.BlockSpec((L, W2), lambda b, c: (rc(b, c), 0))
    grp = pl.BlockSpec((L, SSM_GROUPS * L), lambda b, c: (rc(b, c), 0))
    return pl.pallas_call(
        body, name="ssd_bwd", grid=(dm.Bl, nc),
        in_specs=[wide, pl.BlockSpec((L, SSM_GROUPS * L), lambda b, c: (rc(b, c), ob)),
                  pl.BlockSpec((L, SSM_GROUPS * L), lambda b, c: (rc(b, c), ob + 1)),
                  blk, blk, blk_t, pl.BlockSpec((1, L), lambda b, c: (0, 0)),
                  pl.BlockSpec((1, L, W2), lambda b, c: (rc(b, c), 0, 0)), wide],
        out_specs=[wide, grp, grp, blk, blk, blk_t, pl.BlockSpec((1, L), lambda b, c: (0, 0))],
        out_shape=[jax.ShapeDtypeStruct((dm.T, W2), f32), jax.ShapeDtypeStruct((dm.T, SSM_GROUPS * L), f32),
                   jax.ShapeDtypeStruct((dm.T, SSM_GROUPS * L), f32), jax.ShapeDtypeStruct((dm.T, L), f32),
                   jax.ShapeDtypeStruct((dm.T, L), f32), jax.ShapeDtypeStruct((L, dm.T), f32),
                   jax.ShapeDtypeStruct((1, L), f32)],
        scratch_shapes=[pltpu.VMEM((L, W2), f32)] * 4,
        compiler_params=_cp(dimension_semantics=("arbitrary", "arbitrary")),
    )(xbc, xbc, xbc, dt, cum, cumt, _pad_lanes(d_skip), hs, dy)


def _gnorm_fwd(y, proj, w, dm):
    tr = _pick(dm.T, (256, 128))
    row = pl.BlockSpec((tr, dm.W2), lambda i: (i, 0))

    def body(y_ref, z_ref, w_ref, o_ref):
        z = z_ref[...]
        yg = y_ref[...] * (z * _sigmoid(z))
        r = lax.rsqrt(jnp.mean(yg * yg, axis=-1, keepdims=True) + EPS)
        o_ref[...] = (yg * r * w_ref[...]).astype(o_ref.dtype)

    return pl.pallas_call(
        body, name="gnorm_fwd", grid=(dm.T // tr,), in_specs=[row, row, pl.BlockSpec((1, dm.W2), lambda i: (0, 0))],
        out_specs=row, out_shape=jax.ShapeDtypeStruct((dm.T, dm.W2), ACT_DTYPE), compiler_params=_cp(),
    )(y, proj, w.reshape(1, dm.W2))


def _gnorm_bwd(dn, y, proj, w, dm):
    tr = _pick(dm.T, (256, 128))
    row = pl.BlockSpec((tr, dm.W2), lambda i: (i, 0))
    vec = pl.BlockSpec((1, dm.W2), lambda i: (0, 0))

    def body(dn_ref, y_ref, z_ref, w_ref, dy_ref, dz_ref, dw_ref):
        @pl.when(pl.program_id(0) == 0)
        def _():
            dw_ref[...] = jnp.zeros_like(dw_ref)

        z = z_ref[...]
        sg = _sigmoid(z)
        sz = z * sg
        yv = y_ref[...]
        yg = yv * sz
        r = lax.rsqrt(jnp.mean(yg * yg, axis=-1, keepdims=True) + EPS)
        n = yg * r
        dout = dn_ref[...]
        dnn = dout * w_ref[...]
        dyg = r * (dnn - n * jnp.mean(dnn * n, axis=-1, keepdims=True))
        dy_ref[...] = dyg * sz
        dz_ref[...] = (dyg * yv * (sg * (1.0 + z * (1.0 - sg)))).astype(dz_ref.dtype)
        dw_ref[...] += jnp.sum(dout * n, axis=0, keepdims=True)

    return pl.pallas_call(
        body, name="gnorm_bwd", grid=(dm.T // tr,), in_specs=[row, row, row, vec], out_specs=[row, row, vec],
        out_shape=[jax.ShapeDtypeStruct((dm.T, dm.W2), f32), jax.ShapeDtypeStruct((dm.T, dm.W2), ACT_DTYPE),
                   jax.ShapeDtypeStruct((1, dm.W2), f32)],
        compiler_params=_cp(dimension_semantics=("arbitrary",)),
    )(dn, y, proj, w.reshape(1, dm.W2))


def _pool_mixed(u, g, row):
    s = u
    for k in range(POOL_GROUPS):
        s = jnp.where(k <= g, s + _shift_down(s, 1 << k, row), s)
    w = jnp.left_shift(2, g)
    cnt = jnp.minimum(row + 1, w).astype(f32)
    return s / cnt - u, cnt


def _pool_fwd(proj, pw, scale, dm):
    Dg = dm.Dg
    ou, og = dm.c_pu // Dg, dm.c_pg // Dg

    def body(u_ref, g_ref, w_ref, s_ref, o_ref):
        u = u_ref[...]
        row = _iota2(u.shape, 0)
        mixed, _ = _pool_mixed(u, pl.program_id(1), row)
        lin = _nn(_mx(mixed), _mx(w_ref[0]))
        gt = g_ref[...]
        o_ref[...] = (lin * s_ref[...] * (gt * _sigmoid(gt))).astype(o_ref.dtype)

    return pl.pallas_call(
        body, name="pool_fwd", grid=(dm.Bl, POOL_GROUPS),
        in_specs=[pl.BlockSpec((dm.S, Dg), lambda b, g: (b, ou + g)), pl.BlockSpec((dm.S, Dg), lambda b, g: (b, og + g)),
                  pl.BlockSpec((1, Dg, Dg), lambda b, g: (g, 0, 0)), pl.BlockSpec((1, Dg), lambda b, g: (0, g))],
        out_specs=pl.BlockSpec((dm.S, Dg), lambda b, g: (b, g)),
        out_shape=jax.ShapeDtypeStruct((dm.T, dm.D), ACT_DTYPE), compiler_params=_cp(),
    )(proj, proj, pw, scale.reshape(1, dm.D))


def _pool_bwd(proj, dout, pw, scale, dm):
    Dg = dm.Dg
    ou, og = dm.c_pu // Dg, dm.c_pg // Dg

    def body(u_ref, g_ref, d_ref, w_ref, s_ref, du_ref, dg_ref, dw_ref, ds_ref):
        @pl.when(pl.program_id(1) == 0)
        def _():
            dw_ref[...] = jnp.zeros_like(dw_ref)
            ds_ref[...] = jnp.zeros_like(ds_ref)

        g = pl.program_id(0)
        u = u_ref[...]
        row = _iota2(u.shape, 0)
        mixed, cnt = _pool_mixed(u, g, row)
        mixed_b = _mx(mixed)
        wb = _mx(w_ref[0])
        lin = _nn(mixed_b, wb)
        gt = g_ref[...]
        sg = _sigmoid(gt)
        silu = gt * sg
        d = d_ref[...]
        sc = s_ref[...]
        dlin = d * sc * silu
        ds_ref[...] += jnp.sum(d * lin * silu, axis=0, keepdims=True)
        dg_ref[...] = (d * lin * sc * (sg * (1.0 + gt * (1.0 - sg)))).astype(dg_ref.dtype)
        dlin_b = _mx(dlin)
        dmixed = _nt(dlin_b, wb)
        dw_ref[0] += _tn(mixed_b, dlin_b)
        r = dmixed / cnt
        for k in range(POOL_GROUPS):
            r = jnp.where(k <= g, r + _shift_up(r, 1 << k, row), r)
        du_ref[...] = (r - dmixed).astype(du_ref.dtype)

    return pl.pallas_call(
        body, name="pool_bwd", grid=(POOL_GROUPS, dm.Bl),
        in_specs=[pl.BlockSpec((dm.S, Dg), lambda g, b: (b, ou + g)), pl.BlockSpec((dm.S, Dg), lambda g, b: (b, og + g)),
                  pl.BlockSpec((dm.S, Dg), lambda g, b: (b, g)), pl.BlockSpec((1, Dg, Dg), lambda g, b: (g, 0, 0)),
                  pl.BlockSpec((1, Dg), lambda g, b: (0, g))],
        out_specs=[pl.BlockSpec((dm.S, Dg), lambda g, b: (b, g)), pl.BlockSpec((dm.S, Dg), lambda g, b: (b, g)),
                   pl.BlockSpec((1, Dg, Dg), lambda g, b: (g, 0, 0)), pl.BlockSpec((1, Dg), lambda g, b: (0, g))],
        out_shape=[jax.ShapeDtypeStruct((dm.T, dm.D), ACT_DTYPE), jax.ShapeDtypeStruct((dm.T, dm.D), ACT_DTYPE),
                   jax.ShapeDtypeStruct((POOL_GROUPS, Dg, Dg), f32), jax.ShapeDtypeStruct((1, dm.D), f32)],
        compiler_params=_cp(dimension_semantics=("arbitrary", "arbitrary")),
    )(proj, proj, dout, pw, scale.reshape(1, dm.D))


SB_TILE = 256


def _split2(a):
    hi = a.astype(bf16)
    return hi, (a - hi.astype(f32)).astype(bf16)


def _exact2_nn(a, u):
    hi, lo = _split2(a)
    return _nn(hi, u) + _nn(lo, u)


def _sb_tile(qb, kb, base, u, causal):
    z = _nt(qb, kb)
    lb = jnp.minimum(z, 0.0) - jnp.log(1.0 + jnp.exp(-jnp.abs(z)))
    lm = lb - z
    if causal is not None:
        lm = jnp.where(causal, lm, 0.0)
    att = jnp.exp(lb + base + _exact2_nn(lm, u))
    if causal is not None:
        att = jnp.where(causal, att, 0.0)
    return lb, att, lm


def _sb_fwd(proj, dm):
    L, S, D, TQ = LANES, dm.S, dm.D, SB_TILE
    oq, og = dm.c_qkv // L, dm.c_sbg // L
    nb = D // L
    scale = HEAD ** -0.5

    def body(q_ref, k_ref, v_ref, g_ref, o_ref, lt_ref, out_ref, k_s, va_s, vb_s):
        lo = _iota2((1, L), 1) < HEAD
        hi = jnp.logical_not(lo)
        k_s[...] = _mx(k_ref[...])
        vf = v_ref[...]
        va_s[...] = _mx(jnp.where(lo, vf, 0.0))
        vb_s[...] = _mx(jnp.where(hi, vf, 0.0))
        before = _iota2((TQ, TQ), 0) > _iota2((TQ, TQ), 1)
        ugt = before.astype(bf16)
        v_s = (va_s, vb_s)

        def qloop(qt, _):
            rows = pl.ds(pl.multiple_of(qt * TQ, TQ), TQ)
            qs = q_ref[rows, :] * scale
            qb = [_mx(jnp.where(m, qs, 0.0)) for m in (lo, hi)]

            def tile(kt, carry, causal):
                runs, acc = carry
                krows = pl.ds(pl.multiple_of(kt * TQ, TQ), TQ)
                k_t = k_s[krows, :]
                new_runs = []
                for h in range(2):
                    _, att, lm = _sb_tile(qb[h], k_t, runs[h], ugt, causal)
                    acc = acc + _nn(_mx(att), v_s[h][krows, :])
                    new_runs.append(runs[h] + jnp.sum(lm, axis=1, keepdims=True))
                return tuple(new_runs), acc

            zc = jnp.zeros((TQ, 1), f32)
            carry = tile(qt, ((zc, zc), jnp.zeros((TQ, L), f32)), before)
            runs, acc = lax.fori_loop(0, qt, lambda jj, c: tile(qt - 1 - jj, c, None), carry)
            o_ref[rows, :] = acc
            lt_ref[rows, :] = jnp.where(lo, runs[0], runs[1])
            gt = g_ref[rows, :]
            out_ref[rows, :] = (acc * (gt * _sigmoid(gt))).astype(out_ref.dtype)
            return 0

        lax.fori_loop(0, S // TQ, qloop, 0)

    def col(o):
        return pl.BlockSpec((S, L), lambda b, p: (b, o + p))

    return pl.pallas_call(
        body, name="sb_fwd", grid=(dm.Bl, nb),
        in_specs=[col(oq), col(oq + nb), col(oq + 2 * nb), col(og)], out_specs=[col(0), col(0), col(0)],
        out_shape=[jax.ShapeDtypeStruct((dm.T, D), f32), jax.ShapeDtypeStruct((dm.T, D), f32),
                   jax.ShapeDtypeStruct((dm.T, D), ACT_DTYPE)],
        scratch_shapes=[pltpu.VMEM((S, L), MXU_DTYPE)] * 3,
        compiler_params=_cp(),
    )(proj, proj, proj, proj)


def _sb_bwd(proj, o, lt, dsb, dm):
    L, S, D, TQ = LANES, dm.S, dm.D, SB_TILE
    oq, og = dm.c_qkv // L, dm.c_sbg // L
    nb = D // L
    scale = HEAD ** -0.5

    def body(q_ref, k_ref, v_ref, g_ref, o_ref, lt_ref, d_ref, dq_ref, dk_ref, dv_ref, dg_ref,
             k_s, ka_s, kb_s, v_s, dk_acc, dv_acc):
        lo = _iota2((1, L), 1) < HEAD
        hi = jnp.logical_not(lo)
        kf = k_ref[...]
        k_s[...] = _mx(kf)
        ka_s[...] = _mx(jnp.where(lo, kf, 0.0))
        kb_s[...] = _mx(jnp.where(hi, kf, 0.0))
        v_s[...] = _mx(v_ref[...])
        dk_acc[...] = jnp.zeros_like(dk_acc)
        dv_acc[...] = jnp.zeros_like(dv_acc)
        before = _iota2((TQ, TQ), 0) > _iota2((TQ, TQ), 1)
        neg_ule = -((_iota2((TQ, TQ), 0) <= _iota2((TQ, TQ), 1)).astype(bf16))
        ult = (_iota2((TQ, TQ), 0) < _iota2((TQ, TQ), 1)).astype(bf16)
        k_m = (ka_s, kb_s)

        def qloop(qt, _):
            rows = pl.ds(pl.multiple_of(qt * TQ, TQ), TQ)
            gt = g_ref[rows, :]
            sg = _sigmoid(gt)
            dsb_b = d_ref[rows, :]
            do = dsb_b * (gt * sg)
            dg_ref[rows, :] = (dsb_b * o_ref[rows, :] * (sg * (1.0 + gt * (1.0 - sg)))).astype(dg_ref.dtype)
            qs = q_ref[rows, :] * scale
            qb = [_mx(jnp.where(m, qs, 0.0)) for m in (lo, hi)]
            dob = [_mx(jnp.where(m, do, 0.0)) for m in (lo, hi)]
            ltot = [lt_ref[rows, 0:1], lt_ref[rows, HEAD:HEAD + 1]]

            def tile(kt, carry, causal):
                runs, rgs, dq = carry
                krows = pl.ds(pl.multiple_of(kt * TQ, TQ), TQ)
                k_t, v_t = k_s[krows, :], v_s[krows, :]
                dk_t = jnp.zeros((TQ, L), f32)
                dv_t = jnp.zeros((TQ, L), f32)
                new_runs, new_rgs = [], []
                for h in range(2):
                    lb, att, lm = _sb_tile(qb[h], k_t, ltot[h] - runs[h], neg_ule, causal)
                    gm = att * _nt(dob[h], v_t)
                    pre = rgs[h] + _exact2_nn(gm, ult)
                    beta = jnp.exp(lb)
                    dz = gm * (1.0 - beta) - pre * beta
                    if causal is not None:
                        dz = jnp.where(causal, dz, 0.0)
                    dz = _mx(dz)
                    dq = dq + _nn(dz, k_m[h][krows, :])
                    dk_t = dk_t + _tn(dz, qb[h])
                    dv_t = dv_t + _tn(_mx(att), dob[h])
                    new_runs.append(runs[h] + jnp.sum(lm, axis=1, keepdims=True))
                    new_rgs.append(rgs[h] + jnp.sum(gm, axis=1, keepdims=True))
                dk_acc[krows, :] += dk_t
                dv_acc[krows, :] += dv_t
                return tuple(new_runs), tuple(new_rgs), dq

            zc = jnp.zeros((TQ, 1), f32)
            carry = lax.fori_loop(0, qt, lambda kt, c: tile(kt, c, None), ((zc, zc), (zc, zc), jnp.zeros((TQ, L), f32)))
            _, _, dq = tile(qt, carry, before)
            dq_ref[rows, :] = (dq * scale).astype(dq_ref.dtype)
            return 0

        lax.fori_loop(0, S // TQ, qloop, 0)
        dk_ref[...] = dk_acc[...].astype(dk_ref.dtype)
        dv_ref[...] = dv_acc[...].astype(dv_ref.dtype)

    def col(off):
        return pl.BlockSpec((S, L), lambda b, p: (b, off + p))

    out = jax.ShapeDtypeStruct((dm.T, D), ACT_DTYPE)
    return pl.pallas_call(
        body, name="sb_bwd", grid=(dm.Bl, nb),
        in_specs=[col(oq), col(oq + nb), col(oq + 2 * nb), col(og), col(0), col(0), col(0)],
        out_specs=[col(0), col(0), col(0), col(0)], out_shape=[out, out, out, out],
        scratch_shapes=[pltpu.VMEM((S, L), MXU_DTYPE)] * 4 + [pltpu.VMEM((S, L), f32)] * 2,
        compiler_params=_cp(),
    )(proj, proj, proj, proj, o, lt, dsb)


def _merge_fwd(proj, ys, dm):
    tr = _pick(dm.T, (256, 128))
    ct = 512
    om = dm.c_mrg // ct
    nb = dm.D // ct
    blk = pl.BlockSpec((tr, ct), lambda i, j: (i, j))

    def body(l0, l1, l2, y0, y1, y2, o_ref):
        acc = _sigmoid(l0[...]) * y0[...] + _sigmoid(l1[...]) * y1[...] + _sigmoid(l2[...]) * y2[...]
        o_ref[...] = acc.astype(o_ref.dtype)

    return pl.pallas_call(
        body, name="merge_fwd", grid=(dm.T // tr, nb),
        in_specs=[pl.BlockSpec((tr, ct), functools.partial(lambda i, j, b: (i, om + b * nb + j), b=b)) for b in range(3)] + [blk] * 3,
        out_specs=blk, out_shape=jax.ShapeDtypeStruct((dm.T, dm.D), ACT_DTYPE), compiler_params=_cp(),
    )(proj, proj, proj, *ys)


def _merge_bwd(proj, ys, dmerged, dm):
    tr = _pick(dm.T, (256, 128))
    ct = 512
    om = dm.c_mrg // ct
    nb = dm.D // ct
    blk = pl.BlockSpec((tr, ct), lambda i, j: (i, j))

    def body(l0, l1, l2, y0, y1, y2, d_ref, dy0, dy1, dy2, dl0, dl1, dl2):
        d = d_ref[...]
        for l_ref, y_ref, dy_ref, dl_ref in ((l0, y0, dy0, dl0), (l1, y1, dy1, dl1), (l2, y2, dy2, dl2)):
            sg = _sigmoid(l_ref[...])
            dy_ref[...] = (d * sg).astype(dy_ref.dtype)
            dl_ref[...] = (d * y_ref[...] * sg * (1.0 - sg)).astype(dl_ref.dtype)

    out = jax.ShapeDtypeStruct((dm.T, dm.D), ACT_DTYPE)
    return pl.pallas_call(
        body, name="merge_bwd", grid=(dm.T // tr, nb),
        in_specs=[pl.BlockSpec((tr, ct), functools.partial(lambda i, j, b: (i, om + b * nb + j), b=b)) for b in range(3)] + [blk] * 4,
        out_specs=[blk] * 6, out_shape=[out] * 6, compiler_params=_cp(),
    )(proj, proj, proj, *ys, dmerged)


def _layer_fwd(x, p, dm):
    h = _rms_fwd(x, p["norm_w"], dm)
    proj = _mm(h, p["w_in"], name="mm_in")
    xbc = _conv_fwd(proj, p["conv_w"], p["conv_b"], dm)
    dt, cum, cumt = _dt_prep(proj, p["dt_bias"], p["a_log"], dm)
    y, hs = _ssd_fwd(xbc, dt, cum, cumt, p["d_skip"], dm)
    ssm_n = _gnorm_fwd(y, proj, p["ssm_norm_w"], dm)
    pool_o = _pool_fwd(proj, p["pool_w"], p["pool_scale"], dm)
    o, lt, sb_o = _sb_fwd(proj, dm)
    ys = (_mm(ssm_n, p["w_proj_ssm"], name="mm_ps"), _mm(pool_o, p["w_proj_pool"], name="mm_pp"),
          _mm(sb_o, p["w_proj_sb"], name="mm_pb"))
    merged = _merge_fwd(proj, ys, dm)
    x_next = _mm(merged, p["w_out"], res=x, name="mm_out")
    saved = dict(x=x, h=h, proj=proj, xbc=xbc, dt=dt, cum=cum, cumt=cumt, y=y, hs=hs, ssm_n=ssm_n, pool_o=pool_o, o=o,
                 lt=lt, sb_o=sb_o, ys=ys, merged=merged)
    return x_next, saved


def _layer_bwd(dx_out, p, sv, dm):
    g = {}
    proj = sv["proj"]
    dmerged = _mm(dx_out, p["w_out"], tb=True, name="mm_dmerged")
    g["w_out"] = _mm(sv["merged"], dx_out, ta=True, name="mm_dwout")
    dy0, dy1, dy2, dl0, dl1, dl2 = _merge_bwd(proj, sv["ys"], dmerged, dm)
    d_ssm_n = _mm(dy0, p["w_proj_ssm"], tb=True, name="mm_dssm")
    g["w_proj_ssm"] = _mm(sv["ssm_n"], dy0, ta=True, name="mm_dwps")
    d_pool_o = _mm(dy1, p["w_proj_pool"], tb=True, name="mm_dpool")
    g["w_proj_pool"] = _mm(sv["pool_o"], dy1, ta=True, name="mm_dwpp")
    d_sb_o = _mm(dy2, p["w_proj_sb"], tb=True, name="mm_dsb")
    g["w_proj_sb"] = _mm(sv["sb_o"], dy2, ta=True, name="mm_dwpb")
    dy, dz, g["ssm_norm_w"] = _gnorm_bwd(d_ssm_n, sv["y"], proj, p["ssm_norm_w"], dm)
    dxs, db, dc, ddt, dcum, dcum_t, dd = _ssd_bwd(sv["xbc"], sv["dt"], sv["cum"], sv["cumt"], p["d_skip"], sv["hs"], dy, dm)
    g["d_skip"] = dd
    d_dtraw, g["dt_bias"], g["a_log"] = _dt_bwd(proj, sv["dt"], dcum, dcum_t, ddt, p["dt_bias"], p["a_log"], dm)
    dxbc, g["conv_w"], g["conv_b"] = _conv_bwd(proj, jnp.concatenate([dxs, db, dc], axis=1), p["conv_w"], p["conv_b"], dm)
    dpu, dpg, g["pool_w"], g["pool_scale"] = _pool_bwd(proj, d_pool_o, p["pool_w"], p["pool_scale"], dm)
    dq, dk, dv, dsg = _sb_bwd(proj, sv["o"], sv["lt"], d_sb_o, dm)
    dproj = jnp.concatenate([dz, dxbc, dpu, dpg, dq, dk, dv, dsg, dl0, dl1, dl2, d_dtraw,
                             jnp.zeros((dm.T, DT_PAD - LANES), ACT_DTYPE)], axis=1)
    dh = _mm(dproj, p["w_in"], tb=True, name="mm_dh")
    g["w_in"] = _mm(sv["h"], dproj, ta=True, name="mm_dwin")
    dx, g["norm_w"] = _rms_bwd(sv["x"], dh, dx_out, p["norm_w"], dm)
    return dx, g


def _local_step(x, tgt, layers, final_norm_w, dm):
    saved = []
    for p in layers:
        x, sv = _layer_fwd(x, p, dm)
        saved.append(sv)
    dx, dfn, loss = _loss_head(x, tgt, final_norm_w, dm)
    grads = [None] * len(layers)
    for l in reversed(range(len(layers))):
        dx, grads[l] = _layer_bwd(dx, layers[l], saved[l], dm)
    return loss, dx, grads, dfn


def _row_tile(rows, cols):
    cap = max(8, (1 << 18) // cols)
    for t in (1024, 512, 256, 128, 64, 32, 16, 8):
        if t <= cap and rows % t == 0:
            return t
    return rows


def _adamw(w, g, m, v, name):
    rows, cols = w.shape
    tr = _row_tile(rows, cols)
    c1 = 1.0 - ADAM_B1 ** ADAM_STEP
    c2 = 1.0 - ADAM_B2 ** ADAM_STEP

    def body(w_ref, g_ref, m_ref, v_ref, d_ref, mo_ref, vo_ref):
        gv = g_ref[...]
        mn = ADAM_B1 * m_ref[...] + (1.0 - ADAM_B1) * gv
        vn = ADAM_B2 * v_ref[...] + (1.0 - ADAM_B2) * (gv * gv)
        d_ref[...] = -ADAM_LR * ((mn / c1) / (jnp.sqrt(vn / c2) + ADAM_EPS) + ADAM_WD * w_ref[...])
        mo_ref[...] = mn
        vo_ref[...] = vn

    blk = pl.BlockSpec((tr, cols), lambda i: (i, 0))
    out = jax.ShapeDtypeStruct((rows, cols), f32)
    return pl.pallas_call(body, name=name, grid=(rows // tr,), in_specs=[blk] * 4, out_specs=[blk] * 3, out_shape=[out] * 3,
                          compiler_params=_cp())(w, g, m, v)


def _sum_parts(a, parts, name):
    n, rows, cols = parts.shape
    tr = _row_tile(rows, cols)

    def body(a_ref, p_ref, o_ref):
        acc = a_ref[...]
        for k in range(n):
            acc = acc + p_ref[k].astype(f32)
        o_ref[...] = acc

    blk = pl.BlockSpec((tr, cols), lambda i: (i, 0))
    return pl.pallas_call(body, name=name, grid=(rows // tr,), in_specs=[blk, pl.BlockSpec((n, tr, cols), lambda i: (0, i, 0))],
                          out_specs=blk, out_shape=jax.ShapeDtypeStruct((rows, cols), f32), compiler_params=_cp())(a, parts)


ICI_KINDS = ("y", "x", "xy")
HBM_SPEC = pl.BlockSpec(memory_space=pltpu.HBM)


def _me():
    return lax.axis_index("x"), lax.axis_index("y"), lax.axis_index("c")


def _peer(kind):
    x, y, c = _me()
    return {"c": (x, y, 1 - c), "y": (x, 1 - y, c), "x": (1 - x, y, c), "xy": (1 - x, 1 - y, c)}[kind]


def _peer_chip(kind):
    x, y, _ = _me()
    return {"y": 2 * x + (1 - y), "x": 2 * (1 - x) + y, "xy": 2 * (1 - x) + (1 - y)}[kind]


def _exchange(send, kinds, name):
    n = len(kinds)

    def body(src, dst, ssem, rsem):
        cps = [pltpu.make_async_remote_copy(src_ref=src.at[k], dst_ref=dst.at[k], send_sem=ssem.at[k], recv_sem=rsem.at[k],
                                            device_id=_peer(kind), device_id_type=MESH) for k, kind in enumerate(kinds)]
        for cp in cps:
            cp.start()
        for cp in cps:
            cp.wait()

    return pl.pallas_call(
        body, name=name, out_shape=jax.ShapeDtypeStruct(send.shape, send.dtype), in_specs=[HBM_SPEC], out_specs=HBM_SPEC,
        scratch_shapes=[pltpu.SemaphoreType.DMA((n,)), pltpu.SemaphoreType.DMA((n,))],
    )(send)


def _allgather_shards(pk):
    ly = pk.shape[0]
    lh = ly // 2

    def body(src, out, ssem, rsem, lsem):
        x, y, c = _me()
        j_me = 2 * x + y
        mine = pl.ds(c * lh, lh)
        theirs = pl.ds((1 - c) * lh, lh)
        local = pltpu.make_async_copy(src, out.at[j_me], lsem)
        local.start()

        def ici(k, kind, j_src):
            return pltpu.make_async_remote_copy(src_ref=src.at[mine], dst_ref=out.at[j_src, mine], send_sem=ssem.at[k],
                                                recv_sem=rsem.at[k], device_id=_peer(kind), device_id_type=MESH)

        def d2d(k, j_src, half):
            return pltpu.make_async_remote_copy(src_ref=out.at[j_src, half], dst_ref=out.at[j_src, half], send_sem=ssem.at[3 + k],
                                                recv_sem=rsem.at[3 + k], device_id=_peer("c"), device_id_type=MESH)

        first = [ici(k, kind, j_me) for k, kind in enumerate(ICI_KINDS)]
        for cp in first:
            cp.start()
        passed = []
        for k, kind in enumerate(ICI_KINDS):
            ici(k, kind, _peer_chip(kind)).wait_recv()
            fwd = d2d(k, _peer_chip(kind), mine)
            fwd.start()
            passed.append(fwd)
        for k, kind in enumerate(ICI_KINDS):
            d2d(k, _peer_chip(kind), theirs).wait_recv()
        for cp in first + passed:
            cp.wait_send()
        local.wait()

    return pl.pallas_call(
        body, name="allgather_shards", out_shape=jax.ShapeDtypeStruct((4,) + pk.shape, pk.dtype), in_specs=[HBM_SPEC],
        out_specs=HBM_SPEC,
        scratch_shapes=[pltpu.SemaphoreType.DMA((6,)), pltpu.SemaphoreType.DMA((6,)), pltpu.SemaphoreType.DMA],
    )(pk)


def _allreduce_small(v):
    rows = v.shape[0]

    def body(v_ref, o_ref, buf, ssem, rsem):
        x, y, c = _me()
        me = 4 * x + 2 * y + c
        buf[0] = v_ref[...]
        cps = []
        for k in range(1, 8):
            peer = (1 - x if k & 4 else x, 1 - y if k & 2 else y, 1 - c if k & 1 else c)
            cps.append(pltpu.make_async_remote_copy(src_ref=v_ref, dst_ref=buf.at[k], send_sem=ssem.at[k - 1], recv_sem=rsem.at[k - 1],
                                                    device_id=peer, device_id_type=MESH))
        for cp in cps:
            cp.start()
        for cp in cps:
            cp.wait()
        acc = buf[jnp.bitwise_xor(me, 0)]
        for d in range(1, 8):
            acc = acc + buf[jnp.bitwise_xor(me, d)]
        o_ref[...] = acc

    vm = pl.BlockSpec(memory_space=pltpu.VMEM)
    return pl.pallas_call(
        body, name="allreduce_small", out_shape=jax.ShapeDtypeStruct(v.shape, f32), in_specs=[vm], out_specs=vm,
        scratch_shapes=[pltpu.VMEM((8, rows, LANES), f32), pltpu.SemaphoreType.DMA((7,)), pltpu.SemaphoreType.DMA((7,))],
    )(v)


SHARDED = ("w_in", "w_proj_ssm", "w_proj_pool", "w_proj_sb", "w_out", "pool_w", "conv_w")
SMALL = ("norm_w", "conv_b", "dt_bias", "a_log", "d_skip", "ssm_norm_w", "pool_scale")


def _pack_rows(dm):
    c4 = dm.IN_COLS // 4
    conv_rows = -(-(2 * dm.CC) // dm.D)
    rows = dict(w_in=c4, w_proj_ssm=dm.W2 // 4, w_proj_pool=dm.D // 4, w_proj_sb=dm.D // 4, w_out=dm.D // 4,
                pool_w=dm.D // 16, conv_w=conv_rows)
    total = sum(rows[n] for n in SHARDED)
    return rows, -(-total // 16) * 16


def _pack(sh, dm, dtype, conv_bits):
    rows, r_tot = _pack_rows(dm)
    ly = sh["w_in"].shape[0]
    parts = []
    for n in SHARDED:
        a = sh[n]
        if n == "conv_w":
            a = lax.bitcast_convert_type(a.astype(f32), bf16) if conv_bits else a.astype(dtype)
            a = a.reshape(ly, -1)
            a = jnp.pad(a, ((0, 0), (0, rows[n] * dm.D - a.shape[1])))
        else:
            a = a.astype(dtype)
        parts.append(a.reshape(ly, rows[n], dm.D))
    used = sum(rows[n] for n in SHARDED)
    parts.append(jnp.zeros((ly, r_tot - used, dm.D), parts[0].dtype))
    return jnp.concatenate(parts, axis=1)


def _unpack(pk, dm, conv_bits):
    rows, _ = _pack_rows(dm)
    lead = pk.shape[:-2]
    c4 = dm.IN_COLS // 4
    shapes = dict(w_in=(dm.D, c4), w_proj_ssm=(dm.W2 // 4, dm.D), w_proj_pool=(dm.D // 4, dm.D), w_proj_sb=(dm.D // 4, dm.D),
                  w_out=(dm.D // 4, dm.D), pool_w=(POOL_GROUPS, dm.Dg // 4, dm.Dg), conv_w=(CONV_WIDTH, dm.CC // 4))
    out, r0 = {}, 0
    for n in SHARDED:
        a = pk[..., r0:r0 + rows[n], :].reshape(lead + (rows[n] * dm.D,))
        r0 += rows[n]
        if n == "conv_w":
            if conv_bits:
                a = lax.bitcast_convert_type(a[..., :2 * dm.CC].reshape(lead + (dm.CC, 2)), f32)
            else:
                a = a[..., :dm.CC]
        out[n] = a.reshape(lead + shapes[n])
    return out


def _join_shards(sh, dm):
    full = {}
    w = jnp.moveaxis(sh["w_in"], 0, 2)
    full["w_in"] = _permute_cols(w.reshape(w.shape[0], dm.D, dm.IN_COLS), dm)
    for n in ("w_proj_ssm", "w_proj_pool", "w_proj_sb", "w_out"):
        a = jnp.moveaxis(sh[n], 0, 1)
        full[n] = a.reshape(a.shape[0], -1, dm.D)
    a = jnp.moveaxis(sh["pool_w"], 0, 2)
    full["pool_w"] = a.reshape(a.shape[0], POOL_GROUPS, dm.Dg, dm.Dg)
    a = jnp.moveaxis(sh["conv_w"], 0, 2)
    full["conv_w"] = a.reshape(a.shape[0], CONV_WIDTH, dm.CC)
    return full


def _split_shards(full, dm):
    sh = {}
    w = _unpermute_cols(full["w_in"], dm)
    sh["w_in"] = jnp.moveaxis(w.reshape(w.shape[0], dm.D, 4, dm.IN_COLS // 4), 2, 0)
    for n in ("w_proj_ssm", "w_proj_pool", "w_proj_sb", "w_out"):
        a = full[n]
        sh[n] = jnp.moveaxis(a.reshape(a.shape[0], 4, a.shape[1] // 4, dm.D), 1, 0)
    a = full["pool_w"]
    sh["pool_w"] = jnp.moveaxis(a.reshape(a.shape[0], POOL_GROUPS, 4, dm.Dg // 4, dm.Dg), 2, 0)
    a = full["conv_w"]
    sh["conv_w"] = jnp.moveaxis(a.reshape(a.shape[0], CONV_WIDTH, 4, dm.CC // 4), 2, 0)
    return sh


def _reduce_scatter(full_grads, dm):
    _, _, c = _me()
    x, y, _ = _me()
    j_me = 2 * x + y
    sh = _split_shards(full_grads, dm)
    pk = jnp.stack([_pack({n: sh[n][j] for n in SHARDED}, dm, f32, False) for j in range(4)])
    _, ly, r, d = pk.shape
    lh = ly // 2
    pk = pk.reshape(4, 2, lh * r, d)
    mine = lax.dynamic_index_in_dim(pk, c, 1, keepdims=False)
    theirs = lax.dynamic_index_in_dim(pk, 1 - c, 1, keepdims=False).astype(WIRE_DTYPE)
    got = _exchange(theirs[None], ("c",), "rs_pair")
    s1 = _sum_parts(mine.reshape(4 * lh * r, d), got.reshape(1, 4 * lh * r, d), "rs_pair_sum").reshape(4, lh * r, d)
    send = jnp.stack([lax.dynamic_index_in_dim(s1, jnp.bitwise_xor(j_me, m), 0, keepdims=False) for m in (1, 2, 3)])
    got = _exchange(send.astype(WIRE_DTYPE), ICI_KINDS, "rs_ici")
    red = _sum_parts(lax.dynamic_index_in_dim(s1, j_me, 0, keepdims=False), got, "rs_ici_sum")
    sib = _exchange(red[None], ("c",), "rs_sibling")[0]
    both = jnp.where(c == 0, jnp.stack([red, sib]), jnp.stack([sib, red])).reshape(ly, r, d)
    return _unpack(both, dm, False)


def _flatten_small(parts):
    flat = jnp.concatenate([p.reshape(-1).astype(f32) for p in parts])
    rows = -(-flat.shape[0] // (8 * LANES)) * 8
    return jnp.pad(flat, (0, rows * LANES - flat.shape[0])).reshape(rows, LANES)


def _unflatten_small(buf, shapes):
    flat = buf.reshape(-1)
    out, o = [], 0
    for s in shapes:
        n = math.prod(s)
        out.append(flat[o:o + n].reshape(s))
        o += n
    return out


def kernel(x, norm_w, w_in, conv_w, conv_b, dt_bias, a_log, d_skip, ssm_norm_w, pool_w, pool_scale, w_proj_ssm, w_proj_pool, w_proj_sb, w_out, final_norm_w, loss_target, m_norm_w, m_w_in, m_conv_w, m_conv_b, m_dt_bias, m_a_log, m_d_skip, m_ssm_norm_w, m_pool_w, m_pool_scale, m_w_proj_ssm, m_w_proj_pool, m_w_proj_sb, m_w_out, m_final_norm_w, v_norm_w, v_w_in, v_conv_w, v_conv_b, v_dt_bias, v_a_log, v_d_skip, v_ssm_norm_w, v_pool_w, v_pool_scale, v_w_proj_ssm, v_w_proj_pool, v_w_proj_sb, v_w_out, v_final_norm_w):
    names = ("norm_w", "w_in", "conv_w", "conv_b", "dt_bias", "a_log", "d_skip", "ssm_norm_w", "pool_w", "pool_scale",
             "w_proj_ssm", "w_proj_pool", "w_proj_sb", "w_out", "final_norm_w")
    w = dict(zip(names, (norm_w, w_in, conv_w, conv_b, dt_bias, a_log, d_skip, ssm_norm_w, pool_w, pool_scale, w_proj_ssm,
                         w_proj_pool, w_proj_sb, w_out, final_norm_w)))
    m = dict(zip(names, (m_norm_w, m_w_in, m_conv_w, m_conv_b, m_dt_bias, m_a_log, m_d_skip, m_ssm_norm_w, m_pool_w, m_pool_scale,
                         m_w_proj_ssm, m_w_proj_pool, m_w_proj_sb, m_w_out, m_final_norm_w)))
    v = dict(zip(names, (v_norm_w, v_w_in, v_conv_w, v_conv_b, v_dt_bias, v_a_log, v_d_skip, v_ssm_norm_w, v_pool_w, v_pool_scale,
                         v_w_proj_ssm, v_w_proj_pool, v_w_proj_sb, v_w_out, v_final_norm_w)))
    bl, s, d = x.shape
    dm = Dims(bl, s, d)
    ly = norm_w.shape[0]

    gathered = _allgather_shards(_pack({n: w[n] for n in SHARDED}, dm, WIRE_DTYPE, True))
    full = _join_shards(_unpack(gathered, dm, True), dm)
    layers = []
    for l in range(ly):
        p = {n: full[n][l] for n in SHARDED}
        p.update({n: w[n][l] for n in SMALL})
        layers.append(p)

    loss_part, dx, grads, dfn = _local_step(x.reshape(dm.T, d), loss_target.reshape(dm.T, d), layers, final_norm_w, dm)

    h = dm.H
    small_parts = [loss_part]
    small_shapes = [(1, LANES)]
    for n in SMALL:
        if n in ("dt_bias", "a_log", "d_skip"):
            small_parts.append(jnp.stack([g[n][0, :h] for g in grads]))
        else:
            small_parts.append(jnp.stack([g[n][0] for g in grads]))
        small_shapes.append(w[n].shape)
    small_parts.append(dfn[0])
    small_shapes.append(final_norm_w.shape)
    red_small = _allreduce_small(_flatten_small(small_parts))
    small_g = _unflatten_small(red_small, small_shapes)
    loss = small_g[0][0, 0]
    g_out = dict(zip(SMALL + ("final_norm_w",), small_g[1:]))

    g_out.update(_reduce_scatter({n: jnp.stack([g[n] for g in grads]) for n in SHARDED}, dm))

    small_names = SMALL + ("final_norm_w",)
    zero_row = jnp.zeros((1, LANES), f32)
    pack_small = lambda t: _flatten_small([zero_row] + [t[n] for n in small_names])
    ds, ms, vs = _adamw(pack_small(w), red_small, pack_small(m), pack_small(v), "adamw_small")
    delta, new_m, new_v = {}, {}, {}
    for tgt, buf in ((delta, ds), (new_m, ms), (new_v, vs)):
        tgt.update(zip(small_names, _unflatten_small(buf, small_shapes)[1:]))
    for n in SHARDED:
        shp = w[n].shape
        two = (math.prod(shp[:-1]), shp[-1])
        dd, mm, vv = _adamw(w[n].reshape(two), g_out[n].reshape(two), m[n].reshape(two), v[n].reshape(two), "adamw_" + n)
        delta[n], new_m[n], new_v[n] = dd.reshape(shp), mm.reshape(shp), vv.reshape(shp)
        g_out[n] = g_out[n].reshape(shp)

    return (loss, dx.reshape(bl, s, d), *[g_out[n] for n in names], *[delta[n] for n in names],
            *[new_m[n] for n in names], *[new_v[n] for n in names])
```

```python
import functools
import math

import jax
import jax.numpy as jnp
from jax import lax
from jax.experimental import pallas as pl
from jax.experimental.pallas import tpu as pltpu

f32 = jnp.float32
bf16 = jnp.bfloat16
MXU_DTYPE = jnp.bfloat16
ACT_DTYPE = jnp.bfloat16
WIRE_DTYPE = jnp.bfloat16

EPS = 1e-6
LANES = 128
HEAD = 64
SSM_GROUPS = 2
CONV_WIDTH = 4
POOL_GROUPS = 4
DT_PAD = 512
N_BRANCHES = 3
VMEM_LIMIT = 56 * 1024 * 1024

ADAM_LR, ADAM_B1, ADAM_B2, ADAM_EPS, ADAM_WD, ADAM_STEP = 0.001, 0.9, 0.999, 1e-08, 0.01, 10
MESH = pl.DeviceIdType.MESH


def _cp(**kw):
    return pltpu.CompilerParams(vmem_limit_bytes=VMEM_LIMIT, **kw)


def _pick(n, prefs):
    for p in prefs:
        if n % p == 0:
            return p
    return n


def _dg(a, b, ca, cb):
    return lax.dot_general(a, b, (((ca,), (cb,)), ((), ())), preferred_element_type=f32)


def _nn(a, b):
    return _dg(a, b, 1, 0)


def _nt(a, b):
    return _dg(a, b, 1, 1)


def _tn(a, b):
    return _dg(a, b, 0, 0)


def _mx(a):
    return a.astype(MXU_DTYPE)


def _split3(a):
    hi = a.astype(bf16)
    r = a - hi.astype(f32)
    mid = r.astype(bf16)
    lo = (r - mid.astype(f32)).astype(bf16)
    return hi, mid, lo


def _exact_nn(a, u):
    hi, mid, lo = _split3(a)
    return _nn(hi, u) + _nn(mid, u) + _nn(lo, u)


def _iota2(shape, dim):
    return lax.broadcasted_iota(jnp.int32, shape, dim)


def _sigmoid(x):
    return 1.0 / (1.0 + jnp.exp(-x))


def _shift_down(v, sh, row):
    return jnp.where(row >= sh, pltpu.roll(v, sh, 0), 0.0)


def _shift_up(v, sh, row):
    n = v.shape[0]
    return jnp.where(row < n - sh, pltpu.roll(v, n - sh, 0), 0.0)


class Dims:
    def __init__(self, bl, s, d):
        self.Bl, self.S, self.D = bl, s, d
        self.T = bl * s
        self.W2 = 2 * d
        self.H = self.W2 // HEAD
        self.hpg = self.H // SSM_GROUPS
        self.CC = self.W2 + 2 * SSM_GROUPS * LANES
        self.Dg = d // POOL_GROUPS
        self.nc = s // LANES
        self.o_dt = self.W2 + self.CC
        self.IN_COLS = 13 * d + 2 * SSM_GROUPS * LANES + self.H
        self.c_z = 0
        self.c_xbc = self.W2
        self.c_pu = self.W2 + self.CC
        self.c_pg = self.c_pu + d
        self.c_qkv = self.c_pg + d
        self.c_sbg = self.c_qkv + 3 * d
        self.c_mrg = self.c_sbg + d
        self.c_dt = self.c_mrg + 3 * d
        self.NP = self.c_dt + DT_PAD
        assert self.c_dt == self.IN_COLS - self.H
        assert s % LANES == 0 and d % 512 == 0 and self.H <= LANES


def _permute_cols(w, dm):
    pad = jnp.zeros(w.shape[:-1] + (DT_PAD - dm.H,), w.dtype)
    return jnp.concatenate([w[..., :dm.o_dt], w[..., dm.o_dt + dm.H:], w[..., dm.o_dt:dm.o_dt + dm.H], pad], axis=-1)


def _unpermute_cols(w, dm):
    return jnp.concatenate([w[..., :dm.o_dt], w[..., dm.c_dt:dm.c_dt + dm.H], w[..., dm.o_dt:dm.c_dt]], axis=-1)


def _mm(a, b, *, ta=False, tb=False, out_dtype=f32, res=None, name):
    M, K = (a.shape[1], a.shape[0]) if ta else a.shape
    N = b.shape[0] if tb else b.shape[1]
    tm = _pick(M, (1024, 512, 256, 128))
    tn = _pick(N, (1024, 512, 256, 128))
    tk = _pick(K, (1024, 512, 256, 128))
    nk = K // tk

    def body(*refs):
        if res is None:
            a_ref, b_ref, o_ref, acc = refs
        else:
            a_ref, b_ref, r_ref, o_ref, acc = refs
        k = pl.program_id(2)

        @pl.when(k == 0)
        def _():
            acc[...] = jnp.zeros_like(acc)

        acc[...] += _dg(_mx(a_ref[...]), _mx(b_ref[...]), 0 if ta else 1, 1 if tb else 0)

        @pl.when(k == nk - 1)
        def _():
            v = acc[...]
            if res is not None:
                v = v + r_ref[...]
            o_ref[...] = v.astype(out_dtype)

    a_spec = pl.BlockSpec((tk, tm), lambda i, j, k: (k, i)) if ta else pl.BlockSpec((tm, tk), lambda i, j, k: (i, k))
    b_spec = pl.BlockSpec((tn, tk), lambda i, j, k: (j, k)) if tb else pl.BlockSpec((tk, tn), lambda i, j, k: (k, j))
    o_spec = pl.BlockSpec((tm, tn), lambda i, j, k: (i, j))
    in_specs = [a_spec, b_spec] + ([o_spec] if res is not None else [])
    args = (a, b) + ((res,) if res is not None else ())
    return pl.pallas_call(
        body, name=name, grid=(M // tm, N // tn, nk), in_specs=in_specs, out_specs=o_spec,
        out_shape=jax.ShapeDtypeStruct((M, N), out_dtype), scratch_shapes=[pltpu.VMEM((tm, tn), f32)],
        compiler_params=_cp(dimension_semantics=("parallel", "parallel", "arbitrary")),
    )(*args)


def _rms_fwd(x, w, dm):
    tr = _pick(dm.T, (256, 128))

    def body(x_ref, w_ref, o_ref):
        xf = x_ref[...]
        r = lax.rsqrt(jnp.mean(xf * xf, axis=-1, keepdims=True) + EPS)
        o_ref[...] = (xf * r * w_ref[...]).astype(o_ref.dtype)

    return pl.pallas_call(
        body, name="rms_fwd", grid=(dm.T // tr,),
        in_specs=[pl.BlockSpec((tr, dm.D), lambda i: (i, 0)), pl.BlockSpec((1, dm.D), lambda i: (0, 0))],
        out_specs=pl.BlockSpec((tr, dm.D), lambda i: (i, 0)),
        out_shape=jax.ShapeDtypeStruct((dm.T, dm.D), ACT_DTYPE), compiler_params=_cp(),
    )(x, w.reshape(1, dm.D))


def _rms_bwd(x, dh, dres, w, dm):
    tr = _pick(dm.T, (256, 128))

    def body(x_ref, dh_ref, dr_ref, w_ref, dx_ref, dw_ref):
        @pl.when(pl.program_id(0) == 0)
        def _():
            dw_ref[...] = jnp.zeros_like(dw_ref)

        xf = x_ref[...]
        r = lax.rsqrt(jnp.mean(xf * xf, axis=-1, keepdims=True) + EPS)
        xh = xf * r
        dh_ = dh_ref[...]
        dxh = dh_ * w_ref[...]
        dx_ref[...] = dr_ref[...] + r * (dxh - xh * jnp.mean(dxh * xh, axis=-1, keepdims=True))
        dw_ref[...] += jnp.sum(dh_ * xh, axis=0, keepdims=True)

    row = pl.BlockSpec((tr, dm.D), lambda i: (i, 0))
    vec = pl.BlockSpec((1, dm.D), lambda i: (0, 0))
    return pl.pallas_call(
        body, name="rms_bwd", grid=(dm.T // tr,), in_specs=[row, row, row, vec], out_specs=[row, vec],
        out_shape=[jax.ShapeDtypeStruct((dm.T, dm.D), f32), jax.ShapeDtypeStruct((1, dm.D), f32)],
        compiler_params=_cp(dimension_semantics=("arbitrary",)),
    )(x, dh, dres, w.reshape(1, dm.D))


def _loss_head(x, tgt, w, dm):
    tr = _pick(dm.T, (256, 128))

    def body(x_ref, t_ref, w_ref, dx_ref, dw_ref, ls_ref):
        @pl.when(pl.program_id(0) == 0)
        def _():
            dw_ref[...] = jnp.zeros_like(dw_ref)
            ls_ref[...] = jnp.zeros_like(ls_ref)

        xf = x_ref[...]
        r = lax.rsqrt(jnp.mean(xf * xf, axis=-1, keepdims=True) + EPS)
        xh = xf * r
        err = xh * w_ref[...] - t_ref[...]
        per_tok = jnp.mean(err * err, axis=-1, keepdims=True)
        ls_ref[...] += 0.5 * jnp.sum(per_tok, axis=0, keepdims=True)
        dy = err * (1.0 / dm.D)
        dxh = dy * w_ref[...]
        dx_ref[...] = r * (dxh - xh * jnp.mean(dxh * xh, axis=-1, keepdims=True))
        dw_ref[...] += jnp.sum(dy * xh, axis=0, keepdims=True)

    row = pl.BlockSpec((tr, dm.D), lambda i: (i, 0))
    vec = pl.BlockSpec((1, dm.D), lambda i: (0, 0))
    return pl.pallas_call(
        body, name="loss_head", grid=(dm.T // tr,), in_specs=[row, row, vec],
        out_specs=[row, vec, pl.BlockSpec((1, LANES), lambda i: (0, 0))],
        out_shape=[jax.ShapeDtypeStruct((dm.T, dm.D), f32), jax.ShapeDtypeStruct((1, dm.D), f32),
                   jax.ShapeDtypeStruct((1, LANES), f32)],
        compiler_params=_cp(dimension_semantics=("arbitrary",)),
    )(x, tgt, w.reshape(1, dm.D))


def _conv_pre(u, w_ref, b_ref, row):
    acc = b_ref[...] + w_ref[CONV_WIDTH - 1:CONV_WIDTH, :] * u
    for k in range(CONV_WIDTH - 1):
        acc = acc + w_ref[k:k + 1, :] * _shift_down(u, CONV_WIDTH - 1 - k, row)
    return acc


def _conv_fwd(proj, cw, cb, dm):
    cwid = LANES
    off = dm.c_xbc // cwid

    def body(u_ref, w_ref, b_ref, o_ref):
        u = u_ref[...]
        row = _iota2(u.shape, 0)
        pre = _conv_pre(u, w_ref, b_ref, row)
        o_ref[...] = pre * _sigmoid(pre)

    return pl.pallas_call(
        body, name="conv_fwd", grid=(dm.Bl, dm.CC // cwid),
        in_specs=[pl.BlockSpec((dm.S, cwid), lambda b, j: (b, off + j)),
                  pl.BlockSpec((CONV_WIDTH, cwid), lambda b, j: (0, j)), pl.BlockSpec((1, cwid), lambda b, j: (0, j))],
        out_specs=pl.BlockSpec((dm.S, cwid), lambda b, j: (b, j)),
        out_shape=jax.ShapeDtypeStruct((dm.T, dm.CC), f32), compiler_params=_cp(),
    )(proj, cw, cb.reshape(1, dm.CC))


def _conv_bwd(proj, d_out, cw, cb, dm):
    cwid = LANES
    off = dm.c_xbc // cwid

    def body(u_ref, d_ref, w_ref, b_ref, du_ref, dw_ref, db_ref):
        @pl.when(pl.program_id(1) == 0)
        def _():
            dw_ref[...] = jnp.zeros_like(dw_ref)
            db_ref[...] = jnp.zeros_like(db_ref)

        u = u_ref[...]
        row = _iota2(u.shape, 0)
        pre = _conv_pre(u, w_ref, b_ref, row)
        sg = _sigmoid(pre)
        dpre = d_ref[...] * (sg * (1.0 + pre * (1.0 - sg)))
        du = w_ref[CONV_WIDTH - 1:CONV_WIDTH, :] * dpre
        dw_ref[CONV_WIDTH - 1:CONV_WIDTH, :] += jnp.sum(dpre * u, axis=0, keepdims=True)
        for k in range(CONV_WIDTH - 1):
            sh = CONV_WIDTH - 1 - k
            du = du + w_ref[k:k + 1, :] * _shift_up(dpre, sh, row)
            dw_ref[k:k + 1, :] += jnp.sum(dpre * _shift_down(u, sh, row), axis=0, keepdims=True)
        du_ref[...] = du.astype(du_ref.dtype)
        db_ref[...] += jnp.sum(dpre, axis=0, keepdims=True)

    return pl.pallas_call(
        body, name="conv_bwd", grid=(dm.CC // cwid, dm.Bl),
        in_specs=[pl.BlockSpec((dm.S, cwid), lambda j, b: (b, off + j)), pl.BlockSpec((dm.S, cwid), lambda j, b: (b, j)),
                  pl.BlockSpec((CONV_WIDTH, cwid), lambda j, b: (0, j)), pl.BlockSpec((1, cwid), lambda j, b: (0, j))],
        out_specs=[pl.BlockSpec((dm.S, cwid), lambda j, b: (b, j)), pl.BlockSpec((CONV_WIDTH, cwid), lambda j, b: (0, j)),
                   pl.BlockSpec((1, cwid), lambda j, b: (0, j))],
        out_shape=[jax.ShapeDtypeStruct((dm.T, dm.CC), ACT_DTYPE), jax.ShapeDtypeStruct((CONV_WIDTH, dm.CC), f32),
                   jax.ShapeDtypeStruct((1, dm.CC), f32)],
        compiler_params=_cp(dimension_semantics=("arbitrary", "arbitrary")),
    )(proj, d_out, cw, cb.reshape(1, dm.CC))


def _pad_lanes(v):
    return jnp.pad(v, (0, LANES - v.shape[0])).reshape(1, LANES)


def _softplus(x):
    return jnp.maximum(x, 0.0) + jnp.log(1.0 + jnp.exp(-jnp.abs(x)))


def _dt_prep(proj, dt_bias, a_log, dm):
    off = dm.c_dt // LANES

    def body(r_ref, b_ref, al_ref, dt_ref, cum_ref, cumt_ref):
        dt = _softplus(r_ref[...] + b_ref[...])
        adt = dt * (-jnp.exp(al_ref[...]))
        tril = (_iota2((LANES, LANES), 1) <= _iota2((LANES, LANES), 0)).astype(bf16)
        cum = _exact_nn_left(tril, adt)
        dt_ref[...] = dt
        cum_ref[...] = cum
        cumt_ref[...] = cum.T

    blk = pl.BlockSpec((LANES, LANES), lambda i: (i, 0))
    vec = pl.BlockSpec((1, LANES), lambda i: (0, 0))
    return pl.pallas_call(
        body, name="dt_prep", grid=(dm.T // LANES,),
        in_specs=[pl.BlockSpec((LANES, LANES), lambda i: (i, off)), vec, vec],
        out_specs=[blk, blk, pl.BlockSpec((LANES, LANES), lambda i: (0, i))],
        out_shape=[jax.ShapeDtypeStruct((dm.T, LANES), f32), jax.ShapeDtypeStruct((dm.T, LANES), f32),
                   jax.ShapeDtypeStruct((LANES, dm.T), f32)],
        compiler_params=_cp(),
    )(proj, _pad_lanes(dt_bias), _pad_lanes(a_log))


def _exact_nn_left(u, a):
    hi, mid, lo = _split3(a)
    return _nn(u, hi) + _nn(u, mid) + _nn(u, lo)


def _dt_bwd(proj, dt, dcum, dcum_t, ddt, dt_bias, a_log, dm):
    off = dm.c_dt // LANES

    def body(r_ref, dt_ref, dc_ref, dct_ref, dd_ref, b_ref, al_ref, o_ref, db_ref, da_ref):
        @pl.when(pl.program_id(0) == 0)
        def _():
            db_ref[...] = jnp.zeros_like(db_ref)
            da_ref[...] = jnp.zeros_like(da_ref)

        a = -jnp.exp(al_ref[...])
        triu = (_iota2((LANES, LANES), 1) >= _iota2((LANES, LANES), 0)).astype(bf16)
        dadt = _exact_nn_left(triu, dc_ref[...] + dct_ref[...].T)
        d_dt = dd_ref[...] + dadt * a
        d_raw = d_dt * _sigmoid(r_ref[...] + b_ref[...])
        o_ref[...] = d_raw.astype(o_ref.dtype)
        db_ref[...] += jnp.sum(d_raw, axis=0, keepdims=True)
        da_ref[...] += jnp.sum(dadt * dt_ref[...], axis=0, keepdims=True) * a

    blk = pl.BlockSpec((LANES, LANES), lambda i: (i, 0))
    vec = pl.BlockSpec((1, LANES), lambda i: (0, 0))
    return pl.pallas_call(
        body, name="dt_bwd", grid=(dm.T // LANES,),
        in_specs=[pl.BlockSpec((LANES, LANES), lambda i: (i, off)), blk, blk, pl.BlockSpec((LANES, LANES), lambda i: (0, i)),
                  blk, vec, vec],
        out_specs=[blk, vec, vec],
        out_shape=[jax.ShapeDtypeStruct((dm.T, LANES), ACT_DTYPE), jax.ShapeDtypeStruct((1, LANES), f32),
                   jax.ShapeDtypeStruct((1, LANES), f32)],
        compiler_params=_cp(dimension_semantics=("arbitrary",)),
    )(proj, dt, dcum, dcum_t, ddt, _pad_lanes(dt_bias), _pad_lanes(a_log))


def _ssd_common(dm):
    L = LANES
    tri = _iota2((L, L), 0) >= _iota2((L, L), 1)
    lo = _iota2((L, L), 1) < HEAD
    return tri, lo


def _ssd_fwd(xbc, dt, cum, cumt, d_skip, dm):
    L, W2, hpg = LANES, dm.W2, dm.hpg
    nb = W2 // L

    def body(x_ref, b_ref, c_ref, dt_ref, cum_ref, cumt_ref, dsk_ref, y_ref, hs_ref, h_scr):
        @pl.when(pl.program_id(1) == 0)
        def _():
            h_scr[...] = jnp.zeros_like(h_scr)

        hs_ref[0] = h_scr[...]
        tri, lo = _ssd_common(dm)
        for g in range(SSM_GROUPS):
            bb = _mx(b_ref[:, g * L:(g + 1) * L])
            cb_ = _mx(c_ref[:, g * L:(g + 1) * L])
            cbm = _nt(cb_, bb)
            for i in range(hpg // 2):
                h0 = g * hpg + 2 * i
                h1 = h0 + 1
                sl = slice(h0 * HEAD, h0 * HEAD + L)
                x_p = x_ref[:, sl]
                cum0, cum1 = cum_ref[:, h0:h0 + 1], cum_ref[:, h1:h1 + 1]
                cums = jnp.where(lo, cum0, cum1)
                xdt = x_p * jnp.where(lo, dt_ref[:, h0:h0 + 1], dt_ref[:, h1:h1 + 1])
                tot = jnp.where(lo[0:1], cum_ref[L - 1:L, h0:h0 + 1], cum_ref[L - 1:L, h1:h1 + 1])
                y_p = jnp.zeros((L, L), f32)
                for hh, m in ((h0, lo), (h1, jnp.logical_not(lo))):
                    diff = cum_ref[:, hh:hh + 1] - cumt_ref[hh:hh + 1, :]
                    lm = jnp.where(tri, jnp.exp(jnp.minimum(diff, 0.0)), 0.0)
                    y_p = y_p + _nn(_mx(cbm * lm), _mx(jnp.where(m, xdt, 0.0)))
                hp = h_scr[:, sl]
                y_p = y_p + _nn(cb_, _mx(hp)) * jnp.exp(cums)
                y_p = y_p + x_p * jnp.where(lo[0:1], dsk_ref[0:1, h0:h0 + 1], dsk_ref[0:1, h1:h1 + 1])
                y_ref[:, sl] = y_p
                h_scr[:, sl] = hp * jnp.exp(tot) + _tn(bb, _mx(xdt * jnp.exp(tot - cums)))

    nc = dm.nc
    ob = W2 // (SSM_GROUPS * L)
    blk = pl.BlockSpec((L, L), lambda b, c: (b * nc + c, 0))
    return pl.pallas_call(
        body, name="ssd_fwd", grid=(dm.Bl, nc),
        in_specs=[pl.BlockSpec((L, W2), lambda b, c: (b * nc + c, 0)),
                  pl.BlockSpec((L, SSM_GROUPS * L), lambda b, c: (b * nc + c, ob)),
                  pl.BlockSpec((L, SSM_GROUPS * L), lambda b, c: (b * nc + c, ob + 1)),
                  blk, blk, pl.BlockSpec((L, L), lambda b, c: (0, b * nc + c)), pl.BlockSpec((1, L), lambda b, c: (0, 0))],
        out_specs=[pl.BlockSpec((L, W2), lambda b, c: (b * nc + c, 0)), pl.BlockSpec((1, L, W2), lambda b, c: (b * nc + c, 0, 0))],
        out_shape=[jax.ShapeDtypeStruct((dm.T, W2), f32), jax.ShapeDtypeStruct((dm.Bl * nc, L, W2), f32)],
        scratch_shapes=[pltpu.VMEM((L, W2), f32)],
        compiler_params=_cp(dimension_semantics=("arbitrary", "arbitrary")),
    )(xbc, xbc, xbc, dt, cum, cumt, _pad_lanes(d_skip))


def _ssd_bwd(xbc, dt, cum, cumt, d_skip, hs, dy, dm):
    L, W2, hpg = LANES, dm.W2, dm.hpg
    nc = dm.nc

    def body(x_ref, b_ref, c_ref, dt_ref, cum_ref, cumt_ref, dsk_ref, hs_ref, dy_ref,
             dx_ref, db_ref, dc_ref, ddt_ref, dcum_ref, dcr_ref, dd_ref, dh_scr, lane_cum, lane_dt, lane_d):
        @pl.when(pl.program_id(1) == 0)
        def _():
            dh_scr[...] = jnp.zeros_like(dh_scr)

        @pl.when((pl.program_id(0) == 0) & (pl.program_id(1) == 0))
        def _():
            dd_ref[...] = jnp.zeros_like(dd_ref)

        dcr_ref[...] = jnp.zeros_like(dcr_ref)
        tri, lo = _ssd_common(dm)
        last = _iota2((L, L), 0) == L - 1
        for g in range(SSM_GROUPS):
            gs = slice(g * L, (g + 1) * L)
            bb = _mx(b_ref[:, gs])
            cb_ = _mx(c_ref[:, gs])
            cbm = _nt(cb_, bb)
            dcb = jnp.zeros((L, L), f32)
            dc_g = jnp.zeros((L, L), f32)
            db_g = jnp.zeros((L, L), f32)
            for i in range(hpg // 2):
                h0 = g * hpg + 2 * i
                h1 = h0 + 1
                sl = slice(h0 * HEAD, h0 * HEAD + L)
                x_p = x_ref[:, sl]
                dy_p = dy_ref[:, sl]
                dt_p = jnp.where(lo, dt_ref[:, h0:h0 + 1], dt_ref[:, h1:h1 + 1])
                cums = jnp.where(lo, cum_ref[:, h0:h0 + 1], cum_ref[:, h1:h1 + 1])
                tot = jnp.where(lo[0:1], cum_ref[L - 1:L, h0:h0 + 1], cum_ref[L - 1:L, h1:h1 + 1])
                dsk_p = jnp.where(lo[0:1], dsk_ref[0:1, h0:h0 + 1], dsk_ref[0:1, h1:h1 + 1])
                xdt = x_p * dt_p
                ecum = jnp.exp(cums)
                dec = jnp.exp(tot - cums)
                etot = jnp.exp(tot)
                hp = hs_ref[0, :, sl]
                hp_b = _mx(hp)
                dhn = dh_scr[:, sl]
                dhn_b = _mx(dhn)
                y_off = _nn(cb_, hp_b) * ecum
                dch = _mx(dy_p * ecum)
                dc_g = dc_g + _nt(dch, hp_b)
                dh_off = _tn(cb_, dch)
                bds = _nn(bb, dhn_b)
                xdec = xdt * dec
                db_g = db_g + _nt(_mx(xdec), dhn_b)
                dxdt = bds * dec
                sdec = bds * xdec
                tot_lane = jnp.sum(sdec, axis=0, keepdims=True) + jnp.sum(dhn * hp, axis=0, keepdims=True) * etot
                dh_scr[:, sl] = etot * dhn + dh_off
                rsum = []
                for hh, m in ((h0, lo), (h1, jnp.logical_not(lo))):
                    diff = cum_ref[:, hh:hh + 1] - cumt_ref[hh:hh + 1, :]
                    lm = jnp.where(tri, jnp.exp(jnp.minimum(diff, 0.0)), 0.0)
                    w32 = cbm * lm
                    dyh = _mx(jnp.where(m, dy_p, 0.0))
                    dw = _nt(dyh, _mx(jnp.where(m, xdt, 0.0)))
                    dcb = dcb + dw * lm
                    e = dw * w32
                    rsum.append(jnp.sum(e, axis=1, keepdims=True))
                    dcr_ref[hh:hh + 1, :] = -jnp.sum(e, axis=0, keepdims=True)
                    dxdt = dxdt + _tn(_mx(w32), dyh)
                lane_cum[:, sl] = (dy_p * y_off + jnp.where(lo, rsum[0], rsum[1]) * (1.0 / HEAD) - sdec
                                   + jnp.where(last, tot_lane, 0.0))
                lane_dt[:, sl] = dxdt * x_p
                lane_d[:, sl] = dy_p * x_p
                dx_ref[:, sl] = dxdt * dt_p + dsk_p * dy_p
            dcb_b = _mx(dcb)
            dc_ref[:, gs] = dc_g + _nn(dcb_b, bb)
            db_ref[:, gs] = db_g + _tn(dcb_b, cb_)
        sel = (_iota2((W2, L), 0) // HEAD == _iota2((W2, L), 1)).astype(bf16)
        dcum_ref[...] = _exact_nn(lane_cum[...], sel)
        ddt_ref[...] = _exact_nn(lane_dt[...], sel)
        dd_ref[...] += jnp.sum(_exact_nn(lane_d[...], sel), axis=0, keepdims=True)

    ob = W2 // (SSM_GROUPS * L)

    def rc(b, c):
        return b * nc + (nc - 1 - c)

    blk = pl.BlockSpec((L, L), lambda b, c: (rc(b, c), 0))
    blk_t = pl.BlockSpec((L, L), lambda b, c: (0, rc(b, c)))
    wide = pl.BlockSpec((L, W2), lambda b, c: (rc(b, c), 0))
    grp = pl.BlockSpec((L, SSM_GROUPS * L), lambda b, c: (rc(b, c), 0))
    return pl.pallas_call(
        body, name="ssd_bwd", grid=(dm.Bl, nc),
        in_specs=[wide, pl.BlockSpec((L, SSM_GROUPS * L), lambda b, c: (rc(b, c), ob)),
                  pl.BlockSpec((L, SSM_GROUPS * L), lambda b, c: (rc(b, c), ob + 1)),
                  blk, blk, blk_t, pl.BlockSpec((1, L), lambda b, c: (0, 0)),
                  pl.BlockSpec((1, L, W2), lambda b, c: (rc(b, c), 0, 0)), wide],
        out_specs=[wide, grp, grp, blk, blk, blk_t, pl.BlockSpec((1, L), lambda b, c: (0, 0))],
        out_shape=[jax.ShapeDtypeStruct((dm.T, W2), f32), jax.ShapeDtypeStruct((dm.T, SSM_GROUPS * L), f32),
                   jax.ShapeDtypeStruct((dm.T, SSM_GROUPS * L), f32), jax.ShapeDtypeStruct((dm.T, L), f32),
                   jax.ShapeDtypeStruct((dm.T, L), f32), jax.ShapeDtypeStruct((L, dm.T), f32),
                   jax.ShapeDtypeStruct((1, L), f32)],
        scratch_shapes=[pltpu.VMEM((L, W2), f32)] * 4,
        compiler_params=_cp(dimension_semantics=("arbitrary", "arbitrary")),
    )(xbc, xbc, xbc, dt, cum, cumt, _pad_lanes(d_skip), hs, dy)


def _gnorm_fwd(y, proj, w, dm):
    tr = _pick(dm.T, (256, 128))
    row = pl.BlockSpec((tr, dm.W2), lambda i: (i, 0))

    def body(y_ref, z_ref, w_ref, o_ref):
        z = z_ref[...]
        yg = y_ref[...] * (z * _sigmoid(z))
        r = lax.rsqrt(jnp.mean(yg * yg, axis=-1, keepdims=True) + EPS)
        o_ref[...] = (yg * r * w_ref[...]).astype(o_ref.dtype)

    return pl.pallas_call(
        body, name="gnorm_fwd", grid=(dm.T // tr,), in_specs=[row, row, pl.BlockSpec((1, dm.W2), lambda i: (0, 0))],
        out_specs=row, out_shape=jax.ShapeDtypeStruct((dm.T, dm.W2), ACT_DTYPE), compiler_params=_cp(),
    )(y, proj, w.reshape(1, dm.W2))


def _gnorm_bwd(dn, y, proj, w, dm):
    tr = _pick(dm.T, (256, 128))
    row = pl.BlockSpec((tr, dm.W2), lambda i: (i, 0))
    vec = pl.BlockSpec((1, dm.W2), lambda i: (0, 0))

    def body(dn_ref, y_ref, z_ref, w_ref, dy_ref, dz_ref, dw_ref):
        @pl.when(pl.program_id(0) == 0)
        def _():
            dw_ref[...] = jnp.zeros_like(dw_ref)

        z = z_ref[...]
        sg = _sigmoid(z)
        sz = z * sg
        yv = y_ref[...]
        yg = yv * sz
        r = lax.rsqrt(jnp.mean(yg * yg, axis=-1, keepdims=True) + EPS)
        n = yg * r
        dout = dn_ref[...]
        dnn = dout * w_ref[...]
        dyg = r * (dnn - n * jnp.mean(dnn * n, axis=-1, keepdims=True))
        dy_ref[...] = dyg * sz
        dz_ref[...] = (dyg * yv * (sg * (1.0 + z * (1.0 - sg)))).astype(dz_ref.dtype)
        dw_ref[...] += jnp.sum(dout * n, axis=0, keepdims=True)

    return pl.pallas_call(
        body, name="gnorm_bwd", grid=(dm.T // tr,), in_specs=[row, row, row, vec], out_specs=[row, row, vec],
        out_shape=[jax.ShapeDtypeStruct((dm.T, dm.W2), f32), jax.ShapeDtypeStruct((dm.T, dm.W2), ACT_DTYPE),
                   jax.ShapeDtypeStruct((1, dm.W2), f32)],
        compiler_params=_cp(dimension_semantics=("arbitrary",)),
    )(dn, y, proj, w.reshape(1, dm.W2))


def _pool_mixed(u, g, row):
    s = u
    for k in range(POOL_GROUPS):
        s = jnp.where(k <= g, s + _shift_down(s, 1 << k, row), s)
    w = jnp.left_shift(2, g)
    cnt = jnp.minimum(row + 1, w).astype(f32)
    return s / cnt - u, cnt


def _pool_fwd(proj, pw, scale, dm):
    Dg = dm.Dg
    ou, og = dm.c_pu // Dg, dm.c_pg // Dg

    def body(u_ref, g_ref, w_ref, s_ref, o_ref):
        u = u_ref[...]
        row = _iota2(u.shape, 0)
        mixed, _ = _pool_mixed(u, pl.program_id(1), row)
        lin = _nn(_mx(mixed), _mx(w_ref[0]))
        gt = g_ref[...]
        o_ref[...] = (lin * s_ref[...] * (gt * _sigmoid(gt))).astype(o_ref.dtype)

    return pl.pallas_call(
        body, name="pool_fwd", grid=(dm.Bl, POOL_GROUPS),
        in_specs=[pl.BlockSpec((dm.S, Dg), lambda b, g: (b, ou + g)), pl.BlockSpec((dm.S, Dg), lambda b, g: (b, og + g)),
                  pl.BlockSpec((1, Dg, Dg), lambda b, g: (g, 0, 0)), pl.BlockSpec((1, Dg), lambda b, g: (0, g))],
        out_specs=pl.BlockSpec((dm.S, Dg), lambda b, g: (b, g)),
        out_shape=jax.ShapeDtypeStruct((dm.T, dm.D), ACT_DTYPE), compiler_params=_cp(),
    )(proj, proj, pw, scale.reshape(1, dm.D))


def _pool_bwd(proj, dout, pw, scale, dm):
    Dg = dm.Dg
    ou, og = dm.c_pu // Dg, dm.c_pg // Dg

    def body(u_ref, g_ref, d_ref, w_ref, s_ref, du_ref, dg_ref, dw_ref, ds_ref):
        @pl.when(pl.program_id(1) == 0)
        def _():
            dw_ref[...] = jnp.zeros_like(dw_ref)
            ds_ref[...] = jnp.zeros_like(ds_ref)

        g = pl.program_id(0)
        u = u_ref[...]
        row = _iota2(u.shape, 0)
        mixed, cnt = _pool_mixed(u, g, row)
        mixed_b = _mx(mixed)
        wb = _mx(w_ref[0])
        lin = _nn(mixed_b, wb)
        gt = g_ref[...]
        sg = _sigmoid(gt)
        silu = gt * sg
        d = d_ref[...]
        sc = s_ref[...]
        dlin = d * sc * silu
        ds_ref[...] += jnp.sum(d * lin * silu, axis=0, keepdims=True)
        dg_ref[...] = (d * lin * sc * (sg * (1.0 + gt * (1.0 - sg)))).astype(dg_ref.dtype)
        dlin_b = _mx(dlin)
        dmixed = _nt(dlin_b, wb)
        dw_ref[0] += _tn(mixed_b, dlin_b)
        r = dmixed / cnt
        for k in range(POOL_GROUPS):
            r = jnp.where(k <= g, r + _shift_up(r, 1 << k, row), r)
        du_ref[...] = (r - dmixed).astype(du_ref.dtype)

    return pl.pallas_call(
        body, name="pool_bwd", grid=(POOL_GROUPS, dm.Bl),
        in_specs=[pl.BlockSpec((dm.S, Dg), lambda g, b: (b, ou + g)), pl.BlockSpec((dm.S, Dg), lambda g, b: (b, og + g)),
                  pl.BlockSpec((dm.S, Dg), lambda g, b: (b, g)), pl.BlockSpec((1, Dg, Dg), lambda g, b: (g, 0, 0)),
                  pl.BlockSpec((1, Dg), lambda g, b: (0, g))],
        out_specs=[pl.BlockSpec((dm.S, Dg), lambda g, b: (b, g)), pl.BlockSpec((dm.S, Dg), lambda g, b: (b, g)),
                   pl.BlockSpec((1, Dg, Dg), lambda g, b: (g, 0, 0)), pl.BlockSpec((1, Dg), lambda g, b: (0, g))],
        out_shape=[jax.ShapeDtypeStruct((dm.T, dm.D), ACT_DTYPE), jax.ShapeDtypeStruct((dm.T, dm.D), ACT_DTYPE),
                   jax.ShapeDtypeStruct((POOL_GROUPS, Dg, Dg), f32), jax.ShapeDtypeStruct((1, dm.D), f32)],
        compiler_params=_cp(dimension_semantics=("arbitrary", "arbitrary")),
    )(proj, proj, dout, pw, scale.reshape(1, dm.D))


SB_TILE = 512


def _split2(a):
    hi = a.astype(bf16)
    return hi, (a - hi.astype(f32)).astype(bf16)


def _exact2_nn(a, u):
    hi, lo = _split2(a)
    return _nn(hi, u) + _nn(lo, u)


SB_CUM = 256


def _blocked_sums(a, u, suffix):
    nb = a.shape[1] // SB_CUM
    blocks = [a[:, i * SB_CUM:(i + 1) * SB_CUM] for i in range(nb)]
    tots = [jnp.sum(b, axis=1, keepdims=True) for b in blocks]
    out = []
    for i, b in enumerate(blocks):
        s = _exact2_nn(b, u)
        for t in (tots[i + 1:] if suffix else tots[:i]):
            s = s + t
        out.append(s)
    total = tots[0]
    for t in tots[1:]:
        total = total + t
    return (out[0] if nb == 1 else jnp.concatenate(out, axis=1)), total


def _sb_tile(qb, kb, base, u, suffix, causal):
    z = _nt(qb, kb)
    lb = jnp.minimum(z, 0.0) - jnp.log(1.0 + jnp.exp(-jnp.abs(z)))
    lm = lb - z
    if causal is not None:
        lm = jnp.where(causal, lm, 0.0)
    sums, tot = _blocked_sums(lm, u, suffix)
    att = jnp.exp(lb + base + sums) if suffix else jnp.exp(lb + base - sums)
    if causal is not None:
        att = jnp.where(causal, att, 0.0)
    return lb, att, tot


def _sb_fwd(proj, dm):
    L, S, D, TQ = LANES, dm.S, dm.D, SB_TILE
    oq, og = dm.c_qkv // L, dm.c_sbg // L
    nb = D // L
    scale = HEAD ** -0.5

    def body(q_ref, k_ref, v_ref, g_ref, o_ref, lt_ref, out_ref, k_s, va_s, vb_s):
        lo = _iota2((1, L), 1) < HEAD
        hi = jnp.logical_not(lo)
        k_s[...] = _mx(k_ref[...])
        vf = v_ref[...]
        va_s[...] = _mx(jnp.where(lo, vf, 0.0))
        vb_s[...] = _mx(jnp.where(hi, vf, 0.0))
        before = _iota2((TQ, TQ), 0) > _iota2((TQ, TQ), 1)
        ugt = (_iota2((SB_CUM, SB_CUM), 0) > _iota2((SB_CUM, SB_CUM), 1)).astype(bf16)
        v_s = (va_s, vb_s)

        def qloop(qt, _):
            rows = pl.ds(pl.multiple_of(qt * TQ, TQ), TQ)
            qs = q_ref[rows, :] * scale
            qb = [_mx(jnp.where(m, qs, 0.0)) for m in (lo, hi)]

            def tile(kt, carry, causal):
                runs, acc = carry
                krows = pl.ds(pl.multiple_of(kt * TQ, TQ), TQ)
                k_t = k_s[krows, :]
                new_runs = []
                for h in range(2):
                    _, att, tot = _sb_tile(qb[h], k_t, runs[h], ugt, True, causal)
                    acc = acc + _nn(_mx(att), v_s[h][krows, :])
                    new_runs.append(runs[h] + tot)
                return tuple(new_runs), acc

            zc = jnp.zeros((TQ, 1), f32)
            carry = tile(qt, ((zc, zc), jnp.zeros((TQ, L), f32)), before)
            runs, acc = lax.fori_loop(0, qt, lambda jj, c: tile(qt - 1 - jj, c, None), carry)
            o_ref[rows, :] = acc
            lt_ref[rows, :] = jnp.where(lo, runs[0], runs[1])
            gt = g_ref[rows, :]
            out_ref[rows, :] = (acc * (gt * _sigmoid(gt))).astype(out_ref.dtype)
            return 0

        lax.fori_loop(0, S // TQ, qloop, 0)

    def col(o):
        return pl.BlockSpec((S, L), lambda b, p: (b, o + p))

    return pl.pallas_call(
        body, name="sb_fwd", grid=(dm.Bl, nb),
        in_specs=[col(oq), col(oq + nb), col(oq + 2 * nb), col(og)], out_specs=[col(0), col(0), col(0)],
        out_shape=[jax.ShapeDtypeStruct((dm.T, D), f32), jax.ShapeDtypeStruct((dm.T, D), f32),
                   jax.ShapeDtypeStruct((dm.T, D), ACT_DTYPE)],
        scratch_shapes=[pltpu.VMEM((S, L), MXU_DTYPE)] * 3,
        compiler_params=_cp(),
    )(proj, proj, proj, proj)


def _sb_bwd(proj, o, lt, dsb, dm):
    L, S, D, TQ = LANES, dm.S, dm.D, SB_TILE
    oq, og = dm.c_qkv // L, dm.c_sbg // L
    nb = D // L
    scale = HEAD ** -0.5

    def body(q_ref, k_ref, v_ref, g_ref, o_ref, lt_ref, d_ref, dq_ref, dk_ref, dv_ref, dg_ref,
             k_s, ka_s, kb_s, v_s, dk_acc, dv_acc):
        lo = _iota2((1, L), 1) < HEAD
        hi = jnp.logical_not(lo)
        kf = k_ref[...]
        k_s[...] = _mx(kf)
        ka_s[...] = _mx(jnp.where(lo, kf, 0.0))
        kb_s[...] = _mx(jnp.where(hi, kf, 0.0))
        v_s[...] = _mx(v_ref[...])
        dk_acc[...] = jnp.zeros_like(dk_acc)
        dv_acc[...] = jnp.zeros_like(dv_acc)
        before = _iota2((TQ, TQ), 0) > _iota2((TQ, TQ), 1)
        ule = (_iota2((SB_CUM, SB_CUM), 0) <= _iota2((SB_CUM, SB_CUM), 1)).astype(bf16)
        ult = (_iota2((SB_CUM, SB_CUM), 0) < _iota2((SB_CUM, SB_CUM), 1)).astype(bf16)
        k_m = (ka_s, kb_s)

        def qloop(qt, _):
            rows = pl.ds(pl.multiple_of(qt * TQ, TQ), TQ)
            gt = g_ref[rows, :]
            sg = _sigmoid(gt)
            dsb_b = d_ref[rows, :]
            do = dsb_b * (gt * sg)
            dg_ref[rows, :] = (dsb_b * o_ref[rows, :] * (sg * (1.0 + gt * (1.0 - sg)))).astype(dg_ref.dtype)
            qs = q_ref[rows, :] * scale
            qb = [_mx(jnp.where(m, qs, 0.0)) for m in (lo, hi)]
            dob = [_mx(jnp.where(m, do, 0.0)) for m in (lo, hi)]
            ltot = [lt_ref[rows, 0:1], lt_ref[rows, HEAD:HEAD + 1]]

            def tile(kt, carry, causal):
                runs, rgs, dq = carry
                krows = pl.ds(pl.multiple_of(kt * TQ, TQ), TQ)
                k_t, v_t = k_s[krows, :], v_s[krows, :]
                dk_t = jnp.zeros((TQ, L), f32)
                dv_t = jnp.zeros((TQ, L), f32)
                new_runs, new_rgs = [], []
                for h in range(2):
                    lb, att, tot = _sb_tile(qb[h], k_t, ltot[h] - runs[h], ule, False, causal)
                    gm = att * _nt(dob[h], v_t)
                    gsum, gtot = _blocked_sums(gm, ult, False)
                    pre = rgs[h] + gsum
                    beta = jnp.exp(lb)
                    dz = gm * (1.0 - beta) - pre * beta
                    if causal is not None:
                        dz = jnp.where(causal, dz, 0.0)
                    dz = _mx(dz)
                    dq = dq + _nn(dz, k_m[h][krows, :])
                    dk_t = dk_t + _tn(dz, qb[h])
                    dv_t = dv_t + _tn(_mx(att), dob[h])
                    new_runs.append(runs[h] + tot)
                    new_rgs.append(rgs[h] + gtot)
                dk_acc[krows, :] += dk_t
                dv_acc[krows, :] += dv_t
                return tuple(new_runs), tuple(new_rgs), dq

            zc = jnp.zeros((TQ, 1), f32)
            carry = lax.fori_loop(0, qt, lambda kt, c: tile(kt, c, None), ((zc, zc), (zc, zc), jnp.zeros((TQ, L), f32)))
            _, _, dq = tile(qt, carry, before)
            dq_ref[rows, :] = (dq * scale).astype(dq_ref.dtype)
            return 0

        lax.fori_loop(0, S // TQ, qloop, 0)
        dk_ref[...] = dk_acc[...].astype(dk_ref.dtype)
        dv_ref[...] = dv_acc[...].astype(dv_ref.dtype)

    def col(off):
        return pl.BlockSpec((S, L), lambda b, p: (b, off + p))

    out = jax.ShapeDtypeStruct((dm.T, D), ACT_DTYPE)
    return pl.pallas_call(
        body, name="sb_bwd", grid=(dm.Bl, nb),
        in_specs=[col(oq), col(oq + nb), col(oq + 2 * nb), col(og), col(0), col(0), col(0)],
        out_specs=[col(0), col(0), col(0), col(0)], out_shape=[out, out, out, out],
        scratch_shapes=[pltpu.VMEM((S, L), MXU_DTYPE)] * 4 + [pltpu.VMEM((S, L), f32)] * 2,
        compiler_params=_cp(),
    )(proj, proj, proj, proj, o, lt, dsb)


def _merge_fwd(proj, ys, dm):
    tr = _pick(dm.T, (256, 128))
    ct = 512
    om = dm.c_mrg // ct
    nb = dm.D // ct
    blk = pl.BlockSpec((tr, ct), lambda i, j: (i, j))

    def body(l0, l1, l2, y0, y1, y2, o_ref):
        acc = _sigmoid(l0[...]) * y0[...] + _sigmoid(l1[...]) * y1[...] + _sigmoid(l2[...]) * y2[...]
        o_ref[...] = acc.astype(o_ref.dtype)

    return pl.pallas_call(
        body, name="merge_fwd", grid=(dm.T // tr, nb),
        in_specs=[pl.BlockSpec((tr, ct), functools.partial(lambda i, j, b: (i, om + b * nb + j), b=b)) for b in range(3)] + [blk] * 3,
        out_specs=blk, out_shape=jax.ShapeDtypeStruct((dm.T, dm.D), ACT_DTYPE), compiler_params=_cp(),
    )(proj, proj, proj, *ys)


def _merge_bwd(proj, ys, dmerged, dm):
    tr = _pick(dm.T, (256, 128))
    ct = 512
    om = dm.c_mrg // ct
    nb = dm.D // ct
    blk = pl.BlockSpec((tr, ct), lambda i, j: (i, j))

    def body(l0, l1, l2, y0, y1, y2, d_ref, dy0, dy1, dy2, dl0, dl1, dl2):
        d = d_ref[...]
        for l_ref, y_ref, dy_ref, dl_ref in ((l0, y0, dy0, dl0), (l1, y1, dy1, dl1), (l2, y2, dy2, dl2)):
            sg = _sigmoid(l_ref[...])
            dy_ref[...] = (d * sg).astype(dy_ref.dtype)
            dl_ref[...] = (d * y_ref[...] * sg * (1.0 - sg)).astype(dl_ref.dtype)

    out = jax.ShapeDtypeStruct((dm.T, dm.D), ACT_DTYPE)
    return pl.pallas_call(
        body, name="merge_bwd", grid=(dm.T // tr, nb),
        in_specs=[pl.BlockSpec((tr, ct), functools.partial(lambda i, j, b: (i, om + b * nb + j), b=b)) for b in range(3)] + [blk] * 4,
        out_specs=[blk] * 6, out_shape=[out] * 6, compiler_params=_cp(),
    )(proj, proj, proj, *ys, dmerged)


def _layer_fwd(x, p, dm):
    h = _rms_fwd(x, p["norm_w"], dm)
    proj = _mm(h, p["w_in"], name="mm_in")
    xbc = _conv_fwd(proj, p["conv_w"], p["conv_b"], dm)
    dt, cum, cumt = _dt_prep(proj, p["dt_bias"], p["a_log"], dm)
    y, hs = _ssd_fwd(xbc, dt, cum, cumt, p["d_skip"], dm)
    ssm_n = _gnorm_fwd(y, proj, p["ssm_norm_w"], dm)
    pool_o = _pool_fwd(proj, p["pool_w"], p["pool_scale"], dm)
    o, lt, sb_o = _sb_fwd(proj, dm)
    ys = (_mm(ssm_n, p["w_proj_ssm"], name="mm_ps"), _mm(pool_o, p["w_proj_pool"], name="mm_pp"),
          _mm(sb_o, p["w_proj_sb"], name="mm_pb"))
    merged = _merge_fwd(proj, ys, dm)
    x_next = _mm(merged, p["w_out"], res=x, name="mm_out")
    saved = dict(x=x, h=h, proj=proj, xbc=xbc, dt=dt, cum=cum, cumt=cumt, y=y, hs=hs, ssm_n=ssm_n, pool_o=pool_o, o=o,
                 lt=lt, sb_o=sb_o, ys=ys, merged=merged)
    return x_next, saved


def _layer_bwd(dx_out, p, sv, dm):
    g = {}
    proj = sv["proj"]
    dmerged = _mm(dx_out, p["w_out"], tb=True, name="mm_dmerged")
    g["w_out"] = _mm(sv["merged"], dx_out, ta=True, name="mm_dwout")
    dy0, dy1, dy2, dl0, dl1, dl2 = _merge_bwd(proj, sv["ys"], dmerged, dm)
    d_ssm_n = _mm(dy0, p["w_proj_ssm"], tb=True, name="mm_dssm")
    g["w_proj_ssm"] = _mm(sv["ssm_n"], dy0, ta=True, name="mm_dwps")
    d_pool_o = _mm(dy1, p["w_proj_pool"], tb=True, name="mm_dpool")
    g["w_proj_pool"] = _mm(sv["pool_o"], dy1, ta=True, name="mm_dwpp")
    d_sb_o = _mm(dy2, p["w_proj_sb"], tb=True, name="mm_dsb")
    g["w_proj_sb"] = _mm(sv["sb_o"], dy2, ta=True, name="mm_dwpb")
    dy, dz, g["ssm_norm_w"] = _gnorm_bwd(d_ssm_n, sv["y"], proj, p["ssm_norm_w"], dm)
    dxs, db, dc, ddt, dcum, dcum_t, dd = _ssd_bwd(sv["xbc"], sv["dt"], sv["cum"], sv["cumt"], p["d_skip"], sv["hs"], dy, dm)
    g["d_skip"] = dd
    d_dtraw, g["dt_bias"], g["a_log"] = _dt_bwd(proj, sv["dt"], dcum, dcum_t, ddt, p["dt_bias"], p["a_log"], dm)
    dxbc, g["conv_w"], g["conv_b"] = _conv_bwd(proj, jnp.concatenate([dxs, db, dc], axis=1), p["conv_w"], p["conv_b"], dm)
    dpu, dpg, g["pool_w"], g["pool_scale"] = _pool_bwd(proj, d_pool_o, p["pool_w"], p["pool_scale"], dm)
    dq, dk, dv, dsg = _sb_bwd(proj, sv["o"], sv["lt"], d_sb_o, dm)
    dproj = jnp.concatenate([dz, dxbc, dpu, dpg, dq, dk, dv, dsg, dl0, dl1, dl2, d_dtraw,
                             jnp.zeros((dm.T, DT_PAD - LANES), ACT_DTYPE)], axis=1)
    dh = _mm(dproj, p["w_in"], tb=True, name="mm_dh")
    g["w_in"] = _mm(sv["h"], dproj, ta=True, name="mm_dwin")
    dx, g["norm_w"] = _rms_bwd(sv["x"], dh, dx_out, p["norm_w"], dm)
    return dx, g


def _local_step(x, tgt, layers, final_norm_w, dm):
    saved = []
    for p in layers:
        x, sv = _layer_fwd(x, p, dm)
        saved.append(sv)
    dx, dfn, loss = _loss_head(x, tgt, final_norm_w, dm)
    grads = [None] * len(layers)
    for l in reversed(range(len(layers))):
        dx, grads[l] = _layer_bwd(dx, layers[l], saved[l], dm)
    return loss, dx, grads, dfn


def _row_tile(rows, cols):
    cap = max(8, (1 << 18) // cols)
    for t in (1024, 512, 256, 128, 64, 32, 16, 8):
        if t <= cap and rows % t == 0:
            return t
    return rows


def _adamw(w, g, m, v, name):
    rows, cols = w.shape
    tr = _row_tile(rows, cols)
    c1 = 1.0 - ADAM_B1 ** ADAM_STEP
    c2 = 1.0 - ADAM_B2 ** ADAM_STEP

    def body(w_ref, g_ref, m_ref, v_ref, d_ref, mo_ref, vo_ref):
        gv = g_ref[...]
        mn = ADAM_B1 * m_ref[...] + (1.0 - ADAM_B1) * gv
        vn = ADAM_B2 * v_ref[...] + (1.0 - ADAM_B2) * (gv * gv)
        d_ref[...] = -ADAM_LR * ((mn / c1) / (jnp.sqrt(vn / c2) + ADAM_EPS) + ADAM_WD * w_ref[...])
        mo_ref[...] = mn
        vo_ref[...] = vn

    blk = pl.BlockSpec((tr, cols), lambda i: (i, 0))
    out = jax.ShapeDtypeStruct((rows, cols), f32)
    return pl.pallas_call(body, name=name, grid=(rows // tr,), in_specs=[blk] * 4, out_specs=[blk] * 3, out_shape=[out] * 3,
                          compiler_params=_cp())(w, g, m, v)


def _sum_parts(a, parts, name):
    n, rows, cols = parts.shape
    tr = _row_tile(rows, cols)

    def body(a_ref, p_ref, o_ref):
        acc = a_ref[...]
        for k in range(n):
            acc = acc + p_ref[k].astype(f32)
        o_ref[...] = acc

    blk = pl.BlockSpec((tr, cols), lambda i: (i, 0))
    return pl.pallas_call(body, name=name, grid=(rows // tr,), in_specs=[blk, pl.BlockSpec((n, tr, cols), lambda i: (0, i, 0))],
                          out_specs=blk, out_shape=jax.ShapeDtypeStruct((rows, cols), f32), compiler_params=_cp())(a, parts)


ICI_KINDS = ("y", "x", "xy")
HBM_SPEC = pl.BlockSpec(memory_space=pltpu.HBM)


def _me():
    return lax.axis_index("x"), lax.axis_index("y"), lax.axis_index("c")


def _peer(kind):
    x, y, c = _me()
    return {"c": (x, y, 1 - c), "y": (x, 1 - y, c), "x": (1 - x, y, c), "xy": (1 - x, 1 - y, c)}[kind]


def _peer_chip(kind):
    x, y, _ = _me()
    return {"y": 2 * x + (1 - y), "x": 2 * (1 - x) + y, "xy": 2 * (1 - x) + (1 - y)}[kind]


def _exchange(send, kinds, name):
    n = len(kinds)

    def body(src, dst, ssem, rsem):
        cps = [pltpu.make_async_remote_copy(src_ref=src.at[k], dst_ref=dst.at[k], send_sem=ssem.at[k], recv_sem=rsem.at[k],
                                            device_id=_peer(kind), device_id_type=MESH) for k, kind in enumerate(kinds)]
        for cp in cps:
            cp.start()
        for cp in cps:
            cp.wait()

    return pl.pallas_call(
        body, name=name, out_shape=jax.ShapeDtypeStruct(send.shape, send.dtype), in_specs=[HBM_SPEC], out_specs=HBM_SPEC,
        scratch_shapes=[pltpu.SemaphoreType.DMA((n,)), pltpu.SemaphoreType.DMA((n,))],
    )(send)


def _allgather_shards(pk):
    ly = pk.shape[0]
    lh = ly // 2

    def body(src, out, ssem, rsem, lsem):
        x, y, c = _me()
        j_me = 2 * x + y
        mine = pl.ds(c * lh, lh)
        theirs = pl.ds((1 - c) * lh, lh)
        local = pltpu.make_async_copy(src, out.at[j_me], lsem)
        local.start()

        def ici(k, kind, j_src):
            return pltpu.make_async_remote_copy(src_ref=src.at[mine], dst_ref=out.at[j_src, mine], send_sem=ssem.at[k],
                                                recv_sem=rsem.at[k], device_id=_peer(kind), device_id_type=MESH)

        def d2d(k, j_src, half):
            return pltpu.make_async_remote_copy(src_ref=out.at[j_src, half], dst_ref=out.at[j_src, half], send_sem=ssem.at[3 + k],
                                                recv_sem=rsem.at[3 + k], device_id=_peer("c"), device_id_type=MESH)

        first = [ici(k, kind, j_me) for k, kind in enumerate(ICI_KINDS)]
        for cp in first:
            cp.start()
        passed = []
        for k, kind in enumerate(ICI_KINDS):
            ici(k, kind, _peer_chip(kind)).wait_recv()
            fwd = d2d(k, _peer_chip(kind), mine)
            fwd.start()
            passed.append(fwd)
        for k, kind in enumerate(ICI_KINDS):
            d2d(k, _peer_chip(kind), theirs).wait_recv()
        for cp in first + passed:
            cp.wait_send()
        local.wait()

    return pl.pallas_call(
        body, name="allgather_shards", out_shape=jax.ShapeDtypeStruct((4,) + pk.shape, pk.dtype), in_specs=[HBM_SPEC],
        out_specs=HBM_SPEC,
        scratch_shapes=[pltpu.SemaphoreType.DMA((6,)), pltpu.SemaphoreType.DMA((6,)), pltpu.SemaphoreType.DMA],
    )(pk)


def _allreduce_small(v):
    rows = v.shape[0]

    def body(v_ref, o_ref, buf, ssem, rsem):
        x, y, c = _me()
        me = 4 * x + 2 * y + c
        buf[0] = v_ref[...]
        cps = []
        for k in range(1, 8):
            peer = (1 - x if k & 4 else x, 1 - y if k & 2 else y, 1 - c if k & 1 else c)
            cps.append(pltpu.make_async_remote_copy(src_ref=v_ref, dst_ref=buf.at[k], send_sem=ssem.at[k - 1], recv_sem=rsem.at[k - 1],
                                                    device_id=peer, device_id_type=MESH))
        for cp in cps:
            cp.start()
        for cp in cps:
            cp.wait()
        acc = buf[jnp.bitwise_xor(me, 0)]
        for d in range(1, 8):
            acc = acc + buf[jnp.bitwise_xor(me, d)]
        o_ref[...] = acc

    vm = pl.BlockSpec(memory_space=pltpu.VMEM)
    return pl.pallas_call(
        body, name="allreduce_small", out_shape=jax.ShapeDtypeStruct(v.shape, f32), in_specs=[vm], out_specs=vm,
        scratch_shapes=[pltpu.VMEM((8, rows, LANES), f32), pltpu.SemaphoreType.DMA((7,)), pltpu.SemaphoreType.DMA((7,))],
    )(v)


SHARDED = ("w_in", "w_proj_ssm", "w_proj_pool", "w_proj_sb", "w_out", "pool_w", "conv_w")
SMALL = ("norm_w", "conv_b", "dt_bias", "a_log", "d_skip", "ssm_norm_w", "pool_scale")


def _pack_rows(dm):
    c4 = dm.IN_COLS // 4
    conv_rows = -(-(2 * dm.CC) // dm.D)
    rows = dict(w_in=c4, w_proj_ssm=dm.W2 // 4, w_proj_pool=dm.D // 4, w_proj_sb=dm.D // 4, w_out=dm.D // 4,
                pool_w=dm.D // 16, conv_w=conv_rows)
    total = sum(rows[n] for n in SHARDED)
    return rows, -(-total // 16) * 16


def _pack(sh, dm, dtype, conv_bits):
    rows, r_tot = _pack_rows(dm)
    ly = sh["w_in"].shape[0]
    parts = []
    for n in SHARDED:
        a = sh[n]
        if n == "conv_w":
            a = lax.bitcast_convert_type(a.astype(f32), bf16) if conv_bits else a.astype(dtype)
            a = a.reshape(ly, -1)
            a = jnp.pad(a, ((0, 0), (0, rows[n] * dm.D - a.shape[1])))
        else:
            a = a.astype(dtype)
        parts.append(a.reshape(ly, rows[n], dm.D))
    used = sum(rows[n] for n in SHARDED)
    parts.append(jnp.zeros((ly, r_tot - used, dm.D), parts[0].dtype))
    return jnp.concatenate(parts, axis=1)


def _unpack(pk, dm, conv_bits):
    rows, _ = _pack_rows(dm)
    lead = pk.shape[:-2]
    c4 = dm.IN_COLS // 4
    shapes = dict(w_in=(dm.D, c4), w_proj_ssm=(dm.W2 // 4, dm.D), w_proj_pool=(dm.D // 4, dm.D), w_proj_sb=(dm.D // 4, dm.D),
                  w_out=(dm.D // 4, dm.D), pool_w=(POOL_GROUPS, dm.Dg // 4, dm.Dg), conv_w=(CONV_WIDTH, dm.CC // 4))
    out, r0 = {}, 0
    for n in SHARDED:
        a = pk[..., r0:r0 + rows[n], :].reshape(lead + (rows[n] * dm.D,))
        r0 += rows[n]
        if n == "conv_w":
            if conv_bits:
                a = lax.bitcast_convert_type(a[..., :2 * dm.CC].reshape(lead + (dm.CC, 2)), f32)
            else:
                a = a[..., :dm.CC]
        out[n] = a.reshape(lead + shapes[n])
    return out


def _join_shards(sh, dm):
    full = {}
    w = jnp.moveaxis(sh["w_in"], 0, 2)
    full["w_in"] = _permute_cols(w.reshape(w.shape[0], dm.D, dm.IN_COLS), dm)
    for n in ("w_proj_ssm", "w_proj_pool", "w_proj_sb", "w_out"):
        a = jnp.moveaxis(sh[n], 0, 1)
        full[n] = a.reshape(a.shape[0], -1, dm.D)
    a = jnp.moveaxis(sh["pool_w"], 0, 2)
    full["pool_w"] = a.reshape(a.shape[0], POOL_GROUPS, dm.Dg, dm.Dg)
    a = jnp.moveaxis(sh["conv_w"], 0, 2)
    full["conv_w"] = a.reshape(a.shape[0], CONV_WIDTH, dm.CC)
    return full


def _split_shards(full, dm):
    sh = {}
    w = _unpermute_cols(full["w_in"], dm)
    sh["w_in"] = jnp.moveaxis(w.reshape(w.shape[0], dm.D, 4, dm.IN_COLS // 4), 2, 0)
    for n in ("w_proj_ssm", "w_proj_pool", "w_proj_sb", "w_out"):
        a = full[n]
        sh[n] = jnp.moveaxis(a.reshape(a.shape[0], 4, a.shape[1] // 4, dm.D), 1, 0)
    a = full["pool_w"]
    sh["pool_w"] = jnp.moveaxis(a.reshape(a.shape[0], POOL_GROUPS, 4, dm.Dg // 4, dm.Dg), 2, 0)
    a = full["conv_w"]
    sh["conv_w"] = jnp.moveaxis(a.reshape(a.shape[0], CONV_WIDTH, 4, dm.CC // 4), 2, 0)
    return sh


def _reduce_scatter(full_grads, dm):
    _, _, c = _me()
    x, y, _ = _me()
    j_me = 2 * x + y
    sh = _split_shards(full_grads, dm)
    pk = jnp.stack([_pack({n: sh[n][j] for n in SHARDED}, dm, f32, False) for j in range(4)])
    _, ly, r, d = pk.shape
    lh = ly // 2
    pk = pk.reshape(4, 2, lh * r, d)
    mine = lax.dynamic_index_in_dim(pk, c, 1, keepdims=False)
    theirs = lax.dynamic_index_in_dim(pk, 1 - c, 1, keepdims=False).astype(WIRE_DTYPE)
    got = _exchange(theirs[None], ("c",), "rs_pair")
    s1 = _sum_parts(mine.reshape(4 * lh * r, d), got.reshape(1, 4 * lh * r, d), "rs_pair_sum").reshape(4, lh * r, d)
    send = jnp.stack([lax.dynamic_index_in_dim(s1, jnp.bitwise_xor(j_me, m), 0, keepdims=False) for m in (1, 2, 3)])
    got = _exchange(send.astype(WIRE_DTYPE), ICI_KINDS, "rs_ici")
    red = _sum_parts(lax.dynamic_index_in_dim(s1, j_me, 0, keepdims=False), got, "rs_ici_sum")
    sib = _exchange(red[None], ("c",), "rs_sibling")[0]
    both = jnp.where(c == 0, jnp.stack([red, sib]), jnp.stack([sib, red])).reshape(ly, r, d)
    return _unpack(both, dm, False)


def _flatten_small(parts):
    flat = jnp.concatenate([p.reshape(-1).astype(f32) for p in parts])
    rows = -(-flat.shape[0] // (8 * LANES)) * 8
    return jnp.pad(flat, (0, rows * LANES - flat.shape[0])).reshape(rows, LANES)


def _unflatten_small(buf, shapes):
    flat = buf.reshape(-1)
    out, o = [], 0
    for s in shapes:
        n = math.prod(s)
        out.append(flat[o:o + n].reshape(s))
        o += n
    return out


def kernel(x, norm_w, w_in, conv_w, conv_b, dt_bias, a_log, d_skip, ssm_norm_w, pool_w, pool_scale, w_proj_ssm, w_proj_pool, w_proj_sb, w_out, final_norm_w, loss_target, m_norm_w, m_w_in, m_conv_w, m_conv_b, m_dt_bias, m_a_log, m_d_skip, m_ssm_norm_w, m_pool_w, m_pool_scale, m_w_proj_ssm, m_w_proj_pool, m_w_proj_sb, m_w_out, m_final_norm_w, v_norm_w, v_w_in, v_conv_w, v_conv_b, v_dt_bias, v_a_log, v_d_skip, v_ssm_norm_w, v_pool_w, v_pool_scale, v_w_proj_ssm, v_w_proj_pool, v_w_proj_sb, v_w_out, v_final_norm_w):
    names = ("norm_w", "w_in", "conv_w", "conv_b", "dt_bias", "a_log", "d_skip", "ssm_norm_w", "pool_w", "pool_scale",
             "w_proj_ssm", "w_proj_pool", "w_proj_sb", "w_out", "final_norm_w")
    w = dict(zip(names, (norm_w, w_in, conv_w, conv_b, dt_bias, a_log, d_skip, ssm_norm_w, pool_w, pool_scale, w_proj_ssm,
                         w_proj_pool, w_proj_sb, w_out, final_norm_w)))
    m = dict(zip(names, (m_norm_w, m_w_in, m_conv_w, m_conv_b, m_dt_bias, m_a_log, m_d_skip, m_ssm_norm_w, m_pool_w, m_pool_scale,
                         m_w_proj_ssm, m_w_proj_pool, m_w_proj_sb, m_w_out, m_final_norm_w)))
    v = dict(zip(names, (v_norm_w, v_w_in, v_conv_w, v_conv_b, v_dt_bias, v_a_log, v_d_skip, v_ssm_norm_w, v_pool_w, v_pool_scale,
                         v_w_proj_ssm, v_w_proj_pool, v_w_proj_sb, v_w_out, v_final_norm_w)))
    bl, s, d = x.shape
    dm = Dims(bl, s, d)
    ly = norm_w.shape[0]

    gathered = _allgather_shards(_pack({n: w[n] for n in SHARDED}, dm, WIRE_DTYPE, True))
    full = _join_shards(_unpack(gathered, dm, True), dm)
    layers = []
    for l in range(ly):
        p = {n: full[n][l] for n in SHARDED}
        p.update({n: w[n][l] for n in SMALL})
        layers.append(p)

    loss_part, dx, grads, dfn = _local_step(x.reshape(dm.T, d), loss_target.reshape(dm.T, d), layers, final_norm_w, dm)

    h = dm.H
    small_parts = [loss_part]
    small_shapes = [(1, LANES)]
    for n in SMALL:
        if n in ("dt_bias", "a_log", "d_skip"):
            small_parts.append(jnp.stack([g[n][0, :h] for g in grads]))
        else:
            small_parts.append(jnp.stack([g[n][0] for g in grads]))
        small_shapes.append(w[n].shape)
    small_parts.append(dfn[0])
    small_shapes.append(final_norm_w.shape)
    red_small = _allreduce_small(_flatten_small(small_parts))
    small_g = _unflatten_small(red_small, small_shapes)
    loss = small_g[0][0, 0]
    g_out = dict(zip(SMALL + ("final_norm_w",), small_g[1:]))

    g_out.update(_reduce_scatter({n: jnp.stack([g[n] for g in grads]) for n in SHARDED}, dm))

    small_names = SMALL + ("final_norm_w",)
    zero_row = jnp.zeros((1, LANES), f32)
    pack_small = lambda t: _flatten_small([zero_row] + [t[n] for n in small_names])
    ds, ms, vs = _adamw(pack_small(w), red_small, pack_small(m), pack_small(v), "adamw_small")
    delta, new_m, new_v = {}, {}, {}
    for tgt, buf in ((delta, ds), (new_m, ms), (new_v, vs)):
        tgt.update(zip(small_names, _unflatten_small(buf, small_shapes)[1:]))
    for n in SHARDED:
        shp = w[n].shape
        two = (math.prod(shp[:-1]), shp[-1])
        dd, mm, vv = _adamw(w[n].reshape(two), g_out[n].reshape(two), m[n].reshape(two), v[n].reshape(two), "adamw_" + n)
        delta[n], new_m[n], new_v[n] = dd.reshape(shp), mm.reshape(shp), vv.reshape(shp)
        g_out[n] = g_out[n].reshape(shp)

    return (loss, dx.reshape(bl, s, d), *[g_out[n] for n in names], *[delta[n] for n in names],
            *[new_m[n] for n in names], *[new_v[n] for n in names])
```

```python
import functools
import math

import jax
import jax.numpy as jnp
from jax import lax
from jax.experimental import pallas as pl
from jax.experimental.pallas import tpu as pltpu

f32 = jnp.float32
bf16 = jnp.bfloat16
MXU_DTYPE = jnp.bfloat16
ACT_DTYPE = jnp.bfloat16
WIRE_DTYPE = jnp.bfloat16

EPS = 1e-6
LANES = 128
HEAD = 64
SSM_GROUPS = 2
CONV_WIDTH = 4
POOL_GROUPS = 4
DT_PAD = 512
N_BRANCHES = 3
VMEM_LIMIT = 56 * 1024 * 1024

ADAM_LR, ADAM_B1, ADAM_B2, ADAM_EPS, ADAM_WD, ADAM_STEP = 0.001, 0.9, 0.999, 1e-08, 0.01, 10
MESH = pl.DeviceIdType.MESH


def _cp(**kw):
    return pltpu.CompilerParams(vmem_limit_bytes=VMEM_LIMIT, **kw)


def _pick(n, prefs):
    for p in prefs:
        if n % p == 0:
            return p
    return n


def _dg(a, b, ca, cb):
    return lax.dot_general(a, b, (((ca,), (cb,)), ((), ())), preferred_element_type=f32)


def _nn(a, b):
    return _dg(a, b, 1, 0)


def _nt(a, b):
    return _dg(a, b, 1, 1)


def _tn(a, b):
    return _dg(a, b, 0, 0)


def _mx(a):
    return a.astype(MXU_DTYPE)


def _split3(a):
    hi = a.astype(bf16)
    r = a - hi.astype(f32)
    mid = r.astype(bf16)
    lo = (r - mid.astype(f32)).astype(bf16)
    return hi, mid, lo


def _exact_nn(a, u):
    hi, mid, lo = _split3(a)
    return _nn(hi, u) + _nn(mid, u) + _nn(lo, u)


def _iota2(shape, dim):
    return lax.broadcasted_iota(jnp.int32, shape, dim)


def _sigmoid(x):
    return 1.0 / (1.0 + jnp.exp(-x))


def _shift_down(v, sh, row):
    return jnp.where(row >= sh, pltpu.roll(v, sh, 0), 0.0)


def _shift_up(v, sh, row):
    n = v.shape[0]
    return jnp.where(row < n - sh, pltpu.roll(v, n - sh, 0), 0.0)


class Dims:
    def __init__(self, bl, s, d):
        self.Bl, self.S, self.D = bl, s, d
        self.T = bl * s
        self.W2 = 2 * d
        self.H = self.W2 // HEAD
        self.hpg = self.H // SSM_GROUPS
        self.CC = self.W2 + 2 * SSM_GROUPS * LANES
        self.Dg = d // POOL_GROUPS
        self.nc = s // LANES
        self.o_dt = self.W2 + self.CC
        self.IN_COLS = 13 * d + 2 * SSM_GROUPS * LANES + self.H
        self.c_z = 0
        self.c_xbc = self.W2
        self.c_pu = self.W2 + self.CC
        self.c_pg = self.c_pu + d
        self.c_qkv = self.c_pg + d
        self.c_sbg = self.c_qkv + 3 * d
        self.c_mrg = self.c_sbg + d
        self.c_dt = self.c_mrg + 3 * d
        self.NP = self.c_dt + DT_PAD
        assert self.c_dt == self.IN_COLS - self.H
        assert s % LANES == 0 and d % 512 == 0 and self.H <= LANES


def _permute_cols(w, dm):
    pad = jnp.zeros(w.shape[:-1] + (DT_PAD - dm.H,), w.dtype)
    return jnp.concatenate([w[..., :dm.o_dt], w[..., dm.o_dt + dm.H:], w[..., dm.o_dt:dm.o_dt + dm.H], pad], axis=-1)


def _unpermute_cols(w, dm):
    return jnp.concatenate([w[..., :dm.o_dt], w[..., dm.c_dt:dm.c_dt + dm.H], w[..., dm.o_dt:dm.c_dt]], axis=-1)


def _mm(a, b, *, ta=False, tb=False, out_dtype=f32, res=None, name):
    M, K = (a.shape[1], a.shape[0]) if ta else a.shape
    N = b.shape[0] if tb else b.shape[1]
    tm = _pick(M, (1024, 512, 256, 128))
    tn = _pick(N, (1024, 512, 256, 128))
    tk = _pick(K, (1024, 512, 256, 128))
    nk = K // tk

    def body(*refs):
        if res is None:
            a_ref, b_ref, o_ref, acc = refs
        else:
            a_ref, b_ref, r_ref, o_ref, acc = refs
        k = pl.program_id(2)

        @pl.when(k == 0)
        def _():
            acc[...] = jnp.zeros_like(acc)

        acc[...] += _dg(_mx(a_ref[...]), _mx(b_ref[...]), 0 if ta else 1, 1 if tb else 0)

        @pl.when(k == nk - 1)
        def _():
            v = acc[...]
            if res is not None:
                v = v + r_ref[...]
            o_ref[...] = v.astype(out_dtype)

    a_spec = pl.BlockSpec((tk, tm), lambda i, j, k: (k, i)) if ta else pl.BlockSpec((tm, tk), lambda i, j, k: (i, k))
    b_spec = pl.BlockSpec((tn, tk), lambda i, j, k: (j, k)) if tb else pl.BlockSpec((tk, tn), lambda i, j, k: (k, j))
    o_spec = pl.BlockSpec((tm, tn), lambda i, j, k: (i, j))
    in_specs = [a_spec, b_spec] + ([o_spec] if res is not None else [])
    args = (a, b) + ((res,) if res is not None else ())
    return pl.pallas_call(
        body, name=name, grid=(M // tm, N // tn, nk), in_specs=in_specs, out_specs=o_spec,
        out_shape=jax.ShapeDtypeStruct((M, N), out_dtype), scratch_shapes=[pltpu.VMEM((tm, tn), f32)],
        compiler_params=_cp(dimension_semantics=("parallel", "parallel", "arbitrary")),
    )(*args)


def _rms_fwd(x, w, dm):
    tr = _pick(dm.T, (256, 128))

    def body(x_ref, w_ref, o_ref):
        xf = x_ref[...]
        r = lax.rsqrt(jnp.mean(xf * xf, axis=-1, keepdims=True) + EPS)
        o_ref[...] = (xf * r * w_ref[...]).astype(o_ref.dtype)

    return pl.pallas_call(
        body, name="rms_fwd", grid=(dm.T // tr,),
        in_specs=[pl.BlockSpec((tr, dm.D), lambda i: (i, 0)), pl.BlockSpec((1, dm.D), lambda i: (0, 0))],
        out_specs=pl.BlockSpec((tr, dm.D), lambda i: (i, 0)),
        out_shape=jax.ShapeDtypeStruct((dm.T, dm.D), ACT_DTYPE), compiler_params=_cp(),
    )(x, w.reshape(1, dm.D))


def _rms_bwd(x, dh, dres, w, dm):
    tr = _pick(dm.T, (256, 128))

    def body(x_ref, dh_ref, dr_ref, w_ref, dx_ref, dw_ref):
        @pl.when(pl.program_id(0) == 0)
        def _():
            dw_ref[...] = jnp.zeros_like(dw_ref)

        xf = x_ref[...]
        r = lax.rsqrt(jnp.mean(xf * xf, axis=-1, keepdims=True) + EPS)
        xh = xf * r
        dh_ = dh_ref[...]
        dxh = dh_ * w_ref[...]
        dx_ref[...] = dr_ref[...] + r * (dxh - xh * jnp.mean(dxh * xh, axis=-1, keepdims=True))
        dw_ref[...] += jnp.sum(dh_ * xh, axis=0, keepdims=True)

    row = pl.BlockSpec((tr, dm.D), lambda i: (i, 0))
    vec = pl.BlockSpec((1, dm.D), lambda i: (0, 0))
    return pl.pallas_call(
        body, name="rms_bwd", grid=(dm.T // tr,), in_specs=[row, row, row, vec], out_specs=[row, vec],
        out_shape=[jax.ShapeDtypeStruct((dm.T, dm.D), f32), jax.ShapeDtypeStruct((1, dm.D), f32)],
        compiler_params=_cp(dimension_semantics=("arbitrary",)),
    )(x, dh, dres, w.reshape(1, dm.D))


def _loss_head(x, tgt, w, dm):
    tr = _pick(dm.T, (256, 128))

    def body(x_ref, t_ref, w_ref, dx_ref, dw_ref, ls_ref):
        @pl.when(pl.program_id(0) == 0)
        def _():
            dw_ref[...] = jnp.zeros_like(dw_ref)
            ls_ref[...] = jnp.zeros_like(ls_ref)

        xf = x_ref[...]
        r = lax.rsqrt(jnp.mean(xf * xf, axis=-1, keepdims=True) + EPS)
        xh = xf * r
        err = xh * w_ref[...] - t_ref[...]
        per_tok = jnp.mean(err * err, axis=-1, keepdims=True)
        ls_ref[...] += 0.5 * jnp.sum(per_tok, axis=0, keepdims=True)
        dy = err * (1.0 / dm.D)
        dxh = dy * w_ref[...]
        dx_ref[...] = r * (dxh - xh * jnp.mean(dxh * xh, axis=-1, keepdims=True))
        dw_ref[...] += jnp.sum(dy * xh, axis=0, keepdims=True)

    row = pl.BlockSpec((tr, dm.D), lambda i: (i, 0))
    vec = pl.BlockSpec((1, dm.D), lambda i: (0, 0))
    return pl.pallas_call(
        body, name="loss_head", grid=(dm.T // tr,), in_specs=[row, row, vec],
        out_specs=[row, vec, pl.BlockSpec((1, LANES), lambda i: (0, 0))],
        out_shape=[jax.ShapeDtypeStruct((dm.T, dm.D), f32), jax.ShapeDtypeStruct((1, dm.D), f32),
                   jax.ShapeDtypeStruct((1, LANES), f32)],
        compiler_params=_cp(dimension_semantics=("arbitrary",)),
    )(x, tgt, w.reshape(1, dm.D))


def _conv_pre(u, w_ref, b_ref, row):
    acc = b_ref[...] + w_ref[CONV_WIDTH - 1:CONV_WIDTH, :] * u
    for k in range(CONV_WIDTH - 1):
        acc = acc + w_ref[k:k + 1, :] * _shift_down(u, CONV_WIDTH - 1 - k, row)
    return acc


def _conv_fwd(proj, cw, cb, dm):
    cwid = LANES
    off = dm.c_xbc // cwid

    def body(u_ref, w_ref, b_ref, o_ref):
        u = u_ref[...]
        row = _iota2(u.shape, 0)
        pre = _conv_pre(u, w_ref, b_ref, row)
        o_ref[...] = pre * _sigmoid(pre)

    return pl.pallas_call(
        body, name="conv_fwd", grid=(dm.Bl, dm.CC // cwid),
        in_specs=[pl.BlockSpec((dm.S, cwid), lambda b, j: (b, off + j)),
                  pl.BlockSpec((CONV_WIDTH, cwid), lambda b, j: (0, j)), pl.BlockSpec((1, cwid), lambda b, j: (0, j))],
        out_specs=pl.BlockSpec((dm.S, cwid), lambda b, j: (b, j)),
        out_shape=jax.ShapeDtypeStruct((dm.T, dm.CC), f32), compiler_params=_cp(),
    )(proj, cw, cb.reshape(1, dm.CC))


def _conv_bwd(proj, d_out, cw, cb, dm):
    cwid = LANES
    off = dm.c_xbc // cwid

    def body(u_ref, d_ref, w_ref, b_ref, du_ref, dw_ref, db_ref):
        @pl.when(pl.program_id(1) == 0)
        def _():
            dw_ref[...] = jnp.zeros_like(dw_ref)
            db_ref[...] = jnp.zeros_like(db_ref)

        u = u_ref[...]
        row = _iota2(u.shape, 0)
        pre = _conv_pre(u, w_ref, b_ref, row)
        sg = _sigmoid(pre)
        dpre = d_ref[...] * (sg * (1.0 + pre * (1.0 - sg)))
        du = w_ref[CONV_WIDTH - 1:CONV_WIDTH, :] * dpre
        dw_ref[CONV_WIDTH - 1:CONV_WIDTH, :] += jnp.sum(dpre * u, axis=0, keepdims=True)
        for k in range(CONV_WIDTH - 1):
            sh = CONV_WIDTH - 1 - k
            du = du + w_ref[k:k + 1, :] * _shift_up(dpre, sh, row)
            dw_ref[k:k + 1, :] += jnp.sum(dpre * _shift_down(u, sh, row), axis=0, keepdims=True)
        du_ref[...] = du.astype(du_ref.dtype)
        db_ref[...] += jnp.sum(dpre, axis=0, keepdims=True)

    return pl.pallas_call(
        body, name="conv_bwd", grid=(dm.CC // cwid, dm.Bl),
        in_specs=[pl.BlockSpec((dm.S, cwid), lambda j, b: (b, off + j)), pl.BlockSpec((dm.S, cwid), lambda j, b: (b, j)),
                  pl.BlockSpec((CONV_WIDTH, cwid), lambda j, b: (0, j)), pl.BlockSpec((1, cwid), lambda j, b: (0, j))],
        out_specs=[pl.BlockSpec((dm.S, cwid), lambda j, b: (b, j)), pl.BlockSpec((CONV_WIDTH, cwid), lambda j, b: (0, j)),
                   pl.BlockSpec((1, cwid), lambda j, b: (0, j))],
        out_shape=[jax.ShapeDtypeStruct((dm.T, dm.CC), ACT_DTYPE), jax.ShapeDtypeStruct((CONV_WIDTH, dm.CC), f32),
                   jax.ShapeDtypeStruct((1, dm.CC), f32)],
        compiler_params=_cp(dimension_semantics=("arbitrary", "arbitrary")),
    )(proj, d_out, cw, cb.reshape(1, dm.CC))


def _pad_lanes(v):
    return jnp.pad(v, (0, LANES - v.shape[0])).reshape(1, LANES)


def _softplus(x):
    return jnp.maximum(x, 0.0) + jnp.log(1.0 + jnp.exp(-jnp.abs(x)))


def _dt_prep(proj, dt_bias, a_log, dm):
    off = dm.c_dt // LANES

    def body(r_ref, b_ref, al_ref, dt_ref, cum_ref, cumt_ref):
        dt = _softplus(r_ref[...] + b_ref[...])
        adt = dt * (-jnp.exp(al_ref[...]))
        tril = (_iota2((LANES, LANES), 1) <= _iota2((LANES, LANES), 0)).astype(bf16)
        cum = _exact_nn_left(tril, adt)
        dt_ref[...] = dt
        cum_ref[...] = cum
        cumt_ref[...] = cum.T

    blk = pl.BlockSpec((LANES, LANES), lambda i: (i, 0))
    vec = pl.BlockSpec((1, LANES), lambda i: (0, 0))
    return pl.pallas_call(
        body, name="dt_prep", grid=(dm.T // LANES,),
        in_specs=[pl.BlockSpec((LANES, LANES), lambda i: (i, off)), vec, vec],
        out_specs=[blk, blk, pl.BlockSpec((LANES, LANES), lambda i: (0, i))],
        out_shape=[jax.ShapeDtypeStruct((dm.T, LANES), f32), jax.ShapeDtypeStruct((dm.T, LANES), f32),
                   jax.ShapeDtypeStruct((LANES, dm.T), f32)],
        compiler_params=_cp(),
    )(proj, _pad_lanes(dt_bias), _pad_lanes(a_log))


def _exact_nn_left(u, a):
    hi, mid, lo = _split3(a)
    return _nn(u, hi) + _nn(u, mid) + _nn(u, lo)


def _dt_bwd(proj, dt, dcum, dcum_t, ddt, dt_bias, a_log, dm):
    off = dm.c_dt // LANES

    def body(r_ref, dt_ref, dc_ref, dct_ref, dd_ref, b_ref, al_ref, o_ref, db_ref, da_ref):
        @pl.when(pl.program_id(0) == 0)
        def _():
            db_ref[...] = jnp.zeros_like(db_ref)
            da_ref[...] = jnp.zeros_like(da_ref)

        a = -jnp.exp(al_ref[...])
        triu = (_iota2((LANES, LANES), 1) >= _iota2((LANES, LANES), 0)).astype(bf16)
        dadt = _exact_nn_left(triu, dc_ref[...] + dct_ref[...].T)
        d_dt = dd_ref[...] + dadt * a
        d_raw = d_dt * _sigmoid(r_ref[...] + b_ref[...])
        o_ref[...] = d_raw.astype(o_ref.dtype)
        db_ref[...] += jnp.sum(d_raw, axis=0, keepdims=True)
        da_ref[...] += jnp.sum(dadt * dt_ref[...], axis=0, keepdims=True) * a

    blk = pl.BlockSpec((LANES, LANES), lambda i: (i, 0))
    vec = pl.BlockSpec((1, LANES), lambda i: (0, 0))
    return pl.pallas_call(
        body, name="dt_bwd", grid=(dm.T // LANES,),
        in_specs=[pl.BlockSpec((LANES, LANES), lambda i: (i, off)), blk, blk, pl.BlockSpec((LANES, LANES), lambda i: (0, i)),
                  blk, vec, vec],
        out_specs=[blk, vec, vec],
        out_shape=[jax.ShapeDtypeStruct((dm.T, LANES), ACT_DTYPE), jax.ShapeDtypeStruct((1, LANES), f32),
                   jax.ShapeDtypeStruct((1, LANES), f32)],
        compiler_params=_cp(dimension_semantics=("arbitrary",)),
    )(proj, dt, dcum, dcum_t, ddt, _pad_lanes(dt_bias), _pad_lanes(a_log))


def _ssd_common(dm):
    L = LANES
    tri = _iota2((L, L), 0) >= _iota2((L, L), 1)
    lo = _iota2((L, L), 1) < HEAD
    return tri, lo


def _ssd_fwd(xbc, dt, cum, cumt, d_skip, dm):
    L, W2, hpg = LANES, dm.W2, dm.hpg
    nb = W2 // L

    def body(x_ref, b_ref, c_ref, dt_ref, cum_ref, cumt_ref, dsk_ref, y_ref, hs_ref, h_scr):
        @pl.when(pl.program_id(1) == 0)
        def _():
            h_scr[...] = jnp.zeros_like(h_scr)

        hs_ref[0] = h_scr[...]
        tri, lo = _ssd_common(dm)
        for g in range(SSM_GROUPS):
            bb = _mx(b_ref[:, g * L:(g + 1) * L])
            cb_ = _mx(c_ref[:, g * L:(g + 1) * L])
            cbm = _nt(cb_, bb)
            for i in range(hpg // 2):
                h0 = g * hpg + 2 * i
                h1 = h0 + 1
                sl = slice(h0 * HEAD, h0 * HEAD + L)
                x_p = x_ref[:, sl]
                cum0, cum1 = cum_ref[:, h0:h0 + 1], cum_ref[:, h1:h1 + 1]
                cums = jnp.where(lo, cum0, cum1)
                xdt = x_p * jnp.where(lo, dt_ref[:, h0:h0 + 1], dt_ref[:, h1:h1 + 1])
                tot = jnp.where(lo[0:1], cum_ref[L - 1:L, h0:h0 + 1], cum_ref[L - 1:L, h1:h1 + 1])
                y_p = jnp.zeros((L, L), f32)
                for hh, m in ((h0, lo), (h1, jnp.logical_not(lo))):
                    diff = cum_ref[:, hh:hh + 1] - cumt_ref[hh:hh + 1, :]
                    lm = jnp.where(tri, jnp.exp(jnp.minimum(diff, 0.0)), 0.0)
                    y_p = y_p + _nn(_mx(cbm * lm), _mx(jnp.where(m, xdt, 0.0)))
                hp = h_scr[:, sl]
                y_p = y_p + _nn(cb_, _mx(hp)) * jnp.exp(cums)
                y_p = y_p + x_p * jnp.where(lo[0:1], dsk_ref[0:1, h0:h0 + 1], dsk_ref[0:1, h1:h1 + 1])
                y_ref[:, sl] = y_p
                h_scr[:, sl] = hp * jnp.exp(tot) + _tn(bb, _mx(xdt * jnp.exp(tot - cums)))

    nc = dm.nc
    ob = W2 // (SSM_GROUPS * L)
    blk = pl.BlockSpec((L, L), lambda b, c: (b * nc + c, 0))
    return pl.pallas_call(
        body, name="ssd_fwd", grid=(dm.Bl, nc),
        in_specs=[pl.BlockSpec((L, W2), lambda b, c: (b * nc + c, 0)),
                  pl.BlockSpec((L, SSM_GROUPS * L), lambda b, c: (b * nc + c, ob)),
                  pl.BlockSpec((L, SSM_GROUPS * L), lambda b, c: (b * nc + c, ob + 1)),
                  blk, blk, pl.BlockSpec((L, L), lambda b, c: (0, b * nc + c)), pl.BlockSpec((1, L), lambda b, c: (0, 0))],
        out_specs=[pl.BlockSpec((L, W2), lambda b, c: (b * nc + c, 0)), pl.BlockSpec((1, L, W2), lambda b, c: (b * nc + c, 0, 0))],
        out_shape=[jax.ShapeDtypeStruct((dm.T, W2), f32), jax.ShapeDtypeStruct((dm.Bl * nc, L, W2), f32)],
        scratch_shapes=[pltpu.VMEM((L, W2), f32)],
        compiler_params=_cp(dimension_semantics=("arbitrary", "arbitrary")),
    )(xbc, xbc, xbc, dt, cum, cumt, _pad_lanes(d_skip))


def _ssd_bwd(xbc, dt, cum, cumt, d_skip, hs, dy, dm):
    L, W2, hpg = LANES, dm.W2, dm.hpg
    nc = dm.nc

    def body(x_ref, b_ref, c_ref, dt_ref, cum_ref, cumt_ref, dsk_ref, hs_ref, dy_ref,
             dx_ref, db_ref, dc_ref, ddt_ref, dcum_ref, dcr_ref, dd_ref, dh_scr, lane_cum, lane_dt, lane_d):
        @pl.when(pl.program_id(1) == 0)
        def _():
            dh_scr[...] = jnp.zeros_like(dh_scr)

        @pl.when((pl.program_id(0) == 0) & (pl.program_id(1) == 0))
        def _():
            dd_ref[...] = jnp.zeros_like(dd_ref)

        dcr_ref[...] = jnp.zeros_like(dcr_ref)
        tri, lo = _ssd_common(dm)
        last = _iota2((L, L), 0) == L - 1
        for g in range(SSM_GROUPS):
            gs = slice(g * L, (g + 1) * L)
            bb = _mx(b_ref[:, gs])
            cb_ = _mx(c_ref[:, gs])
            cbm = _nt(cb_, bb)
            dcb = jnp.zeros((L, L), f32)
            dc_g = jnp.zeros((L, L), f32)
            db_g = jnp.zeros((L, L), f32)
            for i in range(hpg // 2):
                h0 = g * hpg + 2 * i
                h1 = h0 + 1
                sl = slice(h0 * HEAD, h0 * HEAD + L)
                x_p = x_ref[:, sl]
                dy_p = dy_ref[:, sl]
                dt_p = jnp.where(lo, dt_ref[:, h0:h0 + 1], dt_ref[:, h1:h1 + 1])
                cums = jnp.where(lo, cum_ref[:, h0:h0 + 1], cum_ref[:, h1:h1 + 1])
                tot = jnp.where(lo[0:1], cum_ref[L - 1:L, h0:h0 + 1], cum_ref[L - 1:L, h1:h1 + 1])
                dsk_p = jnp.where(lo[0:1], dsk_ref[0:1, h0:h0 + 1], dsk_ref[0:1, h1:h1 + 1])
                xdt = x_p * dt_p
                ecum = jnp.exp(cums)
                dec = jnp.exp(tot - cums)
                etot = jnp.exp(tot)
                hp = hs_ref[0, :, sl]
                hp_b = _mx(hp)
                dhn = dh_scr[:, sl]
                dhn_b = _mx(dhn)
                y_off = _nn(cb_, hp_b) * ecum
                dch = _mx(dy_p * ecum)
                dc_g = dc_g + _nt(dch, hp_b)
                dh_off = _tn(cb_, dch)
                bds = _nn(bb, dhn_b)
                xdec = xdt * dec
                db_g = db_g + _nt(_mx(xdec), dhn_b)
                dxdt = bds * dec
                sdec = bds * xdec
                tot_lane = jnp.sum(sdec, axis=0, keepdims=True) + jnp.sum(dhn * hp, axis=0, keepdims=True) * etot
                dh_scr[:, sl] = etot * dhn + dh_off
                rsum = []
                for hh, m in ((h0, lo), (h1, jnp.logical_not(lo))):
                    diff = cum_ref[:, hh:hh + 1] - cumt_ref[hh:hh + 1, :]
                    lm = jnp.where(tri, jnp.exp(jnp.minimum(diff, 0.0)), 0.0)
                    w32 = cbm * lm
                    dyh = _mx(jnp.where(m, dy_p, 0.0))
                    dw = _nt(dyh, _mx(jnp.where(m, xdt, 0.0)))
                    dcb = dcb + dw * lm
                    e = dw * w32
                    rsum.append(jnp.sum(e, axis=1, keepdims=True))
                    dcr_ref[hh:hh + 1, :] = -jnp.sum(e, axis=0, keepdims=True)
                    dxdt = dxdt + _tn(_mx(w32), dyh)
                lane_cum[:, sl] = (dy_p * y_off + jnp.where(lo, rsum[0], rsum[1]) * (1.0 / HEAD) - sdec
                                   + jnp.where(last, tot_lane, 0.0))
                lane_dt[:, sl] = dxdt * x_p
                lane_d[:, sl] = dy_p * x_p
                dx_ref[:, sl] = dxdt * dt_p + dsk_p * dy_p
            dcb_b = _mx(dcb)
            dc_ref[:, gs] = dc_g + _nn(dcb_b, bb)
            db_ref[:, gs] = db_g + _tn(dcb_b, cb_)
        sel = (_iota2((W2, L), 0) // HEAD == _iota2((W2, L), 1)).astype(bf16)
        dcum_ref[...] = _exact_nn(lane_cum[...], sel)
        ddt_ref[...] = _exact_nn(lane_dt[...], sel)
        dd_ref[...] += jnp.sum(_exact_nn(lane_d[...], sel), axis=0, keepdims=True)

    ob = W2 // (SSM_GROUPS * L)

    def rc(b, c):
        return b * nc + (nc - 1 - c)

    blk = pl.BlockSpec((L, L), lambda b, c: (rc(b, c), 0))
    blk_t = pl.BlockSpec((L, L), lambda b, c: (0, rc(b, c)))
    wide = pl.BlockSpec((L, W2), lambda b, c: (rc(b, c), 0))
    grp = pl.BlockSpec((L, SSM_GROUPS * L), lambda b, c: (rc(b, c), 0))
    return pl.pallas_call(
        body, name="ssd_bwd", grid=(dm.Bl, nc),
        in_specs=[wide, pl.BlockSpec((L, SSM_GROUPS * L), lambda b, c: (rc(b, c), ob)),
                  pl.BlockSpec((L, SSM_GROUPS * L), lambda b, c: (rc(b, c), ob + 1)),
                  blk, blk, blk_t, pl.BlockSpec((1, L), lambda b, c: (0, 0)),
                  pl.BlockSpec((1, L, W2), lambda b, c: (rc(b, c), 0, 0)), wide],
        out_specs=[wide, grp, grp, blk, blk, blk_t, pl.BlockSpec((1, L), lambda b, c: (0, 0))],
        out_shape=[jax.ShapeDtypeStruct((dm.T, W2), f32), jax.ShapeDtypeStruct((dm.T, SSM_GROUPS * L), f32),
                   jax.ShapeDtypeStruct((dm.T, SSM_GROUPS * L), f32), jax.ShapeDtypeStruct((dm.T, L), f32),
                   jax.ShapeDtypeStruct((dm.T, L), f32), jax.ShapeDtypeStruct((L, dm.T), f32),
                   jax.ShapeDtypeStruct((1, L), f32)],
        scratch_shapes=[pltpu.VMEM((L, W2), f32)] * 4,
        compiler_params=_cp(dimension_semantics=("arbitrary", "arbitrary")),
    )(xbc, xbc, xbc, dt, cum, cumt, _pad_lanes(d_skip), hs, dy)


def _gnorm_fwd(y, proj, w, dm):
    tr = _pick(dm.T, (256, 128))
    row = pl.BlockSpec((tr, dm.W2), lambda i: (i, 0))

    def body(y_ref, z_ref, w_ref, o_ref):
        z = z_ref[...]
        yg = y_ref[...] * (z * _sigmoid(z))
        r = lax.rsqrt(jnp.mean(yg * yg, axis=-1, keepdims=True) + EPS)
        o_ref[...] = (yg * r * w_ref[...]).astype(o_ref.dtype)

    return pl.pallas_call(
        body, name="gnorm_fwd", grid=(dm.T // tr,), in_specs=[row, row, pl.BlockSpec((1, dm.W2), lambda i: (0, 0))],
        out_specs=row, out_shape=jax.ShapeDtypeStruct((dm.T, dm.W2), ACT_DTYPE), compiler_params=_cp(),
    )(y, proj, w.reshape(1, dm.W2))


def _gnorm_bwd(dn, y, proj, w, dm):
    tr = _pick(dm.T, (256, 128))
    row = pl.BlockSpec((tr, dm.W2), lambda i: (i, 0))
    vec = pl.BlockSpec((1, dm.W2), lambda i: (0, 0))

    def body(dn_ref, y_ref, z_ref, w_ref, dy_ref, dz_ref, dw_ref):
        @pl.when(pl.program_id(0) == 0)
        def _():
            dw_ref[...] = jnp.zeros_like(dw_ref)

        z = z_ref[...]
        sg = _sigmoid(z)
        sz = z * sg
        yv = y_ref[...]
        yg = yv * sz
        r = lax.rsqrt(jnp.mean(yg * yg, axis=-1, keepdims=True) + EPS)
        n = yg * r
        dout = dn_ref[...]
        dnn = dout * w_ref[...]
        dyg = r * (dnn - n * jnp.mean(dnn * n, axis=-1, keepdims=True))
        dy_ref[...] = dyg * sz
        dz_ref[...] = (dyg * yv * (sg * (1.0 + z * (1.0 - sg)))).astype(dz_ref.dtype)
        dw_ref[...] += jnp.sum(dout * n, axis=0, keepdims=True)

    return pl.pallas_call(
        body, name="gnorm_bwd", grid=(dm.T // tr,), in_specs=[row, row, row, vec], out_specs=[row, row, vec],
        out_shape=[jax.ShapeDtypeStruct((dm.T, dm.W2), f32), jax.ShapeDtypeStruct((dm.T, dm.W2), ACT_DTYPE),
                   jax.ShapeDtypeStruct((1, dm.W2), f32)],
        compiler_params=_cp(dimension_semantics=("arbitrary",)),
    )(dn, y, proj, w.reshape(1, dm.W2))


def _pool_mixed(u, g, row):
    s = u
    for k in range(POOL_GROUPS):
        s = jnp.where(k <= g, s + _shift_down(s, 1 << k, row), s)
    w = jnp.left_shift(2, g)
    cnt = jnp.minimum(row + 1, w).astype(f32)
    return s / cnt - u, cnt


def _pool_fwd(proj, pw, scale, dm):
    Dg = dm.Dg
    ou, og = dm.c_pu // Dg, dm.c_pg // Dg

    def body(u_ref, g_ref, w_ref, s_ref, o_ref):
        u = u_ref[...]
        row = _iota2(u.shape, 0)
        mixed, _ = _pool_mixed(u, pl.program_id(1), row)
        lin = _nn(_mx(mixed), _mx(w_ref[0]))
        gt = g_ref[...]
        o_ref[...] = (lin * s_ref[...] * (gt * _sigmoid(gt))).astype(o_ref.dtype)

    return pl.pallas_call(
        body, name="pool_fwd", grid=(dm.Bl, POOL_GROUPS),
        in_specs=[pl.BlockSpec((dm.S, Dg), lambda b, g: (b, ou + g)), pl.BlockSpec((dm.S, Dg), lambda b, g: (b, og + g)),
                  pl.BlockSpec((1, Dg, Dg), lambda b, g: (g, 0, 0)), pl.BlockSpec((1, Dg), lambda b, g: (0, g))],
        out_specs=pl.BlockSpec((dm.S, Dg), lambda b, g: (b, g)),
        out_shape=jax.ShapeDtypeStruct((dm.T, dm.D), ACT_DTYPE), compiler_params=_cp(),
    )(proj, proj, pw, scale.reshape(1, dm.D))


def _pool_bwd(proj, dout, pw, scale, dm):
    Dg = dm.Dg
    ou, og = dm.c_pu // Dg, dm.c_pg // Dg

    def body(u_ref, g_ref, d_ref, w_ref, s_ref, du_ref, dg_ref, dw_ref, ds_ref):
        @pl.when(pl.program_id(1) == 0)
        def _():
            dw_ref[...] = jnp.zeros_like(dw_ref)
            ds_ref[...] = jnp.zeros_like(ds_ref)

        g = pl.program_id(0)
        u = u_ref[...]
        row = _iota2(u.shape, 0)
        mixed, cnt = _pool_mixed(u, g, row)
        mixed_b = _mx(mixed)
        wb = _mx(w_ref[0])
        lin = _nn(mixed_b, wb)
        gt = g_ref[...]
        sg = _sigmoid(gt)
        silu = gt * sg
        d = d_ref[...]
        sc = s_ref[...]
        dlin = d * sc * silu
        ds_ref[...] += jnp.sum(d * lin * silu, axis=0, keepdims=True)
        dg_ref[...] = (d * lin * sc * (sg * (1.0 + gt * (1.0 - sg)))).astype(dg_ref.dtype)
        dlin_b = _mx(dlin)
        dmixed = _nt(dlin_b, wb)
        dw_ref[0] += _tn(mixed_b, dlin_b)
        r = dmixed / cnt
        for k in range(POOL_GROUPS):
            r = jnp.where(k <= g, r + _shift_up(r, 1 << k, row), r)
        du_ref[...] = (r - dmixed).astype(du_ref.dtype)

    return pl.pallas_call(
        body, name="pool_bwd", grid=(POOL_GROUPS, dm.Bl),
        in_specs=[pl.BlockSpec((dm.S, Dg), lambda g, b: (b, ou + g)), pl.BlockSpec((dm.S, Dg), lambda g, b: (b, og + g)),
                  pl.BlockSpec((dm.S, Dg), lambda g, b: (b, g)), pl.BlockSpec((1, Dg, Dg), lambda g, b: (g, 0, 0)),
                  pl.BlockSpec((1, Dg), lambda g, b: (0, g))],
        out_specs=[pl.BlockSpec((dm.S, Dg), lambda g, b: (b, g)), pl.BlockSpec((dm.S, Dg), lambda g, b: (b, g)),
                   pl.BlockSpec((1, Dg, Dg), lambda g, b: (g, 0, 0)), pl.BlockSpec((1, Dg), lambda g, b: (0, g))],
        out_shape=[jax.ShapeDtypeStruct((dm.T, dm.D), ACT_DTYPE), jax.ShapeDtypeStruct((dm.T, dm.D), ACT_DTYPE),
                   jax.ShapeDtypeStruct((POOL_GROUPS, Dg, Dg), f32), jax.ShapeDtypeStruct((1, dm.D), f32)],
        compiler_params=_cp(dimension_semantics=("arbitrary", "arbitrary")),
    )(proj, proj, dout, pw, scale.reshape(1, dm.D))


SB_TILE = 512


def _split2(a):
    hi = a.astype(bf16)
    return hi, (a - hi.astype(f32)).astype(bf16)


def _exact2_nn(a, u):
    hi, lo = _split2(a)
    return _nn(hi, u) + _nn(lo, u)


SB_CUM = 256


def _blocked_sums(a, u, suffix):
    nb = a.shape[1] // SB_CUM
    blocks = [a[:, i * SB_CUM:(i + 1) * SB_CUM] for i in range(nb)]
    tots = [jnp.sum(b, axis=1, keepdims=True) for b in blocks]
    out = []
    for i, b in enumerate(blocks):
        s = _exact2_nn(b, u)
        for t in (tots[i + 1:] if suffix else tots[:i]):
            s = s + t
        out.append(s)
    total = tots[0]
    for t in tots[1:]:
        total = total + t
    return (out[0] if nb == 1 else jnp.concatenate(out, axis=1)), total


def _sb_tile(qb, kb, base, u, suffix, causal):
    z = _nt(qb, kb)
    lb = jnp.minimum(z, 0.0) - jnp.log(1.0 + jnp.exp(-jnp.abs(z)))
    lm = lb - z
    if causal is not None:
        lm = jnp.where(causal, lm, 0.0)
    sums, tot = _blocked_sums(lm, u, suffix)
    att = jnp.exp(lb + base + sums) if suffix else jnp.exp(lb + base - sums)
    if causal is not None:
        att = jnp.where(causal, att, 0.0)
    return lb, att, tot


def _sb_fwd(proj, dm):
    L, S, D, TQ = LANES, dm.S, dm.D, SB_TILE
    oq, og = dm.c_qkv // L, dm.c_sbg // L
    nb = D // L
    scale = HEAD ** -0.5

    def body(q_ref, k_ref, v_ref, g_ref, o_ref, lt_ref, out_ref, k_s, va_s, vb_s):
        lo = _iota2((1, L), 1) < HEAD
        hi = jnp.logical_not(lo)
        k_s[...] = _mx(k_ref[...])
        vf = v_ref[...]
        va_s[...] = _mx(jnp.where(lo, vf, 0.0))
        vb_s[...] = _mx(jnp.where(hi, vf, 0.0))
        before = _iota2((TQ, TQ), 0) > _iota2((TQ, TQ), 1)
        ugt = (_iota2((SB_CUM, SB_CUM), 0) > _iota2((SB_CUM, SB_CUM), 1)).astype(bf16)
        v_s = (va_s, vb_s)

        def qloop(qt, _):
            rows = pl.ds(pl.multiple_of(qt * TQ, TQ), TQ)
            qs = q_ref[rows, :] * scale
            qb = [_mx(jnp.where(m, qs, 0.0)) for m in (lo, hi)]

            def tile(kt, carry, causal):
                runs, acc = carry
                krows = pl.ds(pl.multiple_of(kt * TQ, TQ), TQ)
                k_t = k_s[krows, :]
                new_runs = []
                for h in range(2):
                    _, att, tot = _sb_tile(qb[h], k_t, runs[h], ugt, True, causal)
                    acc = acc + _nn(_mx(att), v_s[h][krows, :])
                    new_runs.append(runs[h] + tot)
                return tuple(new_runs), acc

            zc = jnp.zeros((TQ, 1), f32)
            carry = tile(qt, ((zc, zc), jnp.zeros((TQ, L), f32)), before)
            runs, acc = lax.fori_loop(0, qt, lambda jj, c: tile(qt - 1 - jj, c, None), carry)
            o_ref[rows, :] = acc
            lt_ref[rows, :] = jnp.where(lo, runs[0], runs[1])
            gt = g_ref[rows, :]
            out_ref[rows, :] = (acc * (gt * _sigmoid(gt))).astype(out_ref.dtype)
            return 0

        lax.fori_loop(0, S // TQ, qloop, 0)

    def col(o):
        return pl.BlockSpec((S, L), lambda b, p: (b, o + p))

    return pl.pallas_call(
        body, name="sb_fwd", grid=(dm.Bl, nb),
        in_specs=[col(oq), col(oq + nb), col(oq + 2 * nb), col(og)], out_specs=[col(0), col(0), col(0)],
        out_shape=[jax.ShapeDtypeStruct((dm.T, D), f32), jax.ShapeDtypeStruct((dm.T, D), f32),
                   jax.ShapeDtypeStruct((dm.T, D), ACT_DTYPE)],
        scratch_shapes=[pltpu.VMEM((S, L), MXU_DTYPE)] * 3,
        compiler_params=_cp(),
    )(proj, proj, proj, proj)


def _sb_bwd(proj, o, lt, dsb, dm):
    L, S, D, TQ = LANES, dm.S, dm.D, SB_TILE
    oq, og = dm.c_qkv // L, dm.c_sbg // L
    nb = D // L
    scale = HEAD ** -0.5

    def body(q_ref, k_ref, v_ref, g_ref, o_ref, lt_ref, d_ref, dq_ref, dk_ref, dv_ref, dg_ref,
             k_s, ka_s, kb_s, v_s, dk_acc, dv_acc):
        lo = _iota2((1, L), 1) < HEAD
        hi = jnp.logical_not(lo)
        kf = k_ref[...]
        k_s[...] = _mx(kf)
        ka_s[...] = _mx(jnp.where(lo, kf, 0.0))
        kb_s[...] = _mx(jnp.where(hi, kf, 0.0))
        v_s[...] = _mx(v_ref[...])
        dk_acc[...] = jnp.zeros_like(dk_acc)
        dv_acc[...] = jnp.zeros_like(dv_acc)
        before = _iota2((TQ, TQ), 0) > _iota2((TQ, TQ), 1)
        ule = (_iota2((SB_CUM, SB_CUM), 0) <= _iota2((SB_CUM, SB_CUM), 1)).astype(bf16)
        ult = (_iota2((SB_CUM, SB_CUM), 0) < _iota2((SB_CUM, SB_CUM), 1)).astype(bf16)
        k_m = (ka_s, kb_s)

        def qloop(qt, _):
            rows = pl.ds(pl.multiple_of(qt * TQ, TQ), TQ)
            gt = g_ref[rows, :]
            sg = _sigmoid(gt)
            dsb_b = d_ref[rows, :]
            do = dsb_b * (gt * sg)
            dg_ref[rows, :] = (dsb_b * o_ref[rows, :] * (sg * (1.0 + gt * (1.0 - sg)))).astype(dg_ref.dtype)
            qs = q_ref[rows, :] * scale
            qb = [_mx(jnp.where(m, qs, 0.0)) for m in (lo, hi)]
            dob = [_mx(jnp.where(m, do, 0.0)) for m in (lo, hi)]
            ltot = [lt_ref[rows, 0:1], lt_ref[rows, HEAD:HEAD + 1]]

            def tile(kt, carry, causal):
                runs, rgs, dq = carry
                krows = pl.ds(pl.multiple_of(kt * TQ, TQ), TQ)
                k_t, v_t = k_s[krows, :], v_s[krows, :]
                dk_t = jnp.zeros((TQ, L), f32)
                dv_t = jnp.zeros((TQ, L), f32)
                new_runs, new_rgs = [], []
                for h in range(2):
                    lb, att, tot = _sb_tile(qb[h], k_t, ltot[h] - runs[h], ule, False, causal)
                    gm = att * _nt(dob[h], v_t)
                    gsum, gtot = _blocked_sums(gm, ult, False)
                    pre = rgs[h] + gsum
                    beta = jnp.exp(lb)
                    dz = gm * (1.0 - beta) - pre * beta
                    if causal is not None:
                        dz = jnp.where(causal, dz, 0.0)
                    dz = _mx(dz)
                    dq = dq + _nn(dz, k_m[h][krows, :])
                    dk_t = dk_t + _tn(dz, qb[h])
                    dv_t = dv_t + _tn(_mx(att), dob[h])
                    new_runs.append(runs[h] + tot)
                    new_rgs.append(rgs[h] + gtot)
                dk_acc[krows, :] += dk_t
                dv_acc[krows, :] += dv_t
                return tuple(new_runs), tuple(new_rgs), dq

            zc = jnp.zeros((TQ, 1), f32)
            carry = lax.fori_loop(0, qt, lambda kt, c: tile(kt, c, None), ((zc, zc), (zc, zc), jnp.zeros((TQ, L), f32)))
            _, _, dq = tile(qt, carry, before)
            dq_ref[rows, :] = (dq * scale).astype(dq_ref.dtype)
            return 0

        lax.fori_loop(0, S // TQ, qloop, 0)
        dk_ref[...] = dk_acc[...].astype(dk_ref.dtype)
        dv_ref[...] = dv_acc[...].astype(dv_ref.dtype)

    def col(off):
        return pl.BlockSpec((S, L), lambda b, p: (b, off + p))

    out = jax.ShapeDtypeStruct((dm.T, D), ACT_DTYPE)
    return pl.pallas_call(
        body, name="sb_bwd", grid=(dm.Bl, nb),
        in_specs=[col(oq), col(oq + nb), col(oq + 2 * nb), col(og), col(0), col(0), col(0)],
        out_specs=[col(0), col(0), col(0), col(0)], out_shape=[out, out, out, out],
        scratch_shapes=[pltpu.VMEM((S, L), MXU_DTYPE)] * 4 + [pltpu.VMEM((S, L), f32)] * 2,
        compiler_params=_cp(),
    )(proj, proj, proj, proj, o, lt, dsb)


def _merge_fwd(proj, ys, dm):
    tr = _pick(dm.T, (256, 128))
    ct = 512
    om = dm.c_mrg // ct
    nb = dm.D // ct
    blk = pl.BlockSpec((tr, ct), lambda i, j: (i, j))

    def body(l0, l1, l2, y0, y1, y2, o_ref):
        acc = _sigmoid(l0[...]) * y0[...] + _sigmoid(l1[...]) * y1[...] + _sigmoid(l2[...]) * y2[...]
        o_ref[...] = acc.astype(o_ref.dtype)

    return pl.pallas_call(
        body, name="merge_fwd", grid=(dm.T // tr, nb),
        in_specs=[pl.BlockSpec((tr, ct), functools.partial(lambda i, j, b: (i, om + b * nb + j), b=b)) for b in range(3)] + [blk] * 3,
        out_specs=blk, out_shape=jax.ShapeDtypeStruct((dm.T, dm.D), ACT_DTYPE), compiler_params=_cp(),
    )(proj, proj, proj, *ys)


def _merge_bwd(proj, ys, dmerged, dm):
    tr = _pick(dm.T, (256, 128))
    ct = 512
    om = dm.c_mrg // ct
    nb = dm.D // ct
    blk = pl.BlockSpec((tr, ct), lambda i, j: (i, j))

    def body(l0, l1, l2, y0, y1, y2, d_ref, dy0, dy1, dy2, dl0, dl1, dl2):
        d = d_ref[...]
        for l_ref, y_ref, dy_ref, dl_ref in ((l0, y0, dy0, dl0), (l1, y1, dy1, dl1), (l2, y2, dy2, dl2)):
            sg = _sigmoid(l_ref[...])
            dy_ref[...] = (d * sg).astype(dy_ref.dtype)
            dl_ref[...] = (d * y_ref[...] * sg * (1.0 - sg)).astype(dl_ref.dtype)

    out = jax.ShapeDtypeStruct((dm.T, dm.D), ACT_DTYPE)
    return pl.pallas_call(
        body, name="merge_bwd", grid=(dm.T // tr, nb),
        in_specs=[pl.BlockSpec((tr, ct), functools.partial(lambda i, j, b: (i, om + b * nb + j), b=b)) for b in range(3)] + [blk] * 4,
        out_specs=[blk] * 6, out_shape=[out] * 6, compiler_params=_cp(),
    )(proj, proj, proj, *ys, dmerged)


def _layer_fwd(x, p, dm):
    h = _rms_fwd(x, p["norm_w"], dm)
    proj = _mm(h, p["w_in"], name="mm_in")
    xbc = _conv_fwd(proj, p["conv_w"], p["conv_b"], dm)
    dt, cum, cumt = _dt_prep(proj, p["dt_bias"], p["a_log"], dm)
    y, hs = _ssd_fwd(xbc, dt, cum, cumt, p["d_skip"], dm)
    ssm_n = _gnorm_fwd(y, proj, p["ssm_norm_w"], dm)
    pool_o = _pool_fwd(proj, p["pool_w"], p["pool_scale"], dm)
    o, lt, sb_o = _sb_fwd(proj, dm)
    ys = (_mm(ssm_n, p["w_proj_ssm"], name="mm_ps"), _mm(pool_o, p["w_proj_pool"], name="mm_pp"),
          _mm(sb_o, p["w_proj_sb"], name="mm_pb"))
    merged = _merge_fwd(proj, ys, dm)
    x_next = _mm(merged, p["w_out"], res=x, name="mm_out")
    saved = dict(x=x, h=h, proj=proj, xbc=xbc, dt=dt, cum=cum, cumt=cumt, y=y, hs=hs, ssm_n=ssm_n, pool_o=pool_o, o=o,
                 lt=lt, sb_o=sb_o, ys=ys, merged=merged)
    return x_next, saved


def _layer_bwd(dx_out, p, sv, dm):
    g = {}
    proj = sv["proj"]
    dmerged = _mm(dx_out, p["w_out"], tb=True, name="mm_dmerged")
    g["w_out"] = _mm(sv["merged"], dx_out, ta=True, name="mm_dwout")
    dy0, dy1, dy2, dl0, dl1, dl2 = _merge_bwd(proj, sv["ys"], dmerged, dm)
    d_ssm_n = _mm(dy0, p["w_proj_ssm"], tb=True, name="mm_dssm")
    g["w_proj_ssm"] = _mm(sv["ssm_n"], dy0, ta=True, name="mm_dwps")
    d_pool_o = _mm(dy1, p["w_proj_pool"], tb=True, name="mm_dpool")
    g["w_proj_pool"] = _mm(sv["pool_o"], dy1, ta=True, name="mm_dwpp")
    d_sb_o = _mm(dy2, p["w_proj_sb"], tb=True, name="mm_dsb")
    g["w_proj_sb"] = _mm(sv["sb_o"], dy2, ta=True, name="mm_dwpb")
    dy, dz, g["ssm_norm_w"] = _gnorm_bwd(d_ssm_n, sv["y"], proj, p["ssm_norm_w"], dm)
    dxs, db, dc, ddt, dcum, dcum_t, dd = _ssd_bwd(sv["xbc"], sv["dt"], sv["cum"], sv["cumt"], p["d_skip"], sv["hs"], dy, dm)
    g["d_skip"] = dd
    d_dtraw, g["dt_bias"], g["a_log"] = _dt_bwd(proj, sv["dt"], dcum, dcum_t, ddt, p["dt_bias"], p["a_log"], dm)
    dxbc, g["conv_w"], g["conv_b"] = _conv_bwd(proj, jnp.concatenate([dxs, db, dc], axis=1), p["conv_w"], p["conv_b"], dm)
    dpu, dpg, g["pool_w"], g["pool_scale"] = _pool_bwd(proj, d_pool_o, p["pool_w"], p["pool_scale"], dm)
    dq, dk, dv, dsg = _sb_bwd(proj, sv["o"], sv["lt"], d_sb_o, dm)
    dproj = jnp.concatenate([dz, dxbc, dpu, dpg, dq, dk, dv, dsg, dl0, dl1, dl2, d_dtraw,
                             jnp.zeros((dm.T, DT_PAD - LANES), ACT_DTYPE)], axis=1)
    dh = _mm(dproj, p["w_in"], tb=True, name="mm_dh")
    g["w_in"] = _mm(sv["h"], dproj, ta=True, name="mm_dwin")
    dx, g["norm_w"] = _rms_bwd(sv["x"], dh, dx_out, p["norm_w"], dm)
    return dx, g


def _local_step(x, tgt, layers, final_norm_w, dm):
    saved = []
    for p in layers:
        x, sv = _layer_fwd(x, p, dm)
        saved.append(sv)
    dx, dfn, loss = _loss_head(x, tgt, final_norm_w, dm)
    grads = [None] * len(layers)
    for l in reversed(range(len(layers))):
        dx, grads[l] = _layer_bwd(dx, layers[l], saved[l], dm)
    return loss, dx, grads, dfn


def _row_tile(rows, cols):
    cap = max(8, (1 << 18) // cols)
    for t in (1024, 512, 256, 128, 64, 32, 16, 8):
        if t <= cap and rows % t == 0:
            return t
    return rows


def _adamw(w, g, m, v, name):
    rows, cols = w.shape
    tr = _row_tile(rows, cols)
    c1 = 1.0 - ADAM_B1 ** ADAM_STEP
    c2 = 1.0 - ADAM_B2 ** ADAM_STEP

    def body(w_ref, g_ref, m_ref, v_ref, d_ref, mo_ref, vo_ref):
        gv = g_ref[...]
        mn = ADAM_B1 * m_ref[...] + (1.0 - ADAM_B1) * gv
        vn = ADAM_B2 * v_ref[...] + (1.0 - ADAM_B2) * (gv * gv)
        d_ref[...] = -ADAM_LR * ((mn / c1) / (jnp.sqrt(vn / c2) + ADAM_EPS) + ADAM_WD * w_ref[...])
        mo_ref[...] = mn
        vo_ref[...] = vn

    blk = pl.BlockSpec((tr, cols), lambda i: (i, 0))
    out = jax.ShapeDtypeStruct((rows, cols), f32)
    return pl.pallas_call(body, name=name, grid=(rows // tr,), in_specs=[blk] * 4, out_specs=[blk] * 3, out_shape=[out] * 3,
                          compiler_params=_cp())(w, g, m, v)


def _sum_parts(a, parts, name):
    n, rows, cols = parts.shape
    tr = _row_tile(rows, cols)

    def body(a_ref, p_ref, o_ref):
        acc = a_ref[...]
        for k in range(n):
            acc = acc + p_ref[k].astype(f32)
        o_ref[...] = acc

    blk = pl.BlockSpec((tr, cols), lambda i: (i, 0))
    return pl.pallas_call(body, name=name, grid=(rows // tr,), in_specs=[blk, pl.BlockSpec((n, tr, cols), lambda i: (0, i, 0))],
                          out_specs=blk, out_shape=jax.ShapeDtypeStruct((rows, cols), f32), compiler_params=_cp())(a, parts)


ICI_KINDS = ("y", "x", "xy")
HBM_SPEC = pl.BlockSpec(memory_space=pltpu.HBM)


def _me():
    return lax.axis_index("x"), lax.axis_index("y"), lax.axis_index("c")


def _peer(kind):
    x, y, c = _me()
    return {"c": (x, y, 1 - c), "y": (x, 1 - y, c), "x": (1 - x, y, c), "xy": (1 - x, 1 - y, c)}[kind]


def _peer_chip(kind):
    x, y, _ = _me()
    return {"y": 2 * x + (1 - y), "x": 2 * (1 - x) + y, "xy": 2 * (1 - x) + (1 - y)}[kind]


def _exchange(sends, kinds, name):
    n, na = len(kinds), len(sends)

    def body(*refs):
        srcs, dsts, (ssem, rsem) = refs[:na], refs[na:2 * na], refs[2 * na:]
        cps = [pltpu.make_async_remote_copy(src_ref=srcs[i].at[k], dst_ref=dsts[i].at[k], send_sem=ssem.at[i * n + k],
                                            recv_sem=rsem.at[i * n + k], device_id=_peer(kind), device_id_type=MESH)
               for i in range(na) for k, kind in enumerate(kinds)]
        for cp in cps:
            cp.start()
        for cp in cps:
            cp.wait()

    return pl.pallas_call(
        body, name=name, out_shape=[jax.ShapeDtypeStruct(a.shape, a.dtype) for a in sends], in_specs=[HBM_SPEC] * na,
        out_specs=[HBM_SPEC] * na,
        scratch_shapes=[pltpu.SemaphoreType.DMA((na * n,)), pltpu.SemaphoreType.DMA((na * n,))],
    )(*sends)


def _allgather_shards(shards):
    na = len(shards)
    lh = shards[0].shape[0] // 2

    def body(*refs):
        srcs, outs, (ssem, rsem, lsem) = refs[:na], refs[na:2 * na], refs[2 * na:]
        x, y, c = _me()
        j_me = 2 * x + y
        mine = pl.ds(c * lh, lh)
        theirs = pl.ds((1 - c) * lh, lh)
        local = [pltpu.make_async_copy(srcs[i], outs[i].at[j_me], lsem.at[i]) for i in range(na)]
        for cp in local:
            cp.start()

        def ici(i, k, kind, j_src):
            return pltpu.make_async_remote_copy(src_ref=srcs[i].at[mine], dst_ref=outs[i].at[j_src, mine],
                                                send_sem=ssem.at[6 * i + k], recv_sem=rsem.at[6 * i + k],
                                                device_id=_peer(kind), device_id_type=MESH)

        def d2d(i, k, j_src, half):
            return pltpu.make_async_remote_copy(src_ref=outs[i].at[j_src, half], dst_ref=outs[i].at[j_src, half],
                                                send_sem=ssem.at[6 * i + 3 + k], recv_sem=rsem.at[6 * i + 3 + k],
                                                device_id=_peer("c"), device_id_type=MESH)

        first = [ici(i, k, kind, j_me) for i in range(na) for k, kind in enumerate(ICI_KINDS)]
        for cp in first:
            cp.start()
        passed = []
        for i in range(na):
            for k, kind in enumerate(ICI_KINDS):
                ici(i, k, kind, _peer_chip(kind)).wait_recv()
                fwd = d2d(i, k, _peer_chip(kind), mine)
                fwd.start()
                passed.append(fwd)
        for i in range(na):
            for k, kind in enumerate(ICI_KINDS):
                d2d(i, k, _peer_chip(kind), theirs).wait_recv()
        for cp in first + passed:
            cp.wait_send()
        for cp in local:
            cp.wait()

    return pl.pallas_call(
        body, name="allgather_shards", out_shape=[jax.ShapeDtypeStruct((4,) + a.shape, a.dtype) for a in shards],
        in_specs=[HBM_SPEC] * na, out_specs=[HBM_SPEC] * na,
        scratch_shapes=[pltpu.SemaphoreType.DMA((6 * na,)), pltpu.SemaphoreType.DMA((6 * na,)),
                        pltpu.SemaphoreType.DMA((na,))],
    )(*shards)


def _allreduce_small(v):
    rows = v.shape[0]

    def body(v_ref, o_ref, buf, ssem, rsem):
        x, y, c = _me()
        me = 4 * x + 2 * y + c
        buf[0] = v_ref[...]
        cps = []
        for k in range(1, 8):
            peer = (1 - x if k & 4 else x, 1 - y if k & 2 else y, 1 - c if k & 1 else c)
            cps.append(pltpu.make_async_remote_copy(src_ref=v_ref, dst_ref=buf.at[k], send_sem=ssem.at[k - 1], recv_sem=rsem.at[k - 1],
                                                    device_id=peer, device_id_type=MESH))
        for cp in cps:
            cp.start()
        for cp in cps:
            cp.wait()
        acc = buf[jnp.bitwise_xor(me, 0)]
        for d in range(1, 8):
            acc = acc + buf[jnp.bitwise_xor(me, d)]
        o_ref[...] = acc

    vm = pl.BlockSpec(memory_space=pltpu.VMEM)
    return pl.pallas_call(
        body, name="allreduce_small", out_shape=jax.ShapeDtypeStruct(v.shape, f32), in_specs=[vm], out_specs=vm,
        scratch_shapes=[pltpu.VMEM((8, rows, LANES), f32), pltpu.SemaphoreType.DMA((7,)), pltpu.SemaphoreType.DMA((7,))],
    )(v)


SHARDED = ("w_in", "w_proj_ssm", "w_proj_pool", "w_proj_sb", "w_out", "pool_w", "conv_w")
REST = SHARDED[1:6]
SMALL = ("norm_w", "conv_b", "dt_bias", "a_log", "d_skip", "ssm_norm_w", "pool_scale")


def _pack_rows(dm):
    rows = dict(w_proj_ssm=dm.W2 // 4, w_proj_pool=dm.D // 4, w_proj_sb=dm.D // 4, w_out=dm.D // 4, pool_w=dm.D // 16)
    assert all(r % 16 == 0 for r in rows.values())
    return rows


def _pack(sh, dm, dtype):
    rows = _pack_rows(dm)
    ly = sh["w_out"].shape[0]
    return jnp.concatenate([sh[n].astype(dtype).reshape(ly, rows[n], dm.D) for n in REST], axis=1)


def _unpack(pk, dm):
    rows = _pack_rows(dm)
    lead = pk.shape[:-2]
    shapes = dict(w_proj_ssm=(dm.W2 // 4, dm.D), w_proj_pool=(dm.D // 4, dm.D), w_proj_sb=(dm.D // 4, dm.D),
                  w_out=(dm.D // 4, dm.D), pool_w=(POOL_GROUPS, dm.Dg // 4, dm.Dg))
    out, r0 = {}, 0
    for n in REST:
        out[n] = pk[..., r0:r0 + rows[n], :].reshape(lead + shapes[n])
        r0 += rows[n]
    return out


def _join_shards(sh, dm):
    full = {}
    w = jnp.moveaxis(sh["w_in"], 0, 2)
    full["w_in"] = _permute_cols(w.reshape(w.shape[0], dm.D, dm.IN_COLS), dm)
    for n in ("w_proj_ssm", "w_proj_pool", "w_proj_sb", "w_out"):
        a = jnp.moveaxis(sh[n], 0, 1)
        full[n] = a.reshape(a.shape[0], -1, dm.D)
    a = jnp.moveaxis(sh["pool_w"], 0, 2)
    full["pool_w"] = a.reshape(a.shape[0], POOL_GROUPS, dm.Dg, dm.Dg)
    a = jnp.moveaxis(sh["conv_w"], 0, 2)
    full["conv_w"] = a.reshape(a.shape[0], CONV_WIDTH, dm.CC)
    return full


def _split_shards(full, dm):
    sh = {}
    w = _unpermute_cols(full["w_in"], dm)
    sh["w_in"] = jnp.moveaxis(w.reshape(w.shape[0], dm.D, 4, dm.IN_COLS // 4), 2, 0)
    for n in ("w_proj_ssm", "w_proj_pool", "w_proj_sb", "w_out"):
        a = full[n]
        sh[n] = jnp.moveaxis(a.reshape(a.shape[0], 4, a.shape[1] // 4, dm.D), 1, 0)
    a = full["pool_w"]
    sh["pool_w"] = jnp.moveaxis(a.reshape(a.shape[0], POOL_GROUPS, 4, dm.Dg // 4, dm.Dg), 2, 0)
    a = full["conv_w"]
    sh["conv_w"] = jnp.moveaxis(a.reshape(a.shape[0], CONV_WIDTH, 4, dm.CC // 4), 2, 0)
    return sh


def _reduce_scatter(full_grads, dm):
    x, y, c = _me()
    j_me = 2 * x + y
    sh = _split_shards(full_grads, dm)
    by_chip = [sh["w_in"], jnp.stack([_pack({n: sh[n][j] for n in REST}, dm, f32) for j in range(4)]), sh["conv_w"]]
    tags = ("in", "rest", "conv")
    lh = by_chip[0].shape[1] // 2

    def add(mine, parts, name):
        cols = mine.shape[-1]
        return _sum_parts(mine.reshape(-1, cols), parts.reshape(parts.shape[0], -1, cols), name).reshape(mine.shape)

    def pick(a, j):
        return lax.dynamic_index_in_dim(a, j, 0, keepdims=False)

    mine = [lax.dynamic_slice_in_dim(a, c * lh, lh, axis=1) for a in by_chip]
    theirs = [lax.dynamic_slice_in_dim(a, (1 - c) * lh, lh, axis=1).astype(WIRE_DTYPE)[None] for a in by_chip]
    got = _exchange(theirs, ("c",), "rs_pair")
    s1 = [add(a, g, "rs_pair_sum_" + t) for a, g, t in zip(mine, got, tags)]
    send = [jnp.stack([pick(a, jnp.bitwise_xor(j_me, k)) for k in (1, 2, 3)]).astype(WIRE_DTYPE) for a in s1]
    got = _exchange(send, ICI_KINDS, "rs_ici")
    red = [add(pick(a, j_me), g, "rs_ici_sum_" + t) for a, g, t in zip(s1, got, tags)]
    sib = _exchange([r[None] for r in red], ("c",), "rs_sibling")
    both = [jnp.where(c == 0, jnp.concatenate([r, s[0]]), jnp.concatenate([s[0], r])) for r, s in zip(red, sib)]
    out = _unpack(both[1], dm)
    out["w_in"], out["conv_w"] = both[0], both[2]
    return out


def _flatten_small(parts):
    flat = jnp.concatenate([p.reshape(-1).astype(f32) for p in parts])
    rows = -(-flat.shape[0] // (8 * LANES)) * 8
    return jnp.pad(flat, (0, rows * LANES - flat.shape[0])).reshape(rows, LANES)


def _unflatten_small(buf, shapes):
    flat = buf.reshape(-1)
    out, o = [], 0
    for s in shapes:
        n = math.prod(s)
        out.append(flat[o:o + n].reshape(s))
        o += n
    return out


def kernel(x, norm_w, w_in, conv_w, conv_b, dt_bias, a_log, d_skip, ssm_norm_w, pool_w, pool_scale, w_proj_ssm, w_proj_pool, w_proj_sb, w_out, final_norm_w, loss_target, m_norm_w, m_w_in, m_conv_w, m_conv_b, m_dt_bias, m_a_log, m_d_skip, m_ssm_norm_w, m_pool_w, m_pool_scale, m_w_proj_ssm, m_w_proj_pool, m_w_proj_sb, m_w_out, m_final_norm_w, v_norm_w, v_w_in, v_conv_w, v_conv_b, v_dt_bias, v_a_log, v_d_skip, v_ssm_norm_w, v_pool_w, v_pool_scale, v_w_proj_ssm, v_w_proj_pool, v_w_proj_sb, v_w_out, v_final_norm_w):
    names = ("norm_w", "w_in", "conv_w", "conv_b", "dt_bias", "a_log", "d_skip", "ssm_norm_w", "pool_w", "pool_scale",
             "w_proj_ssm", "w_proj_pool", "w_proj_sb", "w_out", "final_norm_w")
    w = dict(zip(names, (norm_w, w_in, conv_w, conv_b, dt_bias, a_log, d_skip, ssm_norm_w, pool_w, pool_scale, w_proj_ssm,
                         w_proj_pool, w_proj_sb, w_out, final_norm_w)))
    m = dict(zip(names, (m_norm_w, m_w_in, m_conv_w, m_conv_b, m_dt_bias, m_a_log, m_d_skip, m_ssm_norm_w, m_pool_w, m_pool_scale,
                         m_w_proj_ssm, m_w_proj_pool, m_w_proj_sb, m_w_out, m_final_norm_w)))
    v = dict(zip(names, (v_norm_w, v_w_in, v_conv_w, v_conv_b, v_dt_bias, v_a_log, v_d_skip, v_ssm_norm_w, v_pool_w, v_pool_scale,
                         v_w_proj_ssm, v_w_proj_pool, v_w_proj_sb, v_w_out, v_final_norm_w)))
    bl, s, d = x.shape
    dm = Dims(bl, s, d)
    ly = norm_w.shape[0]

    g_in, g_rest, g_conv = _allgather_shards([w["w_in"].astype(WIRE_DTYPE), _pack({n: w[n] for n in REST}, dm, WIRE_DTYPE), conv_w])
    shards = _unpack(g_rest, dm)
    shards["w_in"], shards["conv_w"] = g_in, g_conv
    full = _join_shards(shards, dm)
    layers = []
    for l in range(ly):
        p = {n: full[n][l] for n in SHARDED}
        p.update({n: w[n][l] for n in SMALL})
        layers.append(p)

    loss_part, dx, grads, dfn = _local_step(x.reshape(dm.T, d), loss_target.reshape(dm.T, d), layers, final_norm_w, dm)

    h = dm.H
    small_parts = [loss_part]
    small_shapes = [(1, LANES)]
    for n in SMALL:
        if n in ("dt_bias", "a_log", "d_skip"):
            small_parts.append(jnp.stack([g[n][0, :h] for g in grads]))
        else:
            small_parts.append(jnp.stack([g[n][0] for g in grads]))
        small_shapes.append(w[n].shape)
    small_parts.append(dfn[0])
    small_shapes.append(final_norm_w.shape)
    red_small = _allreduce_small(_flatten_small(small_parts))
    small_g = _unflatten_small(red_small, small_shapes)
    loss = small_g[0][0, 0]
    g_out = dict(zip(SMALL + ("final_norm_w",), small_g[1:]))

    g_out.update(_reduce_scatter({n: jnp.stack([g[n] for g in grads]) for n in SHARDED}, dm))

    small_names = SMALL + ("final_norm_w",)
    zero_row = jnp.zeros((1, LANES), f32)
    pack_small = lambda t: _flatten_small([zero_row] + [t[n] for n in small_names])
    ds, ms, vs = _adamw(pack_small(w), red_small, pack_small(m), pack_small(v), "adamw_small")
    delta, new_m, new_v = {}, {}, {}
    for tgt, buf in ((delta, ds), (new_m, ms), (new_v, vs)):
        tgt.update(zip(small_names, _unflatten_small(buf, small_shapes)[1:]))
    for n in SHARDED:
        shp = w[n].shape
        two = (math.prod(shp[:-1]), shp[-1])
        dd, mm, vv = _adamw(w[n].reshape(two), g_out[n].reshape(two), m[n].reshape(two), v[n].reshape(two), "adamw_" + n)
        delta[n], new_m[n], new_v[n] = dd.reshape(shp), mm.reshape(shp), vv.reshape(shp)
        g_out[n] = g_out[n].reshape(shp)

    return (loss, dx.reshape(bl, s, d), *[g_out[n] for n in names], *[delta[n] for n in names],
            *[new_m[n] for n in names], *[new_v[n] for n in names])
```

```python
import functools
import math

import jax
import jax.numpy as jnp
from jax import lax
from jax.experimental import pallas as pl
from jax.experimental.pallas import tpu as pltpu
from jax.experimental.pallas import tpu_sc as plsc

f32 = jnp.float32
bf16 = jnp.bfloat16
MXU_DTYPE = jnp.bfloat16
ACT_DTYPE = jnp.bfloat16
WIRE_DTYPE = jnp.bfloat16

EPS = 1e-6
LANES = 128
HEAD = 64
SSM_GROUPS = 2
CONV_WIDTH = 4
POOL_GROUPS = 4
DT_PAD = 512
N_BRANCHES = 3
VMEM_LIMIT = 56 * 1024 * 1024

ADAM_LR, ADAM_B1, ADAM_B2, ADAM_EPS, ADAM_WD, ADAM_STEP = 0.001, 0.9, 0.999, 1e-08, 0.01, 10
MESH = pl.DeviceIdType.MESH


def _cp(**kw):
    return pltpu.CompilerParams(vmem_limit_bytes=VMEM_LIMIT, **kw)


def _pick(n, prefs):
    for p in prefs:
        if n % p == 0:
            return p
    return n


def _dg(a, b, ca, cb):
    return lax.dot_general(a, b, (((ca,), (cb,)), ((), ())), preferred_element_type=f32)


def _nn(a, b):
    return _dg(a, b, 1, 0)


def _nt(a, b):
    return _dg(a, b, 1, 1)


def _tn(a, b):
    return _dg(a, b, 0, 0)


def _mx(a):
    return a.astype(MXU_DTYPE)


def _split3(a):
    hi = a.astype(bf16)
    r = a - hi.astype(f32)
    mid = r.astype(bf16)
    lo = (r - mid.astype(f32)).astype(bf16)
    return hi, mid, lo


def _exact_nn(a, u):
    hi, mid, lo = _split3(a)
    return _nn(hi, u) + _nn(mid, u) + _nn(lo, u)


def _iota2(shape, dim):
    return lax.broadcasted_iota(jnp.int32, shape, dim)


def _sigmoid(x):
    return 1.0 / (1.0 + jnp.exp(-x))


def _shift_down(v, sh, row):
    return jnp.where(row >= sh, pltpu.roll(v, sh, 0), 0.0)


def _shift_up(v, sh, row):
    n = v.shape[0]
    return jnp.where(row < n - sh, pltpu.roll(v, n - sh, 0), 0.0)


class Dims:
    def __init__(self, bl, s, d):
        self.Bl, self.S, self.D = bl, s, d
        self.T = bl * s
        self.W2 = 2 * d
        self.H = self.W2 // HEAD
        self.hpg = self.H // SSM_GROUPS
        self.CC = self.W2 + 2 * SSM_GROUPS * LANES
        self.Dg = d // POOL_GROUPS
        self.nc = s // LANES
        self.o_dt = self.W2 + self.CC
        self.IN_COLS = 13 * d + 2 * SSM_GROUPS * LANES + self.H
        self.c_z = 0
        self.c_xbc = self.W2
        self.c_pu = self.W2 + self.CC
        self.c_pg = self.c_pu + d
        self.c_qkv = self.c_pg + d
        self.c_sbg = self.c_qkv + 3 * d
        self.c_mrg = self.c_sbg + d
        self.c_dt = self.c_mrg + 3 * d
        self.NP = self.c_dt + DT_PAD
        assert self.c_dt == self.IN_COLS - self.H
        assert s % LANES == 0 and d % 512 == 0 and self.H <= LANES


def _permute_cols(w, dm):
    pad = jnp.zeros(w.shape[:-1] + (DT_PAD - dm.H,), w.dtype)
    return jnp.concatenate([w[..., :dm.o_dt], w[..., dm.o_dt + dm.H:], w[..., dm.o_dt:dm.o_dt + dm.H], pad], axis=-1)


def _unpermute_cols(w, dm):
    return jnp.concatenate([w[..., :dm.o_dt], w[..., dm.c_dt:dm.c_dt + dm.H], w[..., dm.o_dt:dm.c_dt]], axis=-1)


def _mm(a, b, *, ta=False, tb=False, out_dtype=f32, res=None, name):
    M, K = (a.shape[1], a.shape[0]) if ta else a.shape
    N = b.shape[0] if tb else b.shape[1]
    tm = _pick(M, (1024, 512, 256, 128))
    tn = _pick(N, (1024, 512, 256, 128))
    tk = _pick(K, (1024, 512, 256, 128))
    nk = K // tk

    def body(*refs):
        if res is None:
            a_ref, b_ref, o_ref, acc = refs
        else:
            a_ref, b_ref, r_ref, o_ref, acc = refs
        k = pl.program_id(2)

        @pl.when(k == 0)
        def _():
            acc[...] = jnp.zeros_like(acc)

        acc[...] += _dg(_mx(a_ref[...]), _mx(b_ref[...]), 0 if ta else 1, 1 if tb else 0)

        @pl.when(k == nk - 1)
        def _():
            v = acc[...]
            if res is not None:
                v = v + r_ref[...]
            o_ref[...] = v.astype(out_dtype)

    a_spec = pl.BlockSpec((tk, tm), lambda i, j, k: (k, i)) if ta else pl.BlockSpec((tm, tk), lambda i, j, k: (i, k))
    b_spec = pl.BlockSpec((tn, tk), lambda i, j, k: (j, k)) if tb else pl.BlockSpec((tk, tn), lambda i, j, k: (k, j))
    o_spec = pl.BlockSpec((tm, tn), lambda i, j, k: (i, j))
    in_specs = [a_spec, b_spec] + ([o_spec] if res is not None else [])
    args = (a, b) + ((res,) if res is not None else ())
    return pl.pallas_call(
        body, name=name, grid=(M // tm, N // tn, nk), in_specs=in_specs, out_specs=o_spec,
        out_shape=jax.ShapeDtypeStruct((M, N), out_dtype), scratch_shapes=[pltpu.VMEM((tm, tn), f32)],
        compiler_params=_cp(dimension_semantics=("parallel", "parallel", "arbitrary")),
    )(*args)


def _rms_fwd(x, w, dm):
    tr = _pick(dm.T, (256, 128))

    def body(x_ref, w_ref, o_ref):
        xf = x_ref[...]
        r = lax.rsqrt(jnp.mean(xf * xf, axis=-1, keepdims=True) + EPS)
        o_ref[...] = (xf * r * w_ref[...]).astype(o_ref.dtype)

    return pl.pallas_call(
        body, name="rms_fwd", grid=(dm.T // tr,),
        in_specs=[pl.BlockSpec((tr, dm.D), lambda i: (i, 0)), pl.BlockSpec((1, dm.D), lambda i: (0, 0))],
        out_specs=pl.BlockSpec((tr, dm.D), lambda i: (i, 0)),
        out_shape=jax.ShapeDtypeStruct((dm.T, dm.D), ACT_DTYPE), compiler_params=_cp(),
    )(x, w.reshape(1, dm.D))


def _rms_bwd(x, dh, dres, w, dm):
    tr = _pick(dm.T, (256, 128))

    def body(x_ref, dh_ref, dr_ref, w_ref, dx_ref, dw_ref):
        @pl.when(pl.program_id(0) == 0)
        def _():
            dw_ref[...] = jnp.zeros_like(dw_ref)

        xf = x_ref[...]
        r = lax.rsqrt(jnp.mean(xf * xf, axis=-1, keepdims=True) + EPS)
        xh = xf * r
        dh_ = dh_ref[...]
        dxh = dh_ * w_ref[...]
        dx_ref[...] = dr_ref[...] + r * (dxh - xh * jnp.mean(dxh * xh, axis=-1, keepdims=True))
        dw_ref[...] += jnp.sum(dh_ * xh, axis=0, keepdims=True)

    row = pl.BlockSpec((tr, dm.D), lambda i: (i, 0))
    vec = pl.BlockSpec((1, dm.D), lambda i: (0, 0))
    return pl.pallas_call(
        body, name="rms_bwd", grid=(dm.T // tr,), in_specs=[row, row, row, vec], out_specs=[row, vec],
        out_shape=[jax.ShapeDtypeStruct((dm.T, dm.D), f32), jax.ShapeDtypeStruct((1, dm.D), f32)],
        compiler_params=_cp(dimension_semantics=("arbitrary",)),
    )(x, dh, dres, w.reshape(1, dm.D))


def _loss_head(x, tgt, w, dm):
    tr = _pick(dm.T, (256, 128))

    def body(x_ref, t_ref, w_ref, dx_ref, dw_ref, ls_ref):
        @pl.when(pl.program_id(0) == 0)
        def _():
            dw_ref[...] = jnp.zeros_like(dw_ref)
            ls_ref[...] = jnp.zeros_like(ls_ref)

        xf = x_ref[...]
        r = lax.rsqrt(jnp.mean(xf * xf, axis=-1, keepdims=True) + EPS)
        xh = xf * r
        err = xh * w_ref[...] - t_ref[...]
        per_tok = jnp.mean(err * err, axis=-1, keepdims=True)
        ls_ref[...] += 0.5 * jnp.sum(per_tok, axis=0, keepdims=True)
        dy = err * (1.0 / dm.D)
        dxh = dy * w_ref[...]
        dx_ref[...] = r * (dxh - xh * jnp.mean(dxh * xh, axis=-1, keepdims=True))
        dw_ref[...] += jnp.sum(dy * xh, axis=0, keepdims=True)

    row = pl.BlockSpec((tr, dm.D), lambda i: (i, 0))
    vec = pl.BlockSpec((1, dm.D), lambda i: (0, 0))
    return pl.pallas_call(
        body, name="loss_head", grid=(dm.T // tr,), in_specs=[row, row, vec],
        out_specs=[row, vec, pl.BlockSpec((1, LANES), lambda i: (0, 0))],
        out_shape=[jax.ShapeDtypeStruct((dm.T, dm.D), f32), jax.ShapeDtypeStruct((1, dm.D), f32),
                   jax.ShapeDtypeStruct((1, LANES), f32)],
        compiler_params=_cp(dimension_semantics=("arbitrary",)),
    )(x, tgt, w.reshape(1, dm.D))


def _conv_pre(u, w_ref, b_ref, row):
    acc = b_ref[...] + w_ref[CONV_WIDTH - 1:CONV_WIDTH, :] * u
    for k in range(CONV_WIDTH - 1):
        acc = acc + w_ref[k:k + 1, :] * _shift_down(u, CONV_WIDTH - 1 - k, row)
    return acc


def _conv_fwd(proj, cw, cb, dm):
    cwid = LANES
    off = dm.c_xbc // cwid

    def body(u_ref, w_ref, b_ref, o_ref):
        u = u_ref[...]
        row = _iota2(u.shape, 0)
        pre = _conv_pre(u, w_ref, b_ref, row)
        o_ref[...] = pre * _sigmoid(pre)

    return pl.pallas_call(
        body, name="conv_fwd", grid=(dm.Bl, dm.CC // cwid),
        in_specs=[pl.BlockSpec((dm.S, cwid), lambda b, j: (b, off + j)),
                  pl.BlockSpec((CONV_WIDTH, cwid), lambda b, j: (0, j)), pl.BlockSpec((1, cwid), lambda b, j: (0, j))],
        out_specs=pl.BlockSpec((dm.S, cwid), lambda b, j: (b, j)),
        out_shape=jax.ShapeDtypeStruct((dm.T, dm.CC), f32), compiler_params=_cp(),
    )(proj, cw, cb.reshape(1, dm.CC))


def _conv_bwd(proj, d_out, cw, cb, dm):
    cwid = LANES
    off = dm.c_xbc // cwid

    def body(u_ref, d_ref, w_ref, b_ref, du_ref, dw_ref, db_ref):
        @pl.when(pl.program_id(1) == 0)
        def _():
            dw_ref[...] = jnp.zeros_like(dw_ref)
            db_ref[...] = jnp.zeros_like(db_ref)

        u = u_ref[...]
        row = _iota2(u.shape, 0)
        pre = _conv_pre(u, w_ref, b_ref, row)
        sg = _sigmoid(pre)
        dpre = d_ref[...] * (sg * (1.0 + pre * (1.0 - sg)))
        du = w_ref[CONV_WIDTH - 1:CONV_WIDTH, :] * dpre
        dw_ref[CONV_WIDTH - 1:CONV_WIDTH, :] += jnp.sum(dpre * u, axis=0, keepdims=True)
        for k in range(CONV_WIDTH - 1):
            sh = CONV_WIDTH - 1 - k
            du = du + w_ref[k:k + 1, :] * _shift_up(dpre, sh, row)
            dw_ref[k:k + 1, :] += jnp.sum(dpre * _shift_down(u, sh, row), axis=0, keepdims=True)
        du_ref[...] = du.astype(du_ref.dtype)
        db_ref[...] += jnp.sum(dpre, axis=0, keepdims=True)

    return pl.pallas_call(
        body, name="conv_bwd", grid=(dm.CC // cwid, dm.Bl),
        in_specs=[pl.BlockSpec((dm.S, cwid), lambda j, b: (b, off + j)), pl.BlockSpec((dm.S, cwid), lambda j, b: (b, j)),
                  pl.BlockSpec((CONV_WIDTH, cwid), lambda j, b: (0, j)), pl.BlockSpec((1, cwid), lambda j, b: (0, j))],
        out_specs=[pl.BlockSpec((dm.S, cwid), lambda j, b: (b, j)), pl.BlockSpec((CONV_WIDTH, cwid), lambda j, b: (0, j)),
                   pl.BlockSpec((1, cwid), lambda j, b: (0, j))],
        out_shape=[jax.ShapeDtypeStruct((dm.T, dm.CC), ACT_DTYPE), jax.ShapeDtypeStruct((CONV_WIDTH, dm.CC), f32),
                   jax.ShapeDtypeStruct((1, dm.CC), f32)],
        compiler_params=_cp(dimension_semantics=("arbitrary", "arbitrary")),
    )(proj, d_out, cw, cb.reshape(1, dm.CC))


def _pad_lanes(v):
    return jnp.pad(v, (0, LANES - v.shape[0])).reshape(1, LANES)


def _softplus(x):
    return jnp.maximum(x, 0.0) + jnp.log(1.0 + jnp.exp(-jnp.abs(x)))


def _dt_prep(proj, dt_bias, a_log, dm):
    off = dm.c_dt // LANES

    def body(r_ref, b_ref, al_ref, dt_ref, cum_ref, cumt_ref):
        dt = _softplus(r_ref[...] + b_ref[...])
        adt = dt * (-jnp.exp(al_ref[...]))
        tril = (_iota2((LANES, LANES), 1) <= _iota2((LANES, LANES), 0)).astype(bf16)
        cum = _exact_nn_left(tril, adt)
        dt_ref[...] = dt
        cum_ref[...] = cum
        cumt_ref[...] = cum.T

    blk = pl.BlockSpec((LANES, LANES), lambda i: (i, 0))
    vec = pl.BlockSpec((1, LANES), lambda i: (0, 0))
    return pl.pallas_call(
        body, name="dt_prep", grid=(dm.T // LANES,),
        in_specs=[pl.BlockSpec((LANES, LANES), lambda i: (i, off)), vec, vec],
        out_specs=[blk, blk, pl.BlockSpec((LANES, LANES), lambda i: (0, i))],
        out_shape=[jax.ShapeDtypeStruct((dm.T, LANES), f32), jax.ShapeDtypeStruct((dm.T, LANES), f32),
                   jax.ShapeDtypeStruct((LANES, dm.T), f32)],
        compiler_params=_cp(),
    )(proj, _pad_lanes(dt_bias), _pad_lanes(a_log))


def _exact_nn_left(u, a):
    hi, mid, lo = _split3(a)
    return _nn(u, hi) + _nn(u, mid) + _nn(u, lo)


def _dt_bwd(proj, dt, dcum, dcum_t, ddt, dt_bias, a_log, dm):
    off = dm.c_dt // LANES

    def body(r_ref, dt_ref, dc_ref, dct_ref, dd_ref, b_ref, al_ref, o_ref, db_ref, da_ref):
        @pl.when(pl.program_id(0) == 0)
        def _():
            db_ref[...] = jnp.zeros_like(db_ref)
            da_ref[...] = jnp.zeros_like(da_ref)

        a = -jnp.exp(al_ref[...])
        triu = (_iota2((LANES, LANES), 1) >= _iota2((LANES, LANES), 0)).astype(bf16)
        dadt = _exact_nn_left(triu, dc_ref[...] + dct_ref[...].T)
        d_dt = dd_ref[...] + dadt * a
        d_raw = d_dt * _sigmoid(r_ref[...] + b_ref[...])
        o_ref[...] = d_raw.astype(o_ref.dtype)
        db_ref[...] += jnp.sum(d_raw, axis=0, keepdims=True)
        da_ref[...] += jnp.sum(dadt * dt_ref[...], axis=0, keepdims=True) * a

    blk = pl.BlockSpec((LANES, LANES), lambda i: (i, 0))
    vec = pl.BlockSpec((1, LANES), lambda i: (0, 0))
    return pl.pallas_call(
        body, name="dt_bwd", grid=(dm.T // LANES,),
        in_specs=[pl.BlockSpec((LANES, LANES), lambda i: (i, off)), blk, blk, pl.BlockSpec((LANES, LANES), lambda i: (0, i)),
                  blk, vec, vec],
        out_specs=[blk, vec, vec],
        out_shape=[jax.ShapeDtypeStruct((dm.T, LANES), ACT_DTYPE), jax.ShapeDtypeStruct((1, LANES), f32),
                   jax.ShapeDtypeStruct((1, LANES), f32)],
        compiler_params=_cp(dimension_semantics=("arbitrary",)),
    )(proj, dt, dcum, dcum_t, ddt, _pad_lanes(dt_bias), _pad_lanes(a_log))


def _ssd_common(dm):
    L = LANES
    tri = _iota2((L, L), 0) >= _iota2((L, L), 1)
    lo = _iota2((L, L), 1) < HEAD
    return tri, lo


def _ssd_fwd(xbc, dt, cum, cumt, d_skip, dm):
    L, W2, hpg = LANES, dm.W2, dm.hpg
    nb = W2 // L

    def body(x_ref, b_ref, c_ref, dt_ref, cum_ref, cumt_ref, dsk_ref, y_ref, hs_ref, h_scr):
        @pl.when(pl.program_id(1) == 0)
        def _():
            h_scr[...] = jnp.zeros_like(h_scr)

        hs_ref[0] = h_scr[...]
        tri, lo = _ssd_common(dm)
        for g in range(SSM_GROUPS):
            bb = _mx(b_ref[:, g * L:(g + 1) * L])
            cb_ = _mx(c_ref[:, g * L:(g + 1) * L])
            cbm = _nt(cb_, bb)
            for i in range(hpg // 2):
                h0 = g * hpg + 2 * i
                h1 = h0 + 1
                sl = slice(h0 * HEAD, h0 * HEAD + L)
                x_p = x_ref[:, sl]
                cum0, cum1 = cum_ref[:, h0:h0 + 1], cum_ref[:, h1:h1 + 1]
                cums = jnp.where(lo, cum0, cum1)
                xdt = x_p * jnp.where(lo, dt_ref[:, h0:h0 + 1], dt_ref[:, h1:h1 + 1])
                tot = jnp.where(lo[0:1], cum_ref[L - 1:L, h0:h0 + 1], cum_ref[L - 1:L, h1:h1 + 1])
                y_p = jnp.zeros((L, L), f32)
                for hh, m in ((h0, lo), (h1, jnp.logical_not(lo))):
                    diff = cum_ref[:, hh:hh + 1] - cumt_ref[hh:hh + 1, :]
                    lm = jnp.where(tri, jnp.exp(jnp.minimum(diff, 0.0)), 0.0)
                    y_p = y_p + _nn(_mx(cbm * lm), _mx(jnp.where(m, xdt, 0.0)))
                hp = h_scr[:, sl]
                y_p = y_p + _nn(cb_, _mx(hp)) * jnp.exp(cums)
                y_p = y_p + x_p * jnp.where(lo[0:1], dsk_ref[0:1, h0:h0 + 1], dsk_ref[0:1, h1:h1 + 1])
                y_ref[:, sl] = y_p
                h_scr[:, sl] = hp * jnp.exp(tot) + _tn(bb, _mx(xdt * jnp.exp(tot - cums)))

    nc = dm.nc
    ob = W2 // (SSM_GROUPS * L)
    blk = pl.BlockSpec((L, L), lambda b, c: (b * nc + c, 0))
    return pl.pallas_call(
        body, name="ssd_fwd", grid=(dm.Bl, nc),
        in_specs=[pl.BlockSpec((L, W2), lambda b, c: (b * nc + c, 0)),
                  pl.BlockSpec((L, SSM_GROUPS * L), lambda b, c: (b * nc + c, ob)),
                  pl.BlockSpec((L, SSM_GROUPS * L), lambda b, c: (b * nc + c, ob + 1)),
                  blk, blk, pl.BlockSpec((L, L), lambda b, c: (0, b * nc + c)), pl.BlockSpec((1, L), lambda b, c: (0, 0))],
        out_specs=[pl.BlockSpec((L, W2), lambda b, c: (b * nc + c, 0)), pl.BlockSpec((1, L, W2), lambda b, c: (b * nc + c, 0, 0))],
        out_shape=[jax.ShapeDtypeStruct((dm.T, W2), f32), jax.ShapeDtypeStruct((dm.Bl * nc, L, W2), f32)],
        scratch_shapes=[pltpu.VMEM((L, W2), f32)],
        compiler_params=_cp(dimension_semantics=("arbitrary", "arbitrary")),
    )(xbc, xbc, xbc, dt, cum, cumt, _pad_lanes(d_skip))


def _ssd_bwd(xbc, dt, cum, cumt, d_skip, hs, dy, dm):
    L, W2, hpg = LANES, dm.W2, dm.hpg
    nc = dm.nc

    def body(x_ref, b_ref, c_ref, dt_ref, cum_ref, cumt_ref, dsk_ref, hs_ref, dy_ref,
             dx_ref, db_ref, dc_ref, ddt_ref, dcum_ref, dcr_ref, dd_ref, dh_scr, lane_cum, lane_dt, lane_d):
        @pl.when(pl.program_id(1) == 0)
        def _():
            dh_scr[...] = jnp.zeros_like(dh_scr)

        @pl.when((pl.program_id(0) == 0) & (pl.program_id(1) == 0))
        def _():
            dd_ref[...] = jnp.zeros_like(dd_ref)

        dcr_ref[...] = jnp.zeros_like(dcr_ref)
        tri, lo = _ssd_common(dm)
        last = _iota2((L, L), 0) == L - 1
        for g in range(SSM_GROUPS):
            gs = slice(g * L, (g + 1) * L)
            bb = _mx(b_ref[:, gs])
            cb_ = _mx(c_ref[:, gs])
            cbm = _nt(cb_, bb)
            dcb = jnp.zeros((L, L), f32)
            dc_g = jnp.zeros((L, L), f32)
            db_g = jnp.zeros((L, L), f32)
            for i in range(hpg // 2):
                h0 = g * hpg + 2 * i
                h1 = h0 + 1
                sl = slice(h0 * HEAD, h0 * HEAD + L)
                x_p = x_ref[:, sl]
                dy_p = dy_ref[:, sl]
                dt_p = jnp.where(lo, dt_ref[:, h0:h0 + 1], dt_ref[:, h1:h1 + 1])
                cums = jnp.where(lo, cum_ref[:, h0:h0 + 1], cum_ref[:, h1:h1 + 1])
                tot = jnp.where(lo[0:1], cum_ref[L - 1:L, h0:h0 + 1], cum_ref[L - 1:L, h1:h1 + 1])
                dsk_p = jnp.where(lo[0:1], dsk_ref[0:1, h0:h0 + 1], dsk_ref[0:1, h1:h1 + 1])
                xdt = x_p * dt_p
                ecum = jnp.exp(cums)
                dec = jnp.exp(tot - cums)
                etot = jnp.exp(tot)
                hp = hs_ref[0, :, sl]
                hp_b = _mx(hp)
                dhn = dh_scr[:, sl]
                dhn_b = _mx(dhn)
                y_off = _nn(cb_, hp_b) * ecum
                dch = _mx(dy_p * ecum)
                dc_g = dc_g + _nt(dch, hp_b)
                dh_off = _tn(cb_, dch)
                bds = _nn(bb, dhn_b)
                xdec = xdt * dec
                db_g = db_g + _nt(_mx(xdec), dhn_b)
                dxdt = bds * dec
                sdec = bds * xdec
                tot_lane = jnp.sum(sdec, axis=0, keepdims=True) + jnp.sum(dhn * hp, axis=0, keepdims=True) * etot
                dh_scr[:, sl] = etot * dhn + dh_off
                rsum = []
                for hh, m in ((h0, lo), (h1, jnp.logical_not(lo))):
                    diff = cum_ref[:, hh:hh + 1] - cumt_ref[hh:hh + 1, :]
                    lm = jnp.where(tri, jnp.exp(jnp.minimum(diff, 0.0)), 0.0)
                    w32 = cbm * lm
                    dyh = _mx(jnp.where(m, dy_p, 0.0))
                    dw = _nt(dyh, _mx(jnp.where(m, xdt, 0.0)))
                    dcb = dcb + dw * lm
                    e = dw * w32
                    rsum.append(jnp.sum(e, axis=1, keepdims=True))
                    dcr_ref[hh:hh + 1, :] = -jnp.sum(e, axis=0, keepdims=True)
                    dxdt = dxdt + _tn(_mx(w32), dyh)
                lane_cum[:, sl] = (dy_p * y_off + jnp.where(lo, rsum[0], rsum[1]) * (1.0 / HEAD) - sdec
                                   + jnp.where(last, tot_lane, 0.0))
                lane_dt[:, sl] = dxdt * x_p
                lane_d[:, sl] = dy_p * x_p
                dx_ref[:, sl] = dxdt * dt_p + dsk_p * dy_p
            dcb_b = _mx(dcb)
            dc_ref[:, gs] = dc_g + _nn(dcb_b, bb)
            db_ref[:, gs] = db_g + _tn(dcb_b, cb_)
        sel = (_iota2((W2, L), 0) // HEAD == _iota2((W2, L), 1)).astype(bf16)
        dcum_ref[...] = _exact_nn(lane_cum[...], sel)
        ddt_ref[...] = _exact_nn(lane_dt[...], sel)
        dd_ref[...] += jnp.sum(_exact_nn(lane_d[...], sel), axis=0, keepdims=True)

    ob = W2 // (SSM_GROUPS * L)

    def rc(b, c):
        return b * nc + (nc - 1 - c)

    blk = pl.BlockSpec((L, L), lambda b, c: (rc(b, c), 0))
    blk_t = pl.BlockSpec((L, L), lambda b, c: (0, rc(b, c)))
    wide = pl.BlockSpec((L, W2), lambda b, c: (rc(b, c), 0))
    grp = pl.BlockSpec((L, SSM_GROUPS * L), lambda b, c: (rc(b, c), 0))
    return pl.pallas_call(
        body, name="ssd_bwd", grid=(dm.Bl, nc),
        in_specs=[wide, pl.BlockSpec((L, SSM_GROUPS * L), lambda b, c: (rc(b, c), ob)),
                  pl.BlockSpec((L, SSM_GROUPS * L), lambda b, c: (rc(b, c), ob + 1)),
                  blk, blk, blk_t, pl.BlockSpec((1, L), lambda b, c: (0, 0)),
                  pl.BlockSpec((1, L, W2), lambda b, c: (rc(b, c), 0, 0)), wide],
        out_specs=[wide, grp, grp, blk, blk, blk_t, pl.BlockSpec((1, L), lambda b, c: (0, 0))],
        out_shape=[jax.ShapeDtypeStruct((dm.T, W2), f32), jax.ShapeDtypeStruct((dm.T, SSM_GROUPS * L), f32),
                   jax.ShapeDtypeStruct((dm.T, SSM_GROUPS * L), f32), jax.ShapeDtypeStruct((dm.T, L), f32),
                   jax.ShapeDtypeStruct((dm.T, L), f32), jax.ShapeDtypeStruct((L, dm.T), f32),
                   jax.ShapeDtypeStruct((1, L), f32)],
        scratch_shapes=[pltpu.VMEM((L, W2), f32)] * 4,
        compiler_params=_cp(dimension_semantics=("arbitrary", "arbitrary")),
    )(xbc, xbc, xbc, dt, cum, cumt, _pad_lanes(d_skip), hs, dy)


def _gnorm_fwd(y, proj, w, dm):
    tr = _pick(dm.T, (256, 128))
    row = pl.BlockSpec((tr, dm.W2), lambda i: (i, 0))

    def body(y_ref, z_ref, w_ref, o_ref):
        z = z_ref[...]
        yg = y_ref[...] * (z * _sigmoid(z))
        r = lax.rsqrt(jnp.mean(yg * yg, axis=-1, keepdims=True) + EPS)
        o_ref[...] = (yg * r * w_ref[...]).astype(o_ref.dtype)

    return pl.pallas_call(
        body, name="gnorm_fwd", grid=(dm.T // tr,), in_specs=[row, row, pl.BlockSpec((1, dm.W2), lambda i: (0, 0))],
        out_specs=row, out_shape=jax.ShapeDtypeStruct((dm.T, dm.W2), ACT_DTYPE), compiler_params=_cp(),
    )(y, proj, w.reshape(1, dm.W2))


def _gnorm_bwd(dn, y, proj, w, dm):
    tr = _pick(dm.T, (256, 128))
    row = pl.BlockSpec((tr, dm.W2), lambda i: (i, 0))
    vec = pl.BlockSpec((1, dm.W2), lambda i: (0, 0))

    def body(dn_ref, y_ref, z_ref, w_ref, dy_ref, dz_ref, dw_ref):
        @pl.when(pl.program_id(0) == 0)
        def _():
            dw_ref[...] = jnp.zeros_like(dw_ref)

        z = z_ref[...]
        sg = _sigmoid(z)
        sz = z * sg
        yv = y_ref[...]
        yg = yv * sz
        r = lax.rsqrt(jnp.mean(yg * yg, axis=-1, keepdims=True) + EPS)
        n = yg * r
        dout = dn_ref[...]
        dnn = dout * w_ref[...]
        dyg = r * (dnn - n * jnp.mean(dnn * n, axis=-1, keepdims=True))
        dy_ref[...] = dyg * sz
        dz_ref[...] = (dyg * yv * (sg * (1.0 + z * (1.0 - sg)))).astype(dz_ref.dtype)
        dw_ref[...] += jnp.sum(dout * n, axis=0, keepdims=True)

    return pl.pallas_call(
        body, name="gnorm_bwd", grid=(dm.T // tr,), in_specs=[row, row, row, vec], out_specs=[row, row, vec],
        out_shape=[jax.ShapeDtypeStruct((dm.T, dm.W2), f32), jax.ShapeDtypeStruct((dm.T, dm.W2), ACT_DTYPE),
                   jax.ShapeDtypeStruct((1, dm.W2), f32)],
        compiler_params=_cp(dimension_semantics=("arbitrary",)),
    )(dn, y, proj, w.reshape(1, dm.W2))


def _pool_mixed(u, g, row):
    s = u
    for k in range(POOL_GROUPS):
        s = jnp.where(k <= g, s + _shift_down(s, 1 << k, row), s)
    w = jnp.left_shift(2, g)
    cnt = jnp.minimum(row + 1, w).astype(f32)
    return s / cnt - u, cnt


def _pool_fwd(proj, pw, scale, dm):
    Dg = dm.Dg
    ou, og = dm.c_pu // Dg, dm.c_pg // Dg

    def body(u_ref, g_ref, w_ref, s_ref, o_ref):
        u = u_ref[...]
        row = _iota2(u.shape, 0)
        mixed, _ = _pool_mixed(u, pl.program_id(1), row)
        lin = _nn(_mx(mixed), _mx(w_ref[0]))
        gt = g_ref[...]
        o_ref[...] = (lin * s_ref[...] * (gt * _sigmoid(gt))).astype(o_ref.dtype)

    return pl.pallas_call(
        body, name="pool_fwd", grid=(dm.Bl, POOL_GROUPS),
        in_specs=[pl.BlockSpec((dm.S, Dg), lambda b, g: (b, ou + g)), pl.BlockSpec((dm.S, Dg), lambda b, g: (b, og + g)),
                  pl.BlockSpec((1, Dg, Dg), lambda b, g: (g, 0, 0)), pl.BlockSpec((1, Dg), lambda b, g: (0, g))],
        out_specs=pl.BlockSpec((dm.S, Dg), lambda b, g: (b, g)),
        out_shape=jax.ShapeDtypeStruct((dm.T, dm.D), ACT_DTYPE), compiler_params=_cp(),
    )(proj, proj, pw, scale.reshape(1, dm.D))


def _pool_bwd(proj, dout, pw, scale, dm):
    Dg = dm.Dg
    ou, og = dm.c_pu // Dg, dm.c_pg // Dg

    def body(u_ref, g_ref, d_ref, w_ref, s_ref, du_ref, dg_ref, dw_ref, ds_ref):
        @pl.when(pl.program_id(1) == 0)
        def _():
            dw_ref[...] = jnp.zeros_like(dw_ref)
            ds_ref[...] = jnp.zeros_like(ds_ref)

        g = pl.program_id(0)
        u = u_ref[...]
        row = _iota2(u.shape, 0)
        mixed, cnt = _pool_mixed(u, g, row)
        mixed_b = _mx(mixed)
        wb = _mx(w_ref[0])
        lin = _nn(mixed_b, wb)
        gt = g_ref[...]
        sg = _sigmoid(gt)
        silu = gt * sg
        d = d_ref[...]
        sc = s_ref[...]
        dlin = d * sc * silu
        ds_ref[...] += jnp.sum(d * lin * silu, axis=0, keepdims=True)
        dg_ref[...] = (d * lin * sc * (sg * (1.0 + gt * (1.0 - sg)))).astype(dg_ref.dtype)
        dlin_b = _mx(dlin)
        dmixed = _nt(dlin_b, wb)
        dw_ref[0] += _tn(mixed_b, dlin_b)
        r = dmixed / cnt
        for k in range(POOL_GROUPS):
            r = jnp.where(k <= g, r + _shift_up(r, 1 << k, row), r)
        du_ref[...] = (r - dmixed).astype(du_ref.dtype)

    return pl.pallas_call(
        body, name="pool_bwd", grid=(POOL_GROUPS, dm.Bl),
        in_specs=[pl.BlockSpec((dm.S, Dg), lambda g, b: (b, ou + g)), pl.BlockSpec((dm.S, Dg), lambda g, b: (b, og + g)),
                  pl.BlockSpec((dm.S, Dg), lambda g, b: (b, g)), pl.BlockSpec((1, Dg, Dg), lambda g, b: (g, 0, 0)),
                  pl.BlockSpec((1, Dg), lambda g, b: (0, g))],
        out_specs=[pl.BlockSpec((dm.S, Dg), lambda g, b: (b, g)), pl.BlockSpec((dm.S, Dg), lambda g, b: (b, g)),
                   pl.BlockSpec((1, Dg, Dg), lambda g, b: (g, 0, 0)), pl.BlockSpec((1, Dg), lambda g, b: (0, g))],
        out_shape=[jax.ShapeDtypeStruct((dm.T, dm.D), ACT_DTYPE), jax.ShapeDtypeStruct((dm.T, dm.D), ACT_DTYPE),
                   jax.ShapeDtypeStruct((POOL_GROUPS, Dg, Dg), f32), jax.ShapeDtypeStruct((1, dm.D), f32)],
        compiler_params=_cp(dimension_semantics=("arbitrary", "arbitrary")),
    )(proj, proj, dout, pw, scale.reshape(1, dm.D))


SB_TILE = 512


def _split2(a):
    hi = a.astype(bf16)
    return hi, (a - hi.astype(f32)).astype(bf16)


def _exact2_nn(a, u):
    hi, lo = _split2(a)
    return _nn(hi, u) + _nn(lo, u)


SB_CUM = 256


def _blocked_sums(a, u, suffix):
    nb = a.shape[1] // SB_CUM
    blocks = [a[:, i * SB_CUM:(i + 1) * SB_CUM] for i in range(nb)]
    tots = [jnp.sum(b, axis=1, keepdims=True) for b in blocks]
    out = []
    for i, b in enumerate(blocks):
        s = _exact2_nn(b, u)
        for t in (tots[i + 1:] if suffix else tots[:i]):
            s = s + t
        out.append(s)
    total = tots[0]
    for t in tots[1:]:
        total = total + t
    return (out[0] if nb == 1 else jnp.concatenate(out, axis=1)), total


def _sb_tile(qb, kb, base, u, suffix, causal):
    z = _nt(qb, kb)
    lb = jnp.minimum(z, 0.0) - jnp.log(1.0 + jnp.exp(-jnp.abs(z)))
    lm = lb - z
    if causal is not None:
        lm = jnp.where(causal, lm, 0.0)
    sums, tot = _blocked_sums(lm, u, suffix)
    att = jnp.exp(lb + base + sums) if suffix else jnp.exp(lb + base - sums)
    if causal is not None:
        att = jnp.where(causal, att, 0.0)
    return lb, att, tot


def _sb_fwd(proj, dm):
    L, S, D, TQ = LANES, dm.S, dm.D, SB_TILE
    oq, og = dm.c_qkv // L, dm.c_sbg // L
    nb = D // L
    scale = HEAD ** -0.5

    def body(q_ref, k_ref, v_ref, g_ref, o_ref, lt_ref, out_ref, k_s, va_s, vb_s):
        lo = _iota2((1, L), 1) < HEAD
        hi = jnp.logical_not(lo)
        k_s[...] = _mx(k_ref[...])
        vf = v_ref[...]
        va_s[...] = _mx(jnp.where(lo, vf, 0.0))
        vb_s[...] = _mx(jnp.where(hi, vf, 0.0))
        before = _iota2((TQ, TQ), 0) > _iota2((TQ, TQ), 1)
        ugt = (_iota2((SB_CUM, SB_CUM), 0) > _iota2((SB_CUM, SB_CUM), 1)).astype(bf16)
        v_s = (va_s, vb_s)

        def qloop(qt, _):
            rows = pl.ds(pl.multiple_of(qt * TQ, TQ), TQ)
            qs = q_ref[rows, :] * scale
            qb = [_mx(jnp.where(m, qs, 0.0)) for m in (lo, hi)]

            def tile(kt, carry, causal):
                runs, acc = carry
                krows = pl.ds(pl.multiple_of(kt * TQ, TQ), TQ)
                k_t = k_s[krows, :]
                new_runs = []
                for h in range(2):
                    _, att, tot = _sb_tile(qb[h], k_t, runs[h], ugt, True, causal)
                    acc = acc + _nn(_mx(att), v_s[h][krows, :])
                    new_runs.append(runs[h] + tot)
                return tuple(new_runs), acc

            zc = jnp.zeros((TQ, 1), f32)
            carry = tile(qt, ((zc, zc), jnp.zeros((TQ, L), f32)), before)
            runs, acc = lax.fori_loop(0, qt, lambda jj, c: tile(qt - 1 - jj, c, None), carry)
            o_ref[rows, :] = acc
            lt_ref[rows, :] = jnp.where(lo, runs[0], runs[1])
            gt = g_ref[rows, :]
            out_ref[rows, :] = (acc * (gt * _sigmoid(gt))).astype(out_ref.dtype)
            return 0

        lax.fori_loop(0, S // TQ, qloop, 0)

    def col(o):
        return pl.BlockSpec((S, L), lambda b, p: (b, o + p))

    return pl.pallas_call(
        body, name="sb_fwd", grid=(dm.Bl, nb),
        in_specs=[col(oq), col(oq + nb), col(oq + 2 * nb), col(og)], out_specs=[col(0), col(0), col(0)],
        out_shape=[jax.ShapeDtypeStruct((dm.T, D), f32), jax.ShapeDtypeStruct((dm.T, D), f32),
                   jax.ShapeDtypeStruct((dm.T, D), ACT_DTYPE)],
        scratch_shapes=[pltpu.VMEM((S, L), MXU_DTYPE)] * 3,
        compiler_params=_cp(),
    )(proj, proj, proj, proj)


def _sb_bwd(proj, o, lt, dsb, dm):
    L, S, D, TQ = LANES, dm.S, dm.D, SB_TILE
    oq, og = dm.c_qkv // L, dm.c_sbg // L
    nb = D // L
    scale = HEAD ** -0.5

    def body(q_ref, k_ref, v_ref, g_ref, o_ref, lt_ref, d_ref, dq_ref, dk_ref, dv_ref, dg_ref,
             k_s, ka_s, kb_s, v_s, dk_acc, dv_acc):
        lo = _iota2((1, L), 1) < HEAD
        hi = jnp.logical_not(lo)
        kf = k_ref[...]
        k_s[...] = _mx(kf)
        ka_s[...] = _mx(jnp.where(lo, kf, 0.0))
        kb_s[...] = _mx(jnp.where(hi, kf, 0.0))
        v_s[...] = _mx(v_ref[...])
        dk_acc[...] = jnp.zeros_like(dk_acc)
        dv_acc[...] = jnp.zeros_like(dv_acc)
        before = _iota2((TQ, TQ), 0) > _iota2((TQ, TQ), 1)
        ule = (_iota2((SB_CUM, SB_CUM), 0) <= _iota2((SB_CUM, SB_CUM), 1)).astype(bf16)
        ult = (_iota2((SB_CUM, SB_CUM), 0) < _iota2((SB_CUM, SB_CUM), 1)).astype(bf16)
        k_m = (ka_s, kb_s)

        def qloop(qt, _):
            rows = pl.ds(pl.multiple_of(qt * TQ, TQ), TQ)
            gt = g_ref[rows, :]
            sg = _sigmoid(gt)
            dsb_b = d_ref[rows, :]
            do = dsb_b * (gt * sg)
            dg_ref[rows, :] = (dsb_b * o_ref[rows, :] * (sg * (1.0 + gt * (1.0 - sg)))).astype(dg_ref.dtype)
            qs = q_ref[rows, :] * scale
            qb = [_mx(jnp.where(m, qs, 0.0)) for m in (lo, hi)]
            dob = [_mx(jnp.where(m, do, 0.0)) for m in (lo, hi)]
            ltot = [lt_ref[rows, 0:1], lt_ref[rows, HEAD:HEAD + 1]]

            def tile(kt, carry, causal):
                runs, rgs, dq = carry
                krows = pl.ds(pl.multiple_of(kt * TQ, TQ), TQ)
                k_t, v_t = k_s[krows, :], v_s[krows, :]
                dk_t = jnp.zeros((TQ, L), f32)
                dv_t = jnp.zeros((TQ, L), f32)
                new_runs, new_rgs = [], []
                for h in range(2):
                    lb, att, tot = _sb_tile(qb[h], k_t, ltot[h] - runs[h], ule, False, causal)
                    gm = att * _nt(dob[h], v_t)
                    gsum, gtot = _blocked_sums(gm, ult, False)
                    pre = rgs[h] + gsum
                    beta = jnp.exp(lb)
                    dz = gm * (1.0 - beta) - pre * beta
                    if causal is not None:
                        dz = jnp.where(causal, dz, 0.0)
                    dz = _mx(dz)
                    dq = dq + _nn(dz, k_m[h][krows, :])
                    dk_t = dk_t + _tn(dz, qb[h])
                    dv_t = dv_t + _tn(_mx(att), dob[h])
                    new_runs.append(runs[h] + tot)
                    new_rgs.append(rgs[h] + gtot)
                dk_acc[krows, :] += dk_t
                dv_acc[krows, :] += dv_t
                return tuple(new_runs), tuple(new_rgs), dq

            zc = jnp.zeros((TQ, 1), f32)
            carry = lax.fori_loop(0, qt, lambda kt, c: tile(kt, c, None), ((zc, zc), (zc, zc), jnp.zeros((TQ, L), f32)))
            _, _, dq = tile(qt, carry, before)
            dq_ref[rows, :] = (dq * scale).astype(dq_ref.dtype)
            return 0

        lax.fori_loop(0, S // TQ, qloop, 0)
        dk_ref[...] = dk_acc[...].astype(dk_ref.dtype)
        dv_ref[...] = dv_acc[...].astype(dv_ref.dtype)

    def col(off):
        return pl.BlockSpec((S, L), lambda b, p: (b, off + p))

    out = jax.ShapeDtypeStruct((dm.T, D), ACT_DTYPE)
    return pl.pallas_call(
        body, name="sb_bwd", grid=(dm.Bl, nb),
        in_specs=[col(oq), col(oq + nb), col(oq + 2 * nb), col(og), col(0), col(0), col(0)],
        out_specs=[col(0), col(0), col(0), col(0)], out_shape=[out, out, out, out],
        scratch_shapes=[pltpu.VMEM((S, L), MXU_DTYPE)] * 4 + [pltpu.VMEM((S, L), f32)] * 2,
        compiler_params=_cp(),
    )(proj, proj, proj, proj, o, lt, dsb)


def _merge_fwd(proj, ys, dm):
    tr = _pick(dm.T, (256, 128))
    ct = 512
    om = dm.c_mrg // ct
    nb = dm.D // ct
    blk = pl.BlockSpec((tr, ct), lambda i, j: (i, j))

    def body(l0, l1, l2, y0, y1, y2, o_ref):
        acc = _sigmoid(l0[...]) * y0[...] + _sigmoid(l1[...]) * y1[...] + _sigmoid(l2[...]) * y2[...]
        o_ref[...] = acc.astype(o_ref.dtype)

    return pl.pallas_call(
        body, name="merge_fwd", grid=(dm.T // tr, nb),
        in_specs=[pl.BlockSpec((tr, ct), functools.partial(lambda i, j, b: (i, om + b * nb + j), b=b)) for b in range(3)] + [blk] * 3,
        out_specs=blk, out_shape=jax.ShapeDtypeStruct((dm.T, dm.D), ACT_DTYPE), compiler_params=_cp(),
    )(proj, proj, proj, *ys)


def _merge_bwd(proj, ys, dmerged, dm):
    tr = _pick(dm.T, (256, 128))
    ct = 512
    om = dm.c_mrg // ct
    nb = dm.D // ct
    blk = pl.BlockSpec((tr, ct), lambda i, j: (i, j))

    def body(l0, l1, l2, y0, y1, y2, d_ref, dy0, dy1, dy2, dl0, dl1, dl2):
        d = d_ref[...]
        for l_ref, y_ref, dy_ref, dl_ref in ((l0, y0, dy0, dl0), (l1, y1, dy1, dl1), (l2, y2, dy2, dl2)):
            sg = _sigmoid(l_ref[...])
            dy_ref[...] = (d * sg).astype(dy_ref.dtype)
            dl_ref[...] = (d * y_ref[...] * sg * (1.0 - sg)).astype(dl_ref.dtype)

    out = jax.ShapeDtypeStruct((dm.T, dm.D), ACT_DTYPE)
    return pl.pallas_call(
        body, name="merge_bwd", grid=(dm.T // tr, nb),
        in_specs=[pl.BlockSpec((tr, ct), functools.partial(lambda i, j, b: (i, om + b * nb + j), b=b)) for b in range(3)] + [blk] * 4,
        out_specs=[blk] * 6, out_shape=[out] * 6, compiler_params=_cp(),
    )(proj, proj, proj, *ys, dmerged)


def _layer_fwd(x, p, dm):
    h = _rms_fwd(x, p["norm_w"], dm)
    proj = _mm(h, p["w_in"], name="mm_in")
    xbc = _conv_fwd(proj, p["conv_w"], p["conv_b"], dm)
    dt, cum, cumt = _dt_prep(proj, p["dt_bias"], p["a_log"], dm)
    y, hs = _ssd_fwd(xbc, dt, cum, cumt, p["d_skip"], dm)
    ssm_n = _gnorm_fwd(y, proj, p["ssm_norm_w"], dm)
    pool_o = _pool_fwd(proj, p["pool_w"], p["pool_scale"], dm)
    o, lt, sb_o = _sb_fwd(proj, dm)
    ys = (_mm(ssm_n, p["w_proj_ssm"], name="mm_ps"), _mm(pool_o, p["w_proj_pool"], name="mm_pp"),
          _mm(sb_o, p["w_proj_sb"], name="mm_pb"))
    merged = _merge_fwd(proj, ys, dm)
    x_next = _mm(merged, p["w_out"], res=x, name="mm_out")
    saved = dict(x=x, h=h, proj=proj, xbc=xbc, dt=dt, cum=cum, cumt=cumt, y=y, hs=hs, ssm_n=ssm_n, pool_o=pool_o, o=o,
                 lt=lt, sb_o=sb_o, ys=ys, merged=merged)
    return x_next, saved


def _layer_bwd(dx_out, p, sv, dm):
    g = {}
    proj = sv["proj"]
    dmerged = _mm(dx_out, p["w_out"], tb=True, name="mm_dmerged")
    g["w_out"] = _mm(sv["merged"], dx_out, ta=True, name="mm_dwout")
    dy0, dy1, dy2, dl0, dl1, dl2 = _merge_bwd(proj, sv["ys"], dmerged, dm)
    d_ssm_n = _mm(dy0, p["w_proj_ssm"], tb=True, name="mm_dssm")
    g["w_proj_ssm"] = _mm(sv["ssm_n"], dy0, ta=True, name="mm_dwps")
    d_pool_o = _mm(dy1, p["w_proj_pool"], tb=True, name="mm_dpool")
    g["w_proj_pool"] = _mm(sv["pool_o"], dy1, ta=True, name="mm_dwpp")
    d_sb_o = _mm(dy2, p["w_proj_sb"], tb=True, name="mm_dsb")
    g["w_proj_sb"] = _mm(sv["sb_o"], dy2, ta=True, name="mm_dwpb")
    dy, dz, g["ssm_norm_w"] = _gnorm_bwd(d_ssm_n, sv["y"], proj, p["ssm_norm_w"], dm)
    dxs, db, dc, ddt, dcum, dcum_t, dd = _ssd_bwd(sv["xbc"], sv["dt"], sv["cum"], sv["cumt"], p["d_skip"], sv["hs"], dy, dm)
    g["d_skip"] = dd
    d_dtraw, g["dt_bias"], g["a_log"] = _dt_bwd(proj, sv["dt"], dcum, dcum_t, ddt, p["dt_bias"], p["a_log"], dm)
    dxbc, g["conv_w"], g["conv_b"] = _conv_bwd(proj, jnp.concatenate([dxs, db, dc], axis=1), p["conv_w"], p["conv_b"], dm)
    dpu, dpg, g["pool_w"], g["pool_scale"] = _pool_bwd(proj, d_pool_o, p["pool_w"], p["pool_scale"], dm)
    dq, dk, dv, dsg = _sb_bwd(proj, sv["o"], sv["lt"], d_sb_o, dm)
    dproj = jnp.concatenate([dz, dxbc, dpu, dpg, dq, dk, dv, dsg, dl0, dl1, dl2, d_dtraw,
                             jnp.zeros((dm.T, DT_PAD - LANES), ACT_DTYPE)], axis=1)
    dh = _mm(dproj, p["w_in"], tb=True, name="mm_dh")
    g["w_in"] = _mm(sv["h"], dproj, ta=True, name="mm_dwin")
    dx, g["norm_w"] = _rms_bwd(sv["x"], dh, dx_out, p["norm_w"], dm)
    return dx, g


def _local_step(x, tgt, layers, final_norm_w, dm):
    saved = []
    layers = list(layers)
    for l, p in enumerate(layers):
        if callable(p):
            p, x = p(x)
            layers[l] = p
        x, sv = _layer_fwd(x, p, dm)
        saved.append(sv)
    dx, dfn, loss = _loss_head(x, tgt, final_norm_w, dm)
    grads = [None] * len(layers)
    for l in reversed(range(len(layers))):
        dx, grads[l] = _layer_bwd(dx, layers[l], saved[l], dm)
    return loss, dx, grads, dfn


def _row_tile(rows, cols):
    cap = max(8, (1 << 18) // cols)
    for t in (1024, 512, 256, 128, 64, 32, 16, 8):
        if t <= cap and rows % t == 0:
            return t
    return rows


def _adamw(w, g, m, v, name):
    rows, cols = w.shape
    tr = _row_tile(rows, cols)
    c1 = 1.0 - ADAM_B1 ** ADAM_STEP
    c2 = 1.0 - ADAM_B2 ** ADAM_STEP

    def body(w_ref, g_ref, m_ref, v_ref, d_ref, mo_ref, vo_ref):
        gv = g_ref[...]
        mn = ADAM_B1 * m_ref[...] + (1.0 - ADAM_B1) * gv
        vn = ADAM_B2 * v_ref[...] + (1.0 - ADAM_B2) * (gv * gv)
        d_ref[...] = -ADAM_LR * ((mn / c1) / (jnp.sqrt(vn / c2) + ADAM_EPS) + ADAM_WD * w_ref[...])
        mo_ref[...] = mn
        vo_ref[...] = vn

    blk = pl.BlockSpec((tr, cols), lambda i: (i, 0))
    out = jax.ShapeDtypeStruct((rows, cols), f32)
    return pl.pallas_call(body, name=name, grid=(rows // tr,), in_specs=[blk] * 4, out_specs=[blk] * 3, out_shape=[out] * 3,
                          compiler_params=_cp())(w, g, m, v)


def _sum_parts(a, parts, name):
    n, rows, cols = parts.shape
    tr = _row_tile(rows, cols)

    def body(a_ref, p_ref, o_ref):
        acc = a_ref[...]
        for k in range(n):
            acc = acc + p_ref[k].astype(f32)
        o_ref[...] = acc

    blk = pl.BlockSpec((tr, cols), lambda i: (i, 0))
    return pl.pallas_call(body, name=name, grid=(rows // tr,), in_specs=[blk, pl.BlockSpec((n, tr, cols), lambda i: (0, i, 0))],
                          out_specs=blk, out_shape=jax.ShapeDtypeStruct((rows, cols), f32), compiler_params=_cp())(a, parts)


ICI_KINDS = ("y", "x", "xy")
ASYNC_GATHER_ID = 1
HBM_SPEC = pl.BlockSpec(memory_space=pltpu.HBM)


def _me():
    return lax.axis_index("x"), lax.axis_index("y"), lax.axis_index("c")


def _peer(kind):
    x, y, c = _me()
    return {"c": (x, y, 1 - c), "y": (x, 1 - y, c), "x": (1 - x, y, c), "xy": (1 - x, 1 - y, c)}[kind]


def _peer_chip(kind):
    x, y, _ = _me()
    return {"y": 2 * x + (1 - y), "x": 2 * (1 - x) + y, "xy": 2 * (1 - x) + (1 - y)}[kind]


def _exchange(sends, kinds, name):
    n, na = len(kinds), len(sends)

    def body(*refs):
        srcs, dsts, (ssem, rsem) = refs[:na], refs[na:2 * na], refs[2 * na:]
        cps = [pltpu.make_async_remote_copy(src_ref=srcs[i].at[k], dst_ref=dsts[i].at[k], send_sem=ssem.at[i * n + k],
                                            recv_sem=rsem.at[i * n + k], device_id=_peer(kind), device_id_type=MESH)
               for i in range(na) for k, kind in enumerate(kinds)]
        for cp in cps:
            cp.start()
        for cp in cps:
            cp.wait()

    return pl.pallas_call(
        body, name=name, out_shape=[jax.ShapeDtypeStruct(a.shape, a.dtype) for a in sends], in_specs=[HBM_SPEC] * na,
        out_specs=[HBM_SPEC] * na,
        scratch_shapes=[pltpu.SemaphoreType.DMA((na * n,)), pltpu.SemaphoreType.DMA((na * n,))],
    )(*sends)


def _allgather_copies(srcs, outs, ssem, rsem, lsem, lh):
    na = len(srcs)
    x, y, c = _me()
    j_me = 2 * x + y
    mine = pl.ds(c * lh, lh)
    theirs = pl.ds((1 - c) * lh, lh)
    local = [pltpu.make_async_copy(srcs[i], outs[i].at[j_me], lsem.at[i]) for i in range(na)]
    for cp in local:
        cp.start()

    def ici(i, k, kind, j_src):
        return pltpu.make_async_remote_copy(src_ref=srcs[i].at[mine], dst_ref=outs[i].at[j_src, mine],
                                            send_sem=ssem.at[6 * i + k], recv_sem=rsem.at[6 * i + k],
                                            device_id=_peer(kind), device_id_type=MESH)

    def d2d(i, k, j_src, half):
        return pltpu.make_async_remote_copy(src_ref=outs[i].at[j_src, half], dst_ref=outs[i].at[j_src, half],
                                            send_sem=ssem.at[6 * i + 3 + k], recv_sem=rsem.at[6 * i + 3 + k],
                                            device_id=_peer("c"), device_id_type=MESH)

    first = [ici(i, k, kind, j_me) for i in range(na) for k, kind in enumerate(ICI_KINDS)]
    for cp in first:
        cp.start()
    passed = []
    for i in range(na):
        for k, kind in enumerate(ICI_KINDS):
            ici(i, k, kind, _peer_chip(kind)).wait_recv()
            fwd = d2d(i, k, _peer_chip(kind), mine)
            fwd.start()
            passed.append(fwd)
    for i in range(na):
        for k, kind in enumerate(ICI_KINDS):
            d2d(i, k, _peer_chip(kind), theirs).wait_recv()
    for cp in first + passed:
        cp.wait_send()
    for cp in local:
        cp.wait()


def _allgather_sems(na):
    return [pltpu.SemaphoreType.DMA((6 * na,)), pltpu.SemaphoreType.DMA((6 * na,)), pltpu.SemaphoreType.DMA((na,))]


def _allgather_shards(shards):
    na = len(shards)
    lh = shards[0].shape[0] // 2

    def body(*refs):
        _allgather_copies(refs[:na], refs[na:2 * na], *refs[2 * na:], lh)

    return pl.pallas_call(
        body, name="allgather_shards", out_shape=[jax.ShapeDtypeStruct((4,) + a.shape, a.dtype) for a in shards],
        in_specs=[HBM_SPEC] * na, out_specs=[HBM_SPEC] * na, scratch_shapes=_allgather_sems(na),
    )(*shards)


def _allgather_shards_async(shards):
    na = len(shards)
    lh = shards[0].shape[0] // 2
    srcs = [jax.new_ref(a, memory_space=pltpu.MemorySpace.HBM) for a in shards]
    outs = [jax.empty_ref(jax.ShapeDtypeStruct((4,) + a.shape, a.dtype), memory_space=pltpu.MemorySpace.HBM) for a in shards]

    @pl.kernel(mesh=plsc.ScalarSubcoreMesh(axis_name="sequencer", num_cores=1), name="allgather_shards_async",
               scratch_types=tuple(_allgather_sems(na)), compiler_params=pltpu.CompilerParams(collective_id=ASYNC_GATHER_ID))
    def launch(ssem, rsem, lsem):
        barrier = pltpu.get_barrier_semaphore()
        peers = ("c",) + ICI_KINDS
        for kind in peers:
            pl.semaphore_signal(barrier, inc=1, device_id=_peer(kind), device_id_type=MESH)
        pl.semaphore_wait(barrier, len(peers))
        _allgather_copies(srcs, outs, ssem, rsem, lsem, lh)

    launch()
    return [r[...] for r in outs]


def _allreduce_small(v):
    rows = v.shape[0]

    def body(v_ref, o_ref, buf, ssem, rsem):
        x, y, c = _me()
        me = 4 * x + 2 * y + c
        buf[0] = v_ref[...]
        cps = []
        for k in range(1, 8):
            peer = (1 - x if k & 4 else x, 1 - y if k & 2 else y, 1 - c if k & 1 else c)
            cps.append(pltpu.make_async_remote_copy(src_ref=v_ref, dst_ref=buf.at[k], send_sem=ssem.at[k - 1], recv_sem=rsem.at[k - 1],
                                                    device_id=peer, device_id_type=MESH))
        for cp in cps:
            cp.start()
        for cp in cps:
            cp.wait()
        acc = buf[jnp.bitwise_xor(me, 0)]
        for d in range(1, 8):
            acc = acc + buf[jnp.bitwise_xor(me, d)]
        o_ref[...] = acc

    vm = pl.BlockSpec(memory_space=pltpu.VMEM)
    return pl.pallas_call(
        body, name="allreduce_small", out_shape=jax.ShapeDtypeStruct(v.shape, f32), in_specs=[vm], out_specs=vm,
        scratch_shapes=[pltpu.VMEM((8, rows, LANES), f32), pltpu.SemaphoreType.DMA((7,)), pltpu.SemaphoreType.DMA((7,))],
    )(v)


SHARDED = ("w_in", "w_proj_ssm", "w_proj_pool", "w_proj_sb", "w_out", "pool_w", "conv_w")
REST = SHARDED[1:6]
SMALL = ("norm_w", "conv_b", "dt_bias", "a_log", "d_skip", "ssm_norm_w", "pool_scale")


def _pack_rows(dm):
    rows = dict(w_proj_ssm=dm.W2 // 4, w_proj_pool=dm.D // 4, w_proj_sb=dm.D // 4, w_out=dm.D // 4, pool_w=dm.D // 16)
    assert all(r % 16 == 0 for r in rows.values())
    return rows


def _pack(sh, dm, dtype):
    rows = _pack_rows(dm)
    ly = sh["w_out"].shape[0]
    return jnp.concatenate([sh[n].astype(dtype).reshape(ly, rows[n], dm.D) for n in REST], axis=1)


def _unpack(pk, dm):
    rows = _pack_rows(dm)
    lead = pk.shape[:-2]
    shapes = dict(w_proj_ssm=(dm.W2 // 4, dm.D), w_proj_pool=(dm.D // 4, dm.D), w_proj_sb=(dm.D // 4, dm.D),
                  w_out=(dm.D // 4, dm.D), pool_w=(POOL_GROUPS, dm.Dg // 4, dm.Dg))
    out, r0 = {}, 0
    for n in REST:
        out[n] = pk[..., r0:r0 + rows[n], :].reshape(lead + shapes[n])
        r0 += rows[n]
    return out


def _join_shards(sh, dm):
    full = {}
    w = jnp.moveaxis(sh["w_in"], 0, 2)
    full["w_in"] = _permute_cols(w.reshape(w.shape[0], dm.D, dm.IN_COLS), dm)
    for n in ("w_proj_ssm", "w_proj_pool", "w_proj_sb", "w_out"):
        a = jnp.moveaxis(sh[n], 0, 1)
        full[n] = a.reshape(a.shape[0], -1, dm.D)
    a = jnp.moveaxis(sh["pool_w"], 0, 2)
    full["pool_w"] = a.reshape(a.shape[0], POOL_GROUPS, dm.Dg, dm.Dg)
    a = jnp.moveaxis(sh["conv_w"], 0, 2)
    full["conv_w"] = a.reshape(a.shape[0], CONV_WIDTH, dm.CC)
    return full


def _split_shards(full, dm):
    sh = {}
    w = _unpermute_cols(full["w_in"], dm)
    sh["w_in"] = jnp.moveaxis(w.reshape(w.shape[0], dm.D, 4, dm.IN_COLS // 4), 2, 0)
    for n in ("w_proj_ssm", "w_proj_pool", "w_proj_sb", "w_out"):
        a = full[n]
        sh[n] = jnp.moveaxis(a.reshape(a.shape[0], 4, a.shape[1] // 4, dm.D), 1, 0)
    a = full["pool_w"]
    sh["pool_w"] = jnp.moveaxis(a.reshape(a.shape[0], POOL_GROUPS, 4, dm.Dg // 4, dm.Dg), 2, 0)
    a = full["conv_w"]
    sh["conv_w"] = jnp.moveaxis(a.reshape(a.shape[0], CONV_WIDTH, 4, dm.CC // 4), 2, 0)
    return sh


def _reduce_scatter(full_grads, dm):
    x, y, c = _me()
    j_me = 2 * x + y
    sh = _split_shards(full_grads, dm)
    by_chip = [sh["w_in"], jnp.stack([_pack({n: sh[n][j] for n in REST}, dm, f32) for j in range(4)]), sh["conv_w"]]
    tags = ("in", "rest", "conv")
    lh = by_chip[0].shape[1] // 2

    def add(mine, parts, name):
        cols = mine.shape[-1]
        return _sum_parts(mine.reshape(-1, cols), parts.reshape(parts.shape[0], -1, cols), name).reshape(mine.shape)

    def pick(a, j):
        return lax.dynamic_index_in_dim(a, j, 0, keepdims=False)

    mine = [lax.dynamic_slice_in_dim(a, c * lh, lh, axis=1) for a in by_chip]
    theirs = [lax.dynamic_slice_in_dim(a, (1 - c) * lh, lh, axis=1).astype(WIRE_DTYPE)[None] for a in by_chip]
    got = _exchange(theirs, ("c",), "rs_pair")
    s1 = [add(a, g, "rs_pair_sum_" + t) for a, g, t in zip(mine, got, tags)]
    send = [jnp.stack([pick(a, jnp.bitwise_xor(j_me, k)) for k in (1, 2, 3)]).astype(WIRE_DTYPE) for a in s1]
    got = _exchange(send, ICI_KINDS, "rs_ici")
    red = [add(pick(a, j_me), g, "rs_ici_sum_" + t) for a, g, t in zip(s1, got, tags)]
    sib = _exchange([r[None] for r in red], ("c",), "rs_sibling")
    both = [jnp.where(c == 0, jnp.concatenate([r, s[0]]), jnp.concatenate([s[0], r])) for r, s in zip(red, sib)]
    out = _unpack(both[1], dm)
    out["w_in"], out["conv_w"] = both[0], both[2]
    return out


def _flatten_small(parts):
    flat = jnp.concatenate([p.reshape(-1).astype(f32) for p in parts])
    rows = -(-flat.shape[0] // (8 * LANES)) * 8
    return jnp.pad(flat, (0, rows * LANES - flat.shape[0])).reshape(rows, LANES)


def _unflatten_small(buf, shapes):
    flat = buf.reshape(-1)
    out, o = [], 0
    for s in shapes:
        n = math.prod(s)
        out.append(flat[o:o + n].reshape(s))
        o += n
    return out


def kernel(x, norm_w, w_in, conv_w, conv_b, dt_bias, a_log, d_skip, ssm_norm_w, pool_w, pool_scale, w_proj_ssm, w_proj_pool, w_proj_sb, w_out, final_norm_w, loss_target, m_norm_w, m_w_in, m_conv_w, m_conv_b, m_dt_bias, m_a_log, m_d_skip, m_ssm_norm_w, m_pool_w, m_pool_scale, m_w_proj_ssm, m_w_proj_pool, m_w_proj_sb, m_w_out, m_final_norm_w, v_norm_w, v_w_in, v_conv_w, v_conv_b, v_dt_bias, v_a_log, v_d_skip, v_ssm_norm_w, v_pool_w, v_pool_scale, v_w_proj_ssm, v_w_proj_pool, v_w_proj_sb, v_w_out, v_final_norm_w):
    names = ("norm_w", "w_in", "conv_w", "conv_b", "dt_bias", "a_log", "d_skip", "ssm_norm_w", "pool_w", "pool_scale",
             "w_proj_ssm", "w_proj_pool", "w_proj_sb", "w_out", "final_norm_w")
    w = dict(zip(names, (norm_w, w_in, conv_w, conv_b, dt_bias, a_log, d_skip, ssm_norm_w, pool_w, pool_scale, w_proj_ssm,
                         w_proj_pool, w_proj_sb, w_out, final_norm_w)))
    m = dict(zip(names, (m_norm_w, m_w_in, m_conv_w, m_conv_b, m_dt_bias, m_a_log, m_d_skip, m_ssm_norm_w, m_pool_w, m_pool_scale,
                         m_w_proj_ssm, m_w_proj_pool, m_w_proj_sb, m_w_out, m_final_norm_w)))
    v = dict(zip(names, (v_norm_w, v_w_in, v_conv_w, v_conv_b, v_dt_bias, v_a_log, v_d_skip, v_ssm_norm_w, v_pool_w, v_pool_scale,
                         v_w_proj_ssm, v_w_proj_pool, v_w_proj_sb, v_w_out, v_final_norm_w)))
    bl, s, d = x.shape
    dm = Dims(bl, s, d)
    ly = norm_w.shape[0]

    mine = [w["w_in"].astype(WIRE_DTYPE), _pack({n: w[n] for n in REST}, dm, WIRE_DTYPE), conv_w]
    if ly % 4 == 0:
        gathered = [_allgather_shards([a[:ly // 2] for a in mine])]
        later = lax.optimization_barrier(([a[ly // 2:] for a in mine], gathered[0]))[0]
        gathered.append(_allgather_shards_async(later))
    else:
        gathered = [_allgather_shards(mine)]
    def layer_dicts(part, first):
        g_in, g_rest, g_conv = part
        shards = _unpack(g_rest, dm)
        shards["w_in"], shards["conv_w"] = g_in, g_conv
        full = _join_shards(shards, dm)
        return [{**{n: full[n][l] for n in SHARDED}, **{n: w[n][first + l] for n in SMALL}} for l in range(g_in.shape[1])]

    layers = layer_dicts(gathered[0], 0)
    if len(gathered) > 1:
        late = {}

        def late_layer(x, l):
            if not late:
                part, x = lax.optimization_barrier((gathered[1], x))
                late["dicts"] = layer_dicts(part, ly // 2)
            return late["dicts"][l], x

        layers += [functools.partial(late_layer, l=l) for l in range(ly - ly // 2)]

    loss_part, dx, grads, dfn = _local_step(x.reshape(dm.T, d), loss_target.reshape(dm.T, d), layers, final_norm_w, dm)

    h = dm.H
    small_parts = [loss_part]
    small_shapes = [(1, LANES)]
    for n in SMALL:
        if n in ("dt_bias", "a_log", "d_skip"):
            small_parts.append(jnp.stack([g[n][0, :h] for g in grads]))
        else:
            small_parts.append(jnp.stack([g[n][0] for g in grads]))
        small_shapes.append(w[n].shape)
    small_parts.append(dfn[0])
    small_shapes.append(final_norm_w.shape)
    red_small = _allreduce_small(_flatten_small(small_parts))
    small_g = _unflatten_small(red_small, small_shapes)
    loss = small_g[0][0, 0]
    g_out = dict(zip(SMALL + ("final_norm_w",), small_g[1:]))

    g_out.update(_reduce_scatter({n: jnp.stack([g[n] for g in grads]) for n in SHARDED}, dm))

    small_names = SMALL + ("final_norm_w",)
    zero_row = jnp.zeros((1, LANES), f32)
    pack_small = lambda t: _flatten_small([zero_row] + [t[n] for n in small_names])
    ds, ms, vs = _adamw(pack_small(w), red_small, pack_small(m), pack_small(v), "adamw_small")
    delta, new_m, new_v = {}, {}, {}
    for tgt, buf in ((delta, ds), (new_m, ms), (new_v, vs)):
        tgt.update(zip(small_names, _unflatten_small(buf, small_shapes)[1:]))
    for n in SHARDED:
        shp = w[n].shape
        two = (math.prod(shp[:-1]), shp[-1])
        dd, mm, vv = _adamw(w[n].reshape(two), g_out[n].reshape(two), m[n].reshape(two), v[n].reshape(two), "adamw_" + n)
        delta[n], new_m[n], new_v[n] = dd.reshape(shp), mm.reshape(shp), vv.reshape(shp)
        g_out[n] = g_out[n].reshape(shp)

    return (loss, dx.reshape(bl, s, d), *[g_out[n] for n in names], *[delta[n] for n in names],
            *[new_m[n] for n in names], *[new_v[n] for n in names])
```

```python
import functools
import math

import jax
import jax.numpy as jnp
from jax import lax
from jax.experimental import pallas as pl
from jax.experimental.pallas import tpu as pltpu
from jax.experimental.pallas import tpu_sc as plsc

f32 = jnp.float32
bf16 = jnp.bfloat16
MXU_DTYPE = jnp.bfloat16
ACT_DTYPE = jnp.bfloat16
WIRE_DTYPE = jnp.bfloat16

EPS = 1e-6
LANES = 128
HEAD = 64
SSM_GROUPS = 2
CONV_WIDTH = 4
POOL_GROUPS = 4
DT_PAD = 512
N_BRANCHES = 3
VMEM_LIMIT = 56 * 1024 * 1024

ADAM_LR, ADAM_B1, ADAM_B2, ADAM_EPS, ADAM_WD, ADAM_STEP = 0.001, 0.9, 0.999, 1e-08, 0.01, 10
MESH = pl.DeviceIdType.MESH


def _cp(**kw):
    return pltpu.CompilerParams(vmem_limit_bytes=VMEM_LIMIT, **kw)


def _pick(n, prefs):
    for p in prefs:
        if n % p == 0:
            return p
    return n


def _dg(a, b, ca, cb):
    return lax.dot_general(a, b, (((ca,), (cb,)), ((), ())), preferred_element_type=f32)


def _nn(a, b):
    return _dg(a, b, 1, 0)


def _nt(a, b):
    return _dg(a, b, 1, 1)


def _tn(a, b):
    return _dg(a, b, 0, 0)


def _mx(a):
    return a.astype(MXU_DTYPE)


def _split3(a):
    hi = a.astype(bf16)
    r = a - hi.astype(f32)
    mid = r.astype(bf16)
    lo = (r - mid.astype(f32)).astype(bf16)
    return hi, mid, lo


def _exact_nn(a, u):
    hi, mid, lo = _split3(a)
    return _nn(hi, u) + _nn(mid, u) + _nn(lo, u)


def _iota2(shape, dim):
    return lax.broadcasted_iota(jnp.int32, shape, dim)


def _sigmoid(x):
    return 1.0 / (1.0 + jnp.exp(-x))


def _shift_down(v, sh, row):
    return jnp.where(row >= sh, pltpu.roll(v, sh, 0), 0.0)


def _shift_up(v, sh, row):
    n = v.shape[0]
    return jnp.where(row < n - sh, pltpu.roll(v, n - sh, 0), 0.0)


class Dims:
    def __init__(self, bl, s, d):
        self.Bl, self.S, self.D = bl, s, d
        self.T = bl * s
        self.W2 = 2 * d
        self.H = self.W2 // HEAD
        self.hpg = self.H // SSM_GROUPS
        self.CC = self.W2 + 2 * SSM_GROUPS * LANES
        self.Dg = d // POOL_GROUPS
        self.nc = s // LANES
        self.o_dt = self.W2 + self.CC
        self.IN_COLS = 13 * d + 2 * SSM_GROUPS * LANES + self.H
        self.c_z = 0
        self.c_xbc = self.W2
        self.c_pu = self.W2 + self.CC
        self.c_pg = self.c_pu + d
        self.c_qkv = self.c_pg + d
        self.c_sbg = self.c_qkv + 3 * d
        self.c_mrg = self.c_sbg + d
        self.c_dt = self.c_mrg + 3 * d
        self.NP = self.c_dt + DT_PAD
        assert self.c_dt == self.IN_COLS - self.H
        assert s % LANES == 0 and d % 512 == 0 and self.H <= LANES


def _permute_cols(w, dm):
    pad = jnp.zeros(w.shape[:-1] + (DT_PAD - dm.H,), w.dtype)
    return jnp.concatenate([w[..., :dm.o_dt], w[..., dm.o_dt + dm.H:], w[..., dm.o_dt:dm.o_dt + dm.H], pad], axis=-1)


def _unpermute_cols(w, dm):
    return jnp.concatenate([w[..., :dm.o_dt], w[..., dm.c_dt:dm.c_dt + dm.H], w[..., dm.o_dt:dm.c_dt]], axis=-1)


def _mm(a, b, *, ta=False, tb=False, out_dtype=f32, res=None, name):
    M, K = (a.shape[1], a.shape[0]) if ta else a.shape
    N = b.shape[0] if tb else b.shape[1]
    tm = _pick(M, (1024, 512, 256, 128))
    tn = _pick(N, (1024, 512, 256, 128))
    tk = _pick(K, (1024, 512, 256, 128))
    nk = K // tk

    def body(*refs):
        if res is None:
            a_ref, b_ref, o_ref, acc = refs
        else:
            a_ref, b_ref, r_ref, o_ref, acc = refs
        k = pl.program_id(2)

        @pl.when(k == 0)
        def _():
            acc[...] = jnp.zeros_like(acc)

        acc[...] += _dg(_mx(a_ref[...]), _mx(b_ref[...]), 0 if ta else 1, 1 if tb else 0)

        @pl.when(k == nk - 1)
        def _():
            v = acc[...]
            if res is not None:
                v = v + r_ref[...]
            o_ref[...] = v.astype(out_dtype)

    a_spec = pl.BlockSpec((tk, tm), lambda i, j, k: (k, i)) if ta else pl.BlockSpec((tm, tk), lambda i, j, k: (i, k))
    b_spec = pl.BlockSpec((tn, tk), lambda i, j, k: (j, k)) if tb else pl.BlockSpec((tk, tn), lambda i, j, k: (k, j))
    o_spec = pl.BlockSpec((tm, tn), lambda i, j, k: (i, j))
    in_specs = [a_spec, b_spec] + ([o_spec] if res is not None else [])
    args = (a, b) + ((res,) if res is not None else ())
    return pl.pallas_call(
        body, name=name, grid=(M // tm, N // tn, nk), in_specs=in_specs, out_specs=o_spec,
        out_shape=jax.ShapeDtypeStruct((M, N), out_dtype), scratch_shapes=[pltpu.VMEM((tm, tn), f32)],
        compiler_params=_cp(dimension_semantics=("parallel", "parallel", "arbitrary")),
    )(*args)


def _rms_fwd(x, w, dm):
    tr = _pick(dm.T, (256, 128))

    def body(x_ref, w_ref, o_ref):
        xf = x_ref[...]
        r = lax.rsqrt(jnp.mean(xf * xf, axis=-1, keepdims=True) + EPS)
        o_ref[...] = (xf * r * w_ref[...]).astype(o_ref.dtype)

    return pl.pallas_call(
        body, name="rms_fwd", grid=(dm.T // tr,),
        in_specs=[pl.BlockSpec((tr, dm.D), lambda i: (i, 0)), pl.BlockSpec((1, dm.D), lambda i: (0, 0))],
        out_specs=pl.BlockSpec((tr, dm.D), lambda i: (i, 0)),
        out_shape=jax.ShapeDtypeStruct((dm.T, dm.D), ACT_DTYPE), compiler_params=_cp(),
    )(x, w.reshape(1, dm.D))


def _rms_bwd(x, dh, dres, w, dm):
    tr = _pick(dm.T, (256, 128))

    def body(x_ref, dh_ref, dr_ref, w_ref, dx_ref, dw_ref):
        @pl.when(pl.program_id(0) == 0)
        def _():
            dw_ref[...] = jnp.zeros_like(dw_ref)

        xf = x_ref[...]
        r = lax.rsqrt(jnp.mean(xf * xf, axis=-1, keepdims=True) + EPS)
        xh = xf * r
        dh_ = dh_ref[...]
        dxh = dh_ * w_ref[...]
        dx_ref[...] = dr_ref[...] + r * (dxh - xh * jnp.mean(dxh * xh, axis=-1, keepdims=True))
        dw_ref[...] += jnp.sum(dh_ * xh, axis=0, keepdims=True)

    row = pl.BlockSpec((tr, dm.D), lambda i: (i, 0))
    vec = pl.BlockSpec((1, dm.D), lambda i: (0, 0))
    return pl.pallas_call(
        body, name="rms_bwd", grid=(dm.T // tr,), in_specs=[row, row, row, vec], out_specs=[row, vec],
        out_shape=[jax.ShapeDtypeStruct((dm.T, dm.D), f32), jax.ShapeDtypeStruct((1, dm.D), f32)],
        compiler_params=_cp(dimension_semantics=("arbitrary",)),
    )(x, dh, dres, w.reshape(1, dm.D))


def _loss_head(x, tgt, w, dm):
    tr = _pick(dm.T, (256, 128))

    def body(x_ref, t_ref, w_ref, dx_ref, dw_ref, ls_ref):
        @pl.when(pl.program_id(0) == 0)
        def _():
            dw_ref[...] = jnp.zeros_like(dw_ref)
            ls_ref[...] = jnp.zeros_like(ls_ref)

        xf = x_ref[...]
        r = lax.rsqrt(jnp.mean(xf * xf, axis=-1, keepdims=True) + EPS)
        xh = xf * r
        err = xh * w_ref[...] - t_ref[...]
        per_tok = jnp.mean(err * err, axis=-1, keepdims=True)
        ls_ref[...] += 0.5 * jnp.sum(per_tok, axis=0, keepdims=True)
        dy = err * (1.0 / dm.D)
        dxh = dy * w_ref[...]
        dx_ref[...] = r * (dxh - xh * jnp.mean(dxh * xh, axis=-1, keepdims=True))
        dw_ref[...] += jnp.sum(dy * xh, axis=0, keepdims=True)

    row = pl.BlockSpec((tr, dm.D), lambda i: (i, 0))
    vec = pl.BlockSpec((1, dm.D), lambda i: (0, 0))
    return pl.pallas_call(
        body, name="loss_head", grid=(dm.T // tr,), in_specs=[row, row, vec],
        out_specs=[row, vec, pl.BlockSpec((1, LANES), lambda i: (0, 0))],
        out_shape=[jax.ShapeDtypeStruct((dm.T, dm.D), f32), jax.ShapeDtypeStruct((1, dm.D), f32),
                   jax.ShapeDtypeStruct((1, LANES), f32)],
        compiler_params=_cp(dimension_semantics=("arbitrary",)),
    )(x, tgt, w.reshape(1, dm.D))


def _conv_pre(u, w_ref, b_ref, row):
    acc = b_ref[...] + w_ref[CONV_WIDTH - 1:CONV_WIDTH, :] * u
    for k in range(CONV_WIDTH - 1):
        acc = acc + w_ref[k:k + 1, :] * _shift_down(u, CONV_WIDTH - 1 - k, row)
    return acc


def _conv_fwd(proj, cw, cb, dm):
    cwid = LANES
    off = dm.c_xbc // cwid

    def body(u_ref, w_ref, b_ref, o_ref):
        u = u_ref[...]
        row = _iota2(u.shape, 0)
        pre = _conv_pre(u, w_ref, b_ref, row)
        o_ref[...] = pre * _sigmoid(pre)

    return pl.pallas_call(
        body, name="conv_fwd", grid=(dm.Bl, dm.CC // cwid),
        in_specs=[pl.BlockSpec((dm.S, cwid), lambda b, j: (b, off + j)),
                  pl.BlockSpec((CONV_WIDTH, cwid), lambda b, j: (0, j)), pl.BlockSpec((1, cwid), lambda b, j: (0, j))],
        out_specs=pl.BlockSpec((dm.S, cwid), lambda b, j: (b, j)),
        out_shape=jax.ShapeDtypeStruct((dm.T, dm.CC), f32), compiler_params=_cp(),
    )(proj, cw, cb.reshape(1, dm.CC))


def _conv_bwd(proj, d_out, cw, cb, dm):
    cwid = LANES
    off = dm.c_xbc // cwid

    def body(u_ref, d_ref, w_ref, b_ref, du_ref, dw_ref, db_ref):
        @pl.when(pl.program_id(1) == 0)
        def _():
            dw_ref[...] = jnp.zeros_like(dw_ref)
            db_ref[...] = jnp.zeros_like(db_ref)

        u = u_ref[...]
        row = _iota2(u.shape, 0)
        pre = _conv_pre(u, w_ref, b_ref, row)
        sg = _sigmoid(pre)
        dpre = d_ref[...] * (sg * (1.0 + pre * (1.0 - sg)))
        du = w_ref[CONV_WIDTH - 1:CONV_WIDTH, :] * dpre
        dw_ref[CONV_WIDTH - 1:CONV_WIDTH, :] += jnp.sum(dpre * u, axis=0, keepdims=True)
        for k in range(CONV_WIDTH - 1):
            sh = CONV_WIDTH - 1 - k
            du = du + w_ref[k:k + 1, :] * _shift_up(dpre, sh, row)
            dw_ref[k:k + 1, :] += jnp.sum(dpre * _shift_down(u, sh, row), axis=0, keepdims=True)
        du_ref[...] = du.astype(du_ref.dtype)
        db_ref[...] += jnp.sum(dpre, axis=0, keepdims=True)

    return pl.pallas_call(
        body, name="conv_bwd", grid=(dm.CC // cwid, dm.Bl),
        in_specs=[pl.BlockSpec((dm.S, cwid), lambda j, b: (b, off + j)), pl.BlockSpec((dm.S, cwid), lambda j, b: (b, j)),
                  pl.BlockSpec((CONV_WIDTH, cwid), lambda j, b: (0, j)), pl.BlockSpec((1, cwid), lambda j, b: (0, j))],
        out_specs=[pl.BlockSpec((dm.S, cwid), lambda j, b: (b, j)), pl.BlockSpec((CONV_WIDTH, cwid), lambda j, b: (0, j)),
                   pl.BlockSpec((1, cwid), lambda j, b: (0, j))],
        out_shape=[jax.ShapeDtypeStruct((dm.T, dm.CC), ACT_DTYPE), jax.ShapeDtypeStruct((CONV_WIDTH, dm.CC), f32),
                   jax.ShapeDtypeStruct((1, dm.CC), f32)],
        compiler_params=_cp(dimension_semantics=("arbitrary", "arbitrary")),
    )(proj, d_out, cw, cb.reshape(1, dm.CC))


def _pad_lanes(v):
    return jnp.pad(v, (0, LANES - v.shape[0])).reshape(1, LANES)


def _softplus(x):
    return jnp.maximum(x, 0.0) + jnp.log(1.0 + jnp.exp(-jnp.abs(x)))


def _dt_prep(proj, dt_bias, a_log, dm):
    off = dm.c_dt // LANES

    def body(r_ref, b_ref, al_ref, dt_ref, cum_ref, cumt_ref):
        dt = _softplus(r_ref[...] + b_ref[...])
        adt = dt * (-jnp.exp(al_ref[...]))
        tril = (_iota2((LANES, LANES), 1) <= _iota2((LANES, LANES), 0)).astype(bf16)
        cum = _exact_nn_left(tril, adt)
        dt_ref[...] = dt
        cum_ref[...] = cum
        cumt_ref[...] = cum.T

    blk = pl.BlockSpec((LANES, LANES), lambda i: (i, 0))
    vec = pl.BlockSpec((1, LANES), lambda i: (0, 0))
    return pl.pallas_call(
        body, name="dt_prep", grid=(dm.T // LANES,),
        in_specs=[pl.BlockSpec((LANES, LANES), lambda i: (i, off)), vec, vec],
        out_specs=[blk, blk, pl.BlockSpec((LANES, LANES), lambda i: (0, i))],
        out_shape=[jax.ShapeDtypeStruct((dm.T, LANES), f32), jax.ShapeDtypeStruct((dm.T, LANES), f32),
                   jax.ShapeDtypeStruct((LANES, dm.T), f32)],
        compiler_params=_cp(),
    )(proj, _pad_lanes(dt_bias), _pad_lanes(a_log))


def _exact_nn_left(u, a):
    hi, mid, lo = _split3(a)
    return _nn(u, hi) + _nn(u, mid) + _nn(u, lo)


def _dt_bwd(proj, dt, dcum, dcum_t, ddt, dt_bias, a_log, dm):
    off = dm.c_dt // LANES

    def body(r_ref, dt_ref, dc_ref, dct_ref, dd_ref, b_ref, al_ref, o_ref, db_ref, da_ref):
        @pl.when(pl.program_id(0) == 0)
        def _():
            db_ref[...] = jnp.zeros_like(db_ref)
            da_ref[...] = jnp.zeros_like(da_ref)

        a = -jnp.exp(al_ref[...])
        triu = (_iota2((LANES, LANES), 1) >= _iota2((LANES, LANES), 0)).astype(bf16)
        dadt = _exact_nn_left(triu, dc_ref[...] + dct_ref[...].T)
        d_dt = dd_ref[...] + dadt * a
        d_raw = d_dt * _sigmoid(r_ref[...] + b_ref[...])
        o_ref[...] = d_raw.astype(o_ref.dtype)
        db_ref[...] += jnp.sum(d_raw, axis=0, keepdims=True)
        da_ref[...] += jnp.sum(dadt * dt_ref[...], axis=0, keepdims=True) * a

    blk = pl.BlockSpec((LANES, LANES), lambda i: (i, 0))
    vec = pl.BlockSpec((1, LANES), lambda i: (0, 0))
    return pl.pallas_call(
        body, name="dt_bwd", grid=(dm.T // LANES,),
        in_specs=[pl.BlockSpec((LANES, LANES), lambda i: (i, off)), blk, blk, pl.BlockSpec((LANES, LANES), lambda i: (0, i)),
                  blk, vec, vec],
        out_specs=[blk, vec, vec],
        out_shape=[jax.ShapeDtypeStruct((dm.T, LANES), ACT_DTYPE), jax.ShapeDtypeStruct((1, LANES), f32),
                   jax.ShapeDtypeStruct((1, LANES), f32)],
        compiler_params=_cp(dimension_semantics=("arbitrary",)),
    )(proj, dt, dcum, dcum_t, ddt, _pad_lanes(dt_bias), _pad_lanes(a_log))


def _ssd_common(dm):
    L = LANES
    tri = _iota2((L, L), 0) >= _iota2((L, L), 1)
    lo = _iota2((L, L), 1) < HEAD
    return tri, lo


def _ssd_fwd(xbc, dt, cum, cumt, d_skip, dm):
    L, W2, hpg = LANES, dm.W2, dm.hpg
    nb = W2 // L

    def body(x_ref, b_ref, c_ref, dt_ref, cum_ref, cumt_ref, dsk_ref, y_ref, hs_ref, h_scr):
        @pl.when(pl.program_id(1) == 0)
        def _():
            h_scr[...] = jnp.zeros_like(h_scr)

        hs_ref[0] = h_scr[...]
        tri, lo = _ssd_common(dm)
        for g in range(SSM_GROUPS):
            bb = _mx(b_ref[:, g * L:(g + 1) * L])
            cb_ = _mx(c_ref[:, g * L:(g + 1) * L])
            cbm = _nt(cb_, bb)
            for i in range(hpg // 2):
                h0 = g * hpg + 2 * i
                h1 = h0 + 1
                sl = slice(h0 * HEAD, h0 * HEAD + L)
                x_p = x_ref[:, sl]
                cum0, cum1 = cum_ref[:, h0:h0 + 1], cum_ref[:, h1:h1 + 1]
                cums = jnp.where(lo, cum0, cum1)
                xdt = x_p * jnp.where(lo, dt_ref[:, h0:h0 + 1], dt_ref[:, h1:h1 + 1])
                tot = jnp.where(lo[0:1], cum_ref[L - 1:L, h0:h0 + 1], cum_ref[L - 1:L, h1:h1 + 1])
                y_p = jnp.zeros((L, L), f32)
                for hh, m in ((h0, lo), (h1, jnp.logical_not(lo))):
                    diff = cum_ref[:, hh:hh + 1] - cumt_ref[hh:hh + 1, :]
                    lm = jnp.where(tri, jnp.exp(jnp.minimum(diff, 0.0)), 0.0)
                    y_p = y_p + _nn(_mx(cbm * lm), _mx(jnp.where(m, xdt, 0.0)))
                hp = h_scr[:, sl]
                y_p = y_p + _nn(cb_, _mx(hp)) * jnp.exp(cums)
                y_p = y_p + x_p * jnp.where(lo[0:1], dsk_ref[0:1, h0:h0 + 1], dsk_ref[0:1, h1:h1 + 1])
                y_ref[:, sl] = y_p
                h_scr[:, sl] = hp * jnp.exp(tot) + _tn(bb, _mx(xdt * jnp.exp(tot - cums)))

    nc = dm.nc
    ob = W2 // (SSM_GROUPS * L)
    blk = pl.BlockSpec((L, L), lambda b, c: (b * nc + c, 0))
    return pl.pallas_call(
        body, name="ssd_fwd", grid=(dm.Bl, nc),
        in_specs=[pl.BlockSpec((L, W2), lambda b, c: (b * nc + c, 0)),
                  pl.BlockSpec((L, SSM_GROUPS * L), lambda b, c: (b * nc + c, ob)),
                  pl.BlockSpec((L, SSM_GROUPS * L), lambda b, c: (b * nc + c, ob + 1)),
                  blk, blk, pl.BlockSpec((L, L), lambda b, c: (0, b * nc + c)), pl.BlockSpec((1, L), lambda b, c: (0, 0))],
        out_specs=[pl.BlockSpec((L, W2), lambda b, c: (b * nc + c, 0)), pl.BlockSpec((1, L, W2), lambda b, c: (b * nc + c, 0, 0))],
        out_shape=[jax.ShapeDtypeStruct((dm.T, W2), f32), jax.ShapeDtypeStruct((dm.Bl * nc, L, W2), f32)],
        scratch_shapes=[pltpu.VMEM((L, W2), f32)],
        compiler_params=_cp(dimension_semantics=("arbitrary", "arbitrary")),
    )(xbc, xbc, xbc, dt, cum, cumt, _pad_lanes(d_skip))


def _ssd_bwd(xbc, dt, cum, cumt, d_skip, hs, dy, dm):
    L, W2, hpg = LANES, dm.W2, dm.hpg
    nc = dm.nc

    def body(x_ref, b_ref, c_ref, dt_ref, cum_ref, cumt_ref, dsk_ref, hs_ref, dy_ref,
             dx_ref, db_ref, dc_ref, ddt_ref, dcum_ref, dcr_ref, dd_ref, dh_scr, lane_cum, lane_dt, lane_d):
        @pl.when(pl.program_id(1) == 0)
        def _():
            dh_scr[...] = jnp.zeros_like(dh_scr)

        @pl.when((pl.program_id(0) == 0) & (pl.program_id(1) == 0))
        def _():
            dd_ref[...] = jnp.zeros_like(dd_ref)

        dcr_ref[...] = jnp.zeros_like(dcr_ref)
        tri, lo = _ssd_common(dm)
        last = _iota2((L, L), 0) == L - 1
        for g in range(SSM_GROUPS):
            gs = slice(g * L, (g + 1) * L)
            bb = _mx(b_ref[:, gs])
            cb_ = _mx(c_ref[:, gs])
            cbm = _nt(cb_, bb)
            dcb = jnp.zeros((L, L), f32)
            dc_g = jnp.zeros((L, L), f32)
            db_g = jnp.zeros((L, L), f32)
            for i in range(hpg // 2):
                h0 = g * hpg + 2 * i
                h1 = h0 + 1
                sl = slice(h0 * HEAD, h0 * HEAD + L)
                x_p = x_ref[:, sl]
                dy_p = dy_ref[:, sl]
                dt_p = jnp.where(lo, dt_ref[:, h0:h0 + 1], dt_ref[:, h1:h1 + 1])
                cums = jnp.where(lo, cum_ref[:, h0:h0 + 1], cum_ref[:, h1:h1 + 1])
                tot = jnp.where(lo[0:1], cum_ref[L - 1:L, h0:h0 + 1], cum_ref[L - 1:L, h1:h1 + 1])
                dsk_p = jnp.where(lo[0:1], dsk_ref[0:1, h0:h0 + 1], dsk_ref[0:1, h1:h1 + 1])
                xdt = x_p * dt_p
                ecum = jnp.exp(cums)
                dec = jnp.exp(tot - cums)
                etot = jnp.exp(tot)
                hp = hs_ref[0, :, sl]
                hp_b = _mx(hp)
                dhn = dh_scr[:, sl]
                dhn_b = _mx(dhn)
                y_off = _nn(cb_, hp_b) * ecum
                dch = _mx(dy_p * ecum)
                dc_g = dc_g + _nt(dch, hp_b)
                dh_off = _tn(cb_, dch)
                bds = _nn(bb, dhn_b)
                xdec = xdt * dec
                db_g = db_g + _nt(_mx(xdec), dhn_b)
                dxdt = bds * dec
                sdec = bds * xdec
                tot_lane = jnp.sum(sdec, axis=0, keepdims=True) + jnp.sum(dhn * hp, axis=0, keepdims=True) * etot
                dh_scr[:, sl] = etot * dhn + dh_off
                rsum = []
                for hh, m in ((h0, lo), (h1, jnp.logical_not(lo))):
                    diff = cum_ref[:, hh:hh + 1] - cumt_ref[hh:hh + 1, :]
                    lm = jnp.where(tri, jnp.exp(jnp.minimum(diff, 0.0)), 0.0)
                    w32 = cbm * lm
                    dyh = _mx(jnp.where(m, dy_p, 0.0))
                    dw = _nt(dyh, _mx(jnp.where(m, xdt, 0.0)))
                    dcb = dcb + dw * lm
                    e = dw * w32
                    rsum.append(jnp.sum(e, axis=1, keepdims=True))
                    dcr_ref[hh:hh + 1, :] = -jnp.sum(e, axis=0, keepdims=True)
                    dxdt = dxdt + _tn(_mx(w32), dyh)
                lane_cum[:, sl] = (dy_p * y_off + jnp.where(lo, rsum[0], rsum[1]) * (1.0 / HEAD) - sdec
                                   + jnp.where(last, tot_lane, 0.0))
                lane_dt[:, sl] = dxdt * x_p
                lane_d[:, sl] = dy_p * x_p
                dx_ref[:, sl] = dxdt * dt_p + dsk_p * dy_p
            dcb_b = _mx(dcb)
            dc_ref[:, gs] = dc_g + _nn(dcb_b, bb)
            db_ref[:, gs] = db_g + _tn(dcb_b, cb_)
        sel = (_iota2((W2, L), 0) // HEAD == _iota2((W2, L), 1)).astype(bf16)
        dcum_ref[...] = _exact_nn(lane_cum[...], sel)
        ddt_ref[...] = _exact_nn(lane_dt[...], sel)
        dd_ref[...] += jnp.sum(_exact_nn(lane_d[...], sel), axis=0, keepdims=True)

    ob = W2 // (SSM_GROUPS * L)

    def rc(b, c):
        return b * nc + (nc - 1 - c)

    blk = pl.BlockSpec((L, L), lambda b, c: (rc(b, c), 0))
    blk_t = pl.BlockSpec((L, L), lambda b, c: (0, rc(b, c)))
    wide = pl.BlockSpec((L, W2), lambda b, c: (rc(b, c), 0))
    grp = pl.BlockSpec((L, SSM_GROUPS * L), lambda b, c: (rc(b, c), 0))
    return pl.pallas_call(
        body, name="ssd_bwd", grid=(dm.Bl, nc),
        in_specs=[wide, pl.BlockSpec((L, SSM_GROUPS * L), lambda b, c: (rc(b, c), ob)),
                  pl.BlockSpec((L, SSM_GROUPS * L), lambda b, c: (rc(b, c), ob + 1)),
                  blk, blk, blk_t, pl.BlockSpec((1, L), lambda b, c: (0, 0)),
                  pl.BlockSpec((1, L, W2), lambda b, c: (rc(b, c), 0, 0)), wide],
        out_specs=[wide, grp, grp, blk, blk, blk_t, pl.BlockSpec((1, L), lambda b, c: (0, 0))],
        out_shape=[jax.ShapeDtypeStruct((dm.T, W2), f32), jax.ShapeDtypeStruct((dm.T, SSM_GROUPS * L), f32),
                   jax.ShapeDtypeStruct((dm.T, SSM_GROUPS * L), f32), jax.ShapeDtypeStruct((dm.T, L), f32),
                   jax.ShapeDtypeStruct((dm.T, L), f32), jax.ShapeDtypeStruct((L, dm.T), f32),
                   jax.ShapeDtypeStruct((1, L), f32)],
        scratch_shapes=[pltpu.VMEM((L, W2), f32)] * 4,
        compiler_params=_cp(dimension_semantics=("arbitrary", "arbitrary")),
    )(xbc, xbc, xbc, dt, cum, cumt, _pad_lanes(d_skip), hs, dy)


def _gnorm_fwd(y, proj, w, dm):
    tr = _pick(dm.T, (256, 128))
    row = pl.BlockSpec((tr, dm.W2), lambda i: (i, 0))

    def body(y_ref, z_ref, w_ref, o_ref):
        z = z_ref[...]
        yg = y_ref[...] * (z * _sigmoid(z))
        r = lax.rsqrt(jnp.mean(yg * yg, axis=-1, keepdims=True) + EPS)
        o_ref[...] = (yg * r * w_ref[...]).astype(o_ref.dtype)

    return pl.pallas_call(
        body, name="gnorm_fwd", grid=(dm.T // tr,), in_specs=[row, row, pl.BlockSpec((1, dm.W2), lambda i: (0, 0))],
        out_specs=row, out_shape=jax.ShapeDtypeStruct((dm.T, dm.W2), ACT_DTYPE), compiler_params=_cp(),
    )(y, proj, w.reshape(1, dm.W2))


def _gnorm_bwd(dn, y, proj, w, dm):
    tr = _pick(dm.T, (256, 128))
    row = pl.BlockSpec((tr, dm.W2), lambda i: (i, 0))
    vec = pl.BlockSpec((1, dm.W2), lambda i: (0, 0))

    def body(dn_ref, y_ref, z_ref, w_ref, dy_ref, dz_ref, dw_ref):
        @pl.when(pl.program_id(0) == 0)
        def _():
            dw_ref[...] = jnp.zeros_like(dw_ref)

        z = z_ref[...]
        sg = _sigmoid(z)
        sz = z * sg
        yv = y_ref[...]
        yg = yv * sz
        r = lax.rsqrt(jnp.mean(yg * yg, axis=-1, keepdims=True) + EPS)
        n = yg * r
        dout = dn_ref[...]
        dnn = dout * w_ref[...]
        dyg = r * (dnn - n * jnp.mean(dnn * n, axis=-1, keepdims=True))
        dy_ref[...] = dyg * sz
        dz_ref[...] = (dyg * yv * (sg * (1.0 + z * (1.0 - sg)))).astype(dz_ref.dtype)
        dw_ref[...] += jnp.sum(dout * n, axis=0, keepdims=True)

    return pl.pallas_call(
        body, name="gnorm_bwd", grid=(dm.T // tr,), in_specs=[row, row, row, vec], out_specs=[row, row, vec],
        out_shape=[jax.ShapeDtypeStruct((dm.T, dm.W2), f32), jax.ShapeDtypeStruct((dm.T, dm.W2), ACT_DTYPE),
                   jax.ShapeDtypeStruct((1, dm.W2), f32)],
        compiler_params=_cp(dimension_semantics=("arbitrary",)),
    )(dn, y, proj, w.reshape(1, dm.W2))


def _pool_mixed(u, g, row):
    s = u
    for k in range(POOL_GROUPS):
        s = jnp.where(k <= g, s + _shift_down(s, 1 << k, row), s)
    w = jnp.left_shift(2, g)
    cnt = jnp.minimum(row + 1, w).astype(f32)
    return s / cnt - u, cnt


def _pool_fwd(proj, pw, scale, dm):
    Dg = dm.Dg
    ou, og = dm.c_pu // Dg, dm.c_pg // Dg

    def body(u_ref, g_ref, w_ref, s_ref, o_ref):
        u = u_ref[...]
        row = _iota2(u.shape, 0)
        mixed, _ = _pool_mixed(u, pl.program_id(1), row)
        lin = _nn(_mx(mixed), _mx(w_ref[0]))
        gt = g_ref[...]
        o_ref[...] = (lin * s_ref[...] * (gt * _sigmoid(gt))).astype(o_ref.dtype)

    return pl.pallas_call(
        body, name="pool_fwd", grid=(dm.Bl, POOL_GROUPS),
        in_specs=[pl.BlockSpec((dm.S, Dg), lambda b, g: (b, ou + g)), pl.BlockSpec((dm.S, Dg), lambda b, g: (b, og + g)),
                  pl.BlockSpec((1, Dg, Dg), lambda b, g: (g, 0, 0)), pl.BlockSpec((1, Dg), lambda b, g: (0, g))],
        out_specs=pl.BlockSpec((dm.S, Dg), lambda b, g: (b, g)),
        out_shape=jax.ShapeDtypeStruct((dm.T, dm.D), ACT_DTYPE), compiler_params=_cp(),
    )(proj, proj, pw, scale.reshape(1, dm.D))


def _pool_bwd(proj, dout, pw, scale, dm):
    Dg = dm.Dg
    ou, og = dm.c_pu // Dg, dm.c_pg // Dg

    def body(u_ref, g_ref, d_ref, w_ref, s_ref, du_ref, dg_ref, dw_ref, ds_ref):
        @pl.when(pl.program_id(1) == 0)
        def _():
            dw_ref[...] = jnp.zeros_like(dw_ref)
            ds_ref[...] = jnp.zeros_like(ds_ref)

        g = pl.program_id(0)
        u = u_ref[...]
        row = _iota2(u.shape, 0)
        mixed, cnt = _pool_mixed(u, g, row)
        mixed_b = _mx(mixed)
        wb = _mx(w_ref[0])
        lin = _nn(mixed_b, wb)
        gt = g_ref[...]
        sg = _sigmoid(gt)
        silu = gt * sg
        d = d_ref[...]
        sc = s_ref[...]
        dlin = d * sc * silu
        ds_ref[...] += jnp.sum(d * lin * silu, axis=0, keepdims=True)
        dg_ref[...] = (d * lin * sc * (sg * (1.0 + gt * (1.0 - sg)))).astype(dg_ref.dtype)
        dlin_b = _mx(dlin)
        dmixed = _nt(dlin_b, wb)
        dw_ref[0] += _tn(mixed_b, dlin_b)
        r = dmixed / cnt
        for k in range(POOL_GROUPS):
            r = jnp.where(k <= g, r + _shift_up(r, 1 << k, row), r)
        du_ref[...] = (r - dmixed).astype(du_ref.dtype)

    return pl.pallas_call(
        body, name="pool_bwd", grid=(POOL_GROUPS, dm.Bl),
        in_specs=[pl.BlockSpec((dm.S, Dg), lambda g, b: (b, ou + g)), pl.BlockSpec((dm.S, Dg), lambda g, b: (b, og + g)),
                  pl.BlockSpec((dm.S, Dg), lambda g, b: (b, g)), pl.BlockSpec((1, Dg, Dg), lambda g, b: (g, 0, 0)),
                  pl.BlockSpec((1, Dg), lambda g, b: (0, g))],
        out_specs=[pl.BlockSpec((dm.S, Dg), lambda g, b: (b, g)), pl.BlockSpec((dm.S, Dg), lambda g, b: (b, g)),
                   pl.BlockSpec((1, Dg, Dg), lambda g, b: (g, 0, 0)), pl.BlockSpec((1, Dg), lambda g, b: (0, g))],
        out_shape=[jax.ShapeDtypeStruct((dm.T, dm.D), ACT_DTYPE), jax.ShapeDtypeStruct((dm.T, dm.D), ACT_DTYPE),
                   jax.ShapeDtypeStruct((POOL_GROUPS, Dg, Dg), f32), jax.ShapeDtypeStruct((1, dm.D), f32)],
        compiler_params=_cp(dimension_semantics=("arbitrary", "arbitrary")),
    )(proj, proj, dout, pw, scale.reshape(1, dm.D))


SB_TILE = 512


def _split2(a):
    hi = a.astype(bf16)
    return hi, (a - hi.astype(f32)).astype(bf16)


def _exact2_nn(a, u):
    hi, lo = _split2(a)
    return _nn(hi, u) + _nn(lo, u)


SB_CUM = 256


def _blocked_sums(a, u, suffix):
    nb = a.shape[1] // SB_CUM
    blocks = [a[:, i * SB_CUM:(i + 1) * SB_CUM] for i in range(nb)]
    tots = [jnp.sum(b, axis=1, keepdims=True) for b in blocks]
    out = []
    for i, b in enumerate(blocks):
        s = _exact2_nn(b, u)
        for t in (tots[i + 1:] if suffix else tots[:i]):
            s = s + t
        out.append(s)
    total = tots[0]
    for t in tots[1:]:
        total = total + t
    return (out[0] if nb == 1 else jnp.concatenate(out, axis=1)), total


def _sb_tile(qb, kb, base, u, suffix, causal):
    z = _nt(qb, kb)
    lb = jnp.minimum(z, 0.0) - jnp.log(1.0 + jnp.exp(-jnp.abs(z)))
    lm = lb - z
    if causal is not None:
        lm = jnp.where(causal, lm, 0.0)
    sums, tot = _blocked_sums(lm, u, suffix)
    att = jnp.exp(lb + base + sums) if suffix else jnp.exp(lb + base - sums)
    if causal is not None:
        att = jnp.where(causal, att, 0.0)
    return lb, att, tot


def _sb_fwd(proj, dm):
    L, S, D, TQ = LANES, dm.S, dm.D, SB_TILE
    oq, og = dm.c_qkv // L, dm.c_sbg // L
    nb = D // L
    scale = HEAD ** -0.5

    def body(q_ref, k_ref, v_ref, g_ref, o_ref, lt_ref, out_ref, k_s, va_s, vb_s):
        lo = _iota2((1, L), 1) < HEAD
        hi = jnp.logical_not(lo)
        k_s[...] = _mx(k_ref[...])
        vf = v_ref[...]
        va_s[...] = _mx(jnp.where(lo, vf, 0.0))
        vb_s[...] = _mx(jnp.where(hi, vf, 0.0))
        before = _iota2((TQ, TQ), 0) > _iota2((TQ, TQ), 1)
        ugt = (_iota2((SB_CUM, SB_CUM), 0) > _iota2((SB_CUM, SB_CUM), 1)).astype(bf16)
        v_s = (va_s, vb_s)

        def qloop(qt, _):
            rows = pl.ds(pl.multiple_of(qt * TQ, TQ), TQ)
            qs = q_ref[rows, :] * scale
            qb = [_mx(jnp.where(m, qs, 0.0)) for m in (lo, hi)]

            def tile(kt, carry, causal):
                runs, acc = carry
                krows = pl.ds(pl.multiple_of(kt * TQ, TQ), TQ)
                k_t = k_s[krows, :]
                new_runs = []
                for h in range(2):
                    _, att, tot = _sb_tile(qb[h], k_t, runs[h], ugt, True, causal)
                    acc = acc + _nn(_mx(att), v_s[h][krows, :])
                    new_runs.append(runs[h] + tot)
                return tuple(new_runs), acc

            zc = jnp.zeros((TQ, 1), f32)
            carry = tile(qt, ((zc, zc), jnp.zeros((TQ, L), f32)), before)
            runs, acc = lax.fori_loop(0, qt, lambda jj, c: tile(qt - 1 - jj, c, None), carry)
            o_ref[rows, :] = acc
            lt_ref[rows, :] = jnp.where(lo, runs[0], runs[1])
            gt = g_ref[rows, :]
            out_ref[rows, :] = (acc * (gt * _sigmoid(gt))).astype(out_ref.dtype)
            return 0

        lax.fori_loop(0, S // TQ, qloop, 0)

    def col(o):
        return pl.BlockSpec((S, L), lambda b, p: (b, o + p))

    return pl.pallas_call(
        body, name="sb_fwd", grid=(dm.Bl, nb),
        in_specs=[col(oq), col(oq + nb), col(oq + 2 * nb), col(og)], out_specs=[col(0), col(0), col(0)],
        out_shape=[jax.ShapeDtypeStruct((dm.T, D), f32), jax.ShapeDtypeStruct((dm.T, D), f32),
                   jax.ShapeDtypeStruct((dm.T, D), ACT_DTYPE)],
        scratch_shapes=[pltpu.VMEM((S, L), MXU_DTYPE)] * 3,
        compiler_params=_cp(),
    )(proj, proj, proj, proj)


def _sb_bwd(proj, o, lt, dsb, dm):
    L, S, D, TQ = LANES, dm.S, dm.D, SB_TILE
    oq, og = dm.c_qkv // L, dm.c_sbg // L
    nb = D // L
    scale = HEAD ** -0.5

    def body(q_ref, k_ref, v_ref, g_ref, o_ref, lt_ref, d_ref, dq_ref, dk_ref, dv_ref, dg_ref,
             k_s, ka_s, kb_s, v_s, dk_acc, dv_acc):
        lo = _iota2((1, L), 1) < HEAD
        hi = jnp.logical_not(lo)
        kf = k_ref[...]
        k_s[...] = _mx(kf)
        ka_s[...] = _mx(jnp.where(lo, kf, 0.0))
        kb_s[...] = _mx(jnp.where(hi, kf, 0.0))
        v_s[...] = _mx(v_ref[...])
        dk_acc[...] = jnp.zeros_like(dk_acc)
        dv_acc[...] = jnp.zeros_like(dv_acc)
        before = _iota2((TQ, TQ), 0) > _iota2((TQ, TQ), 1)
        ule = (_iota2((SB_CUM, SB_CUM), 0) <= _iota2((SB_CUM, SB_CUM), 1)).astype(bf16)
        ult = (_iota2((SB_CUM, SB_CUM), 0) < _iota2((SB_CUM, SB_CUM), 1)).astype(bf16)
        k_m = (ka_s, kb_s)

        def qloop(qt, _):
            rows = pl.ds(pl.multiple_of(qt * TQ, TQ), TQ)
            gt = g_ref[rows, :]
            sg = _sigmoid(gt)
            dsb_b = d_ref[rows, :]
            do = dsb_b * (gt * sg)
            dg_ref[rows, :] = (dsb_b * o_ref[rows, :] * (sg * (1.0 + gt * (1.0 - sg)))).astype(dg_ref.dtype)
            qs = q_ref[rows, :] * scale
            qb = [_mx(jnp.where(m, qs, 0.0)) for m in (lo, hi)]
            dob = [_mx(jnp.where(m, do, 0.0)) for m in (lo, hi)]
            ltot = [lt_ref[rows, 0:1], lt_ref[rows, HEAD:HEAD + 1]]

            def tile(kt, carry, causal):
                runs, rgs, dq = carry
                krows = pl.ds(pl.multiple_of(kt * TQ, TQ), TQ)
                k_t, v_t = k_s[krows, :], v_s[krows, :]
                dk_t = jnp.zeros((TQ, L), f32)
                dv_t = jnp.zeros((TQ, L), f32)
                new_runs, new_rgs = [], []
                for h in range(2):
                    lb, att, tot = _sb_tile(qb[h], k_t, ltot[h] - runs[h], ule, False, causal)
                    gm = att * _nt(dob[h], v_t)
                    gsum, gtot = _blocked_sums(gm, ult, False)
                    pre = rgs[h] + gsum
                    beta = jnp.exp(lb)
                    dz = gm * (1.0 - beta) - pre * beta
                    if causal is not None:
                        dz = jnp.where(causal, dz, 0.0)
                    dz = _mx(dz)
                    dq = dq + _nn(dz, k_m[h][krows, :])
                    dk_t = dk_t + _tn(dz, qb[h])
                    dv_t = dv_t + _tn(_mx(att), dob[h])
                    new_runs.append(runs[h] + tot)
                    new_rgs.append(rgs[h] + gtot)
                dk_acc[krows, :] += dk_t
                dv_acc[krows, :] += dv_t
                return tuple(new_runs), tuple(new_rgs), dq

            zc = jnp.zeros((TQ, 1), f32)
            carry = lax.fori_loop(0, qt, lambda kt, c: tile(kt, c, None), ((zc, zc), (zc, zc), jnp.zeros((TQ, L), f32)))
            _, _, dq = tile(qt, carry, before)
            dq_ref[rows, :] = (dq * scale).astype(dq_ref.dtype)
            return 0

        lax.fori_loop(0, S // TQ, qloop, 0)
        dk_ref[...] = dk_acc[...].astype(dk_ref.dtype)
        dv_ref[...] = dv_acc[...].astype(dv_ref.dtype)

    def col(off):
        return pl.BlockSpec((S, L), lambda b, p: (b, off + p))

    out = jax.ShapeDtypeStruct((dm.T, D), ACT_DTYPE)
    return pl.pallas_call(
        body, name="sb_bwd", grid=(dm.Bl, nb),
        in_specs=[col(oq), col(oq + nb), col(oq + 2 * nb), col(og), col(0), col(0), col(0)],
        out_specs=[col(0), col(0), col(0), col(0)], out_shape=[out, out, out, out],
        scratch_shapes=[pltpu.VMEM((S, L), MXU_DTYPE)] * 4 + [pltpu.VMEM((S, L), f32)] * 2,
        compiler_params=_cp(),
    )(proj, proj, proj, proj, o, lt, dsb)


def _merge_fwd(proj, ys, dm):
    tr = _pick(dm.T, (256, 128))
    ct = 512
    om = dm.c_mrg // ct
    nb = dm.D // ct
    blk = pl.BlockSpec((tr, ct), lambda i, j: (i, j))

    def body(l0, l1, l2, y0, y1, y2, o_ref):
        acc = _sigmoid(l0[...]) * y0[...] + _sigmoid(l1[...]) * y1[...] + _sigmoid(l2[...]) * y2[...]
        o_ref[...] = acc.astype(o_ref.dtype)

    return pl.pallas_call(
        body, name="merge_fwd", grid=(dm.T // tr, nb),
        in_specs=[pl.BlockSpec((tr, ct), functools.partial(lambda i, j, b: (i, om + b * nb + j), b=b)) for b in range(3)] + [blk] * 3,
        out_specs=blk, out_shape=jax.ShapeDtypeStruct((dm.T, dm.D), ACT_DTYPE), compiler_params=_cp(),
    )(proj, proj, proj, *ys)


def _merge_bwd(proj, ys, dmerged, dm):
    tr = _pick(dm.T, (256, 128))
    ct = 512
    om = dm.c_mrg // ct
    nb = dm.D // ct
    blk = pl.BlockSpec((tr, ct), lambda i, j: (i, j))

    def body(l0, l1, l2, y0, y1, y2, d_ref, dy0, dy1, dy2, dl0, dl1, dl2):
        d = d_ref[...]
        for l_ref, y_ref, dy_ref, dl_ref in ((l0, y0, dy0, dl0), (l1, y1, dy1, dl1), (l2, y2, dy2, dl2)):
            sg = _sigmoid(l_ref[...])
            dy_ref[...] = (d * sg).astype(dy_ref.dtype)
            dl_ref[...] = (d * y_ref[...] * sg * (1.0 - sg)).astype(dl_ref.dtype)

    out = jax.ShapeDtypeStruct((dm.T, dm.D), ACT_DTYPE)
    return pl.pallas_call(
        body, name="merge_bwd", grid=(dm.T // tr, nb),
        in_specs=[pl.BlockSpec((tr, ct), functools.partial(lambda i, j, b: (i, om + b * nb + j), b=b)) for b in range(3)] + [blk] * 4,
        out_specs=[blk] * 6, out_shape=[out] * 6, compiler_params=_cp(),
    )(proj, proj, proj, *ys, dmerged)


def _layer_fwd(x, p, dm):
    h = _rms_fwd(x, p["norm_w"], dm)
    proj = _mm(h, p["w_in"], name="mm_in")
    xbc = _conv_fwd(proj, p["conv_w"], p["conv_b"], dm)
    dt, cum, cumt = _dt_prep(proj, p["dt_bias"], p["a_log"], dm)
    y, hs = _ssd_fwd(xbc, dt, cum, cumt, p["d_skip"], dm)
    ssm_n = _gnorm_fwd(y, proj, p["ssm_norm_w"], dm)
    pool_o = _pool_fwd(proj, p["pool_w"], p["pool_scale"], dm)
    o, lt, sb_o = _sb_fwd(proj, dm)
    ys = (_mm(ssm_n, p["w_proj_ssm"], name="mm_ps"), _mm(pool_o, p["w_proj_pool"], name="mm_pp"),
          _mm(sb_o, p["w_proj_sb"], name="mm_pb"))
    merged = _merge_fwd(proj, ys, dm)
    x_next = _mm(merged, p["w_out"], res=x, name="mm_out")
    saved = dict(x=x, h=h, proj=proj, xbc=xbc, dt=dt, cum=cum, cumt=cumt, y=y, hs=hs, ssm_n=ssm_n, pool_o=pool_o, o=o,
                 lt=lt, sb_o=sb_o, ys=ys, merged=merged)
    return x_next, saved


def _layer_bwd(dx_out, p, sv, dm):
    g = {}
    proj = sv["proj"]
    dmerged = _mm(dx_out, p["w_out"], tb=True, name="mm_dmerged")
    g["w_out"] = _mm(sv["merged"], dx_out, ta=True, name="mm_dwout")
    dy0, dy1, dy2, dl0, dl1, dl2 = _merge_bwd(proj, sv["ys"], dmerged, dm)
    d_ssm_n = _mm(dy0, p["w_proj_ssm"], tb=True, name="mm_dssm")
    g["w_proj_ssm"] = _mm(sv["ssm_n"], dy0, ta=True, name="mm_dwps")
    d_pool_o = _mm(dy1, p["w_proj_pool"], tb=True, name="mm_dpool")
    g["w_proj_pool"] = _mm(sv["pool_o"], dy1, ta=True, name="mm_dwpp")
    d_sb_o = _mm(dy2, p["w_proj_sb"], tb=True, name="mm_dsb")
    g["w_proj_sb"] = _mm(sv["sb_o"], dy2, ta=True, name="mm_dwpb")
    dy, dz, g["ssm_norm_w"] = _gnorm_bwd(d_ssm_n, sv["y"], proj, p["ssm_norm_w"], dm)
    dxs, db, dc, ddt, dcum, dcum_t, dd = _ssd_bwd(sv["xbc"], sv["dt"], sv["cum"], sv["cumt"], p["d_skip"], sv["hs"], dy, dm)
    g["d_skip"] = dd
    d_dtraw, g["dt_bias"], g["a_log"] = _dt_bwd(proj, sv["dt"], dcum, dcum_t, ddt, p["dt_bias"], p["a_log"], dm)
    dxbc, g["conv_w"], g["conv_b"] = _conv_bwd(proj, jnp.concatenate([dxs, db, dc], axis=1), p["conv_w"], p["conv_b"], dm)
    dpu, dpg, g["pool_w"], g["pool_scale"] = _pool_bwd(proj, d_pool_o, p["pool_w"], p["pool_scale"], dm)
    dq, dk, dv, dsg = _sb_bwd(proj, sv["o"], sv["lt"], d_sb_o, dm)
    dproj = jnp.concatenate([dz, dxbc, dpu, dpg, dq, dk, dv, dsg, dl0, dl1, dl2, d_dtraw,
                             jnp.zeros((dm.T, DT_PAD - LANES), ACT_DTYPE)], axis=1)
    dh = _mm(dproj, p["w_in"], tb=True, name="mm_dh")
    g["w_in"] = _mm(sv["h"], dproj, ta=True, name="mm_dwin")
    dx, g["norm_w"] = _rms_bwd(sv["x"], dh, dx_out, p["norm_w"], dm)
    return dx, g


def _local_step(x, tgt, layers, final_norm_w, dm, on_grads=None):
    saved = []
    layers = list(layers)
    for l, p in enumerate(layers):
        if callable(p):
            p, x = p(x)
            layers[l] = p
        x, sv = _layer_fwd(x, p, dm)
        saved.append(sv)
    dx, dfn, loss = _loss_head(x, tgt, final_norm_w, dm)
    grads = [None] * len(layers)
    for l in reversed(range(len(layers))):
        dx, grads[l] = _layer_bwd(dx, layers[l], saved[l], dm)
        if on_grads is not None:
            dx = on_grads(l, grads, dx)
    return loss, dx, grads, dfn


def _row_tile(rows, cols):
    cap = max(8, (1 << 18) // cols)
    for t in (1024, 512, 256, 128, 64, 32, 16, 8):
        if t <= cap and rows % t == 0:
            return t
    return rows


def _adamw(w, g, m, v, name):
    rows, cols = w.shape
    tr = _row_tile(rows, cols)
    c1 = 1.0 - ADAM_B1 ** ADAM_STEP
    c2 = 1.0 - ADAM_B2 ** ADAM_STEP

    def body(w_ref, g_ref, m_ref, v_ref, d_ref, mo_ref, vo_ref):
        gv = g_ref[...]
        mn = ADAM_B1 * m_ref[...] + (1.0 - ADAM_B1) * gv
        vn = ADAM_B2 * v_ref[...] + (1.0 - ADAM_B2) * (gv * gv)
        d_ref[...] = -ADAM_LR * ((mn / c1) / (jnp.sqrt(vn / c2) + ADAM_EPS) + ADAM_WD * w_ref[...])
        mo_ref[...] = mn
        vo_ref[...] = vn

    blk = pl.BlockSpec((tr, cols), lambda i: (i, 0))
    out = jax.ShapeDtypeStruct((rows, cols), f32)
    return pl.pallas_call(body, name=name, grid=(rows // tr,), in_specs=[blk] * 4, out_specs=[blk] * 3, out_shape=[out] * 3,
                          compiler_params=_cp())(w, g, m, v)


def _sum_parts(a, parts, name):
    n, rows, cols = parts.shape
    tr = _row_tile(rows, cols)

    def body(a_ref, p_ref, o_ref):
        acc = a_ref[...]
        for k in range(n):
            acc = acc + p_ref[k].astype(f32)
        o_ref[...] = acc

    blk = pl.BlockSpec((tr, cols), lambda i: (i, 0))
    return pl.pallas_call(body, name=name, grid=(rows // tr,), in_specs=[blk, pl.BlockSpec((n, tr, cols), lambda i: (0, i, 0))],
                          out_specs=blk, out_shape=jax.ShapeDtypeStruct((rows, cols), f32), compiler_params=_cp())(a, parts)


ICI_KINDS = ("y", "x", "xy")
ASYNC_GATHER_ID = 1
ASYNC_IDS = {"rs_pair_late": 2, "rs_ici_late": 3, "rs_sibling_late": 4}
HBM_SPEC = pl.BlockSpec(memory_space=pltpu.HBM)


def _me():
    return lax.axis_index("x"), lax.axis_index("y"), lax.axis_index("c")


def _peer(kind):
    x, y, c = _me()
    return {"c": (x, y, 1 - c), "y": (x, 1 - y, c), "x": (1 - x, y, c), "xy": (1 - x, 1 - y, c)}[kind]


def _peer_chip(kind):
    x, y, _ = _me()
    return {"y": 2 * x + (1 - y), "x": 2 * (1 - x) + y, "xy": 2 * (1 - x) + (1 - y)}[kind]


def _exchange(sends, kinds, name):
    n, na = len(kinds), len(sends)

    def body(*refs):
        _exchange_copies(refs[:na], refs[na:2 * na], *refs[2 * na:], kinds)

    return pl.pallas_call(
        body, name=name, out_shape=[jax.ShapeDtypeStruct(a.shape, a.dtype) for a in sends], in_specs=[HBM_SPEC] * na,
        out_specs=[HBM_SPEC] * na,
        scratch_shapes=[pltpu.SemaphoreType.DMA((na * n,)), pltpu.SemaphoreType.DMA((na * n,))],
    )(*sends)


def _exchange_copies(srcs, dsts, ssem, rsem, kinds):
    n = len(kinds)
    cps = [pltpu.make_async_remote_copy(src_ref=srcs[i].at[k], dst_ref=dsts[i].at[k], send_sem=ssem.at[i * n + k],
                                        recv_sem=rsem.at[i * n + k], device_id=_peer(kind), device_id_type=MESH)
           for i in range(len(srcs)) for k, kind in enumerate(kinds)]
    for cp in cps:
        cp.start()
    for cp in cps:
        cp.wait()


def _exchange_async(sends, kinds, name):
    n, na = len(kinds), len(sends)
    srcs = [jax.new_ref(a, memory_space=pltpu.MemorySpace.HBM) for a in sends]
    dsts = [jax.empty_ref(jax.ShapeDtypeStruct(a.shape, a.dtype), memory_space=pltpu.MemorySpace.HBM) for a in sends]

    @pl.kernel(mesh=plsc.ScalarSubcoreMesh(axis_name="sequencer", num_cores=1), name=name,
               scratch_types=(pltpu.SemaphoreType.DMA((na * n,)), pltpu.SemaphoreType.DMA((na * n,))),
               compiler_params=pltpu.CompilerParams(collective_id=ASYNC_IDS[name]))
    def launch(ssem, rsem):
        barrier = pltpu.get_barrier_semaphore()
        for kind in kinds:
            pl.semaphore_signal(barrier, inc=1, device_id=_peer(kind), device_id_type=MESH)
        pl.semaphore_wait(barrier, n)
        _exchange_copies(srcs, dsts, ssem, rsem, kinds)

    launch()
    return [r[...] for r in dsts]


def _allgather_copies(srcs, outs, ssem, rsem, lsem, lh):
    na = len(srcs)
    x, y, c = _me()
    j_me = 2 * x + y
    mine = pl.ds(c * lh, lh)
    theirs = pl.ds((1 - c) * lh, lh)
    local = [pltpu.make_async_copy(srcs[i], outs[i].at[j_me], lsem.at[i]) for i in range(na)]
    for cp in local:
        cp.start()

    def ici(i, k, kind, j_src):
        return pltpu.make_async_remote_copy(src_ref=srcs[i].at[mine], dst_ref=outs[i].at[j_src, mine],
                                            send_sem=ssem.at[6 * i + k], recv_sem=rsem.at[6 * i + k],
                                            device_id=_peer(kind), device_id_type=MESH)

    def d2d(i, k, j_src, half):
        return pltpu.make_async_remote_copy(src_ref=outs[i].at[j_src, half], dst_ref=outs[i].at[j_src, half],
                                            send_sem=ssem.at[6 * i + 3 + k], recv_sem=rsem.at[6 * i + 3 + k],
                                            device_id=_peer("c"), device_id_type=MESH)

    first = [ici(i, k, kind, j_me) for i in range(na) for k, kind in enumerate(ICI_KINDS)]
    for cp in first:
        cp.start()
    passed = []
    for i in range(na):
        for k, kind in enumerate(ICI_KINDS):
            ici(i, k, kind, _peer_chip(kind)).wait_recv()
            fwd = d2d(i, k, _peer_chip(kind), mine)
            fwd.start()
            passed.append(fwd)
    for i in range(na):
        for k, kind in enumerate(ICI_KINDS):
            d2d(i, k, _peer_chip(kind), theirs).wait_recv()
    for cp in first + passed:
        cp.wait_send()
    for cp in local:
        cp.wait()


def _allgather_sems(na):
    return [pltpu.SemaphoreType.DMA((6 * na,)), pltpu.SemaphoreType.DMA((6 * na,)), pltpu.SemaphoreType.DMA((na,))]


def _allgather_shards(shards):
    na = len(shards)
    lh = shards[0].shape[0] // 2

    def body(*refs):
        _allgather_copies(refs[:na], refs[na:2 * na], *refs[2 * na:], lh)

    return pl.pallas_call(
        body, name="allgather_shards", out_shape=[jax.ShapeDtypeStruct((4,) + a.shape, a.dtype) for a in shards],
        in_specs=[HBM_SPEC] * na, out_specs=[HBM_SPEC] * na, scratch_shapes=_allgather_sems(na),
    )(*shards)


def _allgather_shards_async(shards):
    na = len(shards)
    lh = shards[0].shape[0] // 2
    srcs = [jax.new_ref(a, memory_space=pltpu.MemorySpace.HBM) for a in shards]
    outs = [jax.empty_ref(jax.ShapeDtypeStruct((4,) + a.shape, a.dtype), memory_space=pltpu.MemorySpace.HBM) for a in shards]

    @pl.kernel(mesh=plsc.ScalarSubcoreMesh(axis_name="sequencer", num_cores=1), name="allgather_shards_async",
               scratch_types=tuple(_allgather_sems(na)), compiler_params=pltpu.CompilerParams(collective_id=ASYNC_GATHER_ID))
    def launch(ssem, rsem, lsem):
        barrier = pltpu.get_barrier_semaphore()
        peers = ("c",) + ICI_KINDS
        for kind in peers:
            pl.semaphore_signal(barrier, inc=1, device_id=_peer(kind), device_id_type=MESH)
        pl.semaphore_wait(barrier, len(peers))
        _allgather_copies(srcs, outs, ssem, rsem, lsem, lh)

    launch()
    return [r[...] for r in outs]


def _allreduce_small(v):
    rows = v.shape[0]

    def body(v_ref, o_ref, buf, ssem, rsem):
        x, y, c = _me()
        me = 4 * x + 2 * y + c
        buf[0] = v_ref[...]
        cps = []
        for k in range(1, 8):
            peer = (1 - x if k & 4 else x, 1 - y if k & 2 else y, 1 - c if k & 1 else c)
            cps.append(pltpu.make_async_remote_copy(src_ref=v_ref, dst_ref=buf.at[k], send_sem=ssem.at[k - 1], recv_sem=rsem.at[k - 1],
                                                    device_id=peer, device_id_type=MESH))
        for cp in cps:
            cp.start()
        for cp in cps:
            cp.wait()
        acc = buf[jnp.bitwise_xor(me, 0)]
        for d in range(1, 8):
            acc = acc + buf[jnp.bitwise_xor(me, d)]
        o_ref[...] = acc

    vm = pl.BlockSpec(memory_space=pltpu.VMEM)
    return pl.pallas_call(
        body, name="allreduce_small", out_shape=jax.ShapeDtypeStruct(v.shape, f32), in_specs=[vm], out_specs=vm,
        scratch_shapes=[pltpu.VMEM((8, rows, LANES), f32), pltpu.SemaphoreType.DMA((7,)), pltpu.SemaphoreType.DMA((7,))],
    )(v)


SHARDED = ("w_in", "w_proj_ssm", "w_proj_pool", "w_proj_sb", "w_out", "pool_w", "conv_w")
REST = SHARDED[1:6]
SMALL = ("norm_w", "conv_b", "dt_bias", "a_log", "d_skip", "ssm_norm_w", "pool_scale")


def _pack_rows(dm):
    rows = dict(w_proj_ssm=dm.W2 // 4, w_proj_pool=dm.D // 4, w_proj_sb=dm.D // 4, w_out=dm.D // 4, pool_w=dm.D // 16)
    assert all(r % 16 == 0 for r in rows.values())
    return rows


def _pack(sh, dm, dtype):
    rows = _pack_rows(dm)
    ly = sh["w_out"].shape[0]
    return jnp.concatenate([sh[n].astype(dtype).reshape(ly, rows[n], dm.D) for n in REST], axis=1)


def _unpack(pk, dm):
    rows = _pack_rows(dm)
    lead = pk.shape[:-2]
    shapes = dict(w_proj_ssm=(dm.W2 // 4, dm.D), w_proj_pool=(dm.D // 4, dm.D), w_proj_sb=(dm.D // 4, dm.D),
                  w_out=(dm.D // 4, dm.D), pool_w=(POOL_GROUPS, dm.Dg // 4, dm.Dg))
    out, r0 = {}, 0
    for n in REST:
        out[n] = pk[..., r0:r0 + rows[n], :].reshape(lead + shapes[n])
        r0 += rows[n]
    return out


def _join_shards(sh, dm):
    full = {}
    w = jnp.moveaxis(sh["w_in"], 0, 2)
    full["w_in"] = _permute_cols(w.reshape(w.shape[0], dm.D, dm.IN_COLS), dm)
    for n in ("w_proj_ssm", "w_proj_pool", "w_proj_sb", "w_out"):
        a = jnp.moveaxis(sh[n], 0, 1)
        full[n] = a.reshape(a.shape[0], -1, dm.D)
    a = jnp.moveaxis(sh["pool_w"], 0, 2)
    full["pool_w"] = a.reshape(a.shape[0], POOL_GROUPS, dm.Dg, dm.Dg)
    a = jnp.moveaxis(sh["conv_w"], 0, 2)
    full["conv_w"] = a.reshape(a.shape[0], CONV_WIDTH, dm.CC)
    return full


def _split_shards(full, dm):
    sh = {}
    w = _unpermute_cols(full["w_in"], dm)
    sh["w_in"] = jnp.moveaxis(w.reshape(w.shape[0], dm.D, 4, dm.IN_COLS // 4), 2, 0)
    for n in ("w_proj_ssm", "w_proj_pool", "w_proj_sb", "w_out"):
        a = full[n]
        sh[n] = jnp.moveaxis(a.reshape(a.shape[0], 4, a.shape[1] // 4, dm.D), 1, 0)
    a = full["pool_w"]
    sh["pool_w"] = jnp.moveaxis(a.reshape(a.shape[0], POOL_GROUPS, 4, dm.Dg // 4, dm.Dg), 2, 0)
    a = full["conv_w"]
    sh["conv_w"] = jnp.moveaxis(a.reshape(a.shape[0], CONV_WIDTH, 4, dm.CC // 4), 2, 0)
    return sh


def _sum_like(mine, parts, name):
    cols = mine.shape[-1]
    return _sum_parts(mine.reshape(-1, cols), parts.reshape(parts.shape[0], -1, cols), name).reshape(mine.shape)


def _slab(a, j):
    return lax.dynamic_index_in_dim(a, j, 0, keepdims=False)


class _ReduceScatter:
    TAGS = ("in", "rest", "conv")

    def __init__(self, full_grads, dm, exchange, suffix=""):
        x, y, self.c = _me()
        self.j_me, self.dm, self.exchange, self.suffix = 2 * x + y, dm, exchange, suffix
        sh = _split_shards(full_grads, dm)
        by_chip = [sh["w_in"], jnp.stack([_pack({n: sh[n][j] for n in REST}, dm, f32) for j in range(4)]), sh["conv_w"]]
        lh = by_chip[0].shape[1] // 2
        self.mine = [lax.dynamic_slice_in_dim(a, self.c * lh, lh, axis=1) for a in by_chip]
        theirs = [lax.dynamic_slice_in_dim(a, (1 - self.c) * lh, lh, axis=1).astype(WIRE_DTYPE)[None] for a in by_chip]
        self.got = exchange(theirs, ("c",), "rs_pair" + suffix)

    def pair_sum(self, dep=None):
        got = self.got
        if dep is not None:
            got, dep = lax.optimization_barrier((got, dep))
        self.s1 = [_sum_like(a, g, "rs_pair_sum_" + t) for a, g, t in zip(self.mine, got, self.TAGS)]
        send = [jnp.stack([_slab(a, jnp.bitwise_xor(self.j_me, k)) for k in (1, 2, 3)]).astype(WIRE_DTYPE) for a in self.s1]
        self.got = self.exchange(send, ICI_KINDS, "rs_ici" + self.suffix)
        return dep

    def ici_sum(self, dep=None):
        got = self.got
        if dep is not None:
            got, dep = lax.optimization_barrier((got, dep))
        self.red = [_sum_like(_slab(a, self.j_me), g, "rs_ici_sum_" + t) for a, g, t in zip(self.s1, got, self.TAGS)]
        self.sib = self.exchange([r[None] for r in self.red], ("c",), "rs_sibling" + self.suffix)
        return dep

    def result(self):
        both = [jnp.where(self.c == 0, jnp.concatenate([r, s[0]]), jnp.concatenate([s[0], r])) for r, s in zip(self.red, self.sib)]
        out = _unpack(both[1], self.dm)
        out["w_in"], out["conv_w"] = both[0], both[2]
        return out


def _reduce_scatter(full_grads, dm):
    rs = _ReduceScatter(full_grads, dm, _exchange)
    rs.pair_sum()
    rs.ici_sum()
    return rs.result()


def _flatten_small(parts):
    flat = jnp.concatenate([p.reshape(-1).astype(f32) for p in parts])
    rows = -(-flat.shape[0] // (8 * LANES)) * 8
    return jnp.pad(flat, (0, rows * LANES - flat.shape[0])).reshape(rows, LANES)


def _unflatten_small(buf, shapes):
    flat = buf.reshape(-1)
    out, o = [], 0
    for s in shapes:
        n = math.prod(s)
        out.append(flat[o:o + n].reshape(s))
        o += n
    return out


def kernel(x, norm_w, w_in, conv_w, conv_b, dt_bias, a_log, d_skip, ssm_norm_w, pool_w, pool_scale, w_proj_ssm, w_proj_pool, w_proj_sb, w_out, final_norm_w, loss_target, m_norm_w, m_w_in, m_conv_w, m_conv_b, m_dt_bias, m_a_log, m_d_skip, m_ssm_norm_w, m_pool_w, m_pool_scale, m_w_proj_ssm, m_w_proj_pool, m_w_proj_sb, m_w_out, m_final_norm_w, v_norm_w, v_w_in, v_conv_w, v_conv_b, v_dt_bias, v_a_log, v_d_skip, v_ssm_norm_w, v_pool_w, v_pool_scale, v_w_proj_ssm, v_w_proj_pool, v_w_proj_sb, v_w_out, v_final_norm_w):
    names = ("norm_w", "w_in", "conv_w", "conv_b", "dt_bias", "a_log", "d_skip", "ssm_norm_w", "pool_w", "pool_scale",
             "w_proj_ssm", "w_proj_pool", "w_proj_sb", "w_out", "final_norm_w")
    w = dict(zip(names, (norm_w, w_in, conv_w, conv_b, dt_bias, a_log, d_skip, ssm_norm_w, pool_w, pool_scale, w_proj_ssm,
                         w_proj_pool, w_proj_sb, w_out, final_norm_w)))
    m = dict(zip(names, (m_norm_w, m_w_in, m_conv_w, m_conv_b, m_dt_bias, m_a_log, m_d_skip, m_ssm_norm_w, m_pool_w, m_pool_scale,
                         m_w_proj_ssm, m_w_proj_pool, m_w_proj_sb, m_w_out, m_final_norm_w)))
    v = dict(zip(names, (v_norm_w, v_w_in, v_conv_w, v_conv_b, v_dt_bias, v_a_log, v_d_skip, v_ssm_norm_w, v_pool_w, v_pool_scale,
                         v_w_proj_ssm, v_w_proj_pool, v_w_proj_sb, v_w_out, v_final_norm_w)))
    bl, s, d = x.shape
    dm = Dims(bl, s, d)
    ly = norm_w.shape[0]

    mine = [w["w_in"].astype(WIRE_DTYPE), _pack({n: w[n] for n in REST}, dm, WIRE_DTYPE), conv_w]
    if ly % 4 == 0:
        gathered = [_allgather_shards([a[:ly // 2] for a in mine])]
        later = lax.optimization_barrier(([a[ly // 2:] for a in mine], gathered[0]))[0]
        gathered.append(_allgather_shards_async(later))
    else:
        gathered = [_allgather_shards(mine)]
    def layer_dicts(part, first):
        g_in, g_rest, g_conv = part
        shards = _unpack(g_rest, dm)
        shards["w_in"], shards["conv_w"] = g_in, g_conv
        full = _join_shards(shards, dm)
        return [{**{n: full[n][l] for n in SHARDED}, **{n: w[n][first + l] for n in SMALL}} for l in range(g_in.shape[1])]

    layers = layer_dicts(gathered[0], 0)
    if len(gathered) > 1:
        late = {}

        def late_layer(x, l):
            if not late:
                part, x = lax.optimization_barrier((gathered[1], x))
                late["dicts"] = layer_dicts(part, ly // 2)
            return late["dicts"][l], x

        layers += [functools.partial(late_layer, l=l) for l in range(ly - ly // 2)]

    def stacked(grads, lo_l, hi_l):
        return {n: jnp.stack([grads[l][n] for l in range(lo_l, hi_l)]) for n in SHARDED}

    late = {}

    def on_grads(l, grads, dx):
        half = ly // 2
        if l == half:
            late["rs"] = _ReduceScatter(stacked(grads, half, ly), dm, _exchange_async, "_late")
        elif l == half - 1:
            dx = late["rs"].pair_sum(dx)
        elif l == half - 2:
            dx = late["rs"].ici_sum(dx)
        return dx

    loss_part, dx, grads, dfn = _local_step(x.reshape(dm.T, d), loss_target.reshape(dm.T, d), layers, final_norm_w, dm,
                                            on_grads if ly % 4 == 0 else None)

    h = dm.H
    small_parts = [loss_part]
    small_shapes = [(1, LANES)]
    for n in SMALL:
        if n in ("dt_bias", "a_log", "d_skip"):
            small_parts.append(jnp.stack([g[n][0, :h] for g in grads]))
        else:
            small_parts.append(jnp.stack([g[n][0] for g in grads]))
        small_shapes.append(w[n].shape)
    small_parts.append(dfn[0])
    small_shapes.append(final_norm_w.shape)
    red_small = _allreduce_small(_flatten_small(small_parts))
    small_g = _unflatten_small(red_small, small_shapes)
    loss = small_g[0][0, 0]
    g_out = dict(zip(SMALL + ("final_norm_w",), small_g[1:]))

    if late:
        early = _reduce_scatter(stacked(grads, 0, ly // 2), dm)
        rest = late["rs"].result()
        g_out.update({n: jnp.concatenate([early[n], rest[n]]) for n in SHARDED})
    else:
        g_out.update(_reduce_scatter(stacked(grads, 0, ly), dm))

    small_names = SMALL + ("final_norm_w",)
    zero_row = jnp.zeros((1, LANES), f32)
    pack_small = lambda t: _flatten_small([zero_row] + [t[n] for n in small_names])
    ds, ms, vs = _adamw(pack_small(w), red_small, pack_small(m), pack_small(v), "adamw_small")
    delta, new_m, new_v = {}, {}, {}
    for tgt, buf in ((delta, ds), (new_m, ms), (new_v, vs)):
        tgt.update(zip(small_names, _unflatten_small(buf, small_shapes)[1:]))
    for n in SHARDED:
        shp = w[n].shape
        two = (math.prod(shp[:-1]), shp[-1])
        dd, mm, vv = _adamw(w[n].reshape(two), g_out[n].reshape(two), m[n].reshape(two), v[n].reshape(two), "adamw_" + n)
        delta[n], new_m[n], new_v[n] = dd.reshape(shp), mm.reshape(shp), vv.reshape(shp)
        g_out[n] = g_out[n].reshape(shp)

    return (loss, dx.reshape(bl, s, d), *[g_out[n] for n in names], *[delta[n] for n in names],
            *[new_m[n] for n in names], *[new_v[n] for n in names])
```

```python
import functools
import math

import jax
import jax.numpy as jnp
from jax import lax
from jax.experimental import pallas as pl
from jax.experimental.pallas import tpu as pltpu
from jax.experimental.pallas import tpu_sc as plsc

f32 = jnp.float32
bf16 = jnp.bfloat16
MXU_DTYPE = jnp.bfloat16
ACT_DTYPE = jnp.bfloat16
WIRE_DTYPE = jnp.bfloat16

EPS = 1e-6
LANES = 128
HEAD = 64
SSM_GROUPS = 2
CONV_WIDTH = 4
POOL_GROUPS = 4
DT_PAD = 512
N_BRANCHES = 3
VMEM_LIMIT = 56 * 1024 * 1024

ADAM_LR, ADAM_B1, ADAM_B2, ADAM_EPS, ADAM_WD, ADAM_STEP = 0.001, 0.9, 0.999, 1e-08, 0.01, 10
MESH = pl.DeviceIdType.MESH


def _cp(**kw):
    return pltpu.CompilerParams(vmem_limit_bytes=VMEM_LIMIT, **kw)


def _pick(n, prefs):
    for p in prefs:
        if n % p == 0:
            return p
    return n


def _dg(a, b, ca, cb):
    return lax.dot_general(a, b, (((ca,), (cb,)), ((), ())), preferred_element_type=f32)


def _nn(a, b):
    return _dg(a, b, 1, 0)


def _nt(a, b):
    return _dg(a, b, 1, 1)


def _tn(a, b):
    return _dg(a, b, 0, 0)


def _mx(a):
    return a.astype(MXU_DTYPE)


def _split3(a):
    hi = a.astype(bf16)
    r = a - hi.astype(f32)
    mid = r.astype(bf16)
    lo = (r - mid.astype(f32)).astype(bf16)
    return hi, mid, lo


def _exact_nn(a, u):
    hi, mid, lo = _split3(a)
    return _nn(hi, u) + _nn(mid, u) + _nn(lo, u)


def _iota2(shape, dim):
    return lax.broadcasted_iota(jnp.int32, shape, dim)


def _sigmoid(x):
    return 1.0 / (1.0 + jnp.exp(-x))


def _shift_down(v, sh, row):
    return jnp.where(row >= sh, pltpu.roll(v, sh, 0), 0.0)


def _shift_up(v, sh, row):
    n = v.shape[0]
    return jnp.where(row < n - sh, pltpu.roll(v, n - sh, 0), 0.0)


class Dims:
    def __init__(self, bl, s, d):
        self.Bl, self.S, self.D = bl, s, d
        self.T = bl * s
        self.W2 = 2 * d
        self.H = self.W2 // HEAD
        self.hpg = self.H // SSM_GROUPS
        self.CC = self.W2 + 2 * SSM_GROUPS * LANES
        self.Dg = d // POOL_GROUPS
        self.nc = s // LANES
        self.o_dt = self.W2 + self.CC
        self.IN_COLS = 13 * d + 2 * SSM_GROUPS * LANES + self.H
        self.c_z = 0
        self.c_xbc = self.W2
        self.c_pu = self.W2 + self.CC
        self.c_pg = self.c_pu + d
        self.c_qkv = self.c_pg + d
        self.c_sbg = self.c_qkv + 3 * d
        self.c_mrg = self.c_sbg + d
        self.c_dt = self.c_mrg + 3 * d
        self.NP = self.c_dt + DT_PAD
        assert self.c_dt == self.IN_COLS - self.H
        assert s % LANES == 0 and d % 512 == 0 and self.H <= LANES


def _permute_cols(w, dm):
    pad = jnp.zeros(w.shape[:-1] + (DT_PAD - dm.H,), w.dtype)
    return jnp.concatenate([w[..., :dm.o_dt], w[..., dm.o_dt + dm.H:], w[..., dm.o_dt:dm.o_dt + dm.H], pad], axis=-1)


def _unpermute_cols(w, dm):
    return jnp.concatenate([w[..., :dm.o_dt], w[..., dm.c_dt:dm.c_dt + dm.H], w[..., dm.o_dt:dm.c_dt]], axis=-1)


def _mm(a, b, *, ta=False, tb=False, out_dtype=f32, res=None, name):
    M, K = (a.shape[1], a.shape[0]) if ta else a.shape
    N = b.shape[0] if tb else b.shape[1]
    tm = _pick(M, (1024, 512, 256, 128))
    tn = _pick(N, (2048, 1024, 512, 256, 128))
    tk = _pick(K, (1024, 512, 256, 128))
    nk = K // tk

    def body(*refs):
        if res is None:
            a_ref, b_ref, o_ref, acc = refs
        else:
            a_ref, b_ref, r_ref, o_ref, acc = refs
        k = pl.program_id(2)

        @pl.when(k == 0)
        def _():
            acc[...] = jnp.zeros_like(acc)

        acc[...] += _dg(_mx(a_ref[...]), _mx(b_ref[...]), 0 if ta else 1, 1 if tb else 0)

        @pl.when(k == nk - 1)
        def _():
            v = acc[...]
            if res is not None:
                v = v + r_ref[...]
            o_ref[...] = v.astype(out_dtype)

    a_spec = pl.BlockSpec((tk, tm), lambda i, j, k: (k, i)) if ta else pl.BlockSpec((tm, tk), lambda i, j, k: (i, k))
    b_spec = pl.BlockSpec((tn, tk), lambda i, j, k: (j, k)) if tb else pl.BlockSpec((tk, tn), lambda i, j, k: (k, j))
    o_spec = pl.BlockSpec((tm, tn), lambda i, j, k: (i, j))
    in_specs = [a_spec, b_spec] + ([o_spec] if res is not None else [])
    args = (a, b) + ((res,) if res is not None else ())
    return pl.pallas_call(
        body, name=name, grid=(M // tm, N // tn, nk), in_specs=in_specs, out_specs=o_spec,
        out_shape=jax.ShapeDtypeStruct((M, N), out_dtype), scratch_shapes=[pltpu.VMEM((tm, tn), f32)],
        compiler_params=_cp(dimension_semantics=("parallel", "parallel", "arbitrary")),
    )(*args)


def _rms_fwd(x, w, dm):
    tr = _pick(dm.T, (256, 128))

    def body(x_ref, w_ref, o_ref):
        xf = x_ref[...]
        r = lax.rsqrt(jnp.mean(xf * xf, axis=-1, keepdims=True) + EPS)
        o_ref[...] = (xf * r * w_ref[...]).astype(o_ref.dtype)

    return pl.pallas_call(
        body, name="rms_fwd", grid=(dm.T // tr,),
        in_specs=[pl.BlockSpec((tr, dm.D), lambda i: (i, 0)), pl.BlockSpec((1, dm.D), lambda i: (0, 0))],
        out_specs=pl.BlockSpec((tr, dm.D), lambda i: (i, 0)),
        out_shape=jax.ShapeDtypeStruct((dm.T, dm.D), ACT_DTYPE), compiler_params=_cp(),
    )(x, w.reshape(1, dm.D))


def _rms_bwd(x, dh, dres, w, dm):
    tr = _pick(dm.T, (256, 128))

    def body(x_ref, dh_ref, dr_ref, w_ref, dx_ref, dw_ref):
        @pl.when(pl.program_id(0) == 0)
        def _():
            dw_ref[...] = jnp.zeros_like(dw_ref)

        xf = x_ref[...]
        r = lax.rsqrt(jnp.mean(xf * xf, axis=-1, keepdims=True) + EPS)
        xh = xf * r
        dh_ = dh_ref[...]
        dxh = dh_ * w_ref[...]
        dx_ref[...] = dr_ref[...] + r * (dxh - xh * jnp.mean(dxh * xh, axis=-1, keepdims=True))
        dw_ref[...] += jnp.sum(dh_ * xh, axis=0, keepdims=True)

    row = pl.BlockSpec((tr, dm.D), lambda i: (i, 0))
    vec = pl.BlockSpec((1, dm.D), lambda i: (0, 0))
    return pl.pallas_call(
        body, name="rms_bwd", grid=(dm.T // tr,), in_specs=[row, row, row, vec], out_specs=[row, vec],
        out_shape=[jax.ShapeDtypeStruct((dm.T, dm.D), f32), jax.ShapeDtypeStruct((1, dm.D), f32)],
        compiler_params=_cp(dimension_semantics=("arbitrary",)),
    )(x, dh, dres, w.reshape(1, dm.D))


def _loss_head(x, tgt, w, dm):
    tr = _pick(dm.T, (256, 128))

    def body(x_ref, t_ref, w_ref, dx_ref, dw_ref, ls_ref):
        @pl.when(pl.program_id(0) == 0)
        def _():
            dw_ref[...] = jnp.zeros_like(dw_ref)
            ls_ref[...] = jnp.zeros_like(ls_ref)

        xf = x_ref[...]
        r = lax.rsqrt(jnp.mean(xf * xf, axis=-1, keepdims=True) + EPS)
        xh = xf * r
        err = xh * w_ref[...] - t_ref[...]
        per_tok = jnp.mean(err * err, axis=-1, keepdims=True)
        ls_ref[...] += 0.5 * jnp.sum(per_tok, axis=0, keepdims=True)
        dy = err * (1.0 / dm.D)
        dxh = dy * w_ref[...]
        dx_ref[...] = r * (dxh - xh * jnp.mean(dxh * xh, axis=-1, keepdims=True))
        dw_ref[...] += jnp.sum(dy * xh, axis=0, keepdims=True)

    row = pl.BlockSpec((tr, dm.D), lambda i: (i, 0))
    vec = pl.BlockSpec((1, dm.D), lambda i: (0, 0))
    return pl.pallas_call(
        body, name="loss_head", grid=(dm.T // tr,), in_specs=[row, row, vec],
        out_specs=[row, vec, pl.BlockSpec((1, LANES), lambda i: (0, 0))],
        out_shape=[jax.ShapeDtypeStruct((dm.T, dm.D), f32), jax.ShapeDtypeStruct((1, dm.D), f32),
                   jax.ShapeDtypeStruct((1, LANES), f32)],
        compiler_params=_cp(dimension_semantics=("arbitrary",)),
    )(x, tgt, w.reshape(1, dm.D))


def _conv_pre(u, w_ref, b_ref, row):
    acc = b_ref[...] + w_ref[CONV_WIDTH - 1:CONV_WIDTH, :] * u
    for k in range(CONV_WIDTH - 1):
        acc = acc + w_ref[k:k + 1, :] * _shift_down(u, CONV_WIDTH - 1 - k, row)
    return acc


def _conv_fwd(proj, cw, cb, dm):
    cwid = LANES
    off = dm.c_xbc // cwid

    def body(u_ref, w_ref, b_ref, o_ref):
        u = u_ref[...]
        row = _iota2(u.shape, 0)
        pre = _conv_pre(u, w_ref, b_ref, row)
        o_ref[...] = pre * _sigmoid(pre)

    return pl.pallas_call(
        body, name="conv_fwd", grid=(dm.Bl, dm.CC // cwid),
        in_specs=[pl.BlockSpec((dm.S, cwid), lambda b, j: (b, off + j)),
                  pl.BlockSpec((CONV_WIDTH, cwid), lambda b, j: (0, j)), pl.BlockSpec((1, cwid), lambda b, j: (0, j))],
        out_specs=pl.BlockSpec((dm.S, cwid), lambda b, j: (b, j)),
        out_shape=jax.ShapeDtypeStruct((dm.T, dm.CC), f32), compiler_params=_cp(),
    )(proj, cw, cb.reshape(1, dm.CC))


def _conv_bwd(proj, d_out, cw, cb, dm):
    cwid = LANES
    off = dm.c_xbc // cwid

    def body(u_ref, d_ref, w_ref, b_ref, du_ref, dw_ref, db_ref):
        @pl.when(pl.program_id(1) == 0)
        def _():
            dw_ref[...] = jnp.zeros_like(dw_ref)
            db_ref[...] = jnp.zeros_like(db_ref)

        u = u_ref[...]
        row = _iota2(u.shape, 0)
        pre = _conv_pre(u, w_ref, b_ref, row)
        sg = _sigmoid(pre)
        dpre = d_ref[...] * (sg * (1.0 + pre * (1.0 - sg)))
        du = w_ref[CONV_WIDTH - 1:CONV_WIDTH, :] * dpre
        dw_ref[CONV_WIDTH - 1:CONV_WIDTH, :] += jnp.sum(dpre * u, axis=0, keepdims=True)
        for k in range(CONV_WIDTH - 1):
            sh = CONV_WIDTH - 1 - k
            du = du + w_ref[k:k + 1, :] * _shift_up(dpre, sh, row)
            dw_ref[k:k + 1, :] += jnp.sum(dpre * _shift_down(u, sh, row), axis=0, keepdims=True)
        du_ref[...] = du.astype(du_ref.dtype)
        db_ref[...] += jnp.sum(dpre, axis=0, keepdims=True)

    return pl.pallas_call(
        body, name="conv_bwd", grid=(dm.CC // cwid, dm.Bl),
        in_specs=[pl.BlockSpec((dm.S, cwid), lambda j, b: (b, off + j)), pl.BlockSpec((dm.S, cwid), lambda j, b: (b, j)),
                  pl.BlockSpec((CONV_WIDTH, cwid), lambda j, b: (0, j)), pl.BlockSpec((1, cwid), lambda j, b: (0, j))],
        out_specs=[pl.BlockSpec((dm.S, cwid), lambda j, b: (b, j)), pl.BlockSpec((CONV_WIDTH, cwid), lambda j, b: (0, j)),
                   pl.BlockSpec((1, cwid), lambda j, b: (0, j))],
        out_shape=[jax.ShapeDtypeStruct((dm.T, dm.CC), ACT_DTYPE), jax.ShapeDtypeStruct((CONV_WIDTH, dm.CC), f32),
                   jax.ShapeDtypeStruct((1, dm.CC), f32)],
        compiler_params=_cp(dimension_semantics=("arbitrary", "arbitrary")),
    )(proj, d_out, cw, cb.reshape(1, dm.CC))


def _pad_lanes(v):
    return jnp.pad(v, (0, LANES - v.shape[0])).reshape(1, LANES)


def _softplus(x):
    return jnp.maximum(x, 0.0) + jnp.log(1.0 + jnp.exp(-jnp.abs(x)))


def _dt_prep(proj, dt_bias, a_log, dm):
    off = dm.c_dt // LANES

    def body(r_ref, b_ref, al_ref, dt_ref, cum_ref, cumt_ref):
        dt = _softplus(r_ref[...] + b_ref[...])
        adt = dt * (-jnp.exp(al_ref[...]))
        tril = (_iota2((LANES, LANES), 1) <= _iota2((LANES, LANES), 0)).astype(bf16)
        cum = _exact_nn_left(tril, adt)
        dt_ref[...] = dt
        cum_ref[...] = cum
        cumt_ref[...] = cum.T

    blk = pl.BlockSpec((LANES, LANES), lambda i: (i, 0))
    vec = pl.BlockSpec((1, LANES), lambda i: (0, 0))
    return pl.pallas_call(
        body, name="dt_prep", grid=(dm.T // LANES,),
        in_specs=[pl.BlockSpec((LANES, LANES), lambda i: (i, off)), vec, vec],
        out_specs=[blk, blk, pl.BlockSpec((LANES, LANES), lambda i: (0, i))],
        out_shape=[jax.ShapeDtypeStruct((dm.T, LANES), f32), jax.ShapeDtypeStruct((dm.T, LANES), f32),
                   jax.ShapeDtypeStruct((LANES, dm.T), f32)],
        compiler_params=_cp(),
    )(proj, _pad_lanes(dt_bias), _pad_lanes(a_log))


def _exact_nn_left(u, a):
    hi, mid, lo = _split3(a)
    return _nn(u, hi) + _nn(u, mid) + _nn(u, lo)


def _dt_bwd(proj, dt, dcum, dcum_t, ddt, dt_bias, a_log, dm):
    off = dm.c_dt // LANES

    def body(r_ref, dt_ref, dc_ref, dct_ref, dd_ref, b_ref, al_ref, o_ref, db_ref, da_ref):
        @pl.when(pl.program_id(0) == 0)
        def _():
            db_ref[...] = jnp.zeros_like(db_ref)
            da_ref[...] = jnp.zeros_like(da_ref)

        a = -jnp.exp(al_ref[...])
        triu = (_iota2((LANES, LANES), 1) >= _iota2((LANES, LANES), 0)).astype(bf16)
        dadt = _exact_nn_left(triu, dc_ref[...] + dct_ref[...].T)
        d_dt = dd_ref[...] + dadt * a
        d_raw = d_dt * _sigmoid(r_ref[...] + b_ref[...])
        o_ref[...] = d_raw.astype(o_ref.dtype)
        db_ref[...] += jnp.sum(d_raw, axis=0, keepdims=True)
        da_ref[...] += jnp.sum(dadt * dt_ref[...], axis=0, keepdims=True) * a

    blk = pl.BlockSpec((LANES, LANES), lambda i: (i, 0))
    vec = pl.BlockSpec((1, LANES), lambda i: (0, 0))
    return pl.pallas_call(
        body, name="dt_bwd", grid=(dm.T // LANES,),
        in_specs=[pl.BlockSpec((LANES, LANES), lambda i: (i, off)), blk, blk, pl.BlockSpec((LANES, LANES), lambda i: (0, i)),
                  blk, vec, vec],
        out_specs=[blk, vec, vec],
        out_shape=[jax.ShapeDtypeStruct((dm.T, LANES), ACT_DTYPE), jax.ShapeDtypeStruct((1, LANES), f32),
                   jax.ShapeDtypeStruct((1, LANES), f32)],
        compiler_params=_cp(dimension_semantics=("arbitrary",)),
    )(proj, dt, dcum, dcum_t, ddt, _pad_lanes(dt_bias), _pad_lanes(a_log))


def _ssd_common(dm):
    L = LANES
    tri = _iota2((L, L), 0) >= _iota2((L, L), 1)
    lo = _iota2((L, L), 1) < HEAD
    return tri, lo


def _ssd_fwd(xbc, dt, cum, cumt, d_skip, dm):
    L, W2, hpg = LANES, dm.W2, dm.hpg
    nb = W2 // L

    def body(x_ref, b_ref, c_ref, dt_ref, cum_ref, cumt_ref, dsk_ref, y_ref, hs_ref, h_scr):
        @pl.when(pl.program_id(1) == 0)
        def _():
            h_scr[...] = jnp.zeros_like(h_scr)

        hs_ref[0] = h_scr[...]
        tri, lo = _ssd_common(dm)
        for g in range(SSM_GROUPS):
            bb = _mx(b_ref[:, g * L:(g + 1) * L])
            cb_ = _mx(c_ref[:, g * L:(g + 1) * L])
            cbm = _nt(cb_, bb)
            for i in range(hpg // 2):
                h0 = g * hpg + 2 * i
                h1 = h0 + 1
                sl = slice(h0 * HEAD, h0 * HEAD + L)
                x_p = x_ref[:, sl]
                cum0, cum1 = cum_ref[:, h0:h0 + 1], cum_ref[:, h1:h1 + 1]
                cums = jnp.where(lo, cum0, cum1)
                xdt = x_p * jnp.where(lo, dt_ref[:, h0:h0 + 1], dt_ref[:, h1:h1 + 1])
                tot = jnp.where(lo[0:1], cum_ref[L - 1:L, h0:h0 + 1], cum_ref[L - 1:L, h1:h1 + 1])
                y_p = jnp.zeros((L, L), f32)
                for hh, m in ((h0, lo), (h1, jnp.logical_not(lo))):
                    diff = cum_ref[:, hh:hh + 1] - cumt_ref[hh:hh + 1, :]
                    lm = jnp.where(tri, jnp.exp(jnp.minimum(diff, 0.0)), 0.0)
                    y_p = y_p + _nn(_mx(cbm * lm), _mx(jnp.where(m, xdt, 0.0)))
                hp = h_scr[:, sl]
                y_p = y_p + _nn(cb_, _mx(hp)) * jnp.exp(cums)
                y_p = y_p + x_p * jnp.where(lo[0:1], dsk_ref[0:1, h0:h0 + 1], dsk_ref[0:1, h1:h1 + 1])
                y_ref[:, sl] = y_p
                h_scr[:, sl] = hp * jnp.exp(tot) + _tn(bb, _mx(xdt * jnp.exp(tot - cums)))

    nc = dm.nc
    ob = W2 // (SSM_GROUPS * L)
    blk = pl.BlockSpec((L, L), lambda b, c: (b * nc + c, 0))
    return pl.pallas_call(
        body, name="ssd_fwd", grid=(dm.Bl, nc),
        in_specs=[pl.BlockSpec((L, W2), lambda b, c: (b * nc + c, 0)),
                  pl.BlockSpec((L, SSM_GROUPS * L), lambda b, c: (b * nc + c, ob)),
                  pl.BlockSpec((L, SSM_GROUPS * L), lambda b, c: (b * nc + c, ob + 1)),
                  blk, blk, pl.BlockSpec((L, L), lambda b, c: (0, b * nc + c)), pl.BlockSpec((1, L), lambda b, c: (0, 0))],
        out_specs=[pl.BlockSpec((L, W2), lambda b, c: (b * nc + c, 0)), pl.BlockSpec((1, L, W2), lambda b, c: (b * nc + c, 0, 0))],
        out_shape=[jax.ShapeDtypeStruct((dm.T, W2), f32), jax.ShapeDtypeStruct((dm.Bl * nc, L, W2), f32)],
        scratch_shapes=[pltpu.VMEM((L, W2), f32)],
        compiler_params=_cp(dimension_semantics=("arbitrary", "arbitrary")),
    )(xbc, xbc, xbc, dt, cum, cumt, _pad_lanes(d_skip))


def _ssd_bwd(xbc, dt, cum, cumt, d_skip, hs, dy, dm):
    L, W2, hpg = LANES, dm.W2, dm.hpg
    nc = dm.nc

    def body(x_ref, b_ref, c_ref, dt_ref, cum_ref, cumt_ref, dsk_ref, hs_ref, dy_ref,
             dx_ref, db_ref, dc_ref, ddt_ref, dcum_ref, dcr_ref, dd_ref, dh_scr, lane_cum, lane_dt, lane_d):
        @pl.when(pl.program_id(1) == 0)
        def _():
            dh_scr[...] = jnp.zeros_like(dh_scr)

        @pl.when((pl.program_id(0) == 0) & (pl.program_id(1) == 0))
        def _():
            dd_ref[...] = jnp.zeros_like(dd_ref)

        dcr_ref[...] = jnp.zeros_like(dcr_ref)
        tri, lo = _ssd_common(dm)
        last = _iota2((L, L), 0) == L - 1
        for g in range(SSM_GROUPS):
            gs = slice(g * L, (g + 1) * L)
            bb = _mx(b_ref[:, gs])
            cb_ = _mx(c_ref[:, gs])
            cbm = _nt(cb_, bb)
            dcb = jnp.zeros((L, L), f32)
            dc_g = jnp.zeros((L, L), f32)
            db_g = jnp.zeros((L, L), f32)
            for i in range(hpg // 2):
                h0 = g * hpg + 2 * i
                h1 = h0 + 1
                sl = slice(h0 * HEAD, h0 * HEAD + L)
                x_p = x_ref[:, sl]
                dy_p = dy_ref[:, sl]
                dt_p = jnp.where(lo, dt_ref[:, h0:h0 + 1], dt_ref[:, h1:h1 + 1])
                cums = jnp.where(lo, cum_ref[:, h0:h0 + 1], cum_ref[:, h1:h1 + 1])
                tot = jnp.where(lo[0:1], cum_ref[L - 1:L, h0:h0 + 1], cum_ref[L - 1:L, h1:h1 + 1])
                dsk_p = jnp.where(lo[0:1], dsk_ref[0:1, h0:h0 + 1], dsk_ref[0:1, h1:h1 + 1])
                xdt = x_p * dt_p
                ecum = jnp.exp(cums)
                dec = jnp.exp(tot - cums)
                etot = jnp.exp(tot)
                hp = hs_ref[0, :, sl]
                hp_b = _mx(hp)
                dhn = dh_scr[:, sl]
                dhn_b = _mx(dhn)
                y_off = _nn(cb_, hp_b) * ecum
                dch = _mx(dy_p * ecum)
                dc_g = dc_g + _nt(dch, hp_b)
                dh_off = _tn(cb_, dch)
                bds = _nn(bb, dhn_b)
                xdec = xdt * dec
                db_g = db_g + _nt(_mx(xdec), dhn_b)
                dxdt = bds * dec
                sdec = bds * xdec
                tot_lane = jnp.sum(sdec, axis=0, keepdims=True) + jnp.sum(dhn * hp, axis=0, keepdims=True) * etot
                dh_scr[:, sl] = etot * dhn + dh_off
                rsum = []
                for hh, m in ((h0, lo), (h1, jnp.logical_not(lo))):
                    diff = cum_ref[:, hh:hh + 1] - cumt_ref[hh:hh + 1, :]
                    lm = jnp.where(tri, jnp.exp(jnp.minimum(diff, 0.0)), 0.0)
                    w32 = cbm * lm
                    dyh = _mx(jnp.where(m, dy_p, 0.0))
                    dw = _nt(dyh, _mx(jnp.where(m, xdt, 0.0)))
                    dcb = dcb + dw * lm
                    e = dw * w32
                    rsum.append(jnp.sum(e, axis=1, keepdims=True))
                    dcr_ref[hh:hh + 1, :] = -jnp.sum(e, axis=0, keepdims=True)
                    dxdt = dxdt + _tn(_mx(w32), dyh)
                lane_cum[:, sl] = (dy_p * y_off + jnp.where(lo, rsum[0], rsum[1]) * (1.0 / HEAD) - sdec
                                   + jnp.where(last, tot_lane, 0.0))
                lane_dt[:, sl] = dxdt * x_p
                lane_d[:, sl] = dy_p * x_p
                dx_ref[:, sl] = dxdt * dt_p + dsk_p * dy_p
            dcb_b = _mx(dcb)
            dc_ref[:, gs] = dc_g + _nn(dcb_b, bb)
            db_ref[:, gs] = db_g + _tn(dcb_b, cb_)
        sel = (_iota2((W2, L), 0) // HEAD == _iota2((W2, L), 1)).astype(bf16)
        dcum_ref[...] = _exact_nn(lane_cum[...], sel)
        ddt_ref[...] = _exact_nn(lane_dt[...], sel)
        dd_ref[...] += jnp.sum(_exact_nn(lane_d[...], sel), axis=0, keepdims=True)

    ob = W2 // (SSM_GROUPS * L)

    def rc(b, c):
        return b * nc + (nc - 1 - c)

    blk = pl.BlockSpec((L, L), lambda b, c: (rc(b, c), 0))
    blk_t = pl.BlockSpec((L, L), lambda b, c: (0, rc(b, c)))
    wide = pl.BlockSpec((L, W2), lambda b, c: (rc(b, c), 0))
    grp = pl.BlockSpec((L, SSM_GROUPS * L), lambda b, c: (rc(b, c), 0))
    return pl.pallas_call(
        body, name="ssd_bwd", grid=(dm.Bl, nc),
        in_specs=[wide, pl.BlockSpec((L, SSM_GROUPS * L), lambda b, c: (rc(b, c), ob)),
                  pl.BlockSpec((L, SSM_GROUPS * L), lambda b, c: (rc(b, c), ob + 1)),
                  blk, blk, blk_t, pl.BlockSpec((1, L), lambda b, c: (0, 0)),
                  pl.BlockSpec((1, L, W2), lambda b, c: (rc(b, c), 0, 0)), wide],
        out_specs=[wide, grp, grp, blk, blk, blk_t, pl.BlockSpec((1, L), lambda b, c: (0, 0))],
        out_shape=[jax.ShapeDtypeStruct((dm.T, W2), f32), jax.ShapeDtypeStruct((dm.T, SSM_GROUPS * L), f32),
                   jax.ShapeDtypeStruct((dm.T, SSM_GROUPS * L), f32), jax.ShapeDtypeStruct((dm.T, L), f32),
                   jax.ShapeDtypeStruct((dm.T, L), f32), jax.ShapeDtypeStruct((L, dm.T), f32),
                   jax.ShapeDtypeStruct((1, L), f32)],
        scratch_shapes=[pltpu.VMEM((L, W2), f32)] * 4,
        compiler_params=_cp(dimension_semantics=("arbitrary", "arbitrary")),
    )(xbc, xbc, xbc, dt, cum, cumt, _pad_lanes(d_skip), hs, dy)


def _gnorm_fwd(y, proj, w, dm):
    tr = _pick(dm.T, (256, 128))
    row = pl.BlockSpec((tr, dm.W2), lambda i: (i, 0))

    def body(y_ref, z_ref, w_ref, o_ref):
        z = z_ref[...]
        yg = y_ref[...] * (z * _sigmoid(z))
        r = lax.rsqrt(jnp.mean(yg * yg, axis=-1, keepdims=True) + EPS)
        o_ref[...] = (yg * r * w_ref[...]).astype(o_ref.dtype)

    return pl.pallas_call(
        body, name="gnorm_fwd", grid=(dm.T // tr,), in_specs=[row, row, pl.BlockSpec((1, dm.W2), lambda i: (0, 0))],
        out_specs=row, out_shape=jax.ShapeDtypeStruct((dm.T, dm.W2), ACT_DTYPE), compiler_params=_cp(),
    )(y, proj, w.reshape(1, dm.W2))


def _gnorm_bwd(dn, y, proj, w, dm):
    tr = _pick(dm.T, (256, 128))
    row = pl.BlockSpec((tr, dm.W2), lambda i: (i, 0))
    vec = pl.BlockSpec((1, dm.W2), lambda i: (0, 0))

    def body(dn_ref, y_ref, z_ref, w_ref, dy_ref, dz_ref, dw_ref):
        @pl.when(pl.program_id(0) == 0)
        def _():
            dw_ref[...] = jnp.zeros_like(dw_ref)

        z = z_ref[...]
        sg = _sigmoid(z)
        sz = z * sg
        yv = y_ref[...]
        yg = yv * sz
        r = lax.rsqrt(jnp.mean(yg * yg, axis=-1, keepdims=True) + EPS)
        n = yg * r
        dout = dn_ref[...]
        dnn = dout * w_ref[...]
        dyg = r * (dnn - n * jnp.mean(dnn * n, axis=-1, keepdims=True))
        dy_ref[...] = dyg * sz
        dz_ref[...] = (dyg * yv * (sg * (1.0 + z * (1.0 - sg)))).astype(dz_ref.dtype)
        dw_ref[...] += jnp.sum(dout * n, axis=0, keepdims=True)

    return pl.pallas_call(
        body, name="gnorm_bwd", grid=(dm.T // tr,), in_specs=[row, row, row, vec], out_specs=[row, row, vec],
        out_shape=[jax.ShapeDtypeStruct((dm.T, dm.W2), f32), jax.ShapeDtypeStruct((dm.T, dm.W2), ACT_DTYPE),
                   jax.ShapeDtypeStruct((1, dm.W2), f32)],
        compiler_params=_cp(dimension_semantics=("arbitrary",)),
    )(dn, y, proj, w.reshape(1, dm.W2))


def _pool_mixed(u, g, row):
    s = u
    for k in range(POOL_GROUPS):
        s = jnp.where(k <= g, s + _shift_down(s, 1 << k, row), s)
    w = jnp.left_shift(2, g)
    cnt = jnp.minimum(row + 1, w).astype(f32)
    return s / cnt - u, cnt


def _pool_fwd(proj, pw, scale, dm):
    Dg = dm.Dg
    ou, og = dm.c_pu // Dg, dm.c_pg // Dg

    def body(u_ref, g_ref, w_ref, s_ref, o_ref):
        u = u_ref[...]
        row = _iota2(u.shape, 0)
        mixed, _ = _pool_mixed(u, pl.program_id(1), row)
        lin = _nn(_mx(mixed), _mx(w_ref[0]))
        gt = g_ref[...]
        o_ref[...] = (lin * s_ref[...] * (gt * _sigmoid(gt))).astype(o_ref.dtype)

    return pl.pallas_call(
        body, name="pool_fwd", grid=(dm.Bl, POOL_GROUPS),
        in_specs=[pl.BlockSpec((dm.S, Dg), lambda b, g: (b, ou + g)), pl.BlockSpec((dm.S, Dg), lambda b, g: (b, og + g)),
                  pl.BlockSpec((1, Dg, Dg), lambda b, g: (g, 0, 0)), pl.BlockSpec((1, Dg), lambda b, g: (0, g))],
        out_specs=pl.BlockSpec((dm.S, Dg), lambda b, g: (b, g)),
        out_shape=jax.ShapeDtypeStruct((dm.T, dm.D), ACT_DTYPE), compiler_params=_cp(),
    )(proj, proj, pw, scale.reshape(1, dm.D))


def _pool_bwd(proj, dout, pw, scale, dm):
    Dg = dm.Dg
    ou, og = dm.c_pu // Dg, dm.c_pg // Dg

    def body(u_ref, g_ref, d_ref, w_ref, s_ref, du_ref, dg_ref, dw_ref, ds_ref):
        @pl.when(pl.program_id(1) == 0)
        def _():
            dw_ref[...] = jnp.zeros_like(dw_ref)
            ds_ref[...] = jnp.zeros_like(ds_ref)

        g = pl.program_id(0)
        u = u_ref[...]
        row = _iota2(u.shape, 0)
        mixed, cnt = _pool_mixed(u, g, row)
        mixed_b = _mx(mixed)
        wb = _mx(w_ref[0])
        lin = _nn(mixed_b, wb)
        gt = g_ref[...]
        sg = _sigmoid(gt)
        silu = gt * sg
        d = d_ref[...]
        sc = s_ref[...]
        dlin = d * sc * silu
        ds_ref[...] += jnp.sum(d * lin * silu, axis=0, keepdims=True)
        dg_ref[...] = (d * lin * sc * (sg * (1.0 + gt * (1.0 - sg)))).astype(dg_ref.dtype)
        dlin_b = _mx(dlin)
        dmixed = _nt(dlin_b, wb)
        dw_ref[0] += _tn(mixed_b, dlin_b)
        r = dmixed / cnt
        for k in range(POOL_GROUPS):
            r = jnp.where(k <= g, r + _shift_up(r, 1 << k, row), r)
        du_ref[...] = (r - dmixed).astype(du_ref.dtype)

    return pl.pallas_call(
        body, name="pool_bwd", grid=(POOL_GROUPS, dm.Bl),
        in_specs=[pl.BlockSpec((dm.S, Dg), lambda g, b: (b, ou + g)), pl.BlockSpec((dm.S, Dg), lambda g, b: (b, og + g)),
                  pl.BlockSpec((dm.S, Dg), lambda g, b: (b, g)), pl.BlockSpec((1, Dg, Dg), lambda g, b: (g, 0, 0)),
                  pl.BlockSpec((1, Dg), lambda g, b: (0, g))],
        out_specs=[pl.BlockSpec((dm.S, Dg), lambda g, b: (b, g)), pl.BlockSpec((dm.S, Dg), lambda g, b: (b, g)),
                   pl.BlockSpec((1, Dg, Dg), lambda g, b: (g, 0, 0)), pl.BlockSpec((1, Dg), lambda g, b: (0, g))],
        out_shape=[jax.ShapeDtypeStruct((dm.T, dm.D), ACT_DTYPE), jax.ShapeDtypeStruct((dm.T, dm.D), ACT_DTYPE),
                   jax.ShapeDtypeStruct((POOL_GROUPS, Dg, Dg), f32), jax.ShapeDtypeStruct((1, dm.D), f32)],
        compiler_params=_cp(dimension_semantics=("arbitrary", "arbitrary")),
    )(proj, proj, dout, pw, scale.reshape(1, dm.D))


SB_TILE = 512


def _split2(a):
    hi = a.astype(bf16)
    return hi, (a - hi.astype(f32)).astype(bf16)


def _exact2_nn(a, u):
    hi, lo = _split2(a)
    return _nn(hi, u) + _nn(lo, u)


SB_CUM = 256


def _blocked_sums(a, u, suffix):
    nb = a.shape[1] // SB_CUM
    blocks = [a[:, i * SB_CUM:(i + 1) * SB_CUM] for i in range(nb)]
    tots = [jnp.sum(b, axis=1, keepdims=True) for b in blocks]
    out = []
    for i, b in enumerate(blocks):
        s = _exact2_nn(b, u)
        for t in (tots[i + 1:] if suffix else tots[:i]):
            s = s + t
        out.append(s)
    total = tots[0]
    for t in tots[1:]:
        total = total + t
    return (out[0] if nb == 1 else jnp.concatenate(out, axis=1)), total


def _sb_tile(qb, kb, base, u, suffix, causal):
    z = _nt(qb, kb)
    lb = jnp.minimum(z, 0.0) - jnp.log(1.0 + jnp.exp(-jnp.abs(z)))
    lm = lb - z
    if causal is not None:
        lm = jnp.where(causal, lm, 0.0)
    sums, tot = _blocked_sums(lm, u, suffix)
    att = jnp.exp(lb + base + sums) if suffix else jnp.exp(lb + base - sums)
    if causal is not None:
        att = jnp.where(causal, att, 0.0)
    return lb, att, tot


def _sb_fwd(proj, dm):
    L, S, D, TQ = LANES, dm.S, dm.D, SB_TILE
    oq, og = dm.c_qkv // L, dm.c_sbg // L
    nb = D // L
    scale = HEAD ** -0.5

    def body(q_ref, k_ref, v_ref, g_ref, o_ref, lt_ref, out_ref, k_s, va_s, vb_s):
        lo = _iota2((1, L), 1) < HEAD
        hi = jnp.logical_not(lo)
        k_s[...] = _mx(k_ref[...])
        vf = v_ref[...]
        va_s[...] = _mx(jnp.where(lo, vf, 0.0))
        vb_s[...] = _mx(jnp.where(hi, vf, 0.0))
        before = _iota2((TQ, TQ), 0) > _iota2((TQ, TQ), 1)
        ugt = (_iota2((SB_CUM, SB_CUM), 0) > _iota2((SB_CUM, SB_CUM), 1)).astype(bf16)
        v_s = (va_s, vb_s)

        def qloop(qt, _):
            rows = pl.ds(pl.multiple_of(qt * TQ, TQ), TQ)
            qs = q_ref[rows, :] * scale
            qb = [_mx(jnp.where(m, qs, 0.0)) for m in (lo, hi)]

            def tile(kt, carry, causal):
                runs, acc = carry
                krows = pl.ds(pl.multiple_of(kt * TQ, TQ), TQ)
                k_t = k_s[krows, :]
                new_runs = []
                for h in range(2):
                    _, att, tot = _sb_tile(qb[h], k_t, runs[h], ugt, True, causal)
                    acc = acc + _nn(_mx(att), v_s[h][krows, :])
                    new_runs.append(runs[h] + tot)
                return tuple(new_runs), acc

            zc = jnp.zeros((TQ, 1), f32)
            carry = tile(qt, ((zc, zc), jnp.zeros((TQ, L), f32)), before)
            runs, acc = lax.fori_loop(0, qt, lambda jj, c: tile(qt - 1 - jj, c, None), carry)
            o_ref[rows, :] = acc
            lt_ref[rows, :] = jnp.where(lo, runs[0], runs[1])
            gt = g_ref[rows, :]
            out_ref[rows, :] = (acc * (gt * _sigmoid(gt))).astype(out_ref.dtype)
            return 0

        lax.fori_loop(0, S // TQ, qloop, 0)

    def col(o):
        return pl.BlockSpec((S, L), lambda b, p: (b, o + p))

    return pl.pallas_call(
        body, name="sb_fwd", grid=(dm.Bl, nb),
        in_specs=[col(oq), col(oq + nb), col(oq + 2 * nb), col(og)], out_specs=[col(0), col(0), col(0)],
        out_shape=[jax.ShapeDtypeStruct((dm.T, D), f32), jax.ShapeDtypeStruct((dm.T, D), f32),
                   jax.ShapeDtypeStruct((dm.T, D), ACT_DTYPE)],
        scratch_shapes=[pltpu.VMEM((S, L), MXU_DTYPE)] * 3,
        compiler_params=_cp(),
    )(proj, proj, proj, proj)


def _sb_bwd(proj, o, lt, dsb, dm):
    L, S, D, TQ = LANES, dm.S, dm.D, SB_TILE
    oq, og = dm.c_qkv // L, dm.c_sbg // L
    nb = D // L
    scale = HEAD ** -0.5

    def body(q_ref, k_ref, v_ref, g_ref, o_ref, lt_ref, d_ref, dq_ref, dk_ref, dv_ref, dg_ref,
             k_s, ka_s, kb_s, v_s, dk_acc, dv_acc):
        lo = _iota2((1, L), 1) < HEAD
        hi = jnp.logical_not(lo)
        kf = k_ref[...]
        k_s[...] = _mx(kf)
        ka_s[...] = _mx(jnp.where(lo, kf, 0.0))
        kb_s[...] = _mx(jnp.where(hi, kf, 0.0))
        v_s[...] = _mx(v_ref[...])
        dk_acc[...] = jnp.zeros_like(dk_acc)
        dv_acc[...] = jnp.zeros_like(dv_acc)
        before = _iota2((TQ, TQ), 0) > _iota2((TQ, TQ), 1)
        ule = (_iota2((SB_CUM, SB_CUM), 0) <= _iota2((SB_CUM, SB_CUM), 1)).astype(bf16)
        ult = (_iota2((SB_CUM, SB_CUM), 0) < _iota2((SB_CUM, SB_CUM), 1)).astype(bf16)
        k_m = (ka_s, kb_s)

        def qloop(qt, _):
            rows = pl.ds(pl.multiple_of(qt * TQ, TQ), TQ)
            gt = g_ref[rows, :]
            sg = _sigmoid(gt)
            dsb_b = d_ref[rows, :]
            do = dsb_b * (gt * sg)
            dg_ref[rows, :] = (dsb_b * o_ref[rows, :] * (sg * (1.0 + gt * (1.0 - sg)))).astype(dg_ref.dtype)
            qs = q_ref[rows, :] * scale
            qb = [_mx(jnp.where(m, qs, 0.0)) for m in (lo, hi)]
            dob = [_mx(jnp.where(m, do, 0.0)) for m in (lo, hi)]
            ltot = [lt_ref[rows, 0:1], lt_ref[rows, HEAD:HEAD + 1]]

            def tile(kt, carry, causal):
                runs, rgs, dq = carry
                krows = pl.ds(pl.multiple_of(kt * TQ, TQ), TQ)
                k_t, v_t = k_s[krows, :], v_s[krows, :]
                dk_t = jnp.zeros((TQ, L), f32)
                dv_t = jnp.zeros((TQ, L), f32)
                new_runs, new_rgs = [], []
                for h in range(2):
                    lb, att, tot = _sb_tile(qb[h], k_t, ltot[h] - runs[h], ule, False, causal)
                    gm = att * _nt(dob[h], v_t)
                    gsum, gtot = _blocked_sums(gm, ult, False)
                    pre = rgs[h] + gsum
                    beta = jnp.exp(lb)
                    dz = gm * (1.0 - beta) - pre * beta
                    if causal is not None:
                        dz = jnp.where(causal, dz, 0.0)
                    dz = _mx(dz)
                    dq = dq + _nn(dz, k_m[h][krows, :])
                    dk_t = dk_t + _tn(dz, qb[h])
                    dv_t = dv_t + _tn(_mx(att), dob[h])
                    new_runs.append(runs[h] + tot)
                    new_rgs.append(rgs[h] + gtot)
                dk_acc[krows, :] += dk_t
                dv_acc[krows, :] += dv_t
                return tuple(new_runs), tuple(new_rgs), dq

            zc = jnp.zeros((TQ, 1), f32)
            carry = lax.fori_loop(0, qt, lambda kt, c: tile(kt, c, None), ((zc, zc), (zc, zc), jnp.zeros((TQ, L), f32)))
            _, _, dq = tile(qt, carry, before)
            dq_ref[rows, :] = (dq * scale).astype(dq_ref.dtype)
            return 0

        lax.fori_loop(0, S // TQ, qloop, 0)
        dk_ref[...] = dk_acc[...].astype(dk_ref.dtype)
        dv_ref[...] = dv_acc[...].astype(dv_ref.dtype)

    def col(off):
        return pl.BlockSpec((S, L), lambda b, p: (b, off + p))

    out = jax.ShapeDtypeStruct((dm.T, D), ACT_DTYPE)
    return pl.pallas_call(
        body, name="sb_bwd", grid=(dm.Bl, nb),
        in_specs=[col(oq), col(oq + nb), col(oq + 2 * nb), col(og), col(0), col(0), col(0)],
        out_specs=[col(0), col(0), col(0), col(0)], out_shape=[out, out, out, out],
        scratch_shapes=[pltpu.VMEM((S, L), MXU_DTYPE)] * 4 + [pltpu.VMEM((S, L), f32)] * 2,
        compiler_params=_cp(),
    )(proj, proj, proj, proj, o, lt, dsb)


def _merge_fwd(proj, ys, dm):
    tr = _pick(dm.T, (256, 128))
    ct = 512
    om = dm.c_mrg // ct
    nb = dm.D // ct
    blk = pl.BlockSpec((tr, ct), lambda i, j: (i, j))

    def body(l0, l1, l2, y0, y1, y2, o_ref):
        acc = _sigmoid(l0[...]) * y0[...] + _sigmoid(l1[...]) * y1[...] + _sigmoid(l2[...]) * y2[...]
        o_ref[...] = acc.astype(o_ref.dtype)

    return pl.pallas_call(
        body, name="merge_fwd", grid=(dm.T // tr, nb),
        in_specs=[pl.BlockSpec((tr, ct), functools.partial(lambda i, j, b: (i, om + b * nb + j), b=b)) for b in range(3)] + [blk] * 3,
        out_specs=blk, out_shape=jax.ShapeDtypeStruct((dm.T, dm.D), ACT_DTYPE), compiler_params=_cp(),
    )(proj, proj, proj, *ys)


def _merge_bwd(proj, ys, dmerged, dm):
    tr = _pick(dm.T, (256, 128))
    ct = 512
    om = dm.c_mrg // ct
    nb = dm.D // ct
    blk = pl.BlockSpec((tr, ct), lambda i, j: (i, j))

    def body(l0, l1, l2, y0, y1, y2, d_ref, dy0, dy1, dy2, dl0, dl1, dl2):
        d = d_ref[...]
        for l_ref, y_ref, dy_ref, dl_ref in ((l0, y0, dy0, dl0), (l1, y1, dy1, dl1), (l2, y2, dy2, dl2)):
            sg = _sigmoid(l_ref[...])
            dy_ref[...] = (d * sg).astype(dy_ref.dtype)
            dl_ref[...] = (d * y_ref[...] * sg * (1.0 - sg)).astype(dl_ref.dtype)

    out = jax.ShapeDtypeStruct((dm.T, dm.D), ACT_DTYPE)
    return pl.pallas_call(
        body, name="merge_bwd", grid=(dm.T // tr, nb),
        in_specs=[pl.BlockSpec((tr, ct), functools.partial(lambda i, j, b: (i, om + b * nb + j), b=b)) for b in range(3)] + [blk] * 4,
        out_specs=[blk] * 6, out_shape=[out] * 6, compiler_params=_cp(),
    )(proj, proj, proj, *ys, dmerged)


def _layer_fwd(x, p, dm):
    h = _rms_fwd(x, p["norm_w"], dm)
    proj = _mm(h, p["w_in"], name="mm_in")
    xbc = _conv_fwd(proj, p["conv_w"], p["conv_b"], dm)
    dt, cum, cumt = _dt_prep(proj, p["dt_bias"], p["a_log"], dm)
    y, hs = _ssd_fwd(xbc, dt, cum, cumt, p["d_skip"], dm)
    ssm_n = _gnorm_fwd(y, proj, p["ssm_norm_w"], dm)
    pool_o = _pool_fwd(proj, p["pool_w"], p["pool_scale"], dm)
    o, lt, sb_o = _sb_fwd(proj, dm)
    ys = (_mm(ssm_n, p["w_proj_ssm"], name="mm_ps"), _mm(pool_o, p["w_proj_pool"], name="mm_pp"),
          _mm(sb_o, p["w_proj_sb"], name="mm_pb"))
    merged = _merge_fwd(proj, ys, dm)
    x_next = _mm(merged, p["w_out"], res=x, name="mm_out")
    saved = dict(x=x, h=h, proj=proj, xbc=xbc, dt=dt, cum=cum, cumt=cumt, y=y, hs=hs, ssm_n=ssm_n, pool_o=pool_o, o=o,
                 lt=lt, sb_o=sb_o, ys=ys, merged=merged)
    return x_next, saved


def _layer_bwd(dx_out, p, sv, dm):
    g = {}
    proj = sv["proj"]
    dmerged = _mm(dx_out, p["w_out"], tb=True, name="mm_dmerged")
    g["w_out"] = _mm(sv["merged"], dx_out, ta=True, name="mm_dwout")
    dy0, dy1, dy2, dl0, dl1, dl2 = _merge_bwd(proj, sv["ys"], dmerged, dm)
    d_ssm_n = _mm(dy0, p["w_proj_ssm"], tb=True, name="mm_dssm")
    g["w_proj_ssm"] = _mm(sv["ssm_n"], dy0, ta=True, name="mm_dwps")
    d_pool_o = _mm(dy1, p["w_proj_pool"], tb=True, name="mm_dpool")
    g["w_proj_pool"] = _mm(sv["pool_o"], dy1, ta=True, name="mm_dwpp")
    d_sb_o = _mm(dy2, p["w_proj_sb"], tb=True, name="mm_dsb")
    g["w_proj_sb"] = _mm(sv["sb_o"], dy2, ta=True, name="mm_dwpb")
    dy, dz, g["ssm_norm_w"] = _gnorm_bwd(d_ssm_n, sv["y"], proj, p["ssm_norm_w"], dm)
    dxs, db, dc, ddt, dcum, dcum_t, dd = _ssd_bwd(sv["xbc"], sv["dt"], sv["cum"], sv["cumt"], p["d_skip"], sv["hs"], dy, dm)
    g["d_skip"] = dd
    d_dtraw, g["dt_bias"], g["a_log"] = _dt_bwd(proj, sv["dt"], dcum, dcum_t, ddt, p["dt_bias"], p["a_log"], dm)
    dxbc, g["conv_w"], g["conv_b"] = _conv_bwd(proj, jnp.concatenate([dxs, db, dc], axis=1), p["conv_w"], p["conv_b"], dm)
    dpu, dpg, g["pool_w"], g["pool_scale"] = _pool_bwd(proj, d_pool_o, p["pool_w"], p["pool_scale"], dm)
    dq, dk, dv, dsg = _sb_bwd(proj, sv["o"], sv["lt"], d_sb_o, dm)
    dproj = jnp.concatenate([dz, dxbc, dpu, dpg, dq, dk, dv, dsg, dl0, dl1, dl2, d_dtraw,
                             jnp.zeros((dm.T, DT_PAD - LANES), ACT_DTYPE)], axis=1)
    dh = _mm(dproj, p["w_in"], tb=True, name="mm_dh")
    g["w_in"] = _mm(sv["h"], dproj, ta=True, name="mm_dwin")
    dx, g["norm_w"] = _rms_bwd(sv["x"], dh, dx_out, p["norm_w"], dm)
    return dx, g


def _local_step(x, tgt, layers, final_norm_w, dm, on_grads=None):
    saved = []
    layers = list(layers)
    for l, p in enumerate(layers):
        if callable(p):
            p, x = p(x)
            layers[l] = p
        x, sv = _layer_fwd(x, p, dm)
        saved.append(sv)
    dx, dfn, loss = _loss_head(x, tgt, final_norm_w, dm)
    grads = [None] * len(layers)
    for l in reversed(range(len(layers))):
        dx, grads[l] = _layer_bwd(dx, layers[l], saved[l], dm)
        if on_grads is not None:
            dx = on_grads(l, grads, dx)
    return loss, dx, grads, dfn


def _row_tile(rows, cols):
    cap = max(8, (1 << 18) // cols)
    for t in (1024, 512, 256, 128, 64, 32, 16, 8):
        if t <= cap and rows % t == 0:
            return t
    return rows


def _adamw(w, g, m, v, name):
    rows, cols = w.shape
    tr = _row_tile(rows, cols)
    c1 = 1.0 - ADAM_B1 ** ADAM_STEP
    c2 = 1.0 - ADAM_B2 ** ADAM_STEP

    def body(w_ref, g_ref, m_ref, v_ref, d_ref, mo_ref, vo_ref):
        gv = g_ref[...]
        mn = ADAM_B1 * m_ref[...] + (1.0 - ADAM_B1) * gv
        vn = ADAM_B2 * v_ref[...] + (1.0 - ADAM_B2) * (gv * gv)
        d_ref[...] = -ADAM_LR * ((mn / c1) / (jnp.sqrt(vn / c2) + ADAM_EPS) + ADAM_WD * w_ref[...])
        mo_ref[...] = mn
        vo_ref[...] = vn

    blk = pl.BlockSpec((tr, cols), lambda i: (i, 0))
    out = jax.ShapeDtypeStruct((rows, cols), f32)
    return pl.pallas_call(body, name=name, grid=(rows // tr,), in_specs=[blk] * 4, out_specs=[blk] * 3, out_shape=[out] * 3,
                          compiler_params=_cp())(w, g, m, v)


def _sum_parts(a, parts, name):
    n, rows, cols = parts.shape
    tr = _row_tile(rows, cols)

    def body(a_ref, p_ref, o_ref):
        acc = a_ref[...]
        for k in range(n):
            acc = acc + p_ref[k].astype(f32)
        o_ref[...] = acc

    blk = pl.BlockSpec((tr, cols), lambda i: (i, 0))
    return pl.pallas_call(body, name=name, grid=(rows // tr,), in_specs=[blk, pl.BlockSpec((n, tr, cols), lambda i: (0, i, 0))],
                          out_specs=blk, out_shape=jax.ShapeDtypeStruct((rows, cols), f32), compiler_params=_cp())(a, parts)


ICI_KINDS = ("y", "x", "xy")
ASYNC_GATHER_ID = 1
ASYNC_IDS = {"rs_pair_late": 2, "rs_ici_late": 3, "rs_sibling_late": 4}
HBM_SPEC = pl.BlockSpec(memory_space=pltpu.HBM)


def _me():
    return lax.axis_index("x"), lax.axis_index("y"), lax.axis_index("c")


def _peer(kind):
    x, y, c = _me()
    return {"c": (x, y, 1 - c), "y": (x, 1 - y, c), "x": (1 - x, y, c), "xy": (1 - x, 1 - y, c)}[kind]


def _peer_chip(kind):
    x, y, _ = _me()
    return {"y": 2 * x + (1 - y), "x": 2 * (1 - x) + y, "xy": 2 * (1 - x) + (1 - y)}[kind]


def _exchange(sends, kinds, name):
    n, na = len(kinds), len(sends)

    def body(*refs):
        _exchange_copies(refs[:na], refs[na:2 * na], *refs[2 * na:], kinds)

    return pl.pallas_call(
        body, name=name, out_shape=[jax.ShapeDtypeStruct(a.shape, a.dtype) for a in sends], in_specs=[HBM_SPEC] * na,
        out_specs=[HBM_SPEC] * na,
        scratch_shapes=[pltpu.SemaphoreType.DMA((na * n,)), pltpu.SemaphoreType.DMA((na * n,))],
    )(*sends)


def _exchange_copies(srcs, dsts, ssem, rsem, kinds):
    n = len(kinds)
    cps = [pltpu.make_async_remote_copy(src_ref=srcs[i].at[k], dst_ref=dsts[i].at[k], send_sem=ssem.at[i * n + k],
                                        recv_sem=rsem.at[i * n + k], device_id=_peer(kind), device_id_type=MESH)
           for i in range(len(srcs)) for k, kind in enumerate(kinds)]
    for cp in cps:
        cp.start()
    for cp in cps:
        cp.wait()


def _exchange_async(sends, kinds, name):
    n, na = len(kinds), len(sends)
    srcs = [jax.new_ref(a, memory_space=pltpu.MemorySpace.HBM) for a in sends]
    dsts = [jax.empty_ref(jax.ShapeDtypeStruct(a.shape, a.dtype), memory_space=pltpu.MemorySpace.HBM) for a in sends]

    @pl.kernel(mesh=plsc.ScalarSubcoreMesh(axis_name="sequencer", num_cores=1), name=name,
               scratch_types=(pltpu.SemaphoreType.DMA((na * n,)), pltpu.SemaphoreType.DMA((na * n,))),
               compiler_params=pltpu.CompilerParams(collective_id=ASYNC_IDS[name]))
    def launch(ssem, rsem):
        barrier = pltpu.get_barrier_semaphore()
        for kind in kinds:
            pl.semaphore_signal(barrier, inc=1, device_id=_peer(kind), device_id_type=MESH)
        pl.semaphore_wait(barrier, n)
        _exchange_copies(srcs, dsts, ssem, rsem, kinds)

    launch()
    return [r[...] for r in dsts]


def _allgather_copies(srcs, outs, ssem, rsem, lsem, lh):
    na = len(srcs)
    x, y, c = _me()
    j_me = 2 * x + y
    mine = pl.ds(c * lh, lh)
    theirs = pl.ds((1 - c) * lh, lh)
    local = [pltpu.make_async_copy(srcs[i], outs[i].at[j_me], lsem.at[i]) for i in range(na)]
    for cp in local:
        cp.start()

    def ici(i, k, kind, j_src):
        return pltpu.make_async_remote_copy(src_ref=srcs[i].at[mine], dst_ref=outs[i].at[j_src, mine],
                                            send_sem=ssem.at[6 * i + k], recv_sem=rsem.at[6 * i + k],
                                            device_id=_peer(kind), device_id_type=MESH)

    def d2d(i, k, j_src, half):
        return pltpu.make_async_remote_copy(src_ref=outs[i].at[j_src, half], dst_ref=outs[i].at[j_src, half],
                                            send_sem=ssem.at[6 * i + 3 + k], recv_sem=rsem.at[6 * i + 3 + k],
                                            device_id=_peer("c"), device_id_type=MESH)

    first = [ici(i, k, kind, j_me) for i in range(na) for k, kind in enumerate(ICI_KINDS)]
    for cp in first:
        cp.start()
    passed = []
    for i in range(na):
        for k, kind in enumerate(ICI_KINDS):
            ici(i, k, kind, _peer_chip(kind)).wait_recv()
            fwd = d2d(i, k, _peer_chip(kind), mine)
            fwd.start()
            passed.append(fwd)
    for i in range(na):
        for k, kind in enumerate(ICI_KINDS):
            d2d(i, k, _peer_chip(kind), theirs).wait_recv()
    for cp in first + passed:
        cp.wait_send()
    for cp in local:
        cp.wait()


def _allgather_sems(na):
    return [pltpu.SemaphoreType.DMA((6 * na,)), pltpu.SemaphoreType.DMA((6 * na,)), pltpu.SemaphoreType.DMA((na,))]


def _allgather_shards(shards):
    na = len(shards)
    lh = shards[0].shape[0] // 2

    def body(*refs):
        _allgather_copies(refs[:na], refs[na:2 * na], *refs[2 * na:], lh)

    return pl.pallas_call(
        body, name="allgather_shards", out_shape=[jax.ShapeDtypeStruct((4,) + a.shape, a.dtype) for a in shards],
        in_specs=[HBM_SPEC] * na, out_specs=[HBM_SPEC] * na, scratch_shapes=_allgather_sems(na),
    )(*shards)


def _allgather_shards_async(shards):
    na = len(shards)
    lh = shards[0].shape[0] // 2
    srcs = [jax.new_ref(a, memory_space=pltpu.MemorySpace.HBM) for a in shards]
    outs = [jax.empty_ref(jax.ShapeDtypeStruct((4,) + a.shape, a.dtype), memory_space=pltpu.MemorySpace.HBM) for a in shards]

    @pl.kernel(mesh=plsc.ScalarSubcoreMesh(axis_name="sequencer", num_cores=1), name="allgather_shards_async",
               scratch_types=tuple(_allgather_sems(na)), compiler_params=pltpu.CompilerParams(collective_id=ASYNC_GATHER_ID))
    def launch(ssem, rsem, lsem):
        barrier = pltpu.get_barrier_semaphore()
        peers = ("c",) + ICI_KINDS
        for kind in peers:
            pl.semaphore_signal(barrier, inc=1, device_id=_peer(kind), device_id_type=MESH)
        pl.semaphore_wait(barrier, len(peers))
        _allgather_copies(srcs, outs, ssem, rsem, lsem, lh)

    launch()
    return [r[...] for r in outs]


def _allreduce_small(v):
    rows = v.shape[0]

    def body(v_ref, o_ref, buf, ssem, rsem):
        x, y, c = _me()
        me = 4 * x + 2 * y + c
        buf[0] = v_ref[...]
        cps = []
        for k in range(1, 8):
            peer = (1 - x if k & 4 else x, 1 - y if k & 2 else y, 1 - c if k & 1 else c)
            cps.append(pltpu.make_async_remote_copy(src_ref=v_ref, dst_ref=buf.at[k], send_sem=ssem.at[k - 1], recv_sem=rsem.at[k - 1],
                                                    device_id=peer, device_id_type=MESH))
        for cp in cps:
            cp.start()
        for cp in cps:
            cp.wait()
        acc = buf[jnp.bitwise_xor(me, 0)]
        for d in range(1, 8):
            acc = acc + buf[jnp.bitwise_xor(me, d)]
        o_ref[...] = acc

    vm = pl.BlockSpec(memory_space=pltpu.VMEM)
    return pl.pallas_call(
        body, name="allreduce_small", out_shape=jax.ShapeDtypeStruct(v.shape, f32), in_specs=[vm], out_specs=vm,
        scratch_shapes=[pltpu.VMEM((8, rows, LANES), f32), pltpu.SemaphoreType.DMA((7,)), pltpu.SemaphoreType.DMA((7,))],
    )(v)


SHARDED = ("w_in", "w_proj_ssm", "w_proj_pool", "w_proj_sb", "w_out", "pool_w", "conv_w")
REST = SHARDED[1:6]
SMALL = ("norm_w", "conv_b", "dt_bias", "a_log", "d_skip", "ssm_norm_w", "pool_scale")


def _pack_rows(dm):
    rows = dict(w_proj_ssm=dm.W2 // 4, w_proj_pool=dm.D // 4, w_proj_sb=dm.D // 4, w_out=dm.D // 4, pool_w=dm.D // 16)
    assert all(r % 16 == 0 for r in rows.values())
    return rows


def _pack(sh, dm, dtype):
    rows = _pack_rows(dm)
    ly = sh["w_out"].shape[0]
    return jnp.concatenate([sh[n].astype(dtype).reshape(ly, rows[n], dm.D) for n in REST], axis=1)


def _unpack(pk, dm):
    rows = _pack_rows(dm)
    lead = pk.shape[:-2]
    shapes = dict(w_proj_ssm=(dm.W2 // 4, dm.D), w_proj_pool=(dm.D // 4, dm.D), w_proj_sb=(dm.D // 4, dm.D),
                  w_out=(dm.D // 4, dm.D), pool_w=(POOL_GROUPS, dm.Dg // 4, dm.Dg))
    out, r0 = {}, 0
    for n in REST:
        out[n] = pk[..., r0:r0 + rows[n], :].reshape(lead + shapes[n])
        r0 += rows[n]
    return out


def _join_shards(sh, dm):
    full = {}
    w = jnp.moveaxis(sh["w_in"], 0, 2)
    full["w_in"] = _permute_cols(w.reshape(w.shape[0], dm.D, dm.IN_COLS), dm)
    for n in ("w_proj_ssm", "w_proj_pool", "w_proj_sb", "w_out"):
        a = jnp.moveaxis(sh[n], 0, 1)
        full[n] = a.reshape(a.shape[0], -1, dm.D)
    a = jnp.moveaxis(sh["pool_w"], 0, 2)
    full["pool_w"] = a.reshape(a.shape[0], POOL_GROUPS, dm.Dg, dm.Dg)
    a = jnp.moveaxis(sh["conv_w"], 0, 2)
    full["conv_w"] = a.reshape(a.shape[0], CONV_WIDTH, dm.CC)
    return full


def _split_shards(full, dm):
    sh = {}
    w = _unpermute_cols(full["w_in"], dm)
    sh["w_in"] = jnp.moveaxis(w.reshape(w.shape[0], dm.D, 4, dm.IN_COLS // 4), 2, 0)
    for n in ("w_proj_ssm", "w_proj_pool", "w_proj_sb", "w_out"):
        a = full[n]
        sh[n] = jnp.moveaxis(a.reshape(a.shape[0], 4, a.shape[1] // 4, dm.D), 1, 0)
    a = full["pool_w"]
    sh["pool_w"] = jnp.moveaxis(a.reshape(a.shape[0], POOL_GROUPS, 4, dm.Dg // 4, dm.Dg), 2, 0)
    a = full["conv_w"]
    sh["conv_w"] = jnp.moveaxis(a.reshape(a.shape[0], CONV_WIDTH, 4, dm.CC // 4), 2, 0)
    return sh


def _sum_like(mine, parts, name):
    cols = mine.shape[-1]
    return _sum_parts(mine.reshape(-1, cols), parts.reshape(parts.shape[0], -1, cols), name).reshape(mine.shape)


def _slab(a, j):
    return lax.dynamic_index_in_dim(a, j, 0, keepdims=False)


class _ReduceScatter:
    TAGS = ("in", "rest", "conv")

    def __init__(self, full_grads, dm, exchange, suffix=""):
        x, y, self.c = _me()
        self.j_me, self.dm, self.exchange, self.suffix = 2 * x + y, dm, exchange, suffix
        sh = _split_shards(full_grads, dm)
        by_chip = [sh["w_in"], jnp.stack([_pack({n: sh[n][j] for n in REST}, dm, f32) for j in range(4)]), sh["conv_w"]]
        lh = by_chip[0].shape[1] // 2
        self.mine = [lax.dynamic_slice_in_dim(a, self.c * lh, lh, axis=1) for a in by_chip]
        theirs = [lax.dynamic_slice_in_dim(a, (1 - self.c) * lh, lh, axis=1).astype(WIRE_DTYPE)[None] for a in by_chip]
        self.got = exchange(theirs, ("c",), "rs_pair" + suffix)

    def pair_sum(self, dep=None):
        got = self.got
        if dep is not None:
            got, dep = lax.optimization_barrier((got, dep))
        self.s1 = [_sum_like(a, g, "rs_pair_sum_" + t) for a, g, t in zip(self.mine, got, self.TAGS)]
        send = [jnp.stack([_slab(a, jnp.bitwise_xor(self.j_me, k)) for k in (1, 2, 3)]).astype(WIRE_DTYPE) for a in self.s1]
        self.got = self.exchange(send, ICI_KINDS, "rs_ici" + self.suffix)
        return dep

    def ici_sum(self, dep=None):
        got = self.got
        if dep is not None:
            got, dep = lax.optimization_barrier((got, dep))
        self.red = [_sum_like(_slab(a, self.j_me), g, "rs_ici_sum_" + t) for a, g, t in zip(self.s1, got, self.TAGS)]
        self.sib = self.exchange([r[None] for r in self.red], ("c",), "rs_sibling" + self.suffix)
        return dep

    def result(self):
        both = [jnp.where(self.c == 0, jnp.concatenate([r, s[0]]), jnp.concatenate([s[0], r])) for r, s in zip(self.red, self.sib)]
        out = _unpack(both[1], self.dm)
        out["w_in"], out["conv_w"] = both[0], both[2]
        return out


def _reduce_scatter(full_grads, dm):
    rs = _ReduceScatter(full_grads, dm, _exchange)
    rs.pair_sum()
    rs.ici_sum()
    return rs.result()


def _flatten_small(parts):
    flat = jnp.concatenate([p.reshape(-1).astype(f32) for p in parts])
    rows = -(-flat.shape[0] // (8 * LANES)) * 8
    return jnp.pad(flat, (0, rows * LANES - flat.shape[0])).reshape(rows, LANES)


def _unflatten_small(buf, shapes):
    flat = buf.reshape(-1)
    out, o = [], 0
    for s in shapes:
        n = math.prod(s)
        out.append(flat[o:o + n].reshape(s))
        o += n
    return out


def kernel(x, norm_w, w_in, conv_w, conv_b, dt_bias, a_log, d_skip, ssm_norm_w, pool_w, pool_scale, w_proj_ssm, w_proj_pool, w_proj_sb, w_out, final_norm_w, loss_target, m_norm_w, m_w_in, m_conv_w, m_conv_b, m_dt_bias, m_a_log, m_d_skip, m_ssm_norm_w, m_pool_w, m_pool_scale, m_w_proj_ssm, m_w_proj_pool, m_w_proj_sb, m_w_out, m_final_norm_w, v_norm_w, v_w_in, v_conv_w, v_conv_b, v_dt_bias, v_a_log, v_d_skip, v_ssm_norm_w, v_pool_w, v_pool_scale, v_w_proj_ssm, v_w_proj_pool, v_w_proj_sb, v_w_out, v_final_norm_w):
    names = ("norm_w", "w_in", "conv_w", "conv_b", "dt_bias", "a_log", "d_skip", "ssm_norm_w", "pool_w", "pool_scale",
             "w_proj_ssm", "w_proj_pool", "w_proj_sb", "w_out", "final_norm_w")
    w = dict(zip(names, (norm_w, w_in, conv_w, conv_b, dt_bias, a_log, d_skip, ssm_norm_w, pool_w, pool_scale, w_proj_ssm,
                         w_proj_pool, w_proj_sb, w_out, final_norm_w)))
    m = dict(zip(names, (m_norm_w, m_w_in, m_conv_w, m_conv_b, m_dt_bias, m_a_log, m_d_skip, m_ssm_norm_w, m_pool_w, m_pool_scale,
                         m_w_proj_ssm, m_w_proj_pool, m_w_proj_sb, m_w_out, m_final_norm_w)))
    v = dict(zip(names, (v_norm_w, v_w_in, v_conv_w, v_conv_b, v_dt_bias, v_a_log, v_d_skip, v_ssm_norm_w, v_pool_w, v_pool_scale,
                         v_w_proj_ssm, v_w_proj_pool, v_w_proj_sb, v_w_out, v_final_norm_w)))
    bl, s, d = x.shape
    dm = Dims(bl, s, d)
    ly = norm_w.shape[0]

    mine = [w["w_in"].astype(WIRE_DTYPE), _pack({n: w[n] for n in REST}, dm, WIRE_DTYPE), conv_w]
    if ly % 4 == 0:
        gathered = [_allgather_shards([a[:ly // 2] for a in mine])]
        later = lax.optimization_barrier(([a[ly // 2:] for a in mine], gathered[0]))[0]
        gathered.append(_allgather_shards_async(later))
    else:
        gathered = [_allgather_shards(mine)]
    def layer_dicts(part, first):
        g_in, g_rest, g_conv = part
        shards = _unpack(g_rest, dm)
        shards["w_in"], shards["conv_w"] = g_in, g_conv
        full = _join_shards(shards, dm)
        return [{**{n: full[n][l] for n in SHARDED}, **{n: w[n][first + l] for n in SMALL}} for l in range(g_in.shape[1])]

    layers = layer_dicts(gathered[0], 0)
    if len(gathered) > 1:
        late = {}

        def late_layer(x, l):
            if not late:
                part, x = lax.optimization_barrier((gathered[1], x))
                late["dicts"] = layer_dicts(part, ly // 2)
            return late["dicts"][l], x

        layers += [functools.partial(late_layer, l=l) for l in range(ly - ly // 2)]

    def stacked(grads, lo_l, hi_l):
        return {n: jnp.stack([grads[l][n] for l in range(lo_l, hi_l)]) for n in SHARDED}

    late = {}

    def on_grads(l, grads, dx):
        half = ly // 2
        if l == half:
            late["rs"] = _ReduceScatter(stacked(grads, half, ly), dm, _exchange_async, "_late")
        elif l == half - 1:
            dx = late["rs"].pair_sum(dx)
        elif l == half - 2:
            dx = late["rs"].ici_sum(dx)
        return dx

    loss_part, dx, grads, dfn = _local_step(x.reshape(dm.T, d), loss_target.reshape(dm.T, d), layers, final_norm_w, dm,
                                            on_grads if ly % 4 == 0 else None)

    h = dm.H
    small_parts = [loss_part]
    small_shapes = [(1, LANES)]
    for n in SMALL:
        if n in ("dt_bias", "a_log", "d_skip"):
            small_parts.append(jnp.stack([g[n][0, :h] for g in grads]))
        else:
            small_parts.append(jnp.stack([g[n][0] for g in grads]))
        small_shapes.append(w[n].shape)
    small_parts.append(dfn[0])
    small_shapes.append(final_norm_w.shape)
    red_small = _allreduce_small(_flatten_small(small_parts))
    small_g = _unflatten_small(red_small, small_shapes)
    loss = small_g[0][0, 0]
    g_out = dict(zip(SMALL + ("final_norm_w",), small_g[1:]))

    if late:
        early = _reduce_scatter(stacked(grads, 0, ly // 2), dm)
        rest = late["rs"].result()
        g_out.update({n: jnp.concatenate([early[n], rest[n]]) for n in SHARDED})
    else:
        g_out.update(_reduce_scatter(stacked(grads, 0, ly), dm))

    small_names = SMALL + ("final_norm_w",)
    zero_row = jnp.zeros((1, LANES), f32)
    pack_small = lambda t: _flatten_small([zero_row] + [t[n] for n in small_names])
    ds, ms, vs = _adamw(pack_small(w), red_small, pack_small(m), pack_small(v), "adamw_small")
    delta, new_m, new_v = {}, {}, {}
    for tgt, buf in ((delta, ds), (new_m, ms), (new_v, vs)):
        tgt.update(zip(small_names, _unflatten_small(buf, small_shapes)[1:]))
    for n in SHARDED:
        shp = w[n].shape
        two = (math.prod(shp[:-1]), shp[-1])
        dd, mm, vv = _adamw(w[n].reshape(two), g_out[n].reshape(two), m[n].reshape(two), v[n].reshape(two), "adamw_" + n)
        delta[n], new_m[n], new_v[n] = dd.reshape(shp), mm.reshape(shp), vv.reshape(shp)
        g_out[n] = g_out[n].reshape(shp)

    return (loss, dx.reshape(bl, s, d), *[g_out[n] for n in names], *[delta[n] for n in names],
            *[new_m[n] for n in names], *[new_v[n] for n in names])
```

```python
import functools
import math

import jax
import jax.numpy as jnp
from jax import lax
from jax.experimental import pallas as pl
from jax.experimental.pallas import tpu as pltpu
from jax.experimental.pallas import tpu_sc as plsc

f32 = jnp.float32
bf16 = jnp.bfloat16
MXU_DTYPE = jnp.bfloat16
ACT_DTYPE = jnp.bfloat16
WIRE_DTYPE = jnp.bfloat16

EPS = 1e-6
LANES = 128
HEAD = 64
SSM_GROUPS = 2
CONV_WIDTH = 4
POOL_GROUPS = 4
DT_PAD = 512
N_BRANCHES = 3
VMEM_LIMIT = 56 * 1024 * 1024

ADAM_LR, ADAM_B1, ADAM_B2, ADAM_EPS, ADAM_WD, ADAM_STEP = 0.001, 0.9, 0.999, 1e-08, 0.01, 10
MESH = pl.DeviceIdType.MESH


def _cp(**kw):
    return pltpu.CompilerParams(vmem_limit_bytes=VMEM_LIMIT, **kw)


def _pick(n, prefs):
    for p in prefs:
        if n % p == 0:
            return p
    return n


def _dg(a, b, ca, cb):
    return lax.dot_general(a, b, (((ca,), (cb,)), ((), ())), preferred_element_type=f32)


def _nn(a, b):
    return _dg(a, b, 1, 0)


def _nt(a, b):
    return _dg(a, b, 1, 1)


def _tn(a, b):
    return _dg(a, b, 0, 0)


def _mx(a):
    return a.astype(MXU_DTYPE)


def _split3(a):
    hi = a.astype(bf16)
    r = a - hi.astype(f32)
    mid = r.astype(bf16)
    lo = (r - mid.astype(f32)).astype(bf16)
    return hi, mid, lo


def _exact_nn(a, u):
    hi, mid, lo = _split3(a)
    return _nn(hi, u) + _nn(mid, u) + _nn(lo, u)


def _iota2(shape, dim):
    return lax.broadcasted_iota(jnp.int32, shape, dim)


def _sigmoid(x):
    return 1.0 / (1.0 + jnp.exp(-x))


def _shift_down(v, sh, row):
    return jnp.where(row >= sh, pltpu.roll(v, sh, 0), 0.0)


def _shift_up(v, sh, row):
    n = v.shape[0]
    return jnp.where(row < n - sh, pltpu.roll(v, n - sh, 0), 0.0)


class Dims:
    def __init__(self, bl, s, d):
        self.Bl, self.S, self.D = bl, s, d
        self.T = bl * s
        self.W2 = 2 * d
        self.H = self.W2 // HEAD
        self.hpg = self.H // SSM_GROUPS
        self.CC = self.W2 + 2 * SSM_GROUPS * LANES
        self.Dg = d // POOL_GROUPS
        self.nc = s // LANES
        self.o_dt = self.W2 + self.CC
        self.IN_COLS = 13 * d + 2 * SSM_GROUPS * LANES + self.H
        self.c_z = 0
        self.c_xbc = self.W2
        self.c_pu = self.W2 + self.CC
        self.c_pg = self.c_pu + d
        self.c_qkv = self.c_pg + d
        self.c_sbg = self.c_qkv + 3 * d
        self.c_mrg = self.c_sbg + d
        self.c_dt = self.c_mrg + 3 * d
        self.NP = self.c_dt + DT_PAD
        assert self.c_dt == self.IN_COLS - self.H
        assert s % LANES == 0 and d % 512 == 0 and self.H <= LANES


def _permute_cols(w, dm):
    pad = jnp.zeros(w.shape[:-1] + (DT_PAD - dm.H,), w.dtype)
    return jnp.concatenate([w[..., :dm.o_dt], w[..., dm.o_dt + dm.H:], w[..., dm.o_dt:dm.o_dt + dm.H], pad], axis=-1)


def _unpermute_cols(w, dm):
    return jnp.concatenate([w[..., :dm.o_dt], w[..., dm.c_dt:dm.c_dt + dm.H], w[..., dm.o_dt:dm.c_dt]], axis=-1)


def _mm(a, b, *, ta=False, tb=False, out_dtype=f32, res=None, name):
    M, K = (a.shape[1], a.shape[0]) if ta else a.shape
    N = b.shape[0] if tb else b.shape[1]
    tm = _pick(M, (1024, 512, 256, 128))
    tn = _pick(N, (2048, 1024, 512, 256, 128))
    tk = _pick(K, (2048, 1024, 512, 256, 128) if tb else (1024, 512, 256, 128))
    nk = K // tk

    def body(*refs):
        if res is None:
            a_ref, b_ref, o_ref, acc = refs
        else:
            a_ref, b_ref, r_ref, o_ref, acc = refs
        k = pl.program_id(2)

        @pl.when(k == 0)
        def _():
            acc[...] = jnp.zeros_like(acc)

        acc[...] += _dg(_mx(a_ref[...]), _mx(b_ref[...]), 0 if ta else 1, 1 if tb else 0)

        @pl.when(k == nk - 1)
        def _():
            v = acc[...]
            if res is not None:
                v = v + r_ref[...]
            o_ref[...] = v.astype(out_dtype)

    a_spec = pl.BlockSpec((tk, tm), lambda i, j, k: (k, i)) if ta else pl.BlockSpec((tm, tk), lambda i, j, k: (i, k))
    b_spec = pl.BlockSpec((tn, tk), lambda i, j, k: (j, k)) if tb else pl.BlockSpec((tk, tn), lambda i, j, k: (k, j))
    o_spec = pl.BlockSpec((tm, tn), lambda i, j, k: (i, j))
    in_specs = [a_spec, b_spec] + ([o_spec] if res is not None else [])
    args = (a, b) + ((res,) if res is not None else ())
    return pl.pallas_call(
        body, name=name, grid=(M // tm, N // tn, nk), in_specs=in_specs, out_specs=o_spec,
        out_shape=jax.ShapeDtypeStruct((M, N), out_dtype), scratch_shapes=[pltpu.VMEM((tm, tn), f32)],
        compiler_params=_cp(dimension_semantics=("parallel", "parallel", "arbitrary")),
    )(*args)


def _rms_fwd(x, w, dm):
    tr = _pick(dm.T, (256, 128))

    def body(x_ref, w_ref, o_ref):
        xf = x_ref[...]
        r = lax.rsqrt(jnp.mean(xf * xf, axis=-1, keepdims=True) + EPS)
        o_ref[...] = (xf * r * w_ref[...]).astype(o_ref.dtype)

    return pl.pallas_call(
        body, name="rms_fwd", grid=(dm.T // tr,),
        in_specs=[pl.BlockSpec((tr, dm.D), lambda i: (i, 0)), pl.BlockSpec((1, dm.D), lambda i: (0, 0))],
        out_specs=pl.BlockSpec((tr, dm.D), lambda i: (i, 0)),
        out_shape=jax.ShapeDtypeStruct((dm.T, dm.D), ACT_DTYPE), compiler_params=_cp(),
    )(x, w.reshape(1, dm.D))


def _rms_bwd(x, dh, dres, w, dm):
    tr = _pick(dm.T, (256, 128))

    def body(x_ref, dh_ref, dr_ref, w_ref, dx_ref, dw_ref):
        @pl.when(pl.program_id(0) == 0)
        def _():
            dw_ref[...] = jnp.zeros_like(dw_ref)

        xf = x_ref[...]
        r = lax.rsqrt(jnp.mean(xf * xf, axis=-1, keepdims=True) + EPS)
        xh = xf * r
        dh_ = dh_ref[...]
        dxh = dh_ * w_ref[...]
        dx_ref[...] = dr_ref[...] + r * (dxh - xh * jnp.mean(dxh * xh, axis=-1, keepdims=True))
        dw_ref[...] += jnp.sum(dh_ * xh, axis=0, keepdims=True)

    row = pl.BlockSpec((tr, dm.D), lambda i: (i, 0))
    vec = pl.BlockSpec((1, dm.D), lambda i: (0, 0))
    return pl.pallas_call(
        body, name="rms_bwd", grid=(dm.T // tr,), in_specs=[row, row, row, vec], out_specs=[row, vec],
        out_shape=[jax.ShapeDtypeStruct((dm.T, dm.D), f32), jax.ShapeDtypeStruct((1, dm.D), f32)],
        compiler_params=_cp(dimension_semantics=("arbitrary",)),
    )(x, dh, dres, w.reshape(1, dm.D))


def _loss_head(x, tgt, w, dm):
    tr = _pick(dm.T, (256, 128))

    def body(x_ref, t_ref, w_ref, dx_ref, dw_ref, ls_ref):
        @pl.when(pl.program_id(0) == 0)
        def _():
            dw_ref[...] = jnp.zeros_like(dw_ref)
            ls_ref[...] = jnp.zeros_like(ls_ref)

        xf = x_ref[...]
        r = lax.rsqrt(jnp.mean(xf * xf, axis=-1, keepdims=True) + EPS)
        xh = xf * r
        err = xh * w_ref[...] - t_ref[...]
        per_tok = jnp.mean(err * err, axis=-1, keepdims=True)
        ls_ref[...] += 0.5 * jnp.sum(per_tok, axis=0, keepdims=True)
        dy = err * (1.0 / dm.D)
        dxh = dy * w_ref[...]
        dx_ref[...] = r * (dxh - xh * jnp.mean(dxh * xh, axis=-1, keepdims=True))
        dw_ref[...] += jnp.sum(dy * xh, axis=0, keepdims=True)

    row = pl.BlockSpec((tr, dm.D), lambda i: (i, 0))
    vec = pl.BlockSpec((1, dm.D), lambda i: (0, 0))
    return pl.pallas_call(
        body, name="loss_head", grid=(dm.T // tr,), in_specs=[row, row, vec],
        out_specs=[row, vec, pl.BlockSpec((1, LANES), lambda i: (0, 0))],
        out_shape=[jax.ShapeDtypeStruct((dm.T, dm.D), f32), jax.ShapeDtypeStruct((1, dm.D), f32),
                   jax.ShapeDtypeStruct((1, LANES), f32)],
        compiler_params=_cp(dimension_semantics=("arbitrary",)),
    )(x, tgt, w.reshape(1, dm.D))


def _conv_pre(u, w_ref, b_ref, row):
    acc = b_ref[...] + w_ref[CONV_WIDTH - 1:CONV_WIDTH, :] * u
    for k in range(CONV_WIDTH - 1):
        acc = acc + w_ref[k:k + 1, :] * _shift_down(u, CONV_WIDTH - 1 - k, row)
    return acc


def _conv_fwd(proj, cw, cb, dm):
    cwid = LANES
    off = dm.c_xbc // cwid

    def body(u_ref, w_ref, b_ref, o_ref):
        u = u_ref[...]
        row = _iota2(u.shape, 0)
        pre = _conv_pre(u, w_ref, b_ref, row)
        o_ref[...] = pre * _sigmoid(pre)

    return pl.pallas_call(
        body, name="conv_fwd", grid=(dm.Bl, dm.CC // cwid),
        in_specs=[pl.BlockSpec((dm.S, cwid), lambda b, j: (b, off + j)),
                  pl.BlockSpec((CONV_WIDTH, cwid), lambda b, j: (0, j)), pl.BlockSpec((1, cwid), lambda b, j: (0, j))],
        out_specs=pl.BlockSpec((dm.S, cwid), lambda b, j: (b, j)),
        out_shape=jax.ShapeDtypeStruct((dm.T, dm.CC), f32), compiler_params=_cp(),
    )(proj, cw, cb.reshape(1, dm.CC))


def _conv_bwd(proj, d_out, cw, cb, dm):
    cwid = LANES
    off = dm.c_xbc // cwid

    def body(u_ref, d_ref, w_ref, b_ref, du_ref, dw_ref, db_ref):
        @pl.when(pl.program_id(1) == 0)
        def _():
            dw_ref[...] = jnp.zeros_like(dw_ref)
            db_ref[...] = jnp.zeros_like(db_ref)

        u = u_ref[...]
        row = _iota2(u.shape, 0)
        pre = _conv_pre(u, w_ref, b_ref, row)
        sg = _sigmoid(pre)
        dpre = d_ref[...] * (sg * (1.0 + pre * (1.0 - sg)))
        du = w_ref[CONV_WIDTH - 1:CONV_WIDTH, :] * dpre
        dw_ref[CONV_WIDTH - 1:CONV_WIDTH, :] += jnp.sum(dpre * u, axis=0, keepdims=True)
        for k in range(CONV_WIDTH - 1):
            sh = CONV_WIDTH - 1 - k
            du = du + w_ref[k:k + 1, :] * _shift_up(dpre, sh, row)
            dw_ref[k:k + 1, :] += jnp.sum(dpre * _shift_down(u, sh, row), axis=0, keepdims=True)
        du_ref[...] = du.astype(du_ref.dtype)
        db_ref[...] += jnp.sum(dpre, axis=0, keepdims=True)

    return pl.pallas_call(
        body, name="conv_bwd", grid=(dm.CC // cwid, dm.Bl),
        in_specs=[pl.BlockSpec((dm.S, cwid), lambda j, b: (b, off + j)), pl.BlockSpec((dm.S, cwid), lambda j, b: (b, j)),
                  pl.BlockSpec((CONV_WIDTH, cwid), lambda j, b: (0, j)), pl.BlockSpec((1, cwid), lambda j, b: (0, j))],
        out_specs=[pl.BlockSpec((dm.S, cwid), lambda j, b: (b, j)), pl.BlockSpec((CONV_WIDTH, cwid), lambda j, b: (0, j)),
                   pl.BlockSpec((1, cwid), lambda j, b: (0, j))],
        out_shape=[jax.ShapeDtypeStruct((dm.T, dm.CC), ACT_DTYPE), jax.ShapeDtypeStruct((CONV_WIDTH, dm.CC), f32),
                   jax.ShapeDtypeStruct((1, dm.CC), f32)],
        compiler_params=_cp(dimension_semantics=("arbitrary", "arbitrary")),
    )(proj, d_out, cw, cb.reshape(1, dm.CC))


def _pad_lanes(v):
    return jnp.pad(v, (0, LANES - v.shape[0])).reshape(1, LANES)


def _softplus(x):
    return jnp.maximum(x, 0.0) + jnp.log(1.0 + jnp.exp(-jnp.abs(x)))


def _dt_prep(proj, dt_bias, a_log, dm):
    off = dm.c_dt // LANES

    def body(r_ref, b_ref, al_ref, dt_ref, cum_ref, cumt_ref):
        dt = _softplus(r_ref[...] + b_ref[...])
        adt = dt * (-jnp.exp(al_ref[...]))
        tril = (_iota2((LANES, LANES), 1) <= _iota2((LANES, LANES), 0)).astype(bf16)
        cum = _exact_nn_left(tril, adt)
        dt_ref[...] = dt
        cum_ref[...] = cum
        cumt_ref[...] = cum.T

    blk = pl.BlockSpec((LANES, LANES), lambda i: (i, 0))
    vec = pl.BlockSpec((1, LANES), lambda i: (0, 0))
    return pl.pallas_call(
        body, name="dt_prep", grid=(dm.T // LANES,),
        in_specs=[pl.BlockSpec((LANES, LANES), lambda i: (i, off)), vec, vec],
        out_specs=[blk, blk, pl.BlockSpec((LANES, LANES), lambda i: (0, i))],
        out_shape=[jax.ShapeDtypeStruct((dm.T, LANES), f32), jax.ShapeDtypeStruct((dm.T, LANES), f32),
                   jax.ShapeDtypeStruct((LANES, dm.T), f32)],
        compiler_params=_cp(),
    )(proj, _pad_lanes(dt_bias), _pad_lanes(a_log))


def _exact_nn_left(u, a):
    hi, mid, lo = _split3(a)
    return _nn(u, hi) + _nn(u, mid) + _nn(u, lo)


def _dt_bwd(proj, dt, dcum, dcum_t, ddt, dt_bias, a_log, dm):
    off = dm.c_dt // LANES

    def body(r_ref, dt_ref, dc_ref, dct_ref, dd_ref, b_ref, al_ref, o_ref, db_ref, da_ref):
        @pl.when(pl.program_id(0) == 0)
        def _():
            db_ref[...] = jnp.zeros_like(db_ref)
            da_ref[...] = jnp.zeros_like(da_ref)

        a = -jnp.exp(al_ref[...])
        triu = (_iota2((LANES, LANES), 1) >= _iota2((LANES, LANES), 0)).astype(bf16)
        dadt = _exact_nn_left(triu, dc_ref[...] + dct_ref[...].T)
        d_dt = dd_ref[...] + dadt * a
        d_raw = d_dt * _sigmoid(r_ref[...] + b_ref[...])
        o_ref[...] = d_raw.astype(o_ref.dtype)
        db_ref[...] += jnp.sum(d_raw, axis=0, keepdims=True)
        da_ref[...] += jnp.sum(dadt * dt_ref[...], axis=0, keepdims=True) * a

    blk = pl.BlockSpec((LANES, LANES), lambda i: (i, 0))
    vec = pl.BlockSpec((1, LANES), lambda i: (0, 0))
    return pl.pallas_call(
        body, name="dt_bwd", grid=(dm.T // LANES,),
        in_specs=[pl.BlockSpec((LANES, LANES), lambda i: (i, off)), blk, blk, pl.BlockSpec((LANES, LANES), lambda i: (0, i)),
                  blk, vec, vec],
        out_specs=[blk, vec, vec],
        out_shape=[jax.ShapeDtypeStruct((dm.T, LANES), ACT_DTYPE), jax.ShapeDtypeStruct((1, LANES), f32),
                   jax.ShapeDtypeStruct((1, LANES), f32)],
        compiler_params=_cp(dimension_semantics=("arbitrary",)),
    )(proj, dt, dcum, dcum_t, ddt, _pad_lanes(dt_bias), _pad_lanes(a_log))


def _ssd_common(dm):
    L = LANES
    tri = _iota2((L, L), 0) >= _iota2((L, L), 1)
    lo = _iota2((L, L), 1) < HEAD
    return tri, lo


def _ssd_fwd(xbc, dt, cum, cumt, d_skip, dm):
    L, W2, hpg = LANES, dm.W2, dm.hpg
    nb = W2 // L

    def body(x_ref, b_ref, c_ref, dt_ref, cum_ref, cumt_ref, dsk_ref, y_ref, hs_ref, h_scr):
        @pl.when(pl.program_id(1) == 0)
        def _():
            h_scr[...] = jnp.zeros_like(h_scr)

        hs_ref[0] = h_scr[...]
        tri, lo = _ssd_common(dm)
        for g in range(SSM_GROUPS):
            bb = _mx(b_ref[:, g * L:(g + 1) * L])
            cb_ = _mx(c_ref[:, g * L:(g + 1) * L])
            cbm = _nt(cb_, bb)
            for i in range(hpg // 2):
                h0 = g * hpg + 2 * i
                h1 = h0 + 1
                sl = slice(h0 * HEAD, h0 * HEAD + L)
                x_p = x_ref[:, sl]
                cum0, cum1 = cum_ref[:, h0:h0 + 1], cum_ref[:, h1:h1 + 1]
                cums = jnp.where(lo, cum0, cum1)
                xdt = x_p * jnp.where(lo, dt_ref[:, h0:h0 + 1], dt_ref[:, h1:h1 + 1])
                tot = jnp.where(lo[0:1], cum_ref[L - 1:L, h0:h0 + 1], cum_ref[L - 1:L, h1:h1 + 1])
                y_p = jnp.zeros((L, L), f32)
                for hh, m in ((h0, lo), (h1, jnp.logical_not(lo))):
                    diff = cum_ref[:, hh:hh + 1] - cumt_ref[hh:hh + 1, :]
                    lm = jnp.where(tri, jnp.exp(jnp.minimum(diff, 0.0)), 0.0)
                    y_p = y_p + _nn(_mx(cbm * lm), _mx(jnp.where(m, xdt, 0.0)))
                hp = h_scr[:, sl]
                y_p = y_p + _nn(cb_, _mx(hp)) * jnp.exp(cums)
                y_p = y_p + x_p * jnp.where(lo[0:1], dsk_ref[0:1, h0:h0 + 1], dsk_ref[0:1, h1:h1 + 1])
                y_ref[:, sl] = y_p
                h_scr[:, sl] = hp * jnp.exp(tot) + _tn(bb, _mx(xdt * jnp.exp(tot - cums)))

    nc = dm.nc
    ob = W2 // (SSM_GROUPS * L)
    blk = pl.BlockSpec((L, L), lambda b, c: (b * nc + c, 0))
    return pl.pallas_call(
        body, name="ssd_fwd", grid=(dm.Bl, nc),
        in_specs=[pl.BlockSpec((L, W2), lambda b, c: (b * nc + c, 0)),
                  pl.BlockSpec((L, SSM_GROUPS * L), lambda b, c: (b * nc + c, ob)),
                  pl.BlockSpec((L, SSM_GROUPS * L), lambda b, c: (b * nc + c, ob + 1)),
                  blk, blk, pl.BlockSpec((L, L), lambda b, c: (0, b * nc + c)), pl.BlockSpec((1, L), lambda b, c: (0, 0))],
        out_specs=[pl.BlockSpec((L, W2), lambda b, c: (b * nc + c, 0)), pl.BlockSpec((1, L, W2), lambda b, c: (b * nc + c, 0, 0))],
        out_shape=[jax.ShapeDtypeStruct((dm.T, W2), f32), jax.ShapeDtypeStruct((dm.Bl * nc, L, W2), f32)],
        scratch_shapes=[pltpu.VMEM((L, W2), f32)],
        compiler_params=_cp(dimension_semantics=("arbitrary", "arbitrary")),
    )(xbc, xbc, xbc, dt, cum, cumt, _pad_lanes(d_skip))


def _ssd_bwd(xbc, dt, cum, cumt, d_skip, hs, dy, dm):
    L, W2, hpg = LANES, dm.W2, dm.hpg
    nc = dm.nc

    def body(x_ref, b_ref, c_ref, dt_ref, cum_ref, cumt_ref, dsk_ref, hs_ref, dy_ref,
             dx_ref, db_ref, dc_ref, ddt_ref, dcum_ref, dcr_ref, dd_ref, dh_scr, lane_cum, lane_dt, lane_d):
        @pl.when(pl.program_id(1) == 0)
        def _():
            dh_scr[...] = jnp.zeros_like(dh_scr)

        @pl.when((pl.program_id(0) == 0) & (pl.program_id(1) == 0))
        def _():
            dd_ref[...] = jnp.zeros_like(dd_ref)

        dcr_ref[...] = jnp.zeros_like(dcr_ref)
        tri, lo = _ssd_common(dm)
        last = _iota2((L, L), 0) == L - 1
        for g in range(SSM_GROUPS):
            gs = slice(g * L, (g + 1) * L)
            bb = _mx(b_ref[:, gs])
            cb_ = _mx(c_ref[:, gs])
            cbm = _nt(cb_, bb)
            dcb = jnp.zeros((L, L), f32)
            dc_g = jnp.zeros((L, L), f32)
            db_g = jnp.zeros((L, L), f32)
            for i in range(hpg // 2):
                h0 = g * hpg + 2 * i
                h1 = h0 + 1
                sl = slice(h0 * HEAD, h0 * HEAD + L)
                x_p = x_ref[:, sl]
                dy_p = dy_ref[:, sl]
                dt_p = jnp.where(lo, dt_ref[:, h0:h0 + 1], dt_ref[:, h1:h1 + 1])
                cums = jnp.where(lo, cum_ref[:, h0:h0 + 1], cum_ref[:, h1:h1 + 1])
                tot = jnp.where(lo[0:1], cum_ref[L - 1:L, h0:h0 + 1], cum_ref[L - 1:L, h1:h1 + 1])
                dsk_p = jnp.where(lo[0:1], dsk_ref[0:1, h0:h0 + 1], dsk_ref[0:1, h1:h1 + 1])
                xdt = x_p * dt_p
                ecum = jnp.exp(cums)
                dec = jnp.exp(tot - cums)
                etot = jnp.exp(tot)
                hp = hs_ref[0, :, sl]
                hp_b = _mx(hp)
                dhn = dh_scr[:, sl]
                dhn_b = _mx(dhn)
                y_off = _nn(cb_, hp_b) * ecum
                dch = _mx(dy_p * ecum)
                dc_g = dc_g + _nt(dch, hp_b)
                dh_off = _tn(cb_, dch)
                bds = _nn(bb, dhn_b)
                xdec = xdt * dec
                db_g = db_g + _nt(_mx(xdec), dhn_b)
                dxdt = bds * dec
                sdec = bds * xdec
                tot_lane = jnp.sum(sdec, axis=0, keepdims=True) + jnp.sum(dhn * hp, axis=0, keepdims=True) * etot
                dh_scr[:, sl] = etot * dhn + dh_off
                rsum = []
                for hh, m in ((h0, lo), (h1, jnp.logical_not(lo))):
                    diff = cum_ref[:, hh:hh + 1] - cumt_ref[hh:hh + 1, :]
                    lm = jnp.where(tri, jnp.exp(jnp.minimum(diff, 0.0)), 0.0)
                    w32 = cbm * lm
                    dyh = _mx(jnp.where(m, dy_p, 0.0))
                    dw = _nt(dyh, _mx(jnp.where(m, xdt, 0.0)))
                    dcb = dcb + dw * lm
                    e = dw * w32
                    rsum.append(jnp.sum(e, axis=1, keepdims=True))
                    dcr_ref[hh:hh + 1, :] = -jnp.sum(e, axis=0, keepdims=True)
                    dxdt = dxdt + _tn(_mx(w32), dyh)
                lane_cum[:, sl] = (dy_p * y_off + jnp.where(lo, rsum[0], rsum[1]) * (1.0 / HEAD) - sdec
                                   + jnp.where(last, tot_lane, 0.0))
                lane_dt[:, sl] = dxdt * x_p
                lane_d[:, sl] = dy_p * x_p
                dx_ref[:, sl] = dxdt * dt_p + dsk_p * dy_p
            dcb_b = _mx(dcb)
            dc_ref[:, gs] = dc_g + _nn(dcb_b, bb)
            db_ref[:, gs] = db_g + _tn(dcb_b, cb_)
        sel = (_iota2((W2, L), 0) // HEAD == _iota2((W2, L), 1)).astype(bf16)
        dcum_ref[...] = _exact_nn(lane_cum[...], sel)
        ddt_ref[...] = _exact_nn(lane_dt[...], sel)
        dd_ref[...] += jnp.sum(_exact_nn(lane_d[...], sel), axis=0, keepdims=True)

    ob = W2 // (SSM_GROUPS * L)

    def rc(b, c):
        return b * nc + (nc - 1 - c)

    blk = pl.BlockSpec((L, L), lambda b, c: (rc(b, c), 0))
    blk_t = pl.BlockSpec((L, L), lambda b, c: (0, rc(b, c)))
    wide = pl.BlockSpec((L, W2), lambda b, c: (rc(b, c), 0))
    grp = pl.BlockSpec((L, SSM_GROUPS * L), lambda b, c: (rc(b, c), 0))
    return pl.pallas_call(
        body, name="ssd_bwd", grid=(dm.Bl, nc),
        in_specs=[wide, pl.BlockSpec((L, SSM_GROUPS * L), lambda b, c: (rc(b, c), ob)),
                  pl.BlockSpec((L, SSM_GROUPS * L), lambda b, c: (rc(b, c), ob + 1)),
                  blk, blk, blk_t, pl.BlockSpec((1, L), lambda b, c: (0, 0)),
                  pl.BlockSpec((1, L, W2), lambda b, c: (rc(b, c), 0, 0)), wide],
        out_specs=[wide, grp, grp, blk, blk, blk_t, pl.BlockSpec((1, L), lambda b, c: (0, 0))],
        out_shape=[jax.ShapeDtypeStruct((dm.T, W2), f32), jax.ShapeDtypeStruct((dm.T, SSM_GROUPS * L), f32),
                   jax.ShapeDtypeStruct((dm.T, SSM_GROUPS * L), f32), jax.ShapeDtypeStruct((dm.T, L), f32),
                   jax.ShapeDtypeStruct((dm.T, L), f32), jax.ShapeDtypeStruct((L, dm.T), f32),
                   jax.ShapeDtypeStruct((1, L), f32)],
        scratch_shapes=[pltpu.VMEM((L, W2), f32)] * 4,
        compiler_params=_cp(dimension_semantics=("arbitrary", "arbitrary")),
    )(xbc, xbc, xbc, dt, cum, cumt, _pad_lanes(d_skip), hs, dy)


def _gnorm_fwd(y, proj, w, dm):
    tr = _pick(dm.T, (256, 128))
    row = pl.BlockSpec((tr, dm.W2), lambda i: (i, 0))

    def body(y_ref, z_ref, w_ref, o_ref):
        z = z_ref[...]
        yg = y_ref[...] * (z * _sigmoid(z))
        r = lax.rsqrt(jnp.mean(yg * yg, axis=-1, keepdims=True) + EPS)
        o_ref[...] = (yg * r * w_ref[...]).astype(o_ref.dtype)

    return pl.pallas_call(
        body, name="gnorm_fwd", grid=(dm.T // tr,), in_specs=[row, row, pl.BlockSpec((1, dm.W2), lambda i: (0, 0))],
        out_specs=row, out_shape=jax.ShapeDtypeStruct((dm.T, dm.W2), ACT_DTYPE), compiler_params=_cp(),
    )(y, proj, w.reshape(1, dm.W2))


def _gnorm_bwd(dn, y, proj, w, dm):
    tr = _pick(dm.T, (256, 128))
    row = pl.BlockSpec((tr, dm.W2), lambda i: (i, 0))
    vec = pl.BlockSpec((1, dm.W2), lambda i: (0, 0))

    def body(dn_ref, y_ref, z_ref, w_ref, dy_ref, dz_ref, dw_ref):
        @pl.when(pl.program_id(0) == 0)
        def _():
            dw_ref[...] = jnp.zeros_like(dw_ref)

        z = z_ref[...]
        sg = _sigmoid(z)
        sz = z * sg
        yv = y_ref[...]
        yg = yv * sz
        r = lax.rsqrt(jnp.mean(yg * yg, axis=-1, keepdims=True) + EPS)
        n = yg * r
        dout = dn_ref[...]
        dnn = dout * w_ref[...]
        dyg = r * (dnn - n * jnp.mean(dnn * n, axis=-1, keepdims=True))
        dy_ref[...] = dyg * sz
        dz_ref[...] = (dyg * yv * (sg * (1.0 + z * (1.0 - sg)))).astype(dz_ref.dtype)
        dw_ref[...] += jnp.sum(dout * n, axis=0, keepdims=True)

    return pl.pallas_call(
        body, name="gnorm_bwd", grid=(dm.T // tr,), in_specs=[row, row, row, vec], out_specs=[row, row, vec],
        out_shape=[jax.ShapeDtypeStruct((dm.T, dm.W2), f32), jax.ShapeDtypeStruct((dm.T, dm.W2), ACT_DTYPE),
                   jax.ShapeDtypeStruct((1, dm.W2), f32)],
        compiler_params=_cp(dimension_semantics=("arbitrary",)),
    )(dn, y, proj, w.reshape(1, dm.W2))


def _pool_mixed(u, g, row):
    s = u
    for k in range(POOL_GROUPS):
        s = jnp.where(k <= g, s + _shift_down(s, 1 << k, row), s)
    w = jnp.left_shift(2, g)
    cnt = jnp.minimum(row + 1, w).astype(f32)
    return s / cnt - u, cnt


def _pool_fwd(proj, pw, scale, dm):
    Dg = dm.Dg
    ou, og = dm.c_pu // Dg, dm.c_pg // Dg

    def body(u_ref, g_ref, w_ref, s_ref, o_ref):
        u = u_ref[...]
        row = _iota2(u.shape, 0)
        mixed, _ = _pool_mixed(u, pl.program_id(1), row)
        lin = _nn(_mx(mixed), _mx(w_ref[0]))
        gt = g_ref[...]
        o_ref[...] = (lin * s_ref[...] * (gt * _sigmoid(gt))).astype(o_ref.dtype)

    return pl.pallas_call(
        body, name="pool_fwd", grid=(dm.Bl, POOL_GROUPS),
        in_specs=[pl.BlockSpec((dm.S, Dg), lambda b, g: (b, ou + g)), pl.BlockSpec((dm.S, Dg), lambda b, g: (b, og + g)),
                  pl.BlockSpec((1, Dg, Dg), lambda b, g: (g, 0, 0)), pl.BlockSpec((1, Dg), lambda b, g: (0, g))],
        out_specs=pl.BlockSpec((dm.S, Dg), lambda b, g: (b, g)),
        out_shape=jax.ShapeDtypeStruct((dm.T, dm.D), ACT_DTYPE), compiler_params=_cp(),
    )(proj, proj, pw, scale.reshape(1, dm.D))


def _pool_bwd(proj, dout, pw, scale, dm):
    Dg = dm.Dg
    ou, og = dm.c_pu // Dg, dm.c_pg // Dg

    def body(u_ref, g_ref, d_ref, w_ref, s_ref, du_ref, dg_ref, dw_ref, ds_ref):
        @pl.when(pl.program_id(1) == 0)
        def _():
            dw_ref[...] = jnp.zeros_like(dw_ref)
            ds_ref[...] = jnp.zeros_like(ds_ref)

        g = pl.program_id(0)
        u = u_ref[...]
        row = _iota2(u.shape, 0)
        mixed, cnt = _pool_mixed(u, g, row)
        mixed_b = _mx(mixed)
        wb = _mx(w_ref[0])
        lin = _nn(mixed_b, wb)
        gt = g_ref[...]
        sg = _sigmoid(gt)
        silu = gt * sg
        d = d_ref[...]
        sc = s_ref[...]
        dlin = d * sc * silu
        ds_ref[...] += jnp.sum(d * lin * silu, axis=0, keepdims=True)
        dg_ref[...] = (d * lin * sc * (sg * (1.0 + gt * (1.0 - sg)))).astype(dg_ref.dtype)
        dlin_b = _mx(dlin)
        dmixed = _nt(dlin_b, wb)
        dw_ref[0] += _tn(mixed_b, dlin_b)
        r = dmixed / cnt
        for k in range(POOL_GROUPS):
            r = jnp.where(k <= g, r + _shift_up(r, 1 << k, row), r)
        du_ref[...] = (r - dmixed).astype(du_ref.dtype)

    return pl.pallas_call(
        body, name="pool_bwd", grid=(POOL_GROUPS, dm.Bl),
        in_specs=[pl.BlockSpec((dm.S, Dg), lambda g, b: (b, ou + g)), pl.BlockSpec((dm.S, Dg), lambda g, b: (b, og + g)),
                  pl.BlockSpec((dm.S, Dg), lambda g, b: (b, g)), pl.BlockSpec((1, Dg, Dg), lambda g, b: (g, 0, 0)),
                  pl.BlockSpec((1, Dg), lambda g, b: (0, g))],
        out_specs=[pl.BlockSpec((dm.S, Dg), lambda g, b: (b, g)), pl.BlockSpec((dm.S, Dg), lambda g, b: (b, g)),
                   pl.BlockSpec((1, Dg, Dg), lambda g, b: (g, 0, 0)), pl.BlockSpec((1, Dg), lambda g, b: (0, g))],
        out_shape=[jax.ShapeDtypeStruct((dm.T, dm.D), ACT_DTYPE), jax.ShapeDtypeStruct((dm.T, dm.D), ACT_DTYPE),
                   jax.ShapeDtypeStruct((POOL_GROUPS, Dg, Dg), f32), jax.ShapeDtypeStruct((1, dm.D), f32)],
        compiler_params=_cp(dimension_semantics=("arbitrary", "arbitrary")),
    )(proj, proj, dout, pw, scale.reshape(1, dm.D))


SB_TILE = 512


def _split2(a):
    hi = a.astype(bf16)
    return hi, (a - hi.astype(f32)).astype(bf16)


def _exact2_nn(a, u):
    hi, lo = _split2(a)
    return _nn(hi, u) + _nn(lo, u)


SB_CUM = 256


def _blocked_sums(a, u, suffix):
    nb = a.shape[1] // SB_CUM
    blocks = [a[:, i * SB_CUM:(i + 1) * SB_CUM] for i in range(nb)]
    tots = [jnp.sum(b, axis=1, keepdims=True) for b in blocks]
    out = []
    for i, b in enumerate(blocks):
        s = _exact2_nn(b, u)
        for t in (tots[i + 1:] if suffix else tots[:i]):
            s = s + t
        out.append(s)
    total = tots[0]
    for t in tots[1:]:
        total = total + t
    return (out[0] if nb == 1 else jnp.concatenate(out, axis=1)), total


def _sb_tile(qb, kb, base, u, suffix, causal):
    z = _nt(qb, kb)
    lb = jnp.minimum(z, 0.0) - jnp.log(1.0 + jnp.exp(-jnp.abs(z)))
    lm = lb - z
    if causal is not None:
        lm = jnp.where(causal, lm, 0.0)
    sums, tot = _blocked_sums(lm, u, suffix)
    att = jnp.exp(lb + base + sums) if suffix else jnp.exp(lb + base - sums)
    if causal is not None:
        att = jnp.where(causal, att, 0.0)
    return lb, att, tot


def _sb_fwd(proj, dm):
    L, S, D, TQ = LANES, dm.S, dm.D, SB_TILE
    oq, og = dm.c_qkv // L, dm.c_sbg // L
    nb = D // L
    scale = HEAD ** -0.5

    def body(q_ref, k_ref, v_ref, g_ref, o_ref, lt_ref, out_ref, k_s, va_s, vb_s):
        lo = _iota2((1, L), 1) < HEAD
        hi = jnp.logical_not(lo)
        k_s[...] = _mx(k_ref[...])
        vf = v_ref[...]
        va_s[...] = _mx(jnp.where(lo, vf, 0.0))
        vb_s[...] = _mx(jnp.where(hi, vf, 0.0))
        before = _iota2((TQ, TQ), 0) > _iota2((TQ, TQ), 1)
        ugt = (_iota2((SB_CUM, SB_CUM), 0) > _iota2((SB_CUM, SB_CUM), 1)).astype(bf16)
        v_s = (va_s, vb_s)

        def qloop(qt, _):
            rows = pl.ds(pl.multiple_of(qt * TQ, TQ), TQ)
            qs = q_ref[rows, :] * scale
            qb = [_mx(jnp.where(m, qs, 0.0)) for m in (lo, hi)]

            def tile(kt, carry, causal):
                runs, acc = carry
                krows = pl.ds(pl.multiple_of(kt * TQ, TQ), TQ)
                k_t = k_s[krows, :]
                new_runs = []
                for h in range(2):
                    _, att, tot = _sb_tile(qb[h], k_t, runs[h], ugt, True, causal)
                    acc = acc + _nn(_mx(att), v_s[h][krows, :])
                    new_runs.append(runs[h] + tot)
                return tuple(new_runs), acc

            zc = jnp.zeros((TQ, 1), f32)
            carry = tile(qt, ((zc, zc), jnp.zeros((TQ, L), f32)), before)
            runs, acc = lax.fori_loop(0, qt, lambda jj, c: tile(qt - 1 - jj, c, None), carry)
            o_ref[rows, :] = acc
            lt_ref[rows, :] = jnp.where(lo, runs[0], runs[1])
            gt = g_ref[rows, :]
            out_ref[rows, :] = (acc * (gt * _sigmoid(gt))).astype(out_ref.dtype)
            return 0

        lax.fori_loop(0, S // TQ, qloop, 0)

    def col(o):
        return pl.BlockSpec((S, L), lambda b, p: (b, o + p))

    return pl.pallas_call(
        body, name="sb_fwd", grid=(dm.Bl, nb),
        in_specs=[col(oq), col(oq + nb), col(oq + 2 * nb), col(og)], out_specs=[col(0), col(0), col(0)],
        out_shape=[jax.ShapeDtypeStruct((dm.T, D), f32), jax.ShapeDtypeStruct((dm.T, D), f32),
                   jax.ShapeDtypeStruct((dm.T, D), ACT_DTYPE)],
        scratch_shapes=[pltpu.VMEM((S, L), MXU_DTYPE)] * 3,
        compiler_params=_cp(),
    )(proj, proj, proj, proj)


def _sb_bwd(proj, o, lt, dsb, dm):
    L, S, D, TQ = LANES, dm.S, dm.D, SB_TILE
    oq, og = dm.c_qkv // L, dm.c_sbg // L
    nb = D // L
    scale = HEAD ** -0.5

    def body(q_ref, k_ref, v_ref, g_ref, o_ref, lt_ref, d_ref, dq_ref, dk_ref, dv_ref, dg_ref,
             k_s, ka_s, kb_s, v_s, dk_acc, dv_acc):
        lo = _iota2((1, L), 1) < HEAD
        hi = jnp.logical_not(lo)
        kf = k_ref[...]
        k_s[...] = _mx(kf)
        ka_s[...] = _mx(jnp.where(lo, kf, 0.0))
        kb_s[...] = _mx(jnp.where(hi, kf, 0.0))
        v_s[...] = _mx(v_ref[...])
        dk_acc[...] = jnp.zeros_like(dk_acc)
        dv_acc[...] = jnp.zeros_like(dv_acc)
        before = _iota2((TQ, TQ), 0) > _iota2((TQ, TQ), 1)
        ule = (_iota2((SB_CUM, SB_CUM), 0) <= _iota2((SB_CUM, SB_CUM), 1)).astype(bf16)
        ult = (_iota2((SB_CUM, SB_CUM), 0) < _iota2((SB_CUM, SB_CUM), 1)).astype(bf16)
        k_m = (ka_s, kb_s)

        def qloop(qt, _):
            rows = pl.ds(pl.multiple_of(qt * TQ, TQ), TQ)
            gt = g_ref[rows, :]
            sg = _sigmoid(gt)
            dsb_b = d_ref[rows, :]
            do = dsb_b * (gt * sg)
            dg_ref[rows, :] = (dsb_b * o_ref[rows, :] * (sg * (1.0 + gt * (1.0 - sg)))).astype(dg_ref.dtype)
            qs = q_ref[rows, :] * scale
            qb = [_mx(jnp.where(m, qs, 0.0)) for m in (lo, hi)]
            dob = [_mx(jnp.where(m, do, 0.0)) for m in (lo, hi)]
            ltot = [lt_ref[rows, 0:1], lt_ref[rows, HEAD:HEAD + 1]]

            def tile(kt, carry, causal):
                runs, rgs, dq = carry
                krows = pl.ds(pl.multiple_of(kt * TQ, TQ), TQ)
                k_t, v_t = k_s[krows, :], v_s[krows, :]
                dk_t = jnp.zeros((TQ, L), f32)
                dv_t = jnp.zeros((TQ, L), f32)
                new_runs, new_rgs = [], []
                for h in range(2):
                    lb, att, tot = _sb_tile(qb[h], k_t, ltot[h] - runs[h], ule, False, causal)
                    gm = att * _nt(dob[h], v_t)
                    gsum, gtot = _blocked_sums(gm, ult, False)
                    pre = rgs[h] + gsum
                    beta = jnp.exp(lb)
                    dz = gm * (1.0 - beta) - pre * beta
                    if causal is not None:
                        dz = jnp.where(causal, dz, 0.0)
                    dz = _mx(dz)
                    dq = dq + _nn(dz, k_m[h][krows, :])
                    dk_t = dk_t + _tn(dz, qb[h])
                    dv_t = dv_t + _tn(_mx(att), dob[h])
                    new_runs.append(runs[h] + tot)
                    new_rgs.append(rgs[h] + gtot)
                dk_acc[krows, :] += dk_t
                dv_acc[krows, :] += dv_t
                return tuple(new_runs), tuple(new_rgs), dq

            zc = jnp.zeros((TQ, 1), f32)
            carry = lax.fori_loop(0, qt, lambda kt, c: tile(kt, c, None), ((zc, zc), (zc, zc), jnp.zeros((TQ, L), f32)))
            _, _, dq = tile(qt, carry, before)
            dq_ref[rows, :] = (dq * scale).astype(dq_ref.dtype)
            return 0

        lax.fori_loop(0, S // TQ, qloop, 0)
        dk_ref[...] = dk_acc[...].astype(dk_ref.dtype)
        dv_ref[...] = dv_acc[...].astype(dv_ref.dtype)

    def col(off):
        return pl.BlockSpec((S, L), lambda b, p: (b, off + p))

    out = jax.ShapeDtypeStruct((dm.T, D), ACT_DTYPE)
    return pl.pallas_call(
        body, name="sb_bwd", grid=(dm.Bl, nb),
        in_specs=[col(oq), col(oq + nb), col(oq + 2 * nb), col(og), col(0), col(0), col(0)],
        out_specs=[col(0), col(0), col(0), col(0)], out_shape=[out, out, out, out],
        scratch_shapes=[pltpu.VMEM((S, L), MXU_DTYPE)] * 4 + [pltpu.VMEM((S, L), f32)] * 2,
        compiler_params=_cp(),
    )(proj, proj, proj, proj, o, lt, dsb)


def _merge_fwd(proj, ys, dm):
    tr = _pick(dm.T, (256, 128))
    ct = 512
    om = dm.c_mrg // ct
    nb = dm.D // ct
    blk = pl.BlockSpec((tr, ct), lambda i, j: (i, j))

    def body(l0, l1, l2, y0, y1, y2, o_ref):
        acc = _sigmoid(l0[...]) * y0[...] + _sigmoid(l1[...]) * y1[...] + _sigmoid(l2[...]) * y2[...]
        o_ref[...] = acc.astype(o_ref.dtype)

    return pl.pallas_call(
        body, name="merge_fwd", grid=(dm.T // tr, nb),
        in_specs=[pl.BlockSpec((tr, ct), functools.partial(lambda i, j, b: (i, om + b * nb + j), b=b)) for b in range(3)] + [blk] * 3,
        out_specs=blk, out_shape=jax.ShapeDtypeStruct((dm.T, dm.D), ACT_DTYPE), compiler_params=_cp(),
    )(proj, proj, proj, *ys)


def _merge_bwd(proj, ys, dmerged, dm):
    tr = _pick(dm.T, (256, 128))
    ct = 512
    om = dm.c_mrg // ct
    nb = dm.D // ct
    blk = pl.BlockSpec((tr, ct), lambda i, j: (i, j))

    def body(l0, l1, l2, y0, y1, y2, d_ref, dy0, dy1, dy2, dl0, dl1, dl2):
        d = d_ref[...]
        for l_ref, y_ref, dy_ref, dl_ref in ((l0, y0, dy0, dl0), (l1, y1, dy1, dl1), (l2, y2, dy2, dl2)):
            sg = _sigmoid(l_ref[...])
            dy_ref[...] = (d * sg).astype(dy_ref.dtype)
            dl_ref[...] = (d * y_ref[...] * sg * (1.0 - sg)).astype(dl_ref.dtype)

    out = jax.ShapeDtypeStruct((dm.T, dm.D), ACT_DTYPE)
    return pl.pallas_call(
        body, name="merge_bwd", grid=(dm.T // tr, nb),
        in_specs=[pl.BlockSpec((tr, ct), functools.partial(lambda i, j, b: (i, om + b * nb + j), b=b)) for b in range(3)] + [blk] * 4,
        out_specs=[blk] * 6, out_shape=[out] * 6, compiler_params=_cp(),
    )(proj, proj, proj, *ys, dmerged)


def _layer_fwd(x, p, dm):
    h = _rms_fwd(x, p["norm_w"], dm)
    proj = _mm(h, p["w_in"], name="mm_in")
    xbc = _conv_fwd(proj, p["conv_w"], p["conv_b"], dm)
    dt, cum, cumt = _dt_prep(proj, p["dt_bias"], p["a_log"], dm)
    y, hs = _ssd_fwd(xbc, dt, cum, cumt, p["d_skip"], dm)
    ssm_n = _gnorm_fwd(y, proj, p["ssm_norm_w"], dm)
    pool_o = _pool_fwd(proj, p["pool_w"], p["pool_scale"], dm)
    o, lt, sb_o = _sb_fwd(proj, dm)
    ys = (_mm(ssm_n, p["w_proj_ssm"], name="mm_ps"), _mm(pool_o, p["w_proj_pool"], name="mm_pp"),
          _mm(sb_o, p["w_proj_sb"], name="mm_pb"))
    merged = _merge_fwd(proj, ys, dm)
    x_next = _mm(merged, p["w_out"], res=x, name="mm_out")
    saved = dict(x=x, h=h, proj=proj, xbc=xbc, dt=dt, cum=cum, cumt=cumt, y=y, hs=hs, ssm_n=ssm_n, pool_o=pool_o, o=o,
                 lt=lt, sb_o=sb_o, ys=ys, merged=merged)
    return x_next, saved


def _layer_bwd(dx_out, p, sv, dm):
    g = {}
    proj = sv["proj"]
    dmerged = _mm(dx_out, p["w_out"], tb=True, name="mm_dmerged")
    g["w_out"] = _mm(sv["merged"], dx_out, ta=True, name="mm_dwout")
    dy0, dy1, dy2, dl0, dl1, dl2 = _merge_bwd(proj, sv["ys"], dmerged, dm)
    d_ssm_n = _mm(dy0, p["w_proj_ssm"], tb=True, name="mm_dssm")
    g["w_proj_ssm"] = _mm(sv["ssm_n"], dy0, ta=True, name="mm_dwps")
    d_pool_o = _mm(dy1, p["w_proj_pool"], tb=True, name="mm_dpool")
    g["w_proj_pool"] = _mm(sv["pool_o"], dy1, ta=True, name="mm_dwpp")
    d_sb_o = _mm(dy2, p["w_proj_sb"], tb=True, name="mm_dsb")
    g["w_proj_sb"] = _mm(sv["sb_o"], dy2, ta=True, name="mm_dwpb")
    dy, dz, g["ssm_norm_w"] = _gnorm_bwd(d_ssm_n, sv["y"], proj, p["ssm_norm_w"], dm)
    dxs, db, dc, ddt, dcum, dcum_t, dd = _ssd_bwd(sv["xbc"], sv["dt"], sv["cum"], sv["cumt"], p["d_skip"], sv["hs"], dy, dm)
    g["d_skip"] = dd
    d_dtraw, g["dt_bias"], g["a_log"] = _dt_bwd(proj, sv["dt"], dcum, dcum_t, ddt, p["dt_bias"], p["a_log"], dm)
    dxbc, g["conv_w"], g["conv_b"] = _conv_bwd(proj, jnp.concatenate([dxs, db, dc], axis=1), p["conv_w"], p["conv_b"], dm)
    dpu, dpg, g["pool_w"], g["pool_scale"] = _pool_bwd(proj, d_pool_o, p["pool_w"], p["pool_scale"], dm)
    dq, dk, dv, dsg = _sb_bwd(proj, sv["o"], sv["lt"], d_sb_o, dm)
    dproj = jnp.concatenate([dz, dxbc, dpu, dpg, dq, dk, dv, dsg, dl0, dl1, dl2, d_dtraw,
                             jnp.zeros((dm.T, DT_PAD - LANES), ACT_DTYPE)], axis=1)
    dh = _mm(dproj, p["w_in"], tb=True, name="mm_dh")
    g["w_in"] = _mm(sv["h"], dproj, ta=True, name="mm_dwin")
    dx, g["norm_w"] = _rms_bwd(sv["x"], dh, dx_out, p["norm_w"], dm)
    return dx, g


def _local_step(x, tgt, layers, final_norm_w, dm, on_grads=None):
    saved = []
    layers = list(layers)
    for l, p in enumerate(layers):
        if callable(p):
            p, x = p(x)
            layers[l] = p
        x, sv = _layer_fwd(x, p, dm)
        saved.append(sv)
    dx, dfn, loss = _loss_head(x, tgt, final_norm_w, dm)
    grads = [None] * len(layers)
    for l in reversed(range(len(layers))):
        dx, grads[l] = _layer_bwd(dx, layers[l], saved[l], dm)
        if on_grads is not None:
            dx = on_grads(l, grads, dx)
    return loss, dx, grads, dfn


def _row_tile(rows, cols):
    cap = max(8, (1 << 18) // cols)
    for t in (1024, 512, 256, 128, 64, 32, 16, 8):
        if t <= cap and rows % t == 0:
            return t
    return rows


def _adamw(w, g, m, v, name):
    rows, cols = w.shape
    tr = _row_tile(rows, cols)
    c1 = 1.0 - ADAM_B1 ** ADAM_STEP
    c2 = 1.0 - ADAM_B2 ** ADAM_STEP

    def body(w_ref, g_ref, m_ref, v_ref, d_ref, mo_ref, vo_ref):
        gv = g_ref[...]
        mn = ADAM_B1 * m_ref[...] + (1.0 - ADAM_B1) * gv
        vn = ADAM_B2 * v_ref[...] + (1.0 - ADAM_B2) * (gv * gv)
        d_ref[...] = -ADAM_LR * ((mn / c1) / (jnp.sqrt(vn / c2) + ADAM_EPS) + ADAM_WD * w_ref[...])
        mo_ref[...] = mn
        vo_ref[...] = vn

    blk = pl.BlockSpec((tr, cols), lambda i: (i, 0))
    out = jax.ShapeDtypeStruct((rows, cols), f32)
    return pl.pallas_call(body, name=name, grid=(rows // tr,), in_specs=[blk] * 4, out_specs=[blk] * 3, out_shape=[out] * 3,
                          compiler_params=_cp())(w, g, m, v)


def _sum_parts(a, parts, name):
    n, rows, cols = parts.shape
    tr = _row_tile(rows, cols)

    def body(a_ref, p_ref, o_ref):
        acc = a_ref[...]
        for k in range(n):
            acc = acc + p_ref[k].astype(f32)
        o_ref[...] = acc

    blk = pl.BlockSpec((tr, cols), lambda i: (i, 0))
    return pl.pallas_call(body, name=name, grid=(rows // tr,), in_specs=[blk, pl.BlockSpec((n, tr, cols), lambda i: (0, i, 0))],
                          out_specs=blk, out_shape=jax.ShapeDtypeStruct((rows, cols), f32), compiler_params=_cp())(a, parts)


ICI_KINDS = ("y", "x", "xy")
ASYNC_GATHER_ID = 1
ASYNC_IDS = {"rs_pair_late": 2, "rs_ici_late": 3, "rs_sibling_late": 4}
HBM_SPEC = pl.BlockSpec(memory_space=pltpu.HBM)


def _me():
    return lax.axis_index("x"), lax.axis_index("y"), lax.axis_index("c")


def _peer(kind):
    x, y, c = _me()
    return {"c": (x, y, 1 - c), "y": (x, 1 - y, c), "x": (1 - x, y, c), "xy": (1 - x, 1 - y, c)}[kind]


def _peer_chip(kind):
    x, y, _ = _me()
    return {"y": 2 * x + (1 - y), "x": 2 * (1 - x) + y, "xy": 2 * (1 - x) + (1 - y)}[kind]


def _exchange(sends, kinds, name):
    n, na = len(kinds), len(sends)

    def body(*refs):
        _exchange_copies(refs[:na], refs[na:2 * na], *refs[2 * na:], kinds)

    return pl.pallas_call(
        body, name=name, out_shape=[jax.ShapeDtypeStruct(a.shape, a.dtype) for a in sends], in_specs=[HBM_SPEC] * na,
        out_specs=[HBM_SPEC] * na,
        scratch_shapes=[pltpu.SemaphoreType.DMA((na * n,)), pltpu.SemaphoreType.DMA((na * n,))],
    )(*sends)


def _exchange_copies(srcs, dsts, ssem, rsem, kinds):
    n = len(kinds)
    cps = [pltpu.make_async_remote_copy(src_ref=srcs[i].at[k], dst_ref=dsts[i].at[k], send_sem=ssem.at[i * n + k],
                                        recv_sem=rsem.at[i * n + k], device_id=_peer(kind), device_id_type=MESH)
           for i in range(len(srcs)) for k, kind in enumerate(kinds)]
    for cp in cps:
        cp.start()
    for cp in cps:
        cp.wait()


def _exchange_async(sends, kinds, name):
    n, na = len(kinds), len(sends)
    srcs = [jax.new_ref(a, memory_space=pltpu.MemorySpace.HBM) for a in sends]
    dsts = [jax.empty_ref(jax.ShapeDtypeStruct(a.shape, a.dtype), memory_space=pltpu.MemorySpace.HBM) for a in sends]

    @pl.kernel(mesh=plsc.ScalarSubcoreMesh(axis_name="sequencer", num_cores=1), name=name,
               scratch_types=(pltpu.SemaphoreType.DMA((na * n,)), pltpu.SemaphoreType.DMA((na * n,))),
               compiler_params=pltpu.CompilerParams(collective_id=ASYNC_IDS[name]))
    def launch(ssem, rsem):
        barrier = pltpu.get_barrier_semaphore()
        for kind in kinds:
            pl.semaphore_signal(barrier, inc=1, device_id=_peer(kind), device_id_type=MESH)
        pl.semaphore_wait(barrier, n)
        _exchange_copies(srcs, dsts, ssem, rsem, kinds)

    launch()
    return [r[...] for r in dsts]


def _allgather_copies(srcs, outs, ssem, rsem, lsem, lh):
    na = len(srcs)
    x, y, c = _me()
    j_me = 2 * x + y
    mine = pl.ds(c * lh, lh)
    theirs = pl.ds((1 - c) * lh, lh)
    local = [pltpu.make_async_copy(srcs[i], outs[i].at[j_me], lsem.at[i]) for i in range(na)]
    for cp in local:
        cp.start()

    def ici(i, k, kind, j_src):
        return pltpu.make_async_remote_copy(src_ref=srcs[i].at[mine], dst_ref=outs[i].at[j_src, mine],
                                            send_sem=ssem.at[6 * i + k], recv_sem=rsem.at[6 * i + k],
                                            device_id=_peer(kind), device_id_type=MESH)

    def d2d(i, k, j_src, half):
        return pltpu.make_async_remote_copy(src_ref=outs[i].at[j_src, half], dst_ref=outs[i].at[j_src, half],
                                            send_sem=ssem.at[6 * i + 3 + k], recv_sem=rsem.at[6 * i + 3 + k],
                                            device_id=_peer("c"), device_id_type=MESH)

    first = [ici(i, k, kind, j_me) for i in range(na) for k, kind in enumerate(ICI_KINDS)]
    for cp in first:
        cp.start()
    passed = []
    for i in range(na):
        for k, kind in enumerate(ICI_KINDS):
            ici(i, k, kind, _peer_chip(kind)).wait_recv()
            fwd = d2d(i, k, _peer_chip(kind), mine)
            fwd.start()
            passed.append(fwd)
    for i in range(na):
        for k, kind in enumerate(ICI_KINDS):
            d2d(i, k, _peer_chip(kind), theirs).wait_recv()
    for cp in first + passed:
        cp.wait_send()
    for cp in local:
        cp.wait()


def _allgather_sems(na):
    return [pltpu.SemaphoreType.DMA((6 * na,)), pltpu.SemaphoreType.DMA((6 * na,)), pltpu.SemaphoreType.DMA((na,))]


def _allgather_shards(shards):
    na = len(shards)
    lh = shards[0].shape[0] // 2

    def body(*refs):
        _allgather_copies(refs[:na], refs[na:2 * na], *refs[2 * na:], lh)

    return pl.pallas_call(
        body, name="allgather_shards", out_shape=[jax.ShapeDtypeStruct((4,) + a.shape, a.dtype) for a in shards],
        in_specs=[HBM_SPEC] * na, out_specs=[HBM_SPEC] * na, scratch_shapes=_allgather_sems(na),
    )(*shards)


def _allgather_shards_async(shards):
    na = len(shards)
    lh = shards[0].shape[0] // 2
    srcs = [jax.new_ref(a, memory_space=pltpu.MemorySpace.HBM) for a in shards]
    outs = [jax.empty_ref(jax.ShapeDtypeStruct((4,) + a.shape, a.dtype), memory_space=pltpu.MemorySpace.HBM) for a in shards]

    @pl.kernel(mesh=plsc.ScalarSubcoreMesh(axis_name="sequencer", num_cores=1), name="allgather_shards_async",
               scratch_types=tuple(_allgather_sems(na)), compiler_params=pltpu.CompilerParams(collective_id=ASYNC_GATHER_ID))
    def launch(ssem, rsem, lsem):
        barrier = pltpu.get_barrier_semaphore()
        peers = ("c",) + ICI_KINDS
        for kind in peers:
            pl.semaphore_signal(barrier, inc=1, device_id=_peer(kind), device_id_type=MESH)
        pl.semaphore_wait(barrier, len(peers))
        _allgather_copies(srcs, outs, ssem, rsem, lsem, lh)

    launch()
    return [r[...] for r in outs]


def _allreduce_small(v):
    rows = v.shape[0]

    def body(v_ref, o_ref, buf, ssem, rsem):
        x, y, c = _me()
        me = 4 * x + 2 * y + c
        buf[0] = v_ref[...]
        cps = []
        for k in range(1, 8):
            peer = (1 - x if k & 4 else x, 1 - y if k & 2 else y, 1 - c if k & 1 else c)
            cps.append(pltpu.make_async_remote_copy(src_ref=v_ref, dst_ref=buf.at[k], send_sem=ssem.at[k - 1], recv_sem=rsem.at[k - 1],
                                                    device_id=peer, device_id_type=MESH))
        for cp in cps:
            cp.start()
        for cp in cps:
            cp.wait()
        acc = buf[jnp.bitwise_xor(me, 0)]
        for d in range(1, 8):
            acc = acc + buf[jnp.bitwise_xor(me, d)]
        o_ref[...] = acc

    vm = pl.BlockSpec(memory_space=pltpu.VMEM)
    return pl.pallas_call(
        body, name="allreduce_small", out_shape=jax.ShapeDtypeStruct(v.shape, f32), in_specs=[vm], out_specs=vm,
        scratch_shapes=[pltpu.VMEM((8, rows, LANES), f32), pltpu.SemaphoreType.DMA((7,)), pltpu.SemaphoreType.DMA((7,))],
    )(v)


SHARDED = ("w_in", "w_proj_ssm", "w_proj_pool", "w_proj_sb", "w_out", "pool_w", "conv_w")
REST = SHARDED[1:6]
SMALL = ("norm_w", "conv_b", "dt_bias", "a_log", "d_skip", "ssm_norm_w", "pool_scale")


def _pack_rows(dm):
    rows = dict(w_proj_ssm=dm.W2 // 4, w_proj_pool=dm.D // 4, w_proj_sb=dm.D // 4, w_out=dm.D // 4, pool_w=dm.D // 16)
    assert all(r % 16 == 0 for r in rows.values())
    return rows


def _pack(sh, dm, dtype):
    rows = _pack_rows(dm)
    ly = sh["w_out"].shape[0]
    return jnp.concatenate([sh[n].astype(dtype).reshape(ly, rows[n], dm.D) for n in REST], axis=1)


def _unpack(pk, dm):
    rows = _pack_rows(dm)
    lead = pk.shape[:-2]
    shapes = dict(w_proj_ssm=(dm.W2 // 4, dm.D), w_proj_pool=(dm.D // 4, dm.D), w_proj_sb=(dm.D // 4, dm.D),
                  w_out=(dm.D // 4, dm.D), pool_w=(POOL_GROUPS, dm.Dg // 4, dm.Dg))
    out, r0 = {}, 0
    for n in REST:
        out[n] = pk[..., r0:r0 + rows[n], :].reshape(lead + shapes[n])
        r0 += rows[n]
    return out


def _join_shards(sh, dm):
    full = {}
    w = jnp.moveaxis(sh["w_in"], 0, 2)
    full["w_in"] = _permute_cols(w.reshape(w.shape[0], dm.D, dm.IN_COLS), dm)
    for n in ("w_proj_ssm", "w_proj_pool", "w_proj_sb", "w_out"):
        a = jnp.moveaxis(sh[n], 0, 1)
        full[n] = a.reshape(a.shape[0], -1, dm.D)
    a = jnp.moveaxis(sh["pool_w"], 0, 2)
    full["pool_w"] = a.reshape(a.shape[0], POOL_GROUPS, dm.Dg, dm.Dg)
    a = jnp.moveaxis(sh["conv_w"], 0, 2)
    full["conv_w"] = a.reshape(a.shape[0], CONV_WIDTH, dm.CC)
    return full


def _split_shards(full, dm):
    sh = {}
    w = _unpermute_cols(full["w_in"], dm)
    sh["w_in"] = jnp.moveaxis(w.reshape(w.shape[0], dm.D, 4, dm.IN_COLS // 4), 2, 0)
    for n in ("w_proj_ssm", "w_proj_pool", "w_proj_sb", "w_out"):
        a = full[n]
        sh[n] = jnp.moveaxis(a.reshape(a.shape[0], 4, a.shape[1] // 4, dm.D), 1, 0)
    a = full["pool_w"]
    sh["pool_w"] = jnp.moveaxis(a.reshape(a.shape[0], POOL_GROUPS, 4, dm.Dg // 4, dm.Dg), 2, 0)
    a = full["conv_w"]
    sh["conv_w"] = jnp.moveaxis(a.reshape(a.shape[0], CONV_WIDTH, 4, dm.CC // 4), 2, 0)
    return sh


def _sum_like(mine, parts, name):
    cols = mine.shape[-1]
    return _sum_parts(mine.reshape(-1, cols), parts.reshape(parts.shape[0], -1, cols), name).reshape(mine.shape)


def _slab(a, j):
    return lax.dynamic_index_in_dim(a, j, 0, keepdims=False)


class _ReduceScatter:
    TAGS = ("in", "rest", "conv")

    def __init__(self, full_grads, dm, exchange, suffix=""):
        x, y, self.c = _me()
        self.j_me, self.dm, self.exchange, self.suffix = 2 * x + y, dm, exchange, suffix
        sh = _split_shards(full_grads, dm)
        by_chip = [sh["w_in"], jnp.stack([_pack({n: sh[n][j] for n in REST}, dm, f32) for j in range(4)]), sh["conv_w"]]
        lh = by_chip[0].shape[1] // 2
        self.mine = [lax.dynamic_slice_in_dim(a, self.c * lh, lh, axis=1) for a in by_chip]
        theirs = [lax.dynamic_slice_in_dim(a, (1 - self.c) * lh, lh, axis=1).astype(WIRE_DTYPE)[None] for a in by_chip]
        self.got = exchange(theirs, ("c",), "rs_pair" + suffix)

    def pair_sum(self, dep=None):
        got = self.got
        if dep is not None:
            got, dep = lax.optimization_barrier((got, dep))
        self.s1 = [_sum_like(a, g, "rs_pair_sum_" + t) for a, g, t in zip(self.mine, got, self.TAGS)]
        send = [jnp.stack([_slab(a, jnp.bitwise_xor(self.j_me, k)) for k in (1, 2, 3)]).astype(WIRE_DTYPE) for a in self.s1]
        self.got = self.exchange(send, ICI_KINDS, "rs_ici" + self.suffix)
        return dep

    def ici_sum(self, dep=None):
        got = self.got
        if dep is not None:
            got, dep = lax.optimization_barrier((got, dep))
        self.red = [_sum_like(_slab(a, self.j_me), g, "rs_ici_sum_" + t) for a, g, t in zip(self.s1, got, self.TAGS)]
        self.sib = self.exchange([r[None] for r in self.red], ("c",), "rs_sibling" + self.suffix)
        return dep

    def result(self):
        both = [jnp.where(self.c == 0, jnp.concatenate([r, s[0]]), jnp.concatenate([s[0], r])) for r, s in zip(self.red, self.sib)]
        out = _unpack(both[1], self.dm)
        out["w_in"], out["conv_w"] = both[0], both[2]
        return out


def _reduce_scatter(full_grads, dm):
    rs = _ReduceScatter(full_grads, dm, _exchange)
    rs.pair_sum()
    rs.ici_sum()
    return rs.result()


def _flatten_small(parts):
    flat = jnp.concatenate([p.reshape(-1).astype(f32) for p in parts])
    rows = -(-flat.shape[0] // (8 * LANES)) * 8
    return jnp.pad(flat, (0, rows * LANES - flat.shape[0])).reshape(rows, LANES)


def _unflatten_small(buf, shapes):
    flat = buf.reshape(-1)
    out, o = [], 0
    for s in shapes:
        n = math.prod(s)
        out.append(flat[o:o + n].reshape(s))
        o += n
    return out


def kernel(x, norm_w, w_in, conv_w, conv_b, dt_bias, a_log, d_skip, ssm_norm_w, pool_w, pool_scale, w_proj_ssm, w_proj_pool, w_proj_sb, w_out, final_norm_w, loss_target, m_norm_w, m_w_in, m_conv_w, m_conv_b, m_dt_bias, m_a_log, m_d_skip, m_ssm_norm_w, m_pool_w, m_pool_scale, m_w_proj_ssm, m_w_proj_pool, m_w_proj_sb, m_w_out, m_final_norm_w, v_norm_w, v_w_in, v_conv_w, v_conv_b, v_dt_bias, v_a_log, v_d_skip, v_ssm_norm_w, v_pool_w, v_pool_scale, v_w_proj_ssm, v_w_proj_pool, v_w_proj_sb, v_w_out, v_final_norm_w):
    names = ("norm_w", "w_in", "conv_w", "conv_b", "dt_bias", "a_log", "d_skip", "ssm_norm_w", "pool_w", "pool_scale",
             "w_proj_ssm", "w_proj_pool", "w_proj_sb", "w_out", "final_norm_w")
    w = dict(zip(names, (norm_w, w_in, conv_w, conv_b, dt_bias, a_log, d_skip, ssm_norm_w, pool_w, pool_scale, w_proj_ssm,
                         w_proj_pool, w_proj_sb, w_out, final_norm_w)))
    m = dict(zip(names, (m_norm_w, m_w_in, m_conv_w, m_conv_b, m_dt_bias, m_a_log, m_d_skip, m_ssm_norm_w, m_pool_w, m_pool_scale,
                         m_w_proj_ssm, m_w_proj_pool, m_w_proj_sb, m_w_out, m_final_norm_w)))
    v = dict(zip(names, (v_norm_w, v_w_in, v_conv_w, v_conv_b, v_dt_bias, v_a_log, v_d_skip, v_ssm_norm_w, v_pool_w, v_pool_scale,
                         v_w_proj_ssm, v_w_proj_pool, v_w_proj_sb, v_w_out, v_final_norm_w)))
    bl, s, d = x.shape
    dm = Dims(bl, s, d)
    ly = norm_w.shape[0]

    mine = [w["w_in"].astype(WIRE_DTYPE), _pack({n: w[n] for n in REST}, dm, WIRE_DTYPE), conv_w]
    if ly % 4 == 0:
        gathered = [_allgather_shards([a[:ly // 2] for a in mine])]
        later = lax.optimization_barrier(([a[ly // 2:] for a in mine], gathered[0]))[0]
        gathered.append(_allgather_shards_async(later))
    else:
        gathered = [_allgather_shards(mine)]
    def layer_dicts(part, first):
        g_in, g_rest, g_conv = part
        shards = _unpack(g_rest, dm)
        shards["w_in"], shards["conv_w"] = g_in, g_conv
        full = _join_shards(shards, dm)
        return [{**{n: full[n][l] for n in SHARDED}, **{n: w[n][first + l] for n in SMALL}} for l in range(g_in.shape[1])]

    layers = layer_dicts(gathered[0], 0)
    if len(gathered) > 1:
        late = {}

        def late_layer(x, l):
            if not late:
                part, x = lax.optimization_barrier((gathered[1], x))
                late["dicts"] = layer_dicts(part, ly // 2)
            return late["dicts"][l], x

        layers += [functools.partial(late_layer, l=l) for l in range(ly - ly // 2)]

    def stacked(grads, lo_l, hi_l):
        return {n: jnp.stack([grads[l][n] for l in range(lo_l, hi_l)]) for n in SHARDED}

    late = {}

    def on_grads(l, grads, dx):
        half = ly // 2
        if l == half:
            late["rs"] = _ReduceScatter(stacked(grads, half, ly), dm, _exchange_async, "_late")
        elif l == half - 1:
            dx = late["rs"].pair_sum(dx)
        elif l == half - 2:
            dx = late["rs"].ici_sum(dx)
        return dx

    loss_part, dx, grads, dfn = _local_step(x.reshape(dm.T, d), loss_target.reshape(dm.T, d), layers, final_norm_w, dm,
                                            on_grads if ly % 4 == 0 else None)

    h = dm.H
    small_parts = [loss_part]
    small_shapes = [(1, LANES)]
    for n in SMALL:
        if n in ("dt_bias", "a_log", "d_skip"):
            small_parts.append(jnp.stack([g[n][0, :h] for g in grads]))
        else:
            small_parts.append(jnp.stack([g[n][0] for g in grads]))
        small_shapes.append(w[n].shape)
    small_parts.append(dfn[0])
    small_shapes.append(final_norm_w.shape)
    red_small = _allreduce_small(_flatten_small(small_parts))
    small_g = _unflatten_small(red_small, small_shapes)
    loss = small_g[0][0, 0]
    g_out = dict(zip(SMALL + ("final_norm_w",), small_g[1:]))

    if late:
        early = _reduce_scatter(stacked(grads, 0, ly // 2), dm)
        rest = late["rs"].result()
        g_out.update({n: jnp.concatenate([early[n], rest[n]]) for n in SHARDED})
    else:
        g_out.update(_reduce_scatter(stacked(grads, 0, ly), dm))

    small_names = SMALL + ("final_norm_w",)
    zero_row = jnp.zeros((1, LANES), f32)
    pack_small = lambda t: _flatten_small([zero_row] + [t[n] for n in small_names])
    ds, ms, vs = _adamw(pack_small(w), red_small, pack_small(m), pack_small(v), "adamw_small")
    delta, new_m, new_v = {}, {}, {}
    for tgt, buf in ((delta, ds), (new_m, ms), (new_v, vs)):
        tgt.update(zip(small_names, _unflatten_small(buf, small_shapes)[1:]))
    for n in SHARDED:
        shp = w[n].shape
        two = (math.prod(shp[:-1]), shp[-1])
        dd, mm, vv = _adamw(w[n].reshape(two), g_out[n].reshape(two), m[n].reshape(two), v[n].reshape(two), "adamw_" + n)
        delta[n], new_m[n], new_v[n] = dd.reshape(shp), mm.reshape(shp), vv.reshape(shp)
        g_out[n] = g_out[n].reshape(shp)

    return (loss, dx.reshape(bl, s, d), *[g_out[n] for n in names], *[delta[n] for n in names],
            *[new_m[n] for n in names], *[new_v[n] for n in names])
```

```python
import functools
import math

import jax
import jax.numpy as jnp
from jax import lax
from jax.experimental import pallas as pl
from jax.experimental.pallas import tpu as pltpu
from jax.experimental.pallas import tpu_sc as plsc

f32 = jnp.float32
bf16 = jnp.bfloat16
MXU_DTYPE = jnp.bfloat16
ACT_DTYPE = jnp.bfloat16
WIRE_DTYPE = jnp.bfloat16

EPS = 1e-6
LANES = 128
HEAD = 64
SSM_GROUPS = 2
CONV_WIDTH = 4
POOL_GROUPS = 4
DT_PAD = 512
N_BRANCHES = 3
VMEM_LIMIT = 56 * 1024 * 1024

ADAM_LR, ADAM_B1, ADAM_B2, ADAM_EPS, ADAM_WD, ADAM_STEP = 0.001, 0.9, 0.999, 1e-08, 0.01, 10
MESH = pl.DeviceIdType.MESH


def _cp(**kw):
    return pltpu.CompilerParams(vmem_limit_bytes=VMEM_LIMIT, **kw)


def _pick(n, prefs):
    for p in prefs:
        if n % p == 0:
            return p
    return n


def _dg(a, b, ca, cb):
    return lax.dot_general(a, b, (((ca,), (cb,)), ((), ())), preferred_element_type=f32)


def _nn(a, b):
    return _dg(a, b, 1, 0)


def _nt(a, b):
    return _dg(a, b, 1, 1)


def _tn(a, b):
    return _dg(a, b, 0, 0)


def _mx(a):
    return a.astype(MXU_DTYPE)


def _split3(a):
    hi = a.astype(bf16)
    r = a - hi.astype(f32)
    mid = r.astype(bf16)
    lo = (r - mid.astype(f32)).astype(bf16)
    return hi, mid, lo


def _exact_nn(a, u):
    hi, mid, lo = _split3(a)
    return _nn(hi, u) + _nn(mid, u) + _nn(lo, u)


def _iota2(shape, dim):
    return lax.broadcasted_iota(jnp.int32, shape, dim)


def _sigmoid(x):
    return 1.0 / (1.0 + jnp.exp(-x))


def _shift_down(v, sh, row):
    return jnp.where(row >= sh, pltpu.roll(v, sh, 0), 0.0)


def _shift_up(v, sh, row):
    n = v.shape[0]
    return jnp.where(row < n - sh, pltpu.roll(v, n - sh, 0), 0.0)


class Dims:
    def __init__(self, bl, s, d):
        self.Bl, self.S, self.D = bl, s, d
        self.T = bl * s
        self.W2 = 2 * d
        self.H = self.W2 // HEAD
        self.hpg = self.H // SSM_GROUPS
        self.CC = self.W2 + 2 * SSM_GROUPS * LANES
        self.Dg = d // POOL_GROUPS
        self.nc = s // LANES
        self.o_dt = self.W2 + self.CC
        self.IN_COLS = 13 * d + 2 * SSM_GROUPS * LANES + self.H
        self.c_z = 0
        self.c_xbc = self.W2
        self.c_pu = self.W2 + self.CC
        self.c_pg = self.c_pu + d
        self.c_qkv = self.c_pg + d
        self.c_sbg = self.c_qkv + 3 * d
        self.c_mrg = self.c_sbg + d
        self.c_dt = self.c_mrg + 3 * d
        self.NP = self.c_dt + DT_PAD
        assert self.c_dt == self.IN_COLS - self.H
        assert s % LANES == 0 and d % 512 == 0 and self.H <= LANES


def _permute_cols(w, dm):
    pad = jnp.zeros(w.shape[:-1] + (DT_PAD - dm.H,), w.dtype)
    return jnp.concatenate([w[..., :dm.o_dt], w[..., dm.o_dt + dm.H:], w[..., dm.o_dt:dm.o_dt + dm.H], pad], axis=-1)


def _unpermute_cols(w, dm):
    return jnp.concatenate([w[..., :dm.o_dt], w[..., dm.c_dt:dm.c_dt + dm.H], w[..., dm.o_dt:dm.c_dt]], axis=-1)


def _mm(a, b, *, ta=False, tb=False, out_dtype=f32, res=None, name):
    M, K = (a.shape[1], a.shape[0]) if ta else a.shape
    N = b.shape[0] if tb else b.shape[1]
    tm = _pick(M, (1024, 512, 256, 128))
    tn = _pick(N, (2048, 1024, 512, 256, 128))
    tk = _pick(K, (2048, 1024, 512, 256, 128) if tb else (1024, 512, 256, 128))
    nk = K // tk

    def body(*refs):
        if res is None:
            a_ref, b_ref, o_ref, acc = refs
        else:
            a_ref, b_ref, r_ref, o_ref, acc = refs
        k = pl.program_id(2)

        @pl.when(k == 0)
        def _():
            acc[...] = jnp.zeros_like(acc)

        acc[...] += _dg(_mx(a_ref[...]), _mx(b_ref[...]), 0 if ta else 1, 1 if tb else 0)

        @pl.when(k == nk - 1)
        def _():
            v = acc[...]
            if res is not None:
                v = v + r_ref[...]
            o_ref[...] = v.astype(out_dtype)

    a_spec = pl.BlockSpec((tk, tm), lambda i, j, k: (k, i)) if ta else pl.BlockSpec((tm, tk), lambda i, j, k: (i, k))
    b_spec = pl.BlockSpec((tn, tk), lambda i, j, k: (j, k)) if tb else pl.BlockSpec((tk, tn), lambda i, j, k: (k, j))
    o_spec = pl.BlockSpec((tm, tn), lambda i, j, k: (i, j))
    in_specs = [a_spec, b_spec] + ([o_spec] if res is not None else [])
    args = (a, b) + ((res,) if res is not None else ())
    return pl.pallas_call(
        body, name=name, grid=(M // tm, N // tn, nk), in_specs=in_specs, out_specs=o_spec,
        out_shape=jax.ShapeDtypeStruct((M, N), out_dtype), scratch_shapes=[pltpu.VMEM((tm, tn), f32)],
        compiler_params=_cp(dimension_semantics=("parallel", "parallel", "arbitrary")),
    )(*args)


def _rms_fwd(x, w, dm):
    tr = _pick(dm.T, (256, 128))

    def body(x_ref, w_ref, o_ref):
        xf = x_ref[...]
        r = lax.rsqrt(jnp.mean(xf * xf, axis=-1, keepdims=True) + EPS)
        o_ref[...] = (xf * r * w_ref[...]).astype(o_ref.dtype)

    return pl.pallas_call(
        body, name="rms_fwd", grid=(dm.T // tr,),
        in_specs=[pl.BlockSpec((tr, dm.D), lambda i: (i, 0)), pl.BlockSpec((1, dm.D), lambda i: (0, 0))],
        out_specs=pl.BlockSpec((tr, dm.D), lambda i: (i, 0)),
        out_shape=jax.ShapeDtypeStruct((dm.T, dm.D), ACT_DTYPE), compiler_params=_cp(),
    )(x, w.reshape(1, dm.D))


def _rms_bwd(x, dh, dres, w, dm):
    tr = _pick(dm.T, (256, 128))

    def body(x_ref, dh_ref, dr_ref, w_ref, dx_ref, dw_ref):
        @pl.when(pl.program_id(0) == 0)
        def _():
            dw_ref[...] = jnp.zeros_like(dw_ref)

        xf = x_ref[...]
        r = lax.rsqrt(jnp.mean(xf * xf, axis=-1, keepdims=True) + EPS)
        xh = xf * r
        dh_ = dh_ref[...]
        dxh = dh_ * w_ref[...]
        dx_ref[...] = dr_ref[...] + r * (dxh - xh * jnp.mean(dxh * xh, axis=-1, keepdims=True))
        dw_ref[...] += jnp.sum(dh_ * xh, axis=0, keepdims=True)

    row = pl.BlockSpec((tr, dm.D), lambda i: (i, 0))
    vec = pl.BlockSpec((1, dm.D), lambda i: (0, 0))
    return pl.pallas_call(
        body, name="rms_bwd", grid=(dm.T // tr,), in_specs=[row, row, row, vec], out_specs=[row, vec],
        out_shape=[jax.ShapeDtypeStruct((dm.T, dm.D), f32), jax.ShapeDtypeStruct((1, dm.D), f32)],
        compiler_params=_cp(dimension_semantics=("arbitrary",)),
    )(x, dh, dres, w.reshape(1, dm.D))


def _loss_head(x, tgt, w, dm):
    tr = _pick(dm.T, (256, 128))

    def body(x_ref, t_ref, w_ref, dx_ref, dw_ref, ls_ref):
        @pl.when(pl.program_id(0) == 0)
        def _():
            dw_ref[...] = jnp.zeros_like(dw_ref)
            ls_ref[...] = jnp.zeros_like(ls_ref)

        xf = x_ref[...]
        r = lax.rsqrt(jnp.mean(xf * xf, axis=-1, keepdims=True) + EPS)
        xh = xf * r
        err = xh * w_ref[...] - t_ref[...]
        per_tok = jnp.mean(err * err, axis=-1, keepdims=True)
        ls_ref[...] += 0.5 * jnp.sum(per_tok, axis=0, keepdims=True)
        dy = err * (1.0 / dm.D)
        dxh = dy * w_ref[...]
        dx_ref[...] = r * (dxh - xh * jnp.mean(dxh * xh, axis=-1, keepdims=True))
        dw_ref[...] += jnp.sum(dy * xh, axis=0, keepdims=True)

    row = pl.BlockSpec((tr, dm.D), lambda i: (i, 0))
    vec = pl.BlockSpec((1, dm.D), lambda i: (0, 0))
    return pl.pallas_call(
        body, name="loss_head", grid=(dm.T // tr,), in_specs=[row, row, vec],
        out_specs=[row, vec, pl.BlockSpec((1, LANES), lambda i: (0, 0))],
        out_shape=[jax.ShapeDtypeStruct((dm.T, dm.D), f32), jax.ShapeDtypeStruct((1, dm.D), f32),
                   jax.ShapeDtypeStruct((1, LANES), f32)],
        compiler_params=_cp(dimension_semantics=("arbitrary",)),
    )(x, tgt, w.reshape(1, dm.D))


def _conv_pre(u, w_ref, b_ref, row):
    acc = b_ref[...] + w_ref[CONV_WIDTH - 1:CONV_WIDTH, :] * u
    for k in range(CONV_WIDTH - 1):
        acc = acc + w_ref[k:k + 1, :] * _shift_down(u, CONV_WIDTH - 1 - k, row)
    return acc


def _conv_fwd(proj, cw, cb, dm):
    cwid = LANES
    off = dm.c_xbc // cwid

    def body(u_ref, w_ref, b_ref, o_ref):
        u = u_ref[...]
        row = _iota2(u.shape, 0)
        pre = _conv_pre(u, w_ref, b_ref, row)
        o_ref[...] = pre * _sigmoid(pre)

    return pl.pallas_call(
        body, name="conv_fwd", grid=(dm.Bl, dm.CC // cwid),
        in_specs=[pl.BlockSpec((dm.S, cwid), lambda b, j: (b, off + j)),
                  pl.BlockSpec((CONV_WIDTH, cwid), lambda b, j: (0, j)), pl.BlockSpec((1, cwid), lambda b, j: (0, j))],
        out_specs=pl.BlockSpec((dm.S, cwid), lambda b, j: (b, j)),
        out_shape=jax.ShapeDtypeStruct((dm.T, dm.CC), f32), compiler_params=_cp(),
    )(proj, cw, cb.reshape(1, dm.CC))


def _conv_bwd(proj, d_out, cw, cb, dm):
    cwid = LANES
    off = dm.c_xbc // cwid

    def body(u_ref, d_ref, w_ref, b_ref, du_ref, dw_ref, db_ref):
        @pl.when(pl.program_id(1) == 0)
        def _():
            dw_ref[...] = jnp.zeros_like(dw_ref)
            db_ref[...] = jnp.zeros_like(db_ref)

        u = u_ref[...]
        row = _iota2(u.shape, 0)
        pre = _conv_pre(u, w_ref, b_ref, row)
        sg = _sigmoid(pre)
        dpre = d_ref[...] * (sg * (1.0 + pre * (1.0 - sg)))
        du = w_ref[CONV_WIDTH - 1:CONV_WIDTH, :] * dpre
        dw_ref[CONV_WIDTH - 1:CONV_WIDTH, :] += jnp.sum(dpre * u, axis=0, keepdims=True)
        for k in range(CONV_WIDTH - 1):
            sh = CONV_WIDTH - 1 - k
            du = du + w_ref[k:k + 1, :] * _shift_up(dpre, sh, row)
            dw_ref[k:k + 1, :] += jnp.sum(dpre * _shift_down(u, sh, row), axis=0, keepdims=True)
        du_ref[...] = du.astype(du_ref.dtype)
        db_ref[...] += jnp.sum(dpre, axis=0, keepdims=True)

    return pl.pallas_call(
        body, name="conv_bwd", grid=(dm.CC // cwid, dm.Bl),
        in_specs=[pl.BlockSpec((dm.S, cwid), lambda j, b: (b, off + j)), pl.BlockSpec((dm.S, cwid), lambda j, b: (b, j)),
                  pl.BlockSpec((CONV_WIDTH, cwid), lambda j, b: (0, j)), pl.BlockSpec((1, cwid), lambda j, b: (0, j))],
        out_specs=[pl.BlockSpec((dm.S, cwid), lambda j, b: (b, j)), pl.BlockSpec((CONV_WIDTH, cwid), lambda j, b: (0, j)),
                   pl.BlockSpec((1, cwid), lambda j, b: (0, j))],
        out_shape=[jax.ShapeDtypeStruct((dm.T, dm.CC), ACT_DTYPE), jax.ShapeDtypeStruct((CONV_WIDTH, dm.CC), f32),
                   jax.ShapeDtypeStruct((1, dm.CC), f32)],
        compiler_params=_cp(dimension_semantics=("arbitrary", "arbitrary")),
    )(proj, d_out, cw, cb.reshape(1, dm.CC))


def _pad_lanes(v):
    return jnp.pad(v, (0, LANES - v.shape[0])).reshape(1, LANES)


def _softplus(x):
    return jnp.maximum(x, 0.0) + jnp.log(1.0 + jnp.exp(-jnp.abs(x)))


def _dt_prep(proj, dt_bias, a_log, dm):
    off = dm.c_dt // LANES

    def body(r_ref, b_ref, al_ref, dt_ref, cum_ref, cumt_ref):
        dt = _softplus(r_ref[...] + b_ref[...])
        adt = dt * (-jnp.exp(al_ref[...]))
        tril = (_iota2((LANES, LANES), 1) <= _iota2((LANES, LANES), 0)).astype(bf16)
        cum = _exact_nn_left(tril, adt)
        dt_ref[...] = dt
        cum_ref[...] = cum
        cumt_ref[...] = cum.T

    blk = pl.BlockSpec((LANES, LANES), lambda i: (i, 0))
    vec = pl.BlockSpec((1, LANES), lambda i: (0, 0))
    return pl.pallas_call(
        body, name="dt_prep", grid=(dm.T // LANES,),
        in_specs=[pl.BlockSpec((LANES, LANES), lambda i: (i, off)), vec, vec],
        out_specs=[blk, blk, pl.BlockSpec((LANES, LANES), lambda i: (0, i))],
        out_shape=[jax.ShapeDtypeStruct((dm.T, LANES), f32), jax.ShapeDtypeStruct((dm.T, LANES), f32),
                   jax.ShapeDtypeStruct((LANES, dm.T), f32)],
        compiler_params=_cp(),
    )(proj, _pad_lanes(dt_bias), _pad_lanes(a_log))


def _exact_nn_left(u, a):
    hi, mid, lo = _split3(a)
    return _nn(u, hi) + _nn(u, mid) + _nn(u, lo)


def _dt_bwd(proj, dt, dcum, dcum_t, ddt, dt_bias, a_log, dm):
    off = dm.c_dt // LANES

    def body(r_ref, dt_ref, dc_ref, dct_ref, dd_ref, b_ref, al_ref, o_ref, db_ref, da_ref):
        @pl.when(pl.program_id(0) == 0)
        def _():
            db_ref[...] = jnp.zeros_like(db_ref)
            da_ref[...] = jnp.zeros_like(da_ref)

        a = -jnp.exp(al_ref[...])
        triu = (_iota2((LANES, LANES), 1) >= _iota2((LANES, LANES), 0)).astype(bf16)
        dadt = _exact_nn_left(triu, dc_ref[...] + dct_ref[...].T)
        d_dt = dd_ref[...] + dadt * a
        d_raw = d_dt * _sigmoid(r_ref[...] + b_ref[...])
        o_ref[...] = d_raw.astype(o_ref.dtype)
        db_ref[...] += jnp.sum(d_raw, axis=0, keepdims=True)
        da_ref[...] += jnp.sum(dadt * dt_ref[...], axis=0, keepdims=True) * a

    blk = pl.BlockSpec((LANES, LANES), lambda i: (i, 0))
    vec = pl.BlockSpec((1, LANES), lambda i: (0, 0))
    return pl.pallas_call(
        body, name="dt_bwd", grid=(dm.T // LANES,),
        in_specs=[pl.BlockSpec((LANES, LANES), lambda i: (i, off)), blk, blk, pl.BlockSpec((LANES, LANES), lambda i: (0, i)),
                  blk, vec, vec],
        out_specs=[blk, vec, vec],
        out_shape=[jax.ShapeDtypeStruct((dm.T, LANES), ACT_DTYPE), jax.ShapeDtypeStruct((1, LANES), f32),
                   jax.ShapeDtypeStruct((1, LANES), f32)],
        compiler_params=_cp(dimension_semantics=("arbitrary",)),
    )(proj, dt, dcum, dcum_t, ddt, _pad_lanes(dt_bias), _pad_lanes(a_log))


def _ssd_common(dm):
    L = LANES
    tri = _iota2((L, L), 0) >= _iota2((L, L), 1)
    lo = _iota2((L, L), 1) < HEAD
    return tri, lo


def _ssd_fwd(xbc, dt, cum, cumt, d_skip, dm):
    L, W2, hpg = LANES, dm.W2, dm.hpg
    nb = W2 // L

    def body(x_ref, b_ref, c_ref, dt_ref, cum_ref, cumt_ref, dsk_ref, y_ref, hs_ref, h_scr):
        @pl.when(pl.program_id(1) == 0)
        def _():
            h_scr[...] = jnp.zeros_like(h_scr)

        hs_ref[0] = h_scr[...]
        tri, lo = _ssd_common(dm)
        for g in range(SSM_GROUPS):
            bb = _mx(b_ref[:, g * L:(g + 1) * L])
            cb_ = _mx(c_ref[:, g * L:(g + 1) * L])
            cbm = _nt(cb_, bb)
            for i in range(hpg // 2):
                h0 = g * hpg + 2 * i
                h1 = h0 + 1
                sl = slice(h0 * HEAD, h0 * HEAD + L)
                x_p = x_ref[:, sl]
                cum0, cum1 = cum_ref[:, h0:h0 + 1], cum_ref[:, h1:h1 + 1]
                cums = jnp.where(lo, cum0, cum1)
                xdt = x_p * jnp.where(lo, dt_ref[:, h0:h0 + 1], dt_ref[:, h1:h1 + 1])
                tot = jnp.where(lo[0:1], cum_ref[L - 1:L, h0:h0 + 1], cum_ref[L - 1:L, h1:h1 + 1])
                y_p = jnp.zeros((L, L), f32)
                for hh, m in ((h0, lo), (h1, jnp.logical_not(lo))):
                    diff = cum_ref[:, hh:hh + 1] - cumt_ref[hh:hh + 1, :]
                    lm = jnp.where(tri, jnp.exp(jnp.minimum(diff, 0.0)), 0.0)
                    y_p = y_p + _nn(_mx(cbm * lm), _mx(jnp.where(m, xdt, 0.0)))
                hp = h_scr[:, sl]
                y_p = y_p + _nn(cb_, _mx(hp)) * jnp.exp(cums)
                y_p = y_p + x_p * jnp.where(lo[0:1], dsk_ref[0:1, h0:h0 + 1], dsk_ref[0:1, h1:h1 + 1])
                y_ref[:, sl] = y_p
                h_scr[:, sl] = hp * jnp.exp(tot) + _tn(bb, _mx(xdt * jnp.exp(tot - cums)))

    nc = dm.nc
    ob = W2 // (SSM_GROUPS * L)
    blk = pl.BlockSpec((L, L), lambda b, c: (b * nc + c, 0))
    return pl.pallas_call(
        body, name="ssd_fwd", grid=(dm.Bl, nc),
        in_specs=[pl.BlockSpec((L, W2), lambda b, c: (b * nc + c, 0)),
                  pl.BlockSpec((L, SSM_GROUPS * L), lambda b, c: (b * nc + c, ob)),
                  pl.BlockSpec((L, SSM_GROUPS * L), lambda b, c: (b * nc + c, ob + 1)),
                  blk, blk, pl.BlockSpec((L, L), lambda b, c: (0, b * nc + c)), pl.BlockSpec((1, L), lambda b, c: (0, 0))],
        out_specs=[pl.BlockSpec((L, W2), lambda b, c: (b * nc + c, 0)), pl.BlockSpec((1, L, W2), lambda b, c: (b * nc + c, 0, 0))],
        out_shape=[jax.ShapeDtypeStruct((dm.T, W2), f32), jax.ShapeDtypeStruct((dm.Bl * nc, L, W2), f32)],
        scratch_shapes=[pltpu.VMEM((L, W2), f32)],
        compiler_params=_cp(dimension_semantics=("arbitrary", "arbitrary")),
    )(xbc, xbc, xbc, dt, cum, cumt, _pad_lanes(d_skip))


def _ssd_bwd(xbc, dt, cum, cumt, d_skip, hs, dy, dm):
    L, W2, hpg = LANES, dm.W2, dm.hpg
    nc = dm.nc

    def body(x_ref, b_ref, c_ref, dt_ref, cum_ref, cumt_ref, dsk_ref, hs_ref, dy_ref,
             dx_ref, db_ref, dc_ref, ddt_ref, dcum_ref, dcr_ref, dd_ref, dh_scr, lane_cum, lane_dt, lane_d):
        @pl.when(pl.program_id(1) == 0)
        def _():
            dh_scr[...] = jnp.zeros_like(dh_scr)

        @pl.when((pl.program_id(0) == 0) & (pl.program_id(1) == 0))
        def _():
            dd_ref[...] = jnp.zeros_like(dd_ref)

        dcr_ref[...] = jnp.zeros_like(dcr_ref)
        tri, lo = _ssd_common(dm)
        last = _iota2((L, L), 0) == L - 1
        for g in range(SSM_GROUPS):
            gs = slice(g * L, (g + 1) * L)
            bb = _mx(b_ref[:, gs])
            cb_ = _mx(c_ref[:, gs])
            cbm = _nt(cb_, bb)
            dcb = jnp.zeros((L, L), f32)
            dc_g = jnp.zeros((L, L), f32)
            db_g = jnp.zeros((L, L), f32)
            for i in range(hpg // 2):
                h0 = g * hpg + 2 * i
                h1 = h0 + 1
                sl = slice(h0 * HEAD, h0 * HEAD + L)
                x_p = x_ref[:, sl]
                dy_p = dy_ref[:, sl]
                dt_p = jnp.where(lo, dt_ref[:, h0:h0 + 1], dt_ref[:, h1:h1 + 1])
                cums = jnp.where(lo, cum_ref[:, h0:h0 + 1], cum_ref[:, h1:h1 + 1])
                tot = jnp.where(lo[0:1], cum_ref[L - 1:L, h0:h0 + 1], cum_ref[L - 1:L, h1:h1 + 1])
                dsk_p = jnp.where(lo[0:1], dsk_ref[0:1, h0:h0 + 1], dsk_ref[0:1, h1:h1 + 1])
                xdt = x_p * dt_p
                ecum = jnp.exp(cums)
                dec = jnp.exp(tot - cums)
                etot = jnp.exp(tot)
                hp = hs_ref[0, :, sl]
                hp_b = _mx(hp)
                dhn = dh_scr[:, sl]
                dhn_b = _mx(dhn)
                y_off = _nn(cb_, hp_b) * ecum
                dch = _mx(dy_p * ecum)
                dc_g = dc_g + _nt(dch, hp_b)
                dh_off = _tn(cb_, dch)
                bds = _nn(bb, dhn_b)
                xdec = xdt * dec
                db_g = db_g + _nt(_mx(xdec), dhn_b)
                dxdt = bds * dec
                sdec = bds * xdec
                tot_lane = jnp.sum(sdec, axis=0, keepdims=True) + jnp.sum(dhn * hp, axis=0, keepdims=True) * etot
                dh_scr[:, sl] = etot * dhn + dh_off
                rsum = []
                for hh, m in ((h0, lo), (h1, jnp.logical_not(lo))):
                    diff = cum_ref[:, hh:hh + 1] - cumt_ref[hh:hh + 1, :]
                    lm = jnp.where(tri, jnp.exp(jnp.minimum(diff, 0.0)), 0.0)
                    w32 = cbm * lm
                    dyh = _mx(jnp.where(m, dy_p, 0.0))
                    dw = _nt(dyh, _mx(jnp.where(m, xdt, 0.0)))
                    dcb = dcb + dw * lm
                    e = dw * w32
                    rsum.append(jnp.sum(e, axis=1, keepdims=True))
                    dcr_ref[hh:hh + 1, :] = -jnp.sum(e, axis=0, keepdims=True)
                    dxdt = dxdt + _tn(_mx(w32), dyh)
                lane_cum[:, sl] = (dy_p * y_off + jnp.where(lo, rsum[0], rsum[1]) * (1.0 / HEAD) - sdec
                                   + jnp.where(last, tot_lane, 0.0))
                lane_dt[:, sl] = dxdt * x_p
                lane_d[:, sl] = dy_p * x_p
                dx_ref[:, sl] = dxdt * dt_p + dsk_p * dy_p
            dcb_b = _mx(dcb)
            dc_ref[:, gs] = dc_g + _nn(dcb_b, bb)
            db_ref[:, gs] = db_g + _tn(dcb_b, cb_)
        sel = (_iota2((W2, L), 0) // HEAD == _iota2((W2, L), 1)).astype(bf16)
        dcum_ref[...] = _exact_nn(lane_cum[...], sel)
        ddt_ref[...] = _exact_nn(lane_dt[...], sel)
        dd_ref[...] += jnp.sum(_exact_nn(lane_d[...], sel), axis=0, keepdims=True)

    ob = W2 // (SSM_GROUPS * L)

    def rc(b, c):
        return b * nc + (nc - 1 - c)

    blk = pl.BlockSpec((L, L), lambda b, c: (rc(b, c), 0))
    blk_t = pl.BlockSpec((L, L), lambda b, c: (0, rc(b, c)))
    wide = pl.BlockSpec((L, W2), lambda b, c: (rc(b, c), 0))
    grp = pl.BlockSpec((L, SSM_GROUPS * L), lambda b, c: (rc(b, c), 0))
    return pl.pallas_call(
        body, name="ssd_bwd", grid=(dm.Bl, nc),
        in_specs=[wide, pl.BlockSpec((L, SSM_GROUPS * L), lambda b, c: (rc(b, c), ob)),
                  pl.BlockSpec((L, SSM_GROUPS * L), lambda b, c: (rc(b, c), ob + 1)),
                  blk, blk, blk_t, pl.BlockSpec((1, L), lambda b, c: (0, 0)),
                  pl.BlockSpec((1, L, W2), lambda b, c: (rc(b, c), 0, 0)), wide],
        out_specs=[wide, grp, grp, blk, blk, blk_t, pl.BlockSpec((1, L), lambda b, c: (0, 0))],
        out_shape=[jax.ShapeDtypeStruct((dm.T, W2), f32), jax.ShapeDtypeStruct((dm.T, SSM_GROUPS * L), f32),
                   jax.ShapeDtypeStruct((dm.T, SSM_GROUPS * L), f32), jax.ShapeDtypeStruct((dm.T, L), f32),
                   jax.ShapeDtypeStruct((dm.T, L), f32), jax.ShapeDtypeStruct((L, dm.T), f32),
                   jax.ShapeDtypeStruct((1, L), f32)],
        scratch_shapes=[pltpu.VMEM((L, W2), f32)] * 4,
        compiler_params=_cp(dimension_semantics=("arbitrary", "arbitrary")),
    )(xbc, xbc, xbc, dt, cum, cumt, _pad_lanes(d_skip), hs, dy)


def _gnorm_fwd(y, proj, w, dm):
    tr = _pick(dm.T, (256, 128))
    row = pl.BlockSpec((tr, dm.W2), lambda i: (i, 0))

    def body(y_ref, z_ref, w_ref, o_ref):
        z = z_ref[...]
        yg = y_ref[...] * (z * _sigmoid(z))
        r = lax.rsqrt(jnp.mean(yg * yg, axis=-1, keepdims=True) + EPS)
        o_ref[...] = (yg * r * w_ref[...]).astype(o_ref.dtype)

    return pl.pallas_call(
        body, name="gnorm_fwd", grid=(dm.T // tr,), in_specs=[row, row, pl.BlockSpec((1, dm.W2), lambda i: (0, 0))],
        out_specs=row, out_shape=jax.ShapeDtypeStruct((dm.T, dm.W2), ACT_DTYPE), compiler_params=_cp(),
    )(y, proj, w.reshape(1, dm.W2))


def _gnorm_bwd(dn, y, proj, w, dm):
    tr = _pick(dm.T, (256, 128))
    row = pl.BlockSpec((tr, dm.W2), lambda i: (i, 0))
    vec = pl.BlockSpec((1, dm.W2), lambda i: (0, 0))

    def body(dn_ref, y_ref, z_ref, w_ref, dy_ref, dz_ref, dw_ref):
        @pl.when(pl.program_id(0) == 0)
        def _():
            dw_ref[...] = jnp.zeros_like(dw_ref)

        z = z_ref[...]
        sg = _sigmoid(z)
        sz = z * sg
        yv = y_ref[...]
        yg = yv * sz
        r = lax.rsqrt(jnp.mean(yg * yg, axis=-1, keepdims=True) + EPS)
        n = yg * r
        dout = dn_ref[...]
        dnn = dout * w_ref[...]
        dyg = r * (dnn - n * jnp.mean(dnn * n, axis=-1, keepdims=True))
        dy_ref[...] = dyg * sz
        dz_ref[...] = (dyg * yv * (sg * (1.0 + z * (1.0 - sg)))).astype(dz_ref.dtype)
        dw_ref[...] += jnp.sum(dout * n, axis=0, keepdims=True)

    return pl.pallas_call(
        body, name="gnorm_bwd", grid=(dm.T // tr,), in_specs=[row, row, row, vec], out_specs=[row, row, vec],
        out_shape=[jax.ShapeDtypeStruct((dm.T, dm.W2), f32), jax.ShapeDtypeStruct((dm.T, dm.W2), ACT_DTYPE),
                   jax.ShapeDtypeStruct((1, dm.W2), f32)],
        compiler_params=_cp(dimension_semantics=("arbitrary",)),
    )(dn, y, proj, w.reshape(1, dm.W2))


def _pool_mixed(u, g, row):
    s = u
    for k in range(POOL_GROUPS):
        s = jnp.where(k <= g, s + _shift_down(s, 1 << k, row), s)
    w = jnp.left_shift(2, g)
    cnt = jnp.minimum(row + 1, w).astype(f32)
    return s / cnt - u, cnt


def _pool_fwd(proj, pw, scale, dm):
    Dg = dm.Dg
    ou, og = dm.c_pu // Dg, dm.c_pg // Dg

    def body(u_ref, g_ref, w_ref, s_ref, o_ref):
        u = u_ref[...]
        row = _iota2(u.shape, 0)
        mixed, _ = _pool_mixed(u, pl.program_id(1), row)
        lin = _nn(_mx(mixed), _mx(w_ref[0]))
        gt = g_ref[...]
        o_ref[...] = (lin * s_ref[...] * (gt * _sigmoid(gt))).astype(o_ref.dtype)

    return pl.pallas_call(
        body, name="pool_fwd", grid=(dm.Bl, POOL_GROUPS),
        in_specs=[pl.BlockSpec((dm.S, Dg), lambda b, g: (b, ou + g)), pl.BlockSpec((dm.S, Dg), lambda b, g: (b, og + g)),
                  pl.BlockSpec((1, Dg, Dg), lambda b, g: (g, 0, 0)), pl.BlockSpec((1, Dg), lambda b, g: (0, g))],
        out_specs=pl.BlockSpec((dm.S, Dg), lambda b, g: (b, g)),
        out_shape=jax.ShapeDtypeStruct((dm.T, dm.D), ACT_DTYPE), compiler_params=_cp(),
    )(proj, proj, pw, scale.reshape(1, dm.D))


def _pool_bwd(proj, dout, pw, scale, dm):
    Dg = dm.Dg
    ou, og = dm.c_pu // Dg, dm.c_pg // Dg

    def body(u_ref, g_ref, d_ref, w_ref, s_ref, du_ref, dg_ref, dw_ref, ds_ref):
        @pl.when(pl.program_id(1) == 0)
        def _():
            dw_ref[...] = jnp.zeros_like(dw_ref)
            ds_ref[...] = jnp.zeros_like(ds_ref)

        g = pl.program_id(0)
        u = u_ref[...]
        row = _iota2(u.shape, 0)
        mixed, cnt = _pool_mixed(u, g, row)
        mixed_b = _mx(mixed)
        wb = _mx(w_ref[0])
        lin = _nn(mixed_b, wb)
        gt = g_ref[...]
        sg = _sigmoid(gt)
        silu = gt * sg
        d = d_ref[...]
        sc = s_ref[...]
        dlin = d * sc * silu
        ds_ref[...] += jnp.sum(d * lin * silu, axis=0, keepdims=True)
        dg_ref[...] = (d * lin * sc * (sg * (1.0 + gt * (1.0 - sg)))).astype(dg_ref.dtype)
        dlin_b = _mx(dlin)
        dmixed = _nt(dlin_b, wb)
        dw_ref[0] += _tn(mixed_b, dlin_b)
        r = dmixed / cnt
        for k in range(POOL_GROUPS):
            r = jnp.where(k <= g, r + _shift_up(r, 1 << k, row), r)
        du_ref[...] = (r - dmixed).astype(du_ref.dtype)

    return pl.pallas_call(
        body, name="pool_bwd", grid=(POOL_GROUPS, dm.Bl),
        in_specs=[pl.BlockSpec((dm.S, Dg), lambda g, b: (b, ou + g)), pl.BlockSpec((dm.S, Dg), lambda g, b: (b, og + g)),
                  pl.BlockSpec((dm.S, Dg), lambda g, b: (b, g)), pl.BlockSpec((1, Dg, Dg), lambda g, b: (g, 0, 0)),
                  pl.BlockSpec((1, Dg), lambda g, b: (0, g))],
        out_specs=[pl.BlockSpec((dm.S, Dg), lambda g, b: (b, g)), pl.BlockSpec((dm.S, Dg), lambda g, b: (b, g)),
                   pl.BlockSpec((1, Dg, Dg), lambda g, b: (g, 0, 0)), pl.BlockSpec((1, Dg), lambda g, b: (0, g))],
        out_shape=[jax.ShapeDtypeStruct((dm.T, dm.D), ACT_DTYPE), jax.ShapeDtypeStruct((dm.T, dm.D), ACT_DTYPE),
                   jax.ShapeDtypeStruct((POOL_GROUPS, Dg, Dg), f32), jax.ShapeDtypeStruct((1, dm.D), f32)],
        compiler_params=_cp(dimension_semantics=("arbitrary", "arbitrary")),
    )(proj, proj, dout, pw, scale.reshape(1, dm.D))


SB_TILE = 512


def _split2(a):
    hi = a.astype(bf16)
    return hi, (a - hi.astype(f32)).astype(bf16)


def _exact2_nn(a, u):
    hi, lo = _split2(a)
    return _nn(hi, u) + _nn(lo, u)


SB_CUM = 256


def _blocked_sums(a, u, suffix):
    nb = a.shape[1] // SB_CUM
    blocks = [a[:, i * SB_CUM:(i + 1) * SB_CUM] for i in range(nb)]
    tots = [jnp.sum(b, axis=1, keepdims=True) for b in blocks]
    out = []
    for i, b in enumerate(blocks):
        s = _exact2_nn(b, u)
        for t in (tots[i + 1:] if suffix else tots[:i]):
            s = s + t
        out.append(s)
    total = tots[0]
    for t in tots[1:]:
        total = total + t
    return (out[0] if nb == 1 else jnp.concatenate(out, axis=1)), total


def _sb_tile(qb, kb, base, u, suffix, causal):
    z = _nt(qb, kb)
    lb = jnp.minimum(z, 0.0) - jnp.log(1.0 + jnp.exp(-jnp.abs(z)))
    lm = lb - z
    if causal is not None:
        lm = jnp.where(causal, lm, 0.0)
    sums, tot = _blocked_sums(lm, u, suffix)
    att = jnp.exp(lb + base + sums) if suffix else jnp.exp(lb + base - sums)
    if causal is not None:
        att = jnp.where(causal, att, 0.0)
    return lb, att, tot


def _sb_fwd(proj, dm):
    L, S, D, TQ = LANES, dm.S, dm.D, SB_TILE
    oq, og = dm.c_qkv // L, dm.c_sbg // L
    nb = D // L
    scale = HEAD ** -0.5

    def body(q_ref, k_ref, v_ref, g_ref, o_ref, lt_ref, out_ref, k_s, va_s, vb_s):
        lo = _iota2((1, L), 1) < HEAD
        hi = jnp.logical_not(lo)
        k_s[...] = _mx(k_ref[...])
        vf = v_ref[...]
        va_s[...] = _mx(jnp.where(lo, vf, 0.0))
        vb_s[...] = _mx(jnp.where(hi, vf, 0.0))
        before = _iota2((TQ, TQ), 0) > _iota2((TQ, TQ), 1)
        ugt = (_iota2((SB_CUM, SB_CUM), 0) > _iota2((SB_CUM, SB_CUM), 1)).astype(bf16)
        v_s = (va_s, vb_s)

        def qloop(qt, _):
            rows = pl.ds(pl.multiple_of(qt * TQ, TQ), TQ)
            qs = q_ref[rows, :] * scale
            qb = [_mx(jnp.where(m, qs, 0.0)) for m in (lo, hi)]

            def tile(kt, carry, causal):
                runs, acc = carry
                krows = pl.ds(pl.multiple_of(kt * TQ, TQ), TQ)
                k_t = k_s[krows, :]
                new_runs = []
                for h in range(2):
                    _, att, tot = _sb_tile(qb[h], k_t, runs[h], ugt, True, causal)
                    acc = acc + _nn(_mx(att), v_s[h][krows, :])
                    new_runs.append(runs[h] + tot)
                return tuple(new_runs), acc

            zc = jnp.zeros((TQ, 1), f32)
            carry = tile(qt, ((zc, zc), jnp.zeros((TQ, L), f32)), before)
            runs, acc = lax.fori_loop(0, qt, lambda jj, c: tile(qt - 1 - jj, c, None), carry)
            o_ref[rows, :] = acc
            lt_ref[rows, :] = jnp.where(lo, runs[0], runs[1])
            gt = g_ref[rows, :]
            out_ref[rows, :] = (acc * (gt * _sigmoid(gt))).astype(out_ref.dtype)
            return 0

        lax.fori_loop(0, S // TQ, qloop, 0)

    def col(o):
        return pl.BlockSpec((S, L), lambda b, p: (b, o + p))

    return pl.pallas_call(
        body, name="sb_fwd", grid=(dm.Bl, nb),
        in_specs=[col(oq), col(oq + nb), col(oq + 2 * nb), col(og)], out_specs=[col(0), col(0), col(0)],
        out_shape=[jax.ShapeDtypeStruct((dm.T, D), f32), jax.ShapeDtypeStruct((dm.T, D), f32),
                   jax.ShapeDtypeStruct((dm.T, D), ACT_DTYPE)],
        scratch_shapes=[pltpu.VMEM((S, L), MXU_DTYPE)] * 3,
        compiler_params=_cp(),
    )(proj, proj, proj, proj)


def _sb_bwd(proj, o, lt, dsb, dm):
    L, S, D, TQ = LANES, dm.S, dm.D, SB_TILE
    oq, og = dm.c_qkv // L, dm.c_sbg // L
    nb = D // L
    scale = HEAD ** -0.5

    def body(q_ref, k_ref, v_ref, g_ref, o_ref, lt_ref, d_ref, dq_ref, dk_ref, dv_ref, dg_ref,
             k_s, ka_s, kb_s, v_s, dk_acc, dv_acc):
        lo = _iota2((1, L), 1) < HEAD
        hi = jnp.logical_not(lo)
        kf = k_ref[...]
        k_s[...] = _mx(kf)
        ka_s[...] = _mx(jnp.where(lo, kf, 0.0))
        kb_s[...] = _mx(jnp.where(hi, kf, 0.0))
        v_s[...] = _mx(v_ref[...])
        dk_acc[...] = jnp.zeros_like(dk_acc)
        dv_acc[...] = jnp.zeros_like(dv_acc)
        before = _iota2((TQ, TQ), 0) > _iota2((TQ, TQ), 1)
        ule = (_iota2((SB_CUM, SB_CUM), 0) <= _iota2((SB_CUM, SB_CUM), 1)).astype(bf16)
        ult = (_iota2((SB_CUM, SB_CUM), 0) < _iota2((SB_CUM, SB_CUM), 1)).astype(bf16)
        k_m = (ka_s, kb_s)

        def qloop(qt, _):
            rows = pl.ds(pl.multiple_of(qt * TQ, TQ), TQ)
            gt = g_ref[rows, :]
            sg = _sigmoid(gt)
            dsb_b = d_ref[rows, :]
            do = dsb_b * (gt * sg)
            dg_ref[rows, :] = (dsb_b * o_ref[rows, :] * (sg * (1.0 + gt * (1.0 - sg)))).astype(dg_ref.dtype)
            qs = q_ref[rows, :] * scale
            qb = [_mx(jnp.where(m, qs, 0.0)) for m in (lo, hi)]
            dob = [_mx(jnp.where(m, do, 0.0)) for m in (lo, hi)]
            ltot = [lt_ref[rows, 0:1], lt_ref[rows, HEAD:HEAD + 1]]

            def tile(kt, carry, causal):
                runs, rgs, dq = carry
                krows = pl.ds(pl.multiple_of(kt * TQ, TQ), TQ)
                k_t, v_t = k_s[krows, :], v_s[krows, :]
                dk_t = jnp.zeros((TQ, L), f32)
                dv_t = jnp.zeros((TQ, L), f32)
                new_runs, new_rgs = [], []
                for h in range(2):
                    lb, att, tot = _sb_tile(qb[h], k_t, ltot[h] - runs[h], ule, False, causal)
                    gm = att * _nt(dob[h], v_t)
                    gsum, gtot = _blocked_sums(gm, ult, False)
                    pre = rgs[h] + gsum
                    beta = jnp.exp(lb)
                    dz = gm * (1.0 - beta) - pre * beta
                    if causal is not None:
                        dz = jnp.where(causal, dz, 0.0)
                    dz = _mx(dz)
                    dq = dq + _nn(dz, k_m[h][krows, :])
                    dk_t = dk_t + _tn(dz, qb[h])
                    dv_t = dv_t + _tn(_mx(att), dob[h])
                    new_runs.append(runs[h] + tot)
                    new_rgs.append(rgs[h] + gtot)
                dk_acc[krows, :] += dk_t
                dv_acc[krows, :] += dv_t
                return tuple(new_runs), tuple(new_rgs), dq

            zc = jnp.zeros((TQ, 1), f32)
            carry = lax.fori_loop(0, qt, lambda kt, c: tile(kt, c, None), ((zc, zc), (zc, zc), jnp.zeros((TQ, L), f32)))
            _, _, dq = tile(qt, carry, before)
            dq_ref[rows, :] = (dq * scale).astype(dq_ref.dtype)
            return 0

        lax.fori_loop(0, S // TQ, qloop, 0)
        dk_ref[...] = dk_acc[...].astype(dk_ref.dtype)
        dv_ref[...] = dv_acc[...].astype(dv_ref.dtype)

    def col(off):
        return pl.BlockSpec((S, L), lambda b, p: (b, off + p))

    out = jax.ShapeDtypeStruct((dm.T, D), ACT_DTYPE)
    return pl.pallas_call(
        body, name="sb_bwd", grid=(dm.Bl, nb),
        in_specs=[col(oq), col(oq + nb), col(oq + 2 * nb), col(og), col(0), col(0), col(0)],
        out_specs=[col(0), col(0), col(0), col(0)], out_shape=[out, out, out, out],
        scratch_shapes=[pltpu.VMEM((S, L), MXU_DTYPE)] * 4 + [pltpu.VMEM((S, L), f32)] * 2,
        compiler_params=_cp(),
    )(proj, proj, proj, proj, o, lt, dsb)


def _merge_fwd(proj, ys, dm):
    tr = _pick(dm.T, (256, 128))
    ct = 512
    om = dm.c_mrg // ct
    nb = dm.D // ct
    blk = pl.BlockSpec((tr, ct), lambda i, j: (i, j))

    def body(l0, l1, l2, y0, y1, y2, o_ref):
        acc = _sigmoid(l0[...]) * y0[...] + _sigmoid(l1[...]) * y1[...] + _sigmoid(l2[...]) * y2[...]
        o_ref[...] = acc.astype(o_ref.dtype)

    return pl.pallas_call(
        body, name="merge_fwd", grid=(dm.T // tr, nb),
        in_specs=[pl.BlockSpec((tr, ct), functools.partial(lambda i, j, b: (i, om + b * nb + j), b=b)) for b in range(3)] + [blk] * 3,
        out_specs=blk, out_shape=jax.ShapeDtypeStruct((dm.T, dm.D), ACT_DTYPE), compiler_params=_cp(),
    )(proj, proj, proj, *ys)


def _merge_bwd(proj, ys, dmerged, dm):
    tr = _pick(dm.T, (256, 128))
    ct = 512
    om = dm.c_mrg // ct
    nb = dm.D // ct
    blk = pl.BlockSpec((tr, ct), lambda i, j: (i, j))

    def body(l0, l1, l2, y0, y1, y2, d_ref, dy0, dy1, dy2, dl0, dl1, dl2):
        d = d_ref[...]
        for l_ref, y_ref, dy_ref, dl_ref in ((l0, y0, dy0, dl0), (l1, y1, dy1, dl1), (l2, y2, dy2, dl2)):
            sg = _sigmoid(l_ref[...])
            dy_ref[...] = (d * sg).astype(dy_ref.dtype)
            dl_ref[...] = (d * y_ref[...] * sg * (1.0 - sg)).astype(dl_ref.dtype)

    out = jax.ShapeDtypeStruct((dm.T, dm.D), ACT_DTYPE)
    return pl.pallas_call(
        body, name="merge_bwd", grid=(dm.T // tr, nb),
        in_specs=[pl.BlockSpec((tr, ct), functools.partial(lambda i, j, b: (i, om + b * nb + j), b=b)) for b in range(3)] + [blk] * 4,
        out_specs=[blk] * 6, out_shape=[out] * 6, compiler_params=_cp(),
    )(proj, proj, proj, *ys, dmerged)


def _layer_fwd(x, p, dm):
    h = _rms_fwd(x, p["norm_w"], dm)
    proj = _mm(h, p["w_in"], name="mm_in")
    xbc = _conv_fwd(proj, p["conv_w"], p["conv_b"], dm)
    dt, cum, cumt = _dt_prep(proj, p["dt_bias"], p["a_log"], dm)
    y, hs = _ssd_fwd(xbc, dt, cum, cumt, p["d_skip"], dm)
    ssm_n = _gnorm_fwd(y, proj, p["ssm_norm_w"], dm)
    pool_o = _pool_fwd(proj, p["pool_w"], p["pool_scale"], dm)
    o, lt, sb_o = _sb_fwd(proj, dm)
    ys = (_mm(ssm_n, p["w_proj_ssm"], name="mm_ps"), _mm(pool_o, p["w_proj_pool"], name="mm_pp"),
          _mm(sb_o, p["w_proj_sb"], name="mm_pb"))
    merged = _merge_fwd(proj, ys, dm)
    x_next = _mm(merged, p["w_out"], res=x, name="mm_out")
    saved = dict(x=x, h=h, proj=proj, xbc=xbc, dt=dt, cum=cum, cumt=cumt, y=y, hs=hs, ssm_n=ssm_n, pool_o=pool_o, o=o,
                 lt=lt, sb_o=sb_o, ys=ys, merged=merged)
    return x_next, saved


def _layer_bwd(dx_out, p, sv, dm):
    g = {}
    proj = sv["proj"]
    dmerged = _mm(dx_out, p["w_out"], tb=True, name="mm_dmerged")
    g["w_out"] = _mm(sv["merged"], dx_out, ta=True, name="mm_dwout")
    dy0, dy1, dy2, dl0, dl1, dl2 = _merge_bwd(proj, sv["ys"], dmerged, dm)
    d_ssm_n = _mm(dy0, p["w_proj_ssm"], tb=True, name="mm_dssm")
    g["w_proj_ssm"] = _mm(sv["ssm_n"], dy0, ta=True, name="mm_dwps")
    d_pool_o = _mm(dy1, p["w_proj_pool"], tb=True, name="mm_dpool")
    g["w_proj_pool"] = _mm(sv["pool_o"], dy1, ta=True, name="mm_dwpp")
    d_sb_o = _mm(dy2, p["w_proj_sb"], tb=True, name="mm_dsb")
    g["w_proj_sb"] = _mm(sv["sb_o"], dy2, ta=True, name="mm_dwpb")
    dy, dz, g["ssm_norm_w"] = _gnorm_bwd(d_ssm_n, sv["y"], proj, p["ssm_norm_w"], dm)
    dxs, db, dc, ddt, dcum, dcum_t, dd = _ssd_bwd(sv["xbc"], sv["dt"], sv["cum"], sv["cumt"], p["d_skip"], sv["hs"], dy, dm)
    g["d_skip"] = dd
    d_dtraw, g["dt_bias"], g["a_log"] = _dt_bwd(proj, sv["dt"], dcum, dcum_t, ddt, p["dt_bias"], p["a_log"], dm)
    dxbc, g["conv_w"], g["conv_b"] = _conv_bwd(proj, jnp.concatenate([dxs, db, dc], axis=1), p["conv_w"], p["conv_b"], dm)
    dpu, dpg, g["pool_w"], g["pool_scale"] = _pool_bwd(proj, d_pool_o, p["pool_w"], p["pool_scale"], dm)
    dq, dk, dv, dsg = _sb_bwd(proj, sv["o"], sv["lt"], d_sb_o, dm)
    dproj = jnp.concatenate([dz, dxbc, dpu, dpg, dq, dk, dv, dsg, dl0, dl1, dl2, d_dtraw,
                             jnp.zeros((dm.T, DT_PAD - LANES), ACT_DTYPE)], axis=1)
    dh = _mm(dproj, p["w_in"], tb=True, name="mm_dh")
    g["w_in"] = _mm(sv["h"], dproj, ta=True, name="mm_dwin")
    dx, g["norm_w"] = _rms_bwd(sv["x"], dh, dx_out, p["norm_w"], dm)
    return dx, g


def _local_step(x, tgt, layers, final_norm_w, dm, on_grads=None):
    saved = []
    layers = list(layers)
    for l, p in enumerate(layers):
        if callable(p):
            p, x = p(x)
            layers[l] = p
        x, sv = _layer_fwd(x, p, dm)
        saved.append(sv)
    dx, dfn, loss = _loss_head(x, tgt, final_norm_w, dm)
    grads = [None] * len(layers)
    for l in reversed(range(len(layers))):
        dx, grads[l] = _layer_bwd(dx, layers[l], saved[l], dm)
        if on_grads is not None:
            dx = on_grads(l, grads, dx)
    return loss, dx, grads, dfn


def _row_tile(rows, cols):
    cap = max(8, (1 << 18) // cols)
    for t in (1024, 512, 256, 128, 64, 32, 16, 8):
        if t <= cap and rows % t == 0:
            return t
    return rows


def _adamw(w, g, m, v, name):
    rows, cols = w.shape
    tr = _row_tile(rows, cols)
    c1 = 1.0 - ADAM_B1 ** ADAM_STEP
    c2 = 1.0 - ADAM_B2 ** ADAM_STEP

    def body(w_ref, g_ref, m_ref, v_ref, d_ref, mo_ref, vo_ref):
        gv = g_ref[...]
        mn = ADAM_B1 * m_ref[...] + (1.0 - ADAM_B1) * gv
        vn = ADAM_B2 * v_ref[...] + (1.0 - ADAM_B2) * (gv * gv)
        d_ref[...] = -ADAM_LR * ((mn / c1) / (jnp.sqrt(vn / c2) + ADAM_EPS) + ADAM_WD * w_ref[...])
        mo_ref[...] = mn
        vo_ref[...] = vn

    blk = pl.BlockSpec((tr, cols), lambda i: (i, 0))
    out = jax.ShapeDtypeStruct((rows, cols), f32)
    return pl.pallas_call(body, name=name, grid=(rows // tr,), in_specs=[blk] * 4, out_specs=[blk] * 3, out_shape=[out] * 3,
                          compiler_params=_cp())(w, g, m, v)


def _sum_parts(a, parts, name):
    n, rows, cols = parts.shape
    tr = _row_tile(rows, cols)

    def body(a_ref, p_ref, o_ref):
        acc = a_ref[...]
        for k in range(n):
            acc = acc + p_ref[k].astype(f32)
        o_ref[...] = acc

    blk = pl.BlockSpec((tr, cols), lambda i: (i, 0))
    return pl.pallas_call(body, name=name, grid=(rows // tr,), in_specs=[blk, pl.BlockSpec((n, tr, cols), lambda i: (0, i, 0))],
                          out_specs=blk, out_shape=jax.ShapeDtypeStruct((rows, cols), f32), compiler_params=_cp())(a, parts)


ICI_KINDS = ("y", "x", "xy")
ASYNC_GATHER_ID = 1
ASYNC_IDS = {"rs_pair_late": 2, "rs_ici_late": 3, "rs_sibling_late": 4}
HBM_SPEC = pl.BlockSpec(memory_space=pltpu.HBM)


def _me():
    return lax.axis_index("x"), lax.axis_index("y"), lax.axis_index("c")


def _peer(kind):
    x, y, c = _me()
    return {"c": (x, y, 1 - c), "y": (x, 1 - y, c), "x": (1 - x, y, c), "xy": (1 - x, 1 - y, c)}[kind]


def _peer_chip(kind):
    x, y, _ = _me()
    return {"y": 2 * x + (1 - y), "x": 2 * (1 - x) + y, "xy": 2 * (1 - x) + (1 - y)}[kind]


def _exchange(sends, kinds, name):
    n, na = len(kinds), len(sends)

    def body(*refs):
        _exchange_copies(refs[:na], refs[na:2 * na], *refs[2 * na:], kinds)

    return pl.pallas_call(
        body, name=name, out_shape=[jax.ShapeDtypeStruct(a.shape, a.dtype) for a in sends], in_specs=[HBM_SPEC] * na,
        out_specs=[HBM_SPEC] * na,
        scratch_shapes=[pltpu.SemaphoreType.DMA((na * n,)), pltpu.SemaphoreType.DMA((na * n,))],
    )(*sends)


def _exchange_copies(srcs, dsts, ssem, rsem, kinds):
    n = len(kinds)
    cps = [pltpu.make_async_remote_copy(src_ref=srcs[i].at[k], dst_ref=dsts[i].at[k], send_sem=ssem.at[i * n + k],
                                        recv_sem=rsem.at[i * n + k], device_id=_peer(kind), device_id_type=MESH)
           for i in range(len(srcs)) for k, kind in enumerate(kinds)]
    for cp in cps:
        cp.start()
    for cp in cps:
        cp.wait()


def _exchange_async(sends, kinds, name):
    n, na = len(kinds), len(sends)
    srcs = [jax.new_ref(a, memory_space=pltpu.MemorySpace.HBM) for a in sends]
    dsts = [jax.empty_ref(jax.ShapeDtypeStruct(a.shape, a.dtype), memory_space=pltpu.MemorySpace.HBM) for a in sends]

    @pl.kernel(mesh=plsc.ScalarSubcoreMesh(axis_name="sequencer", num_cores=1), name=name,
               scratch_types=(pltpu.SemaphoreType.DMA((na * n,)), pltpu.SemaphoreType.DMA((na * n,))),
               compiler_params=pltpu.CompilerParams(collective_id=ASYNC_IDS[name]))
    def launch(ssem, rsem):
        barrier = pltpu.get_barrier_semaphore()
        for kind in kinds:
            pl.semaphore_signal(barrier, inc=1, device_id=_peer(kind), device_id_type=MESH)
        pl.semaphore_wait(barrier, n)
        _exchange_copies(srcs, dsts, ssem, rsem, kinds)

    launch()
    return [r[...] for r in dsts]


def _allgather_copies(srcs, outs, ssem, rsem, lsem, lh):
    na = len(srcs)
    x, y, c = _me()
    j_me = 2 * x + y
    mine = pl.ds(c * lh, lh)
    theirs = pl.ds((1 - c) * lh, lh)
    local = [pltpu.make_async_copy(srcs[i], outs[i].at[j_me], lsem.at[i]) for i in range(na)]
    for cp in local:
        cp.start()

    def ici(i, k, kind, j_src):
        return pltpu.make_async_remote_copy(src_ref=srcs[i].at[mine], dst_ref=outs[i].at[j_src, mine],
                                            send_sem=ssem.at[6 * i + k], recv_sem=rsem.at[6 * i + k],
                                            device_id=_peer(kind), device_id_type=MESH)

    def d2d(i, k, j_src, half):
        return pltpu.make_async_remote_copy(src_ref=outs[i].at[j_src, half], dst_ref=outs[i].at[j_src, half],
                                            send_sem=ssem.at[6 * i + 3 + k], recv_sem=rsem.at[6 * i + 3 + k],
                                            device_id=_peer("c"), device_id_type=MESH)

    first = [ici(i, k, kind, j_me) for i in range(na) for k, kind in enumerate(ICI_KINDS)]
    for cp in first:
        cp.start()
    passed = []
    for i in range(na):
        for k, kind in enumerate(ICI_KINDS):
            ici(i, k, kind, _peer_chip(kind)).wait_recv()
            fwd = d2d(i, k, _peer_chip(kind), mine)
            fwd.start()
            passed.append(fwd)
    for i in range(na):
        for k, kind in enumerate(ICI_KINDS):
            d2d(i, k, _peer_chip(kind), theirs).wait_recv()
    for cp in first + passed:
        cp.wait_send()
    for cp in local:
        cp.wait()


def _allgather_sems(na):
    return [pltpu.SemaphoreType.DMA((6 * na,)), pltpu.SemaphoreType.DMA((6 * na,)), pltpu.SemaphoreType.DMA((na,))]


def _allgather_shards(shards):
    na = len(shards)
    lh = shards[0].shape[0] // 2

    def body(*refs):
        _allgather_copies(refs[:na], refs[na:2 * na], *refs[2 * na:], lh)

    return pl.pallas_call(
        body, name="allgather_shards", out_shape=[jax.ShapeDtypeStruct((4,) + a.shape, a.dtype) for a in shards],
        in_specs=[HBM_SPEC] * na, out_specs=[HBM_SPEC] * na, scratch_shapes=_allgather_sems(na),
    )(*shards)


def _allgather_shards_async(shards):
    na = len(shards)
    lh = shards[0].shape[0] // 2
    srcs = [jax.new_ref(a, memory_space=pltpu.MemorySpace.HBM) for a in shards]
    outs = [jax.empty_ref(jax.ShapeDtypeStruct((4,) + a.shape, a.dtype), memory_space=pltpu.MemorySpace.HBM) for a in shards]

    @pl.kernel(mesh=plsc.ScalarSubcoreMesh(axis_name="sequencer", num_cores=1), name="allgather_shards_async",
               scratch_types=tuple(_allgather_sems(na)), compiler_params=pltpu.CompilerParams(collective_id=ASYNC_GATHER_ID))
    def launch(ssem, rsem, lsem):
        barrier = pltpu.get_barrier_semaphore()
        peers = ("c",) + ICI_KINDS
        for kind in peers:
            pl.semaphore_signal(barrier, inc=1, device_id=_peer(kind), device_id_type=MESH)
        pl.semaphore_wait(barrier, len(peers))
        _allgather_copies(srcs, outs, ssem, rsem, lsem, lh)

    launch()
    return [r[...] for r in outs]


def _allreduce_small(v):
    rows = v.shape[0]

    def body(v_ref, o_ref, buf, ssem, rsem):
        x, y, c = _me()
        me = 4 * x + 2 * y + c
        buf[0] = v_ref[...]
        cps = []
        for k in range(1, 8):
            peer = (1 - x if k & 4 else x, 1 - y if k & 2 else y, 1 - c if k & 1 else c)
            cps.append(pltpu.make_async_remote_copy(src_ref=v_ref, dst_ref=buf.at[k], send_sem=ssem.at[k - 1], recv_sem=rsem.at[k - 1],
                                                    device_id=peer, device_id_type=MESH))
        for cp in cps:
            cp.start()
        for cp in cps:
            cp.wait()
        acc = buf[jnp.bitwise_xor(me, 0)]
        for d in range(1, 8):
            acc = acc + buf[jnp.bitwise_xor(me, d)]
        o_ref[...] = acc

    vm = pl.BlockSpec(memory_space=pltpu.VMEM)
    return pl.pallas_call(
        body, name="allreduce_small", out_shape=jax.ShapeDtypeStruct(v.shape, f32), in_specs=[vm], out_specs=vm,
        scratch_shapes=[pltpu.VMEM((8, rows, LANES), f32), pltpu.SemaphoreType.DMA((7,)), pltpu.SemaphoreType.DMA((7,))],
    )(v)


SHARDED = ("w_in", "w_proj_ssm", "w_proj_pool", "w_proj_sb", "w_out", "pool_w", "conv_w")
REST = SHARDED[1:6]
SMALL = ("norm_w", "conv_b", "dt_bias", "a_log", "d_skip", "ssm_norm_w", "pool_scale")


def _pack_rows(dm):
    rows = dict(w_proj_ssm=dm.W2 // 4, w_proj_pool=dm.D // 4, w_proj_sb=dm.D // 4, w_out=dm.D // 4, pool_w=dm.D // 16)
    assert all(r % 16 == 0 for r in rows.values())
    return rows


def _pack(sh, dm, dtype):
    rows = _pack_rows(dm)
    ly = sh["w_out"].shape[0]
    return jnp.concatenate([sh[n].astype(dtype).reshape(ly, rows[n], dm.D) for n in REST], axis=1)


def _unpack(pk, dm):
    rows = _pack_rows(dm)
    lead = pk.shape[:-2]
    shapes = dict(w_proj_ssm=(dm.W2 // 4, dm.D), w_proj_pool=(dm.D // 4, dm.D), w_proj_sb=(dm.D // 4, dm.D),
                  w_out=(dm.D // 4, dm.D), pool_w=(POOL_GROUPS, dm.Dg // 4, dm.Dg))
    out, r0 = {}, 0
    for n in REST:
        out[n] = pk[..., r0:r0 + rows[n], :].reshape(lead + shapes[n])
        r0 += rows[n]
    return out


def _join_shards(sh, dm):
    full = {}
    w = jnp.moveaxis(sh["w_in"], 0, 2)
    full["w_in"] = _permute_cols(w.reshape(w.shape[0], dm.D, dm.IN_COLS), dm)
    for n in ("w_proj_ssm", "w_proj_pool", "w_proj_sb", "w_out"):
        a = jnp.moveaxis(sh[n], 0, 1)
        full[n] = a.reshape(a.shape[0], -1, dm.D)
    a = jnp.moveaxis(sh["pool_w"], 0, 2)
    full["pool_w"] = a.reshape(a.shape[0], POOL_GROUPS, dm.Dg, dm.Dg)
    a = jnp.moveaxis(sh["conv_w"], 0, 2)
    full["conv_w"] = a.reshape(a.shape[0], CONV_WIDTH, dm.CC)
    return full


def _split_shards(full, dm):
    sh = {}
    w = _unpermute_cols(full["w_in"], dm)
    sh["w_in"] = jnp.moveaxis(w.reshape(w.shape[0], dm.D, 4, dm.IN_COLS // 4), 2, 0)
    for n in ("w_proj_ssm", "w_proj_pool", "w_proj_sb", "w_out"):
        a = full[n]
        sh[n] = jnp.moveaxis(a.reshape(a.shape[0], 4, a.shape[1] // 4, dm.D), 1, 0)
    a = full["pool_w"]
    sh["pool_w"] = jnp.moveaxis(a.reshape(a.shape[0], POOL_GROUPS, 4, dm.Dg // 4, dm.Dg), 2, 0)
    a = full["conv_w"]
    sh["conv_w"] = jnp.moveaxis(a.reshape(a.shape[0], CONV_WIDTH, 4, dm.CC // 4), 2, 0)
    return sh


def _sum_like(mine, parts, name):
    cols = mine.shape[-1]
    return _sum_parts(mine.reshape(-1, cols), parts.reshape(parts.shape[0], -1, cols), name).reshape(mine.shape)


def _slab(a, j):
    return lax.dynamic_index_in_dim(a, j, 0, keepdims=False)


class _ReduceScatter:
    TAGS = ("in", "rest", "conv")

    def __init__(self, full_grads, dm, exchange, suffix=""):
        x, y, self.c = _me()
        self.j_me, self.dm, self.exchange, self.suffix = 2 * x + y, dm, exchange, suffix
        sh = _split_shards(full_grads, dm)
        by_chip = [sh["w_in"], jnp.stack([_pack({n: sh[n][j] for n in REST}, dm, f32) for j in range(4)]), sh["conv_w"]]
        lh = by_chip[0].shape[1] // 2
        self.mine = [lax.dynamic_slice_in_dim(a, self.c * lh, lh, axis=1) for a in by_chip]
        theirs = [lax.dynamic_slice_in_dim(a, (1 - self.c) * lh, lh, axis=1).astype(WIRE_DTYPE)[None] for a in by_chip]
        self.got = exchange(theirs, ("c",), "rs_pair" + suffix)

    def pair_sum(self, dep=None):
        got = self.got
        if dep is not None:
            got, dep = lax.optimization_barrier((got, dep))
        self.s1 = [_sum_like(a, g, "rs_pair_sum_" + t) for a, g, t in zip(self.mine, got, self.TAGS)]
        send = [jnp.stack([_slab(a, jnp.bitwise_xor(self.j_me, k)) for k in (1, 2, 3)]).astype(WIRE_DTYPE) for a in self.s1]
        self.got = self.exchange(send, ICI_KINDS, "rs_ici" + self.suffix)
        return dep

    def ici_sum(self, dep=None):
        got = self.got
        if dep is not None:
            got, dep = lax.optimization_barrier((got, dep))
        self.red = [_sum_like(_slab(a, self.j_me), g, "rs_ici_sum_" + t) for a, g, t in zip(self.s1, got, self.TAGS)]
        self.sib = self.exchange([r[None] for r in self.red], ("c",), "rs_sibling" + self.suffix)
        return dep

    def result(self):
        both = [jnp.where(self.c == 0, jnp.concatenate([r, s[0]]), jnp.concatenate([s[0], r])) for r, s in zip(self.red, self.sib)]
        out = _unpack(both[1], self.dm)
        out["w_in"], out["conv_w"] = both[0], both[2]
        return out


def _reduce_scatter(full_grads, dm):
    rs = _ReduceScatter(full_grads, dm, _exchange)
    rs.pair_sum()
    rs.ici_sum()
    return rs.result()


def _flatten_small(parts):
    flat = jnp.concatenate([p.reshape(-1).astype(f32) for p in parts])
    rows = -(-flat.shape[0] // (8 * LANES)) * 8
    return jnp.pad(flat, (0, rows * LANES - flat.shape[0])).reshape(rows, LANES)


def _unflatten_small(buf, shapes):
    flat = buf.reshape(-1)
    out, o = [], 0
    for s in shapes:
        n = math.prod(s)
        out.append(flat[o:o + n].reshape(s))
        o += n
    return out


def kernel(x, norm_w, w_in, conv_w, conv_b, dt_bias, a_log, d_skip, ssm_norm_w, pool_w, pool_scale, w_proj_ssm, w_proj_pool, w_proj_sb, w_out, final_norm_w, loss_target, m_norm_w, m_w_in, m_conv_w, m_conv_b, m_dt_bias, m_a_log, m_d_skip, m_ssm_norm_w, m_pool_w, m_pool_scale, m_w_proj_ssm, m_w_proj_pool, m_w_proj_sb, m_w_out, m_final_norm_w, v_norm_w, v_w_in, v_conv_w, v_conv_b, v_dt_bias, v_a_log, v_d_skip, v_ssm_norm_w, v_pool_w, v_pool_scale, v_w_proj_ssm, v_w_proj_pool, v_w_proj_sb, v_w_out, v_final_norm_w):
    names = ("norm_w", "w_in", "conv_w", "conv_b", "dt_bias", "a_log", "d_skip", "ssm_norm_w", "pool_w", "pool_scale",
             "w_proj_ssm", "w_proj_pool", "w_proj_sb", "w_out", "final_norm_w")
    w = dict(zip(names, (norm_w, w_in, conv_w, conv_b, dt_bias, a_log, d_skip, ssm_norm_w, pool_w, pool_scale, w_proj_ssm,
                         w_proj_pool, w_proj_sb, w_out, final_norm_w)))
    m = dict(zip(names, (m_norm_w, m_w_in, m_conv_w, m_conv_b, m_dt_bias, m_a_log, m_d_skip, m_ssm_norm_w, m_pool_w, m_pool_scale,
                         m_w_proj_ssm, m_w_proj_pool, m_w_proj_sb, m_w_out, m_final_norm_w)))
    v = dict(zip(names, (v_norm_w, v_w_in, v_conv_w, v_conv_b, v_dt_bias, v_a_log, v_d_skip, v_ssm_norm_w, v_pool_w, v_pool_scale,
                         v_w_proj_ssm, v_w_proj_pool, v_w_proj_sb, v_w_out, v_final_norm_w)))
    bl, s, d = x.shape
    dm = Dims(bl, s, d)
    ly = norm_w.shape[0]

    mine = [w["w_in"].astype(WIRE_DTYPE), _pack({n: w[n] for n in REST}, dm, WIRE_DTYPE), conv_w]
    if ly % 4 == 0:
        gathered = [_allgather_shards([a[:ly // 2] for a in mine])]
        later = lax.optimization_barrier(([a[ly // 2:] for a in mine], gathered[0]))[0]
        gathered.append(_allgather_shards_async(later))
    else:
        gathered = [_allgather_shards(mine)]
    def layer_dicts(part, first):
        g_in, g_rest, g_conv = part
        shards = _unpack(g_rest, dm)
        shards["w_in"], shards["conv_w"] = g_in, g_conv
        full = _join_shards(shards, dm)
        return [{**{n: full[n][l] for n in SHARDED}, **{n: w[n][first + l] for n in SMALL}} for l in range(g_in.shape[1])]

    layers = layer_dicts(gathered[0], 0)
    if len(gathered) > 1:
        late = {}

        def late_layer(x, l):
            if not late:
                part, x = lax.optimization_barrier((gathered[1], x))
                late["dicts"] = layer_dicts(part, ly // 2)
            return late["dicts"][l], x

        layers += [functools.partial(late_layer, l=l) for l in range(ly - ly // 2)]

    def stacked(grads, lo_l, hi_l):
        return {n: jnp.stack([grads[l][n] for l in range(lo_l, hi_l)]) for n in SHARDED}

    late = {}

    def on_grads(l, grads, dx):
        half = ly // 2
        if l == half:
            late["rs"] = _ReduceScatter(stacked(grads, half, ly), dm, _exchange_async, "_late")
            late["rs"].pair_sum()
        elif l == half - 1:
            dx = late["rs"].ici_sum(dx)
        return dx

    loss_part, dx, grads, dfn = _local_step(x.reshape(dm.T, d), loss_target.reshape(dm.T, d), layers, final_norm_w, dm,
                                            on_grads if ly % 4 == 0 else None)

    h = dm.H
    small_parts = [loss_part]
    small_shapes = [(1, LANES)]
    for n in SMALL:
        if n in ("dt_bias", "a_log", "d_skip"):
            small_parts.append(jnp.stack([g[n][0, :h] for g in grads]))
        else:
            small_parts.append(jnp.stack([g[n][0] for g in grads]))
        small_shapes.append(w[n].shape)
    small_parts.append(dfn[0])
    small_shapes.append(final_norm_w.shape)
    red_small = _allreduce_small(_flatten_small(small_parts))
    small_g = _unflatten_small(red_small, small_shapes)
    loss = small_g[0][0, 0]
    g_out = dict(zip(SMALL + ("final_norm_w",), small_g[1:]))

    if late:
        early = _reduce_scatter(stacked(grads, 0, ly // 2), dm)
        rest = late["rs"].result()
        g_out.update({n: jnp.concatenate([early[n], rest[n]]) for n in SHARDED})
    else:
        g_out.update(_reduce_scatter(stacked(grads, 0, ly), dm))

    small_names = SMALL + ("final_norm_w",)
    zero_row = jnp.zeros((1, LANES), f32)
    pack_small = lambda t: _flatten_small([zero_row] + [t[n] for n in small_names])
    ds, ms, vs = _adamw(pack_small(w), red_small, pack_small(m), pack_small(v), "adamw_small")
    delta, new_m, new_v = {}, {}, {}
    for tgt, buf in ((delta, ds), (new_m, ms), (new_v, vs)):
        tgt.update(zip(small_names, _unflatten_small(buf, small_shapes)[1:]))
    for n in SHARDED:
        shp = w[n].shape
        two = (math.prod(shp[:-1]), shp[-1])
        dd, mm, vv = _adamw(w[n].reshape(two), g_out[n].reshape(two), m[n].reshape(two), v[n].reshape(two), "adamw_" + n)
        delta[n], new_m[n], new_v[n] = dd.reshape(shp), mm.reshape(shp), vv.reshape(shp)
        g_out[n] = g_out[n].reshape(shp)

    return (loss, dx.reshape(bl, s, d), *[g_out[n] for n in names], *[delta[n] for n in names],
            *[new_m[n] for n in names], *[new_v[n] for n in names])
```
